```python
import jax, jax.numpy as jnp
from jax import lax
import numpy as np

D_MODEL = 1024
BATCH = 4
SEQ = 4096
DEPTH = 2

MEM_LEN = 256
D_MIX = 2 * D_MODEL
D_GROUP = D_MIX // 4
A_HEADS = 4
A_HD = D_GROUP // A_HEADS
B_HEADS = 8
B_KV_HEADS = 2
B_HD = D_GROUP // B_HEADS
WINDOW = 128
BLOCK = 128
ROPE_THETA = 500000.0
ROPE_DIM = B_HD // 4
C_HEADS = 4
C_QK = D_GROUP // (2 * C_HEADS)
C_V = D_GROUP // C_HEADS
CONV_W = 5
D_HEADS = 4
D_HD = D_GROUP // D_HEADS

CHUNK = 64
MAX_POS_OFFSET = 1024
EPS = 1e-6

IN_SIZES = (
    D_GROUP, D_GROUP, D_GROUP, D_GROUP, D_GROUP,
    B_HEADS * B_HD, B_KV_HEADS * B_HD, B_KV_HEADS * B_HD, D_GROUP,
    C_HEADS * C_QK, C_HEADS * C_QK, C_HEADS * C_V, D_GROUP, D_GROUP,
    2 * C_HEADS, 2 * C_HEADS,
    D_GROUP, D_GROUP,
)
D_IN = sum(IN_SIZES)
SPLIT_POINTS = tuple(int(s) for s in np.cumsum(IN_SIZES)[:-1])

kernel_name = "hybrid_hgrn2_swa_mlstm_memxattn_encoder"

F32 = jnp.float32


def _rmsnorm(x, g):
    xf = x.astype(F32)
    y = xf * lax.rsqrt(jnp.mean(xf * xf, axis=-1, keepdims=True) + EPS)
    return (y * g.astype(F32)).astype(x.dtype)


def _head_layernorm(h, g):
    mu = jnp.mean(h, axis=-1, keepdims=True)
    var = jnp.mean(jnp.square(h - mu), axis=-1, keepdims=True)
    return (h - mu) * lax.rsqrt(var + EPS) * g.astype(F32)


def _to_chunks(a):
    b, h, t = a.shape[:3]
    a = a.reshape((b, h, t // CHUNK, CHUNK) + a.shape[3:])
    return jnp.moveaxis(a, 2, 0)


def _from_chunks(a):
    a = jnp.moveaxis(a, 0, 2)
    return a.reshape(a.shape[:2] + (a.shape[2] * a.shape[3],) + a.shape[4:])


def _flip_t(a):
    return jnp.flip(a, axis=2)


def _hgrn2_scan(q, k, v, log_f):
    bsz, nh, _, dk = q.shape
    dv = v.shape[-1]
    mask = jnp.tril(jnp.ones((CHUNK, CHUNK), bool))[:, :, None]

    def step(S, inp):
        qb, kb, vb, gb = inp
        b = jnp.cumsum(gb, axis=2)
        diff = b[:, :, :, None, :] - b[:, :, None, :, :]
        decay = jnp.exp(jnp.where(mask, diff, -jnp.inf))
        attn = jnp.einsum('bhtd,bhsd,bhtsd->bhts', qb, kb, decay)
        o = (jnp.einsum('bhts,bhsv->bhtv', attn, vb)
             + jnp.einsum('bhtd,bhdv->bhtv', qb * jnp.exp(b), S))
        b_last = b[:, :, -1:, :]
        S = (jnp.exp(b_last[:, :, 0, :])[..., None] * S
             + jnp.einsum('bhsd,bhsv->bhdv', kb * jnp.exp(b_last - b), vb))
        return S, o

    S0 = jnp.zeros((bsz, nh, dk, dv), F32)
    _, o = lax.scan(step, S0, (_to_chunks(q), _to_chunks(k), _to_chunks(v), _to_chunks(log_f)))
    return _from_chunks(o)


def _hgrn2_branch(q, i, zf, zb, z, lb, norm_g):
    bsz, t, _ = q.shape
    heads = lambda a: a.astype(F32).reshape(bsz, t, A_HEADS, A_HD).transpose(0, 2, 1, 3)
    lbh = lb.astype(F32).reshape(A_HEADS, 1, A_HD)
    qh, vh = heads(q), heads(i)

    def gates(zg):
        zg = heads(zg)
        f = lbh + (1.0 - lbh) * jax.nn.sigmoid(zg)
        return (1.0 - lbh) * jax.nn.sigmoid(-zg), jnp.log(f)

    kf, gf = gates(zf)
    kb, gb = gates(zb)
    o_f = _hgrn2_scan(qh, kf, vh, gf)
    o_b = _flip_t(_hgrn2_scan(_flip_t(qh), _flip_t(kb), _flip_t(vh), _flip_t(gb)))
    o = (o_f + o_b).transpose(0, 2, 1, 3)
    o = _rmsnorm(o, norm_g.reshape(A_HEADS, A_HD))
    return o.reshape(bsz, t, D_GROUP).astype(z.dtype) * jax.nn.silu(z)


def _rope(x, cos, sin):
    half = ROPE_DIM // 2
    x1 = x[..., :half].astype(F32)
    x2 = x[..., half:ROPE_DIM].astype(F32)
    rot = jnp.concatenate([x1 * cos - x2 * sin, x2 * cos + x1 * sin], axis=-1).astype(x.dtype)
    return jnp.concatenate([rot, x[..., ROPE_DIM:]], axis=-1)


def _window_attention(q, k, v, z, sink, cos, sin):
    bsz, t, _ = q.shape
    nb = t // BLOCK
    grp = B_HEADS // B_KV_HEADS
    q = _rope(q.reshape(bsz, t, B_HEADS, B_HD), cos, sin).reshape(bsz, nb, BLOCK, B_KV_HEADS, grp, B_HD)
    k = _rope(k.reshape(bsz, t, B_KV_HEADS, B_HD), cos, sin)
    v = v.reshape(bsz, t, B_KV_HEADS, B_HD)

    def band(a):
        a = jnp.pad(a, ((0, 0), (BLOCK, BLOCK), (0, 0), (0, 0))).reshape(bsz, nb + 2, BLOCK, B_KV_HEADS, B_HD)
        return jnp.concatenate([a[:, :-2], a[:, 1:-1], a[:, 2:]], axis=2)

    kw, vw = band(k), band(v)
    qpos = jnp.arange(t).reshape(nb, BLOCK)
    kpos = jnp.arange(-BLOCK, t + BLOCK).reshape(nb + 2, BLOCK)
    kpos = jnp.concatenate([kpos[:-2], kpos[1:-1], kpos[2:]], axis=1)
    kp = kpos[:, None, :]
    valid = (jnp.abs(qpos[:, :, None] - kp) <= WINDOW) & (kp >= 0) & (kp < t)
    s = jnp.einsum('bnqkgd,bnskd->bnkgqs', q, kw).astype(F32) * (B_HD ** -0.5)
    s = jnp.where(valid[None, :, None, None], s, -jnp.inf)
    sk = jnp.broadcast_to(sink.astype(F32).reshape(1, 1, B_KV_HEADS, grp, 1, 1), s.shape[:-1] + (1,))
    p = jax.nn.softmax(jnp.concatenate([s, sk], axis=-1), axis=-1)[..., :-1]
    o = jnp.einsum('bnkgqs,bnskd->bnqkgd', p.astype(v.dtype), vw)
    return o.reshape(bsz, t, D_GROUP) * jax.nn.silu(z)


def _centered_dwconv(x, w):
    pad = CONV_W // 2
    return lax.conv_general_dilated(
        x, w[:, None, :].astype(x.dtype), window_strides=(1,), padding=((pad, pad),),
        dimension_numbers=('NWC', 'WIO', 'NWC'), feature_group_count=x.shape[-1])


def _mlstm_scan(q, k, v, log_i, log_f):
    bsz, nh, _, dk = q.shape
    dv = v.shape[-1]
    mask = jnp.tril(jnp.ones((CHUNK, CHUNK), bool))

    def step(carry, inp):
        C, n, m = carry
        qb, kb, vb, ib, fb = inp
        b = jnp.cumsum(fb, axis=-1)
        d_intra = jnp.where(mask, b[..., :, None] - b[..., None, :] + ib[..., None, :], -jnp.inf)
        d_inter = b + m[..., None]
        m_t = jnp.maximum(jnp.max(d_intra, axis=-1), d_inter)
        w = jnp.einsum('bhtd,bhsd->bhts', qb, kb) * jnp.exp(d_intra - m_t[..., None])
        g = jnp.exp(d_inter - m_t)
        num = jnp.einsum('bhts,bhsv->bhtv', w, vb) + g[..., None] * jnp.einsum('bhtd,bhdv->bhtv', qb, C)
        den = jnp.sum(w, axis=-1) + g * jnp.einsum('bhtd,bhd->bht', qb, n)
        h = num / jnp.maximum(jnp.abs(den), jnp.exp(-m_t))[..., None]
        b_last = b[..., -1]
        a = b_last[..., None] - b + ib
        m_new = jnp.maximum(b_last + m, jnp.max(a, axis=-1))
        ws = jnp.exp(a - m_new[..., None])
        decay = jnp.exp(b_last + m - m_new)
        C = decay[..., None, None] * C + jnp.einsum('bhs,bhsd,bhsv->bhdv', ws, kb, vb)
        n = decay[..., None] * n + jnp.einsum('bhs,bhsd->bhd', ws, kb)
        return (C, n, m_new), h

    init = (jnp.zeros((bsz, nh, dk, dv), F32), jnp.zeros((bsz, nh, dk), F32), jnp.zeros((bsz, nh), F32))
    _, h = lax.scan(step, init, tuple(_to_chunks(a) for a in (q, k, v, log_i, log_f)))
    return _from_chunks(h)


def _mlstm_branch(q, k, v, o, z, ig, fg, gate_b, conv_w, norm_g):
    bsz, t, _ = q.shape
    qk = jax.nn.silu(_centered_dwconv(jnp.concatenate([q, k], axis=-1), conv_w))
    q, k = jnp.split(qk, 2, axis=-1)
    heads = lambda a, d: a.astype(F32).reshape(bsz, t, C_HEADS, d).transpose(0, 2, 1, 3)
    qh = heads(q, C_QK) * (C_QK ** -0.5)
    kh = heads(k, C_QK)
    vh = heads(v, C_V)
    gb = gate_b.astype(F32)
    to_dir = lambda a: a.reshape(bsz, t, 2, C_HEADS).transpose(2, 0, 3, 1)
    log_i = to_dir(ig.astype(F32) + gb[:2 * C_HEADS])
    log_f = to_dir(jax.nn.log_sigmoid(fg.astype(F32) + gb[2 * C_HEADS:]))
    h_f = _mlstm_scan(qh, kh, vh, log_i[0], log_f[0])
    h_b = _flip_t(_mlstm_scan(_flip_t(qh), _flip_t(kh), _flip_t(vh), _flip_t(log_i[1]), _flip_t(log_f[1])))
    h = (h_f + h_b).transpose(0, 2, 1, 3) * jax.nn.sigmoid(o.astype(F32)).reshape(bsz, t, C_HEADS, C_V)
    h = _head_layernorm(h, norm_g.reshape(C_HEADS, C_V))
    return h.reshape(bsz, t, D_GROUP).astype(z.dtype) * jax.nn.silu(z)


def _memory_attention(q, z, mem_n, w_kv):
    bsz, t, _ = q.shape
    m = mem_n.shape[1]
    k, v = jnp.split(mem_n @ w_kv, 2, axis=-1)
    k = k.reshape(bsz, m, D_HEADS, D_HD)
    v = v.reshape(bsz, m, D_HEADS, D_HD)
    qh = q.reshape(bsz, t, D_HEADS, D_HD)
    s = jnp.einsum('bthd,bmhd->bhtm', qh, k).astype(F32) * (D_HD ** -0.5)
    p = jax.nn.softmax(s, axis=-1).astype(v.dtype)
    o = jnp.einsum('bhtm,bmhd->bthd', p, v).reshape(bsz, t, D_GROUP)
    return o * jax.nn.silu(z)


def setup_inputs(seed: int = 0) -> dict:
    key = jax.random.key(seed)
    ks = jax.random.split(key, 18)
    nrm = lambda k, shape: jax.random.normal(k, shape, F32)
    x = nrm(ks[0], (BATCH, SEQ, D_MODEL))
    mem = nrm(ks[1], (BATCH, MEM_LEN, D_MODEL))
    positions = (jax.random.randint(ks[2], (BATCH, 1), 0, MAX_POS_OFFSET)
                 + jnp.arange(SEQ)[None, :]).astype(jnp.int32)
    norm_g = 1.0 + 0.02 * nrm(ks[3], (DEPTH, D_MODEL))
    w_in = nrm(ks[4], (DEPTH, D_MODEL, D_IN)) * (D_MODEL ** -0.5)
    hgrn_lb_logits = 0.5 * nrm(ks[5], (DEPTH, D_GROUP))
    hgrn_norm_g = 1.0 + 0.02 * nrm(ks[6], (DEPTH, D_GROUP))
    attn_sink = 0.5 * nrm(ks[7], (DEPTH, B_HEADS))
    mlstm_conv_w = nrm(ks[8], (DEPTH, CONV_W, 2 * C_HEADS * C_QK)) * (CONV_W ** -0.5)
    f_bias = jnp.tile(jnp.linspace(3.0, 6.0, C_HEADS), 2)[None, :] + 0.1 * nrm(ks[9], (DEPTH, 2 * C_HEADS))
    i_bias = 0.1 * nrm(ks[10], (DEPTH, 2 * C_HEADS))
    mlstm_gate_b = jnp.concatenate([i_bias, f_bias], axis=-1)
    mlstm_norm_g = 1.0 + 0.02 * nrm(ks[11], (DEPTH, D_GROUP))
    mem_norm_g = 1.0 + 0.02 * nrm(ks[12], (DEPTH, D_MODEL))
    w_mem_kv = nrm(ks[13], (DEPTH, D_MODEL, 2 * D_GROUP)) * (D_MODEL ** -0.5)
    w_out = nrm(ks[14], (DEPTH, D_MIX, D_MODEL)) * (D_MIX ** -0.5)
    final_norm_g = 1.0 + 0.02 * nrm(ks[15], (D_MODEL,))
    return {
        "x": x, "mem": mem, "positions": positions, "norm_g": norm_g, "w_in": w_in,
        "hgrn_lb_logits": hgrn_lb_logits, "hgrn_norm_g": hgrn_norm_g, "attn_sink": attn_sink,
        "mlstm_conv_w": mlstm_conv_w, "mlstm_gate_b": mlstm_gate_b, "mlstm_norm_g": mlstm_norm_g,
        "mem_norm_g": mem_norm_g, "w_mem_kv": w_mem_kv, "w_out": w_out, "final_norm_g": final_norm_g,
    }


def reference(x, mem, positions, norm_g, w_in, hgrn_lb_logits, hgrn_norm_g, attn_sink,
              mlstm_conv_w, mlstm_gate_b, mlstm_norm_g, mem_norm_g, w_mem_kv, w_out, final_norm_g):
    lb_all = jnp.cumsum(jax.nn.softmax(hgrn_lb_logits.astype(F32), axis=0), axis=0)
    lb_all = lb_all - lb_all[:1]
    inv_freq = ROPE_THETA ** (-jnp.arange(0, ROPE_DIM, 2, dtype=F32) / ROPE_DIM)
    ang = positions.astype(F32)[..., None] * inv_freq
    cos = jnp.cos(ang)[:, :, None, :]
    sin = jnp.sin(ang)[:, :, None, :]
    for l in range(DEPTH):
        h = _rmsnorm(x, norm_g[l])
        u = h @ w_in[l]
        (a_q, a_i, a_ff, a_fb, a_z,
         b_q, b_k, b_v, b_z,
         c_q, c_k, c_v, c_o, c_z, c_ig, c_fg,
         d_q, d_z) = jnp.split(u, SPLIT_POINTS, axis=-1)
        y_a = _hgrn2_branch(a_q, a_i, a_ff, a_fb, a_z, lb_all[l], hgrn_norm_g[l])
        y_b = _window_attention(b_q, b_k, b_v, b_z, attn_sink[l], cos, sin)
        y_c = _mlstm_branch(c_q, c_k, c_v, c_o, c_z, c_ig, c_fg, mlstm_gate_b[l], mlstm_conv_w[l], mlstm_norm_g[l])
        y_d = _memory_attention(d_q, d_z, _rmsnorm(mem, mem_norm_g[l]), w_mem_kv[l])
        y = jnp.concatenate([y_a, y_b, y_c, y_d], axis=-1)
        x = x + y @ w_out[l]
    return _rmsnorm(x, final_norm_g)
```

```python
import functools

import jax
import jax.numpy as jnp
import numpy as np
from jax import lax
from jax.experimental import pallas as pl
from jax.experimental.pallas import tpu as pltpu

F32 = jnp.float32
BF16 = jnp.bfloat16

D_GROUP = 512
A_HEADS, A_HD = 4, 128
B_HEADS, B_KV_HEADS, B_HD = 8, 2, 64
WINDOW = 128
ROPE_THETA = 500000.0
ROPE_DIM = 16
C_HEADS, C_QK, C_V = 4, 64, 128
CONV_W = 5
D_HEADS, D_HD = 4, 128
EPS = 1e-6
IN_SIZES = (
    D_GROUP, D_GROUP, D_GROUP, D_GROUP, D_GROUP,
    B_HEADS * B_HD, B_KV_HEADS * B_HD, B_KV_HEADS * B_HD, D_GROUP,
    C_HEADS * C_QK, C_HEADS * C_QK, C_HEADS * C_V, D_GROUP, D_GROUP,
    2 * C_HEADS, 2 * C_HEADS,
    D_GROUP, D_GROUP,
)
SPLIT_POINTS = tuple(int(s) for s in np.cumsum(IN_SIZES)[:-1])

LANE = 128
SUBLANE = 8
VMEM_LIMIT = 48 * 1024 * 1024

CHUNK = 128
ROW_TILE = 512
NEG = -1e30

(BLK_A_Q, BLK_A_I, BLK_A_FF, BLK_A_FB, BLK_A_Z, BLK_B_Q, BLK_B_Z, BLK_C_QK, BLK_C_V, BLK_C_O,
 BLK_C_Z, BLK_D_Q, BLK_D_Z, BLK_MISC) = range(14)
N_BLK = 14
MISC_K, MISC_V, MISC_G = 0, 128, 256
N_GATE = 4 * C_HEADS


def _dot(a, b):
    return jnp.dot(a, b, preferred_element_type=F32)


def _dot_nt(a, b):
    return lax.dot_general(a, b, (((1,), (1,)), ((), ())), preferred_element_type=F32)


def _dot_tn(a, b):
    return lax.dot_general(a, b, (((0,), (0,)), ((), ())), preferred_element_type=F32)


def _sigmoid(x):
    return 1.0 / (1.0 + jnp.exp(-x))


def _silu(x):
    return x * _sigmoid(x)


def _log_sigmoid(x):
    return jnp.minimum(x, 0.0) - jnp.log(1.0 + jnp.exp(-jnp.abs(x)))


def _cumsum_rows(x, reverse):
    n = x.shape[0]
    row = lax.broadcasted_iota(jnp.int32, x.shape, 0)
    s = 1
    while s < n:
        if reverse:
            x = x + jnp.where(row < n - s, pltpu.roll(x, n - s, axis=0), 0.0)
        else:
            x = x + jnp.where(row >= s, pltpu.roll(x, s, axis=0), 0.0)
        s *= 2
    return x


def _cumsum_lanes(x, reverse):
    n = x.shape[1]
    col = lax.broadcasted_iota(jnp.int32, x.shape, 1)
    s = 1
    while s < n:
        if reverse:
            x = x + jnp.where(col < n - s, pltpu.roll(x, n - s, axis=1), 0.0)
        else:
            x = x + jnp.where(col >= s, pltpu.roll(x, s, axis=1), 0.0)
        s *= 2
    return x


def _in_proj_kernel(x_ref, g_ref, w_ref, wgt_ref, u_ref, gt_ref, h_scr):
    @pl.when(pl.program_id(1) == 0)
    def _():
        x = x_ref[...]
        ms = jnp.mean(x * x, axis=-1, keepdims=True)
        h = (x * lax.rsqrt(ms + EPS) * g_ref[...]).astype(BF16)
        h_scr[...] = h
        gt_ref[...] = _dot_nt(wgt_ref[...], h)

    u_ref[...] = _dot(h_scr[...], w_ref[...])


def _in_proj(xf, g, w_cat, w_gt):
    rows, d = xf.shape
    tm = ROW_TILE
    return pl.pallas_call(
        _in_proj_kernel,
        out_shape=(jax.ShapeDtypeStruct((rows, N_BLK * D_GROUP), F32),
                   jax.ShapeDtypeStruct((N_GATE, rows), F32)),
        grid=(rows // tm, N_BLK),
        in_specs=[
            pl.BlockSpec((tm, d), lambda i, j: (i, 0)),
            pl.BlockSpec((1, d), lambda i, j: (0, 0)),
            pl.BlockSpec((d, D_GROUP), lambda i, j: (0, j)),
            pl.BlockSpec((N_GATE, d), lambda i, j: (0, 0)),
        ],
        out_specs=(
            pl.BlockSpec((tm, D_GROUP), lambda i, j: (i, j)),
            pl.BlockSpec((N_GATE, tm), lambda i, j: (0, i)),
        ),
        scratch_shapes=[pltpu.VMEM((tm, d), BF16)],
        compiler_params=pltpu.CompilerParams(
            dimension_semantics=("parallel", "arbitrary"), vmem_limit_bytes=VMEM_LIMIT),
        name="in_proj",
    )(xf, g.reshape(1, d), w_cat, w_gt)


def _prep_w_in(w):
    (a_q, a_i, a_ff, a_fb, a_z, b_q, b_k, b_v, b_z,
     c_q, c_k, c_v, c_o, c_z, c_ig, c_fg, d_q, d_z) = jnp.split(w, SPLIT_POINTS, axis=1)
    gates = jnp.concatenate([c_ig, c_fg], axis=1)
    used = b_k.shape[1] + b_v.shape[1] + gates.shape[1]
    pad = jnp.zeros((w.shape[0], D_GROUP - used), w.dtype)
    w_cat = jnp.concatenate(
        [a_q, a_i, a_ff, a_fb, a_z, b_q, b_z, c_q, c_k, c_v, c_o, c_z, d_q, d_z,
         b_k, b_v, gates, pad], axis=1)
    return w_cat.astype(BF16), gates.T.astype(BF16)


def _out_proj_kernel(ya_ref, yb_ref, yc_ref, yd_ref, w_ref, x_ref, g_ref, o_ref, *, final):
    acc = x_ref[...]
    for i, y_ref in enumerate((ya_ref, yb_ref, yc_ref, yd_ref)):
        acc = acc + _dot(y_ref[...], w_ref[i * D_GROUP:(i + 1) * D_GROUP, :])
    if final:
        ms = jnp.mean(acc * acc, axis=-1, keepdims=True)
        acc = acc * lax.rsqrt(ms + EPS) * g_ref[...]
    o_ref[...] = acc


def _out_proj(ya, yb, yc, yd, w, xf, g, final):
    rows, d = xf.shape
    tm = ROW_TILE
    yspec = pl.BlockSpec((tm, D_GROUP), lambda i: (i, 0))
    return pl.pallas_call(
        functools.partial(_out_proj_kernel, final=final),
        out_shape=jax.ShapeDtypeStruct((rows, d), F32),
        grid=(rows // tm,),
        in_specs=[yspec, yspec, yspec, yspec,
                  pl.BlockSpec((4 * D_GROUP, d), lambda i: (0, 0)),
                  pl.BlockSpec((tm, d), lambda i: (i, 0)),
                  pl.BlockSpec((1, d), lambda i: (0, 0))],
        out_specs=pl.BlockSpec((tm, d), lambda i: (i, 0)),
        compiler_params=pltpu.CompilerParams(
            dimension_semantics=("parallel",), vmem_limit_bytes=VMEM_LIMIT),
        name="out_proj",
    )(ya, yb, yc, yd, w, xf, g.reshape(1, d))


def _level_ref(b, h, reverse):
    n_rows, n = b.shape
    blk = 2 * h
    r0 = h if reverse else h - 1
    if blk >= SUBLANE:
        pieces = [jnp.broadcast_to(b[s + r0:s + r0 + 1, :], (blk, n)) for s in range(0, n_rows, blk)]
        return pieces[0] if len(pieces) == 1 else jnp.concatenate(pieces, axis=0)
    res = lax.broadcasted_iota(jnp.int32, b.shape, 0) & (blk - 1)
    out = None
    for m in range(blk):
        shift = (m - r0) % n_rows
        cand = b if shift == 0 else pltpu.roll(b, shift, axis=0)
        out = cand if out is None else jnp.where(res == m, cand, out)
    return out


def _hgrn_lower_bound(lbl_ref, layer, depth):
    rows = [lbl_ref[j:j + 1, :] for j in range(depth)]
    mx = functools.reduce(jnp.maximum, rows)
    es = [jnp.exp(r - mx) for r in rows]
    tot = functools.reduce(lambda a, c: a + c, es)
    lb = jnp.zeros_like(rows[0])
    for j in range(1, layer + 1):
        lb = lb + es[j] / tot
    return lb


def _hgrn_kernel(*refs, layer, depth, reverse, final):
    if final:
        q_ref, v_ref, zg_ref, lbl_ref, z_ref, ob_ref, ng_ref, out_ref, st_ref = refs
    else:
        q_ref, v_ref, zg_ref, lbl_ref, out_ref, st_ref = refs
    n_rows = q_ref.shape[0]

    @pl.when(pl.program_id(1) == 0)
    def _():
        st_ref[...] = jnp.zeros_like(st_ref)

    q = q_ref[...]
    zg = zg_ref[...]
    lb = _hgrn_lower_bound(lbl_ref, layer, depth)
    f = lb + (1.0 - lb) * _sigmoid(zg)
    k = (1.0 - lb) * _sigmoid(-zg)
    b = _cumsum_rows(jnp.log(f), reverse)
    tot = b[0:1, :] if reverse else b[n_rows - 1:n_rows, :]
    v_bf = v_ref[...].astype(BF16)
    q_in = (q * jnp.exp(b)).astype(BF16)
    k_out = (k * jnp.exp(tot - b)).astype(BF16)
    dec = jnp.exp(tot)

    row = lax.broadcasted_iota(jnp.int32, q.shape, 0)
    ti = lax.broadcasted_iota(jnp.int32, (n_rows, n_rows), 0)
    si = lax.broadcasted_iota(jnp.int32, (n_rows, n_rows), 1)
    tx = ti ^ si
    ahead = (ti < si) if reverse else (ti > si)
    operands = [(q.astype(BF16), k.astype(BF16), ti == si)]
    h = n_rows // 2
    while h >= 1:
        e = jnp.exp(-jnp.abs(b - _level_ref(b, h, reverse)))
        is_q = ((row & h) == 0) if reverse else ((row & h) != 0)
        c = (jnp.where(is_q, q, k) * e).astype(BF16)
        operands.append((c, c, jnp.where(ahead, tx & (-h), -1) == h))
        h //= 2

    outs = []
    for hd in range(A_HEADS):
        sl = slice(hd * A_HD, (hd + 1) * A_HD)
        att = jnp.zeros((n_rows, n_rows), F32)
        for lhs, rhs, msk in operands:
            att = jnp.where(msk, _dot_nt(lhs[:, sl], rhs[:, sl]), att)
        st = st_ref[hd]
        o = _dot(att.astype(BF16), v_bf[:, sl]) + _dot_nt(q_in[:, sl], st.astype(BF16))
        st_ref[hd] = dec[:, sl] * st + _dot_tn(v_bf[:, sl], k_out[:, sl])
        outs.append(o)
    o = jnp.concatenate(outs, axis=1)

    if final:
        o = o + ob_ref[...]
        ys = []
        for hd in range(A_HEADS):
            sl = slice(hd * A_HD, (hd + 1) * A_HD)
            oh = o[:, sl]
            ms = jnp.mean(oh * oh, axis=-1, keepdims=True)
            ys.append(oh * lax.rsqrt(ms + EPS) * ng_ref[:, sl])
        y = jnp.concatenate(ys, axis=1) * _silu(z_ref[...])
        out_ref[...] = y.astype(out_ref.dtype)
    else:
        out_ref[...] = o


def _chunk_index(c, n_chunks, reverse):
    return n_chunks - 1 - c if reverse else c


def _hgrn(u, lb_logits, layer, batch, *, reverse, ob=None, norm_g=None):
    rows = u.shape[0]
    n_chunks = rows // batch // CHUNK
    depth = lb_logits.shape[0]
    final = ob is not None

    def ublk(blk):
        return pl.BlockSpec(
            (CHUNK, D_GROUP),
            lambda bi, c: (bi * n_chunks + _chunk_index(c, n_chunks, reverse), blk))

    row_spec = pl.BlockSpec(
        (CHUNK, D_GROUP), lambda bi, c: (bi * n_chunks + _chunk_index(c, n_chunks, reverse), 0))
    in_specs = [ublk(BLK_A_Q), ublk(BLK_A_I), ublk(BLK_A_FB if reverse else BLK_A_FF),
                pl.BlockSpec((depth, D_GROUP), lambda bi, c: (0, 0))]
    args = [u, u, u, lb_logits]
    if final:
        in_specs += [ublk(BLK_A_Z), row_spec, pl.BlockSpec((1, D_GROUP), lambda bi, c: (0, 0))]
        args += [u, ob, norm_g.reshape(1, D_GROUP)]
    return pl.pallas_call(
        functools.partial(_hgrn_kernel, layer=layer, depth=depth, reverse=reverse, final=final),
        out_shape=jax.ShapeDtypeStruct((rows, D_GROUP), BF16 if final else F32),
        grid=(batch, n_chunks),
        in_specs=in_specs,
        out_specs=row_spec,
        scratch_shapes=[pltpu.VMEM((A_HEADS, A_HD, A_HD), F32)],
        compiler_params=pltpu.CompilerParams(
            dimension_semantics=("parallel", "arbitrary"), vmem_limit_bytes=VMEM_LIMIT),
        name="hgrn_fwd" if final else "hgrn_bwd",
    )(*args)


def _mlstm_kernel(*refs, reverse, final):
    if final:
        (qkp_ref, qkc_ref, qkn_ref, cw_ref, v_ref, misc_ref, gt_ref, gbr_ref, gbc_ref,
         og_ref, z_ref, hb_ref, ng_ref, out_ref, cst_ref, m_ref) = refs
    else:
        (qkp_ref, qkc_ref, qkn_ref, cw_ref, v_ref, misc_ref, gt_ref, gbr_ref, gbc_ref,
         out_ref, cst_ref, m_ref) = refs
    n_rows = qkc_ref.shape[0]
    c = pl.program_id(1)
    n_chunks = pl.num_programs(1)
    cc = _chunk_index(c, n_chunks, reverse)

    @pl.when(c == 0)
    def _():
        cst_ref[...] = jnp.zeros_like(cst_ref)
        m_ref[...] = jnp.zeros_like(m_ref)

    halo = SUBLANE
    prev = qkp_ref[n_rows - halo:n_rows, :] * (cc > 0).astype(F32)
    nxt = qkn_ref[0:halo, :] * (cc < n_chunks - 1).astype(F32)
    xcat = jnp.concatenate([prev, qkc_ref[...], nxt], axis=0)
    acc = None
    for j in range(CONV_W):
        off = halo + j - CONV_W // 2
        term = cw_ref[j:j + 1, :] * xcat[off:off + n_rows, :]
        acc = term if acc is None else acc + term
    qk = _silu(acc)
    n_qk = C_HEADS * C_QK
    q_all = qk[:, :n_qk] * (C_QK ** -0.5)
    k_all = qk[:, n_qk:]

    gc = misc_ref[:, MISC_G:MISC_G + LANE] + gbr_ref[...]
    b_c = _cumsum_rows(_log_sigmoid(gc), reverse)
    gr = gt_ref[...] + gbc_ref[...]
    li_r = gr[0:2 * C_HEADS, :]
    b_r = _cumsum_lanes(_log_sigmoid(gr[2 * C_HEADS:, :]), reverse)

    ti = lax.broadcasted_iota(jnp.int32, (n_rows, n_rows), 0)
    si = lax.broadcasted_iota(jnp.int32, (n_rows, n_rows), 1)
    causal = (si >= ti) if reverse else (si <= ti)
    v_all = v_ref[...]
    ones = jnp.ones((n_rows, C_V), BF16)
    edge = 0 if reverse else n_rows - 1

    outs = []
    for hd in range(C_HEADS):
        gi = (C_HEADS if reverse else 0) + hd
        bcol = b_c[:, 2 * C_HEADS + gi:2 * C_HEADS + gi + 1]
        icol = gc[:, gi:gi + 1]
        brow = b_r[gi:gi + 1, :]
        irow = li_r[gi:gi + 1, :]
        m_prev = m_ref[hd:hd + 1, 0:1]

        d = jnp.where(causal, (bcol - brow) + irow, NEG)
        d_inter = bcol + m_prev
        m_t = jnp.maximum(jnp.max(d, axis=-1, keepdims=True), d_inter)
        qh = q_all[:, hd * C_QK:(hd + 1) * C_QK].astype(BF16)
        kh = k_all[:, hd * C_QK:(hd + 1) * C_QK]
        w = _dot_nt(qh, kh.astype(BF16)) * jnp.exp(d - m_t)
        g = jnp.exp(d_inter - m_t)
        v_ext = jnp.concatenate([v_all[:, hd * C_V:(hd + 1) * C_V].astype(BF16), ones], axis=1)
        cst = cst_ref[hd]
        num = _dot(w.astype(BF16), v_ext) + g * _dot(qh, cst.astype(BF16))
        den = jnp.maximum(jnp.abs(num[:, C_V:]), jnp.exp(-m_t))
        outs.append(num[:, :C_V] / den)

        b_last = bcol[edge:edge + 1, :]
        a = (b_last - bcol) + icol
        m_new = jnp.maximum(b_last + m_prev, jnp.max(a, axis=0, keepdims=True))
        ws = jnp.exp(a - m_new)
        decay = jnp.exp(b_last + m_prev - m_new)
        cst_ref[hd] = decay * cst + _dot_tn((ws * kh).astype(BF16), v_ext)
        m_ref[hd:hd + 1, :] = jnp.broadcast_to(m_new, (1, LANE))
    hcat = jnp.concatenate(outs, axis=1)

    if final:
        hsum = (hcat + hb_ref[...]) * _sigmoid(og_ref[...])
        ys = []
        for hd in range(C_HEADS):
            sl = slice(hd * C_V, (hd + 1) * C_V)
            hh = hsum[:, sl]
            mu = jnp.mean(hh, axis=-1, keepdims=True)
            var = jnp.mean(jnp.square(hh - mu), axis=-1, keepdims=True)
            ys.append((hh - mu) * lax.rsqrt(var + EPS) * ng_ref[:, sl])
        y = jnp.concatenate(ys, axis=1) * _silu(z_ref[...])
        out_ref[...] = y.astype(out_ref.dtype)
    else:
        out_ref[...] = hcat


def _mlstm(u, gt, conv_w, gate_b, batch, *, reverse, hb=None, norm_g=None):
    rows = u.shape[0]
    n_chunks = rows // batch // CHUNK
    final = hb is not None

    def chunk_row(bi, c, delta=0):
        cc = _chunk_index(c, n_chunks, reverse) + delta
        cc = jnp.clip(cc, 0, n_chunks - 1)
        return bi * n_chunks + cc

    def ublk(blk, delta=0):
        return pl.BlockSpec((CHUNK, D_GROUP), lambda bi, c: (chunk_row(bi, c, delta), blk))

    row_spec = pl.BlockSpec((CHUNK, D_GROUP), lambda bi, c: (chunk_row(bi, c), 0))
    cw = jnp.zeros((SUBLANE, D_GROUP), F32).at[:CONV_W].set(conv_w)
    gb_row = jnp.zeros((1, LANE), F32).at[0, :N_GATE].set(gate_b)
    gb_col = gate_b.reshape(N_GATE, 1)
    in_specs = [ublk(BLK_C_QK, -1), ublk(BLK_C_QK), ublk(BLK_C_QK, 1),
                pl.BlockSpec((SUBLANE, D_GROUP), lambda bi, c: (0, 0)),
                ublk(BLK_C_V), ublk(BLK_MISC),
                pl.BlockSpec((N_GATE, CHUNK), lambda bi, c: (0, chunk_row(bi, c))),
                pl.BlockSpec((1, LANE), lambda bi, c: (0, 0)),
                pl.BlockSpec((N_GATE, 1), lambda bi, c: (0, 0))]
    args = [u, u, u, cw, u, u, gt, gb_row, gb_col]
    if final:
        in_specs += [ublk(BLK_C_O), ublk(BLK_C_Z), row_spec,
                     pl.BlockSpec((1, D_GROUP), lambda bi, c: (0, 0))]
        args += [u, u, hb, norm_g.reshape(1, D_GROUP)]
    return pl.pallas_call(
        functools.partial(_mlstm_kernel, reverse=reverse, final=final),
        out_shape=jax.ShapeDtypeStruct((rows, D_GROUP), BF16 if final else F32),
        grid=(batch, n_chunks),
        in_specs=in_specs,
        out_specs=row_spec,
        scratch_shapes=[pltpu.VMEM((C_HEADS, C_QK, 2 * C_V), F32),
                        pltpu.VMEM((SUBLANE, LANE), F32)],
        compiler_params=pltpu.CompilerParams(
            dimension_semantics=("parallel", "arbitrary"), vmem_limit_bytes=VMEM_LIMIT),
        name="mlstm_fwd" if final else "mlstm_bwd",
    )(*args)


def _rope_table_kernel(pos_ref, invf_ref, cos_ref, sin_ref):
    ang = pos_ref[...].astype(F32) * invf_ref[...]
    r = lax.broadcasted_iota(jnp.int32, ang.shape, 1) & (B_HD - 1)
    half = ROPE_DIM // 2
    sn = jnp.sin(ang)
    cos_ref[...] = jnp.where(r < ROPE_DIM, jnp.cos(ang), 1.0)
    sin_ref[...] = jnp.where(r < half, -sn, jnp.where(r < ROPE_DIM, sn, 0.0))


def _rope_tables(positions):
    rows = positions.size
    half = ROPE_DIM // 2
    inv_freq = ROPE_THETA ** (-jnp.arange(0, ROPE_DIM, 2, dtype=F32) / ROPE_DIM)
    lane = np.arange(LANE) % B_HD
    invf = jnp.where(lane < ROPE_DIM, inv_freq[lane % half], 0.0).astype(F32).reshape(1, LANE)
    tm = ROW_TILE
    return pl.pallas_call(
        _rope_table_kernel,
        out_shape=(jax.ShapeDtypeStruct((rows, LANE), F32),) * 2,
        grid=(rows // tm,),
        in_specs=[pl.BlockSpec((tm, 1), lambda i: (i, 0)),
                  pl.BlockSpec((1, LANE), lambda i: (0, 0))],
        out_specs=(pl.BlockSpec((tm, LANE), lambda i: (i, 0)),) * 2,
        compiler_params=pltpu.CompilerParams(dimension_semantics=("parallel",)),
        name="rope_tables",
    )(positions.reshape(rows, 1), invf)


def _rope(x, cos, sin):
    width = x.shape[1]
    if width > LANE:
        cos = jnp.concatenate([cos] * (width // LANE), axis=1)
        sin = jnp.concatenate([sin] * (width // LANE), axis=1)
    half = ROPE_DIM // 2
    r = lax.broadcasted_iota(jnp.int32, x.shape, 1) & (B_HD - 1)
    partner = jnp.where(r < half, pltpu.roll(x, width - half, axis=1), pltpu.roll(x, half, axis=1))
    return x * cos + partner * sin


def _swa_kernel(q_ref, z_ref, mp_ref, mc_ref, mn_ref, cp_ref, cc_ref, cn_ref,
                sp_ref, sc_ref, sn_ref, sink_ref, out_ref):
    blk = q_ref.shape[0]
    i = pl.program_id(1)
    n_blocks = pl.num_programs(1)
    kv_w = B_KV_HEADS * B_HD
    grp = B_HEADS // B_KV_HEADS

    q = _rope(q_ref[...], cc_ref[...], sc_ref[...])
    k = jnp.concatenate([
        _rope(mp_ref[:, MISC_K:MISC_K + kv_w], cp_ref[...], sp_ref[...]),
        _rope(mc_ref[:, MISC_K:MISC_K + kv_w], cc_ref[...], sc_ref[...]),
        _rope(mn_ref[:, MISC_K:MISC_K + kv_w], cn_ref[...], sn_ref[...])], axis=0).astype(BF16)
    v = jnp.concatenate([mp_ref[:, MISC_V:MISC_V + kv_w], mc_ref[:, MISC_V:MISC_V + kv_w],
                         mn_ref[:, MISC_V:MISC_V + kv_w]], axis=0).astype(BF16)

    qi = lax.broadcasted_iota(jnp.int32, (blk, 3 * blk), 0)
    kc = lax.broadcasted_iota(jnp.int32, (blk, 3 * blk), 1)
    lo = jnp.where(i > 0, 0, blk)
    hi = jnp.where(i < n_blocks - 1, 3 * blk, 2 * blk)
    valid = (jnp.abs(qi - (kc - blk)) <= WINDOW) & (kc >= lo) & (kc < hi)
    valid = jnp.concatenate([valid.astype(jnp.int32)] * grp, axis=0) > 0

    heads = [None] * B_HEADS
    for kvh in range(B_KV_HEADS):
        ksl = slice(kvh * B_HD, (kvh + 1) * B_HD)
        hs = [kvh * grp + j for j in range(grp)]
        qg = jnp.concatenate([q[:, hq * B_HD:(hq + 1) * B_HD] for hq in hs], axis=0).astype(BF16)
        s = _dot_nt(qg, k[:, ksl]) * (B_HD ** -0.5)
        s = jnp.where(valid, s, NEG)
        sk = jnp.concatenate(
            [jnp.broadcast_to(sink_ref[hq:hq + 1, 0:1], (blk, 1)) for hq in hs], axis=0)
        m = jnp.maximum(jnp.max(s, axis=-1, keepdims=True), sk)
        p = jnp.exp(s - m)
        denom = jnp.sum(p, axis=-1, keepdims=True) + jnp.exp(sk - m)
        o = _dot((p / denom).astype(BF16), v[:, ksl])
        for j, hq in enumerate(hs):
            heads[hq] = o[j * blk:(j + 1) * blk, :]
    y = jnp.concatenate(heads, axis=1) * _silu(z_ref[...])
    out_ref[...] = y.astype(out_ref.dtype)


def _swa(u, cos_t, sin_t, sink, batch):
    rows = u.shape[0]
    n_blocks = rows // batch // CHUNK

    def blk_row(bi, i, delta=0):
        return bi * n_blocks + jnp.clip(i + delta, 0, n_blocks - 1)

    def ublk(blk, delta=0):
        return pl.BlockSpec((CHUNK, D_GROUP), lambda bi, i: (blk_row(bi, i, delta), blk))

    def tspec(delta):
        return pl.BlockSpec((CHUNK, LANE), lambda bi, i: (blk_row(bi, i, delta), 0))

    sink_t = jnp.broadcast_to(sink.astype(F32).reshape(B_HEADS, 1), (B_HEADS, LANE))
    return pl.pallas_call(
        _swa_kernel,
        out_shape=jax.ShapeDtypeStruct((rows, D_GROUP), BF16),
        grid=(batch, n_blocks),
        in_specs=[ublk(BLK_B_Q), ublk(BLK_B_Z), ublk(BLK_MISC, -1), ublk(BLK_MISC), ublk(BLK_MISC, 1),
                  tspec(-1), tspec(0), tspec(1), tspec(-1), tspec(0), tspec(1),
                  pl.BlockSpec((B_HEADS, LANE), lambda bi, i: (0, 0))],
        out_specs=pl.BlockSpec((CHUNK, D_GROUP), lambda bi, i: (blk_row(bi, i), 0)),
        compiler_params=pltpu.CompilerParams(
            dimension_semantics=("parallel", "parallel"), vmem_limit_bytes=VMEM_LIMIT),
        name="window_attn",
    )(u, u, u, u, u, cos_t, cos_t, cos_t, sin_t, sin_t, sin_t, sink_t)


def _mem_kv_kernel(mem_ref, g_ref, w_ref, kv_ref):
    x = mem_ref[...]
    ms = jnp.mean(x * x, axis=-1, keepdims=True)
    h = (x * lax.rsqrt(ms + EPS) * g_ref[...]).astype(BF16)
    kv_ref[...] = _dot(h, w_ref[...]).astype(kv_ref.dtype)


def _mem_kv(mem, g, w):
    batch, m_len, d = mem.shape
    n_out = w.shape[1]
    return pl.pallas_call(
        _mem_kv_kernel,
        out_shape=jax.ShapeDtypeStruct((batch * m_len, n_out), BF16),
        grid=(batch,),
        in_specs=[pl.BlockSpec((m_len, d), lambda bi: (bi, 0)),
                  pl.BlockSpec((1, d), lambda bi: (0, 0)),
                  pl.BlockSpec((d, n_out), lambda bi: (0, 0))],
        out_specs=pl.BlockSpec((m_len, n_out), lambda bi: (bi, 0)),
        compiler_params=pltpu.CompilerParams(
            dimension_semantics=("parallel",), vmem_limit_bytes=VMEM_LIMIT),
        name="mem_kv",
    )(mem.reshape(batch * m_len, d), g.reshape(1, d), w)


def _mem_attn_kernel(q_ref, z_ref, kv_ref, out_ref):
    q = q_ref[...].astype(BF16)
    outs = []
    for hd in range(D_HEADS):
        sl = slice(hd * D_HD, (hd + 1) * D_HD)
        k = kv_ref[:, hd * D_HD:(hd + 1) * D_HD]
        v = kv_ref[:, D_GROUP + hd * D_HD:D_GROUP + (hd + 1) * D_HD]
        s = _dot_nt(q[:, sl], k) * (D_HD ** -0.5)
        m = jnp.max(s, axis=-1, keepdims=True)
        p = jnp.exp(s - m)
        p = p / jnp.sum(p, axis=-1, keepdims=True)
        outs.append(_dot(p.astype(BF16), v))
    y = jnp.concatenate(outs, axis=1) * _silu(z_ref[...])
    out_ref[...] = y.astype(out_ref.dtype)


def _mem_attn(u, kv, batch):
    rows = u.shape[0]
    tm = ROW_TILE
    per_batch = rows // batch // tm
    m_len = kv.shape[0] // batch
    return pl.pallas_call(
        _mem_attn_kernel,
        out_shape=jax.ShapeDtypeStruct((rows, D_GROUP), BF16),
        grid=(batch, per_batch),
        in_specs=[pl.BlockSpec((tm, D_GROUP), lambda bi, i: (bi * per_batch + i, BLK_D_Q)),
                  pl.BlockSpec((tm, D_GROUP), lambda bi, i: (bi * per_batch + i, BLK_D_Z)),
                  pl.BlockSpec((m_len, kv.shape[1]), lambda bi, i: (bi, 0))],
        out_specs=pl.BlockSpec((tm, D_GROUP), lambda bi, i: (bi * per_batch + i, 0)),
        compiler_params=pltpu.CompilerParams(
            dimension_semantics=("parallel", "parallel"), vmem_limit_bytes=VMEM_LIMIT),
        name="mem_attn",
    )(u, u, kv)


def kernel(x, mem, positions, norm_g, w_in, hgrn_lb_logits, hgrn_norm_g, attn_sink, mlstm_conv_w,
           mlstm_gate_b, mlstm_norm_g, mem_norm_g, w_mem_kv, w_out, final_norm_g):
    batch, seq, d = x.shape
    depth = w_in.shape[0]
    assert seq % CHUNK == 0 and (batch * seq) % ROW_TILE == 0 and seq % ROW_TILE == 0
    xf = x.reshape(batch * seq, d)
    cos_t, sin_t = _rope_tables(positions)
    lbl = hgrn_lb_logits.astype(F32)
    for layer in range(depth):
        w_cat, w_gt = _prep_w_in(w_in[layer])
        u, gt = _in_proj(xf, norm_g[layer], w_cat, w_gt)
        ob = _hgrn(u, lbl, layer, batch, reverse=True)
        ya = _hgrn(u, lbl, layer, batch, reverse=False, ob=ob, norm_g=hgrn_norm_g[layer])
        yb = _swa(u, cos_t, sin_t, attn_sink[layer], batch)
        hb = _mlstm(u, gt, mlstm_conv_w[layer], mlstm_gate_b[layer], batch, reverse=True)
        yc = _mlstm(u, gt, mlstm_conv_w[layer], mlstm_gate_b[layer], batch, reverse=False,
                    hb=hb, norm_g=mlstm_norm_g[layer])
        kv = _mem_kv(mem, mem_norm_g[layer], w_mem_kv[layer].astype(BF16))
        yd = _mem_attn(u, kv, batch)
        xf = _out_proj(ya, yb, yc, yd, w_out[layer].astype(BF16), xf, final_norm_g,
                       final=(layer == depth - 1))
    return xf.reshape(batch, seq, d)
```

```python
import functools

import jax
import jax.numpy as jnp
import numpy as np
from jax import lax
from jax.experimental import pallas as pl
from jax.experimental.pallas import tpu as pltpu

F32 = jnp.float32
BF16 = jnp.bfloat16

D_GROUP = 512
A_HEADS, A_HD = 4, 128
B_HEADS, B_KV_HEADS, B_HD = 8, 2, 64
WINDOW = 128
ROPE_THETA = 500000.0
ROPE_DIM = 16
C_HEADS, C_QK, C_V = 4, 64, 128
CONV_W = 5
D_HEADS, D_HD = 4, 128
EPS = 1e-6
IN_SIZES = (
    D_GROUP, D_GROUP, D_GROUP, D_GROUP, D_GROUP,
    B_HEADS * B_HD, B_KV_HEADS * B_HD, B_KV_HEADS * B_HD, D_GROUP,
    C_HEADS * C_QK, C_HEADS * C_QK, C_HEADS * C_V, D_GROUP, D_GROUP,
    2 * C_HEADS, 2 * C_HEADS,
    D_GROUP, D_GROUP,
)
SPLIT_POINTS = tuple(int(s) for s in np.cumsum(IN_SIZES)[:-1])

LANE = 128
SUBLANE = 8
VMEM_LIMIT = 48 * 1024 * 1024

CHUNK = 128
ROW_TILE = 512
IN_ROW_TILE = 256
NEG = -1e30

(BLK_A_Q, BLK_A_I, BLK_A_FF, BLK_A_FB, BLK_A_Z, BLK_B_Q, BLK_B_Z, BLK_C_QK, BLK_C_V, BLK_C_O,
 BLK_C_Z, BLK_D_Q, BLK_D_Z, BLK_MISC) = range(14)
N_BLK = 14
MISC_K, MISC_V, MISC_G = 0, 128, 256
N_GATE = 4 * C_HEADS


def _dot(a, b):
    return jnp.dot(a, b, preferred_element_type=F32)


def _dot_nt(a, b):
    return lax.dot_general(a, b, (((1,), (1,)), ((), ())), preferred_element_type=F32)


def _dot_tn(a, b):
    return lax.dot_general(a, b, (((0,), (0,)), ((), ())), preferred_element_type=F32)


def _sigmoid(x):
    return 1.0 / (1.0 + jnp.exp(-x))


def _silu(x):
    return x * _sigmoid(x)


def _log_sigmoid(x):
    return jnp.minimum(x, 0.0) - jnp.log(1.0 + jnp.exp(-jnp.abs(x)))


def _cumsum_rows(x, reverse):
    n = x.shape[0]
    row = lax.broadcasted_iota(jnp.int32, x.shape, 0)
    s = 1
    while s < n:
        if reverse:
            x = x + jnp.where(row < n - s, pltpu.roll(x, n - s, axis=0), 0.0)
        else:
            x = x + jnp.where(row >= s, pltpu.roll(x, s, axis=0), 0.0)
        s *= 2
    return x


def _cumsum_lanes(x, reverse):
    n = x.shape[1]
    col = lax.broadcasted_iota(jnp.int32, x.shape, 1)
    s = 1
    while s < n:
        if reverse:
            x = x + jnp.where(col < n - s, pltpu.roll(x, n - s, axis=1), 0.0)
        else:
            x = x + jnp.where(col >= s, pltpu.roll(x, s, axis=1), 0.0)
        s *= 2
    return x


def _in_proj_kernel(x_ref, g_ref, w_ref, wgt_ref, u_ref, gt_ref):
    x = x_ref[...]
    ms = jnp.mean(x * x, axis=-1, keepdims=True)
    h = (x * lax.rsqrt(ms + EPS) * g_ref[...]).astype(BF16)
    gt_ref[...] = _dot_nt(wgt_ref[...], h)
    for j in range(N_BLK):
        sl = slice(j * D_GROUP, (j + 1) * D_GROUP)
        u_ref[:, sl] = _dot(h, w_ref[:, sl])


def _in_proj(xf, g, w_cat, w_gt):
    rows, d = xf.shape
    tm = IN_ROW_TILE
    n_out = N_BLK * D_GROUP
    return pl.pallas_call(
        _in_proj_kernel,
        out_shape=(jax.ShapeDtypeStruct((rows, n_out), F32),
                   jax.ShapeDtypeStruct((N_GATE, rows), F32)),
        grid=(rows // tm,),
        in_specs=[
            pl.BlockSpec((tm, d), lambda i: (i, 0)),
            pl.BlockSpec((1, d), lambda i: (0, 0)),
            pl.BlockSpec((d, n_out), lambda i: (0, 0), pipeline_mode=pl.Buffered(1)),
            pl.BlockSpec((N_GATE, d), lambda i: (0, 0)),
        ],
        out_specs=(
            pl.BlockSpec((tm, n_out), lambda i: (i, 0)),
            pl.BlockSpec((N_GATE, tm), lambda i: (0, i)),
        ),
        compiler_params=pltpu.CompilerParams(
            dimension_semantics=("parallel",), vmem_limit_bytes=VMEM_LIMIT),
        name="in_proj",
    )(xf, g.reshape(1, d), w_cat, w_gt)


def _prep_w_in(w):
    (a_q, a_i, a_ff, a_fb, a_z, b_q, b_k, b_v, b_z,
     c_q, c_k, c_v, c_o, c_z, c_ig, c_fg, d_q, d_z) = jnp.split(w, SPLIT_POINTS, axis=1)
    gates = jnp.concatenate([c_ig, c_fg], axis=1)
    used = b_k.shape[1] + b_v.shape[1] + gates.shape[1]
    pad = jnp.zeros((w.shape[0], D_GROUP - used), w.dtype)
    w_cat = jnp.concatenate(
        [a_q, a_i, a_ff, a_fb, a_z, b_q, b_z, c_q, c_k, c_v, c_o, c_z, d_q, d_z,
         b_k, b_v, gates, pad], axis=1)
    return w_cat.astype(BF16), gates.T.astype(BF16)


def _out_proj_kernel(ya_ref, yb_ref, yc_ref, yd_ref, w_ref, x_ref, g_ref, o_ref, *, final):
    acc = x_ref[...]
    for i, y_ref in enumerate((ya_ref, yb_ref, yc_ref, yd_ref)):
        acc = acc + _dot(y_ref[...], w_ref[i * D_GROUP:(i + 1) * D_GROUP, :])
    if final:
        ms = jnp.mean(acc * acc, axis=-1, keepdims=True)
        acc = acc * lax.rsqrt(ms + EPS) * g_ref[...]
    o_ref[...] = acc


def _out_proj(ya, yb, yc, yd, w, xf, g, final):
    rows, d = xf.shape
    tm = ROW_TILE
    yspec = pl.BlockSpec((tm, D_GROUP), lambda i: (i, 0))
    return pl.pallas_call(
        functools.partial(_out_proj_kernel, final=final),
        out_shape=jax.ShapeDtypeStruct((rows, d), F32),
        grid=(rows // tm,),
        in_specs=[yspec, yspec, yspec, yspec,
                  pl.BlockSpec((4 * D_GROUP, d), lambda i: (0, 0)),
                  pl.BlockSpec((tm, d), lambda i: (i, 0)),
                  pl.BlockSpec((1, d), lambda i: (0, 0))],
        out_specs=pl.BlockSpec((tm, d), lambda i: (i, 0)),
        compiler_params=pltpu.CompilerParams(
            dimension_semantics=("parallel",), vmem_limit_bytes=VMEM_LIMIT),
        name="out_proj",
    )(ya, yb, yc, yd, w, xf, g.reshape(1, d))


def _level_ref(b, h, reverse):
    n_rows, n = b.shape
    blk = 2 * h
    r0 = h if reverse else h - 1
    if blk >= SUBLANE:
        pieces = [jnp.broadcast_to(b[s + r0:s + r0 + 1, :], (blk, n)) for s in range(0, n_rows, blk)]
        return pieces[0] if len(pieces) == 1 else jnp.concatenate(pieces, axis=0)
    res = lax.broadcasted_iota(jnp.int32, b.shape, 0) & (blk - 1)
    out = None
    for m in range(blk):
        shift = (m - r0) % n_rows
        cand = b if shift == 0 else pltpu.roll(b, shift, axis=0)
        out = cand if out is None else jnp.where(res == m, cand, out)
    return out


def _hgrn_lower_bound(lbl_ref, layer, depth):
    rows = [lbl_ref[j:j + 1, :] for j in range(depth)]
    mx = functools.reduce(jnp.maximum, rows)
    es = [jnp.exp(r - mx) for r in rows]
    tot = functools.reduce(lambda a, c: a + c, es)
    lb = jnp.zeros_like(rows[0])
    for j in range(1, layer + 1):
        lb = lb + es[j] / tot
    return lb


def _hgrn_kernel(*refs, layer, depth, reverse, final):
    if final:
        q_ref, v_ref, zg_ref, lbl_ref, z_ref, ob_ref, ng_ref, out_ref, st_ref = refs
    else:
        q_ref, v_ref, zg_ref, lbl_ref, out_ref, st_ref = refs
    n_rows = q_ref.shape[0]

    @pl.when(pl.program_id(1) == 0)
    def _():
        st_ref[...] = jnp.zeros_like(st_ref)

    q = q_ref[...]
    zg = zg_ref[...]
    lb = _hgrn_lower_bound(lbl_ref, layer, depth)
    f = lb + (1.0 - lb) * _sigmoid(zg)
    k = (1.0 - lb) * _sigmoid(-zg)
    b = _cumsum_rows(jnp.log(f), reverse)
    tot = b[0:1, :] if reverse else b[n_rows - 1:n_rows, :]
    v_bf = v_ref[...].astype(BF16)
    q_in = (q * jnp.exp(b)).astype(BF16)
    k_out = (k * jnp.exp(tot - b)).astype(BF16)
    dec = jnp.exp(tot)

    row = lax.broadcasted_iota(jnp.int32, q.shape, 0)
    ti = lax.broadcasted_iota(jnp.int32, (n_rows, n_rows), 0)
    si = lax.broadcasted_iota(jnp.int32, (n_rows, n_rows), 1)
    tx = ti ^ si
    ahead = (ti < si) if reverse else (ti > si)
    operands = [(q.astype(BF16), k.astype(BF16), ti == si)]
    h = n_rows // 2
    while h >= 1:
        e = jnp.exp(-jnp.abs(b - _level_ref(b, h, reverse)))
        is_q = ((row & h) == 0) if reverse else ((row & h) != 0)
        c = (jnp.where(is_q, q, k) * e).astype(BF16)
        operands.append((c, c, jnp.where(ahead, tx & (-h), -1) == h))
        h //= 2

    outs = []
    for hd in range(A_HEADS):
        sl = slice(hd * A_HD, (hd + 1) * A_HD)
        att = jnp.zeros((n_rows, n_rows), F32)
        for lhs, rhs, msk in operands:
            att = jnp.where(msk, _dot_nt(lhs[:, sl], rhs[:, sl]), att)
        st = st_ref[hd]
        o = _dot(att.astype(BF16), v_bf[:, sl]) + _dot_nt(q_in[:, sl], st.astype(BF16))
        st_ref[hd] = dec[:, sl] * st + _dot_tn(v_bf[:, sl], k_out[:, sl])
        outs.append(o)
    o = jnp.concatenate(outs, axis=1)

    if final:
        o = o + ob_ref[...]
        ys = []
        for hd in range(A_HEADS):
            sl = slice(hd * A_HD, (hd + 1) * A_HD)
            oh = o[:, sl]
            ms = jnp.mean(oh * oh, axis=-1, keepdims=True)
            ys.append(oh * lax.rsqrt(ms + EPS) * ng_ref[:, sl])
        y = jnp.concatenate(ys, axis=1) * _silu(z_ref[...])
        out_ref[...] = y.astype(out_ref.dtype)
    else:
        out_ref[...] = o


def _chunk_index(c, n_chunks, reverse):
    return n_chunks - 1 - c if reverse else c


def _hgrn(u, lb_logits, layer, batch, *, reverse, ob=None, norm_g=None):
    rows = u.shape[0]
    n_chunks = rows // batch // CHUNK
    depth = lb_logits.shape[0]
    final = ob is not None

    def ublk(blk):
        return pl.BlockSpec(
            (CHUNK, D_GROUP),
            lambda bi, c: (bi * n_chunks + _chunk_index(c, n_chunks, reverse), blk))

    row_spec = pl.BlockSpec(
        (CHUNK, D_GROUP), lambda bi, c: (bi * n_chunks + _chunk_index(c, n_chunks, reverse), 0))
    in_specs = [ublk(BLK_A_Q), ublk(BLK_A_I), ublk(BLK_A_FB if reverse else BLK_A_FF),
                pl.BlockSpec((depth, D_GROUP), lambda bi, c: (0, 0))]
    args = [u, u, u, lb_logits]
    if final:
        in_specs += [ublk(BLK_A_Z), row_spec, pl.BlockSpec((1, D_GROUP), lambda bi, c: (0, 0))]
        args += [u, ob, norm_g.reshape(1, D_GROUP)]
    return pl.pallas_call(
        functools.partial(_hgrn_kernel, layer=layer, depth=depth, reverse=reverse, final=final),
        out_shape=jax.ShapeDtypeStruct((rows, D_GROUP), BF16 if final else F32),
        grid=(batch, n_chunks),
        in_specs=in_specs,
        out_specs=row_spec,
        scratch_shapes=[pltpu.VMEM((A_HEADS, A_HD, A_HD), F32)],
        compiler_params=pltpu.CompilerParams(
            dimension_semantics=("parallel", "arbitrary"), vmem_limit_bytes=VMEM_LIMIT),
        name="hgrn_fwd" if final else "hgrn_bwd",
    )(*args)


def _mlstm_kernel(*refs, reverse, final):
    if final:
        (qkp_ref, qkc_ref, qkn_ref, cw_ref, v_ref, misc_ref, gt_ref, gbr_ref, gbc_ref,
         og_ref, z_ref, hb_ref, ng_ref, out_ref, cst_ref, m_ref) = refs
    else:
        (qkp_ref, qkc_ref, qkn_ref, cw_ref, v_ref, misc_ref, gt_ref, gbr_ref, gbc_ref,
         out_ref, cst_ref, m_ref) = refs
    n_rows = qkc_ref.shape[0]
    c = pl.program_id(1)
    n_chunks = pl.num_programs(1)
    cc = _chunk_index(c, n_chunks, reverse)

    @pl.when(c == 0)
    def _():
        cst_ref[...] = jnp.zeros_like(cst_ref)
        m_ref[...] = jnp.zeros_like(m_ref)

    halo = SUBLANE
    prev = qkp_ref[n_rows - halo:n_rows, :] * (cc > 0).astype(F32)
    nxt = qkn_ref[0:halo, :] * (cc < n_chunks - 1).astype(F32)
    xcat = jnp.concatenate([prev, qkc_ref[...], nxt], axis=0)
    acc = None
    for j in range(CONV_W):
        off = halo + j - CONV_W // 2
        term = cw_ref[j:j + 1, :] * xcat[off:off + n_rows, :]
        acc = term if acc is None else acc + term
    qk = _silu(acc)
    n_qk = C_HEADS * C_QK
    q_all = qk[:, :n_qk] * (C_QK ** -0.5)
    k_all = qk[:, n_qk:]

    gc = misc_ref[:, MISC_G:MISC_G + LANE] + gbr_ref[...]
    b_c = _cumsum_rows(_log_sigmoid(gc), reverse)
    gr = gt_ref[...] + gbc_ref[...]
    li_r = gr[0:2 * C_HEADS, :]
    b_r = _cumsum_lanes(_log_sigmoid(gr[2 * C_HEADS:, :]), reverse)

    ti = lax.broadcasted_iota(jnp.int32, (n_rows, n_rows), 0)
    si = lax.broadcasted_iota(jnp.int32, (n_rows, n_rows), 1)
    causal = (si >= ti) if reverse else (si <= ti)
    v_all = v_ref[...]
    ones = jnp.ones((n_rows, C_V), BF16)
    edge = 0 if reverse else n_rows - 1

    outs = []
    for hd in range(C_HEADS):
        gi = (C_HEADS if reverse else 0) + hd
        bcol = b_c[:, 2 * C_HEADS + gi:2 * C_HEADS + gi + 1]
        icol = gc[:, gi:gi + 1]
        brow = b_r[gi:gi + 1, :]
        irow = li_r[gi:gi + 1, :]
        m_prev = m_ref[hd:hd + 1, 0:1]

        d = jnp.where(causal, (bcol - brow) + irow, NEG)
        d_inter = bcol + m_prev
        m_t = jnp.maximum(jnp.max(d, axis=-1, keepdims=True), d_inter)
        qh = q_all[:, hd * C_QK:(hd + 1) * C_QK].astype(BF16)
        kh = k_all[:, hd * C_QK:(hd + 1) * C_QK]
        w = _dot_nt(qh, kh.astype(BF16)) * jnp.exp(d - m_t)
        g = jnp.exp(d_inter - m_t)
        v_ext = jnp.concatenate([v_all[:, hd * C_V:(hd + 1) * C_V].astype(BF16), ones], axis=1)
        cst = cst_ref[hd]
        num = _dot(w.astype(BF16), v_ext) + g * _dot(qh, cst.astype(BF16))
        den = jnp.maximum(jnp.abs(num[:, C_V:]), jnp.exp(-m_t))
        outs.append(num[:, :C_V] / den)

        b_last = bcol[edge:edge + 1, :]
        a = (b_last - bcol) + icol
        m_new = jnp.maximum(b_last + m_prev, jnp.max(a, axis=0, keepdims=True))
        ws = jnp.exp(a - m_new)
        decay = jnp.exp(b_last + m_prev - m_new)
        cst_ref[hd] = decay * cst + _dot_tn((ws * kh).astype(BF16), v_ext)
        m_ref[hd:hd + 1, :] = jnp.broadcast_to(m_new, (1, LANE))
    hcat = jnp.concatenate(outs, axis=1)

    if final:
        hsum = (hcat + hb_ref[...]) * _sigmoid(og_ref[...])
        ys = []
        for hd in range(C_HEADS):
            sl = slice(hd * C_V, (hd + 1) * C_V)
            hh = hsum[:, sl]
            mu = jnp.mean(hh, axis=-1, keepdims=True)
            var = jnp.mean(jnp.square(hh - mu), axis=-1, keepdims=True)
            ys.append((hh - mu) * lax.rsqrt(var + EPS) * ng_ref[:, sl])
        y = jnp.concatenate(ys, axis=1) * _silu(z_ref[...])
        out_ref[...] = y.astype(out_ref.dtype)
    else:
        out_ref[...] = hcat


def _mlstm(u, gt, conv_w, gate_b, batch, *, reverse, hb=None, norm_g=None):
    rows = u.shape[0]
    n_chunks = rows // batch // CHUNK
    final = hb is not None

    def chunk_row(bi, c, delta=0):
        cc = _chunk_index(c, n_chunks, reverse) + delta
        cc = jnp.clip(cc, 0, n_chunks - 1)
        return bi * n_chunks + cc

    def ublk(blk, delta=0):
        return pl.BlockSpec((CHUNK, D_GROUP), lambda bi, c: (chunk_row(bi, c, delta), blk))

    row_spec = pl.BlockSpec((CHUNK, D_GROUP), lambda bi, c: (chunk_row(bi, c), 0))
    cw = jnp.zeros((SUBLANE, D_GROUP), F32).at[:CONV_W].set(conv_w)
    gb_row = jnp.zeros((1, LANE), F32).at[0, :N_GATE].set(gate_b)
    gb_col = gate_b.reshape(N_GATE, 1)
    in_specs = [ublk(BLK_C_QK, -1), ublk(BLK_C_QK), ublk(BLK_C_QK, 1),
                pl.BlockSpec((SUBLANE, D_GROUP), lambda bi, c: (0, 0)),
                ublk(BLK_C_V), ublk(BLK_MISC),
                pl.BlockSpec((N_GATE, CHUNK), lambda bi, c: (0, chunk_row(bi, c))),
                pl.BlockSpec((1, LANE), lambda bi, c: (0, 0)),
                pl.BlockSpec((N_GATE, 1), lambda bi, c: (0, 0))]
    args = [u, u, u, cw, u, u, gt, gb_row, gb_col]
    if final:
        in_specs += [ublk(BLK_C_O), ublk(BLK_C_Z), row_spec,
                     pl.BlockSpec((1, D_GROUP), lambda bi, c: (0, 0))]
        args += [u, u, hb, norm_g.reshape(1, D_GROUP)]
    return pl.pallas_call(
        functools.partial(_mlstm_kernel, reverse=reverse, final=final),
        out_shape=jax.ShapeDtypeStruct((rows, D_GROUP), BF16 if final else F32),
        grid=(batch, n_chunks),
        in_specs=in_specs,
        out_specs=row_spec,
        scratch_shapes=[pltpu.VMEM((C_HEADS, C_QK, 2 * C_V), F32),
                        pltpu.VMEM((SUBLANE, LANE), F32)],
        compiler_params=pltpu.CompilerParams(
            dimension_semantics=("parallel", "arbitrary"), vmem_limit_bytes=VMEM_LIMIT),
        name="mlstm_fwd" if final else "mlstm_bwd",
    )(*args)


def _rope_table_kernel(pos_ref, invf_ref, cos_ref, sin_ref):
    ang = pos_ref[...].astype(F32) * invf_ref[...]
    r = lax.broadcasted_iota(jnp.int32, ang.shape, 1) & (B_HD - 1)
    half = ROPE_DIM // 2
    sn = jnp.sin(ang)
    cos_ref[...] = jnp.where(r < ROPE_DIM, jnp.cos(ang), 1.0)
    sin_ref[...] = jnp.where(r < half, -sn, jnp.where(r < ROPE_DIM, sn, 0.0))


def _rope_tables(positions):
    rows = positions.size
    half = ROPE_DIM // 2
    inv_freq = ROPE_THETA ** (-jnp.arange(0, ROPE_DIM, 2, dtype=F32) / ROPE_DIM)
    lane = np.arange(LANE) % B_HD
    invf = jnp.where(lane < ROPE_DIM, inv_freq[lane % half], 0.0).astype(F32).reshape(1, LANE)
    tm = ROW_TILE
    return pl.pallas_call(
        _rope_table_kernel,
        out_shape=(jax.ShapeDtypeStruct((rows, LANE), F32),) * 2,
        grid=(rows // tm,),
        in_specs=[pl.BlockSpec((tm, 1), lambda i: (i, 0)),
                  pl.BlockSpec((1, LANE), lambda i: (0, 0))],
        out_specs=(pl.BlockSpec((tm, LANE), lambda i: (i, 0)),) * 2,
        compiler_params=pltpu.CompilerParams(dimension_semantics=("parallel",)),
        name="rope_tables",
    )(positions.reshape(rows, 1), invf)


def _rope(x, cos, sin):
    width = x.shape[1]
    if width > LANE:
        cos = jnp.concatenate([cos] * (width // LANE), axis=1)
        sin = jnp.concatenate([sin] * (width // LANE), axis=1)
    half = ROPE_DIM // 2
    r = lax.broadcasted_iota(jnp.int32, x.shape, 1) & (B_HD - 1)
    partner = jnp.where(r < half, pltpu.roll(x, width - half, axis=1), pltpu.roll(x, half, axis=1))
    return x * cos + partner * sin


def _swa_kernel(q_ref, z_ref, mp_ref, mc_ref, mn_ref, cp_ref, cc_ref, cn_ref,
                sp_ref, sc_ref, sn_ref, sink_ref, out_ref):
    blk = q_ref.shape[0]
    i = pl.program_id(1)
    n_blocks = pl.num_programs(1)
    kv_w = B_KV_HEADS * B_HD
    grp = B_HEADS // B_KV_HEADS

    q = (_rope(q_ref[...], cc_ref[...], sc_ref[...]) * (B_HD ** -0.5)).astype(BF16)
    k = jnp.concatenate([
        _rope(mp_ref[:, MISC_K:MISC_K + kv_w], cp_ref[...], sp_ref[...]),
        _rope(mc_ref[:, MISC_K:MISC_K + kv_w], cc_ref[...], sc_ref[...]),
        _rope(mn_ref[:, MISC_K:MISC_K + kv_w], cn_ref[...], sn_ref[...])], axis=0).astype(BF16)
    v = jnp.concatenate([mp_ref[:, MISC_V:MISC_V + kv_w], mc_ref[:, MISC_V:MISC_V + kv_w],
                         mn_ref[:, MISC_V:MISC_V + kv_w]], axis=0)

    kr = lax.broadcasted_iota(jnp.int32, (3 * blk, blk), 0)
    qc = lax.broadcasted_iota(jnp.int32, (3 * blk, blk), 1)
    lo = jnp.where(i > 0, 0, blk)
    hi = jnp.where(i < n_blocks - 1, 3 * blk, 2 * blk)
    valid = (jnp.abs(qc - (kr - blk)) <= WINDOW) & (kr >= lo) & (kr < hi)
    bias = jnp.where(valid, 0.0, NEG)
    ones = jnp.ones((3 * blk, B_HD), F32)

    tiles = []
    pair = []
    for hq in range(B_HEADS):
        kvh = hq // grp
        ksl = slice(kvh * B_HD, (kvh + 1) * B_HD)
        if hq % grp == 0:
            vt = jnp.concatenate([v[:, ksl], ones], axis=1).T.astype(BF16)
        s = _dot_nt(k[:, ksl], q[:, hq * B_HD:(hq + 1) * B_HD]) + bias
        sk = sink_ref[hq:hq + 1, 0:1]
        m = jnp.maximum(jnp.max(s, axis=0, keepdims=True), sk)
        p = jnp.exp(s - m).astype(BF16)
        o = _dot(vt, p)
        denom = o[B_HD:B_HD + 1, :] + jnp.exp(sk - m)
        pair.append(o[:B_HD, :] / denom)
        if len(pair) == LANE // B_HD:
            tiles.append(jnp.concatenate(pair, axis=0).T)
            pair = []
    y = jnp.concatenate(tiles, axis=1) * _silu(z_ref[...])
    out_ref[...] = y.astype(out_ref.dtype)


def _swa(u, cos_t, sin_t, sink, batch):
    rows = u.shape[0]
    n_blocks = rows // batch // CHUNK

    def blk_row(bi, i, delta=0):
        return bi * n_blocks + jnp.clip(i + delta, 0, n_blocks - 1)

    def ublk(blk, delta=0):
        return pl.BlockSpec((CHUNK, D_GROUP), lambda bi, i: (blk_row(bi, i, delta), blk))

    def tspec(delta):
        return pl.BlockSpec((CHUNK, LANE), lambda bi, i: (blk_row(bi, i, delta), 0))

    sink_t = jnp.broadcast_to(sink.astype(F32).reshape(B_HEADS, 1), (B_HEADS, LANE))
    return pl.pallas_call(
        _swa_kernel,
        out_shape=jax.ShapeDtypeStruct((rows, D_GROUP), BF16),
        grid=(batch, n_blocks),
        in_specs=[ublk(BLK_B_Q), ublk(BLK_B_Z), ublk(BLK_MISC, -1), ublk(BLK_MISC), ublk(BLK_MISC, 1),
                  tspec(-1), tspec(0), tspec(1), tspec(-1), tspec(0), tspec(1),
                  pl.BlockSpec((B_HEADS, LANE), lambda bi, i: (0, 0))],
        out_specs=pl.BlockSpec((CHUNK, D_GROUP), lambda bi, i: (blk_row(bi, i), 0)),
        compiler_params=pltpu.CompilerParams(
            dimension_semantics=("parallel", "parallel"), vmem_limit_bytes=VMEM_LIMIT),
        name="window_attn",
    )(u, u, u, u, u, cos_t, cos_t, cos_t, sin_t, sin_t, sin_t, sink_t)


def _mem_kv_kernel(mem_ref, g_ref, w_ref, kv_ref):
    x = mem_ref[...]
    ms = jnp.mean(x * x, axis=-1, keepdims=True)
    h = (x * lax.rsqrt(ms + EPS) * g_ref[...]).astype(BF16)
    kv_ref[...] = _dot(h, w_ref[...]).astype(kv_ref.dtype)


def _mem_kv(mem, g, w):
    batch, m_len, d = mem.shape
    n_out = w.shape[1]
    return pl.pallas_call(
        _mem_kv_kernel,
        out_shape=jax.ShapeDtypeStruct((batch * m_len, n_out), BF16),
        grid=(batch,),
        in_specs=[pl.BlockSpec((m_len, d), lambda bi: (bi, 0)),
                  pl.BlockSpec((1, d), lambda bi: (0, 0)),
                  pl.BlockSpec((d, n_out), lambda bi: (0, 0))],
        out_specs=pl.BlockSpec((m_len, n_out), lambda bi: (bi, 0)),
        compiler_params=pltpu.CompilerParams(
            dimension_semantics=("parallel",), vmem_limit_bytes=VMEM_LIMIT),
        name="mem_kv",
    )(mem.reshape(batch * m_len, d), g.reshape(1, d), w)


def _mem_attn_kernel(q_ref, z_ref, kv_ref, out_ref):
    q = q_ref[...].astype(BF16)
    outs = []
    for hd in range(D_HEADS):
        sl = slice(hd * D_HD, (hd + 1) * D_HD)
        k = kv_ref[:, hd * D_HD:(hd + 1) * D_HD]
        v = kv_ref[:, D_GROUP + hd * D_HD:D_GROUP + (hd + 1) * D_HD]
        s = _dot_nt(q[:, sl], k) * (D_HD ** -0.5)
        m = jnp.max(s, axis=-1, keepdims=True)
        p = jnp.exp(s - m)
        p = p / jnp.sum(p, axis=-1, keepdims=True)
        outs.append(_dot(p.astype(BF16), v))
    y = jnp.concatenate(outs, axis=1) * _silu(z_ref[...])
    out_ref[...] = y.astype(out_ref.dtype)


def _mem_attn(u, kv, batch):
    rows = u.shape[0]
    tm = ROW_TILE
    per_batch = rows // batch // tm
    m_len = kv.shape[0] // batch
    return pl.pallas_call(
        _mem_attn_kernel,
        out_shape=jax.ShapeDtypeStruct((rows, D_GROUP), BF16),
        grid=(batch, per_batch),
        in_specs=[pl.BlockSpec((tm, D_GROUP), lambda bi, i: (bi * per_batch + i, BLK_D_Q)),
                  pl.BlockSpec((tm, D_GROUP), lambda bi, i: (bi * per_batch + i, BLK_D_Z)),
                  pl.BlockSpec((m_len, kv.shape[1]), lambda bi, i: (bi, 0))],
        out_specs=pl.BlockSpec((tm, D_GROUP), lambda bi, i: (bi * per_batch + i, 0)),
        compiler_params=pltpu.CompilerParams(
            dimension_semantics=("parallel", "parallel"), vmem_limit_bytes=VMEM_LIMIT),
        name="mem_attn",
    )(u, u, kv)


def kernel(x, mem, positions, norm_g, w_in, hgrn_lb_logits, hgrn_norm_g, attn_sink, mlstm_conv_w,
           mlstm_gate_b, mlstm_norm_g, mem_norm_g, w_mem_kv, w_out, final_norm_g):
    batch, seq, d = x.shape
    depth = w_in.shape[0]
    assert seq % CHUNK == 0 and seq % ROW_TILE == 0 and (batch * seq) % IN_ROW_TILE == 0
    xf = x.reshape(batch * seq, d)
    cos_t, sin_t = _rope_tables(positions)
    lbl = hgrn_lb_logits.astype(F32)
    for layer in range(depth):
        w_cat, w_gt = _prep_w_in(w_in[layer])
        u, gt = _in_proj(xf, norm_g[layer], w_cat, w_gt)
        ob = _hgrn(u, lbl, layer, batch, reverse=True)
        ya = _hgrn(u, lbl, layer, batch, reverse=False, ob=ob, norm_g=hgrn_norm_g[layer])
        yb = _swa(u, cos_t, sin_t, attn_sink[layer], batch)
        hb = _mlstm(u, gt, mlstm_conv_w[layer], mlstm_gate_b[layer], batch, reverse=True)
        yc = _mlstm(u, gt, mlstm_conv_w[layer], mlstm_gate_b[layer], batch, reverse=False,
                    hb=hb, norm_g=mlstm_norm_g[layer])
        kv = _mem_kv(mem, mem_norm_g[layer], w_mem_kv[layer].astype(BF16))
        yd = _mem_attn(u, kv, batch)
        xf = _out_proj(ya, yb, yc, yd, w_out[layer].astype(BF16), xf, final_norm_g,
                       final=(layer == depth - 1))
    return xf.reshape(batch, seq, d)
```

```python
import functools

import jax
import jax.numpy as jnp
import numpy as np
from jax import lax
from jax.experimental import pallas as pl
from jax.experimental.pallas import tpu as pltpu

F32 = jnp.float32
BF16 = jnp.bfloat16

D_GROUP = 512
A_HEADS, A_HD = 4, 128
B_HEADS, B_KV_HEADS, B_HD = 8, 2, 64
WINDOW = 128
ROPE_THETA = 500000.0
ROPE_DIM = 16
C_HEADS, C_QK, C_V = 4, 64, 128
CONV_W = 5
D_HEADS, D_HD = 4, 128
EPS = 1e-6
IN_SIZES = (
    D_GROUP, D_GROUP, D_GROUP, D_GROUP, D_GROUP,
    B_HEADS * B_HD, B_KV_HEADS * B_HD, B_KV_HEADS * B_HD, D_GROUP,
    C_HEADS * C_QK, C_HEADS * C_QK, C_HEADS * C_V, D_GROUP, D_GROUP,
    2 * C_HEADS, 2 * C_HEADS,
    D_GROUP, D_GROUP,
)
SPLIT_POINTS = tuple(int(s) for s in np.cumsum(IN_SIZES)[:-1])

LANE = 128
SUBLANE = 8
VMEM_LIMIT = 48 * 1024 * 1024

CHUNK = 128
ROW_TILE = 512
IN_ROW_TILE = 256
REC_GROUP = 4
NEG = -1e30

(BLK_A_Q, BLK_A_I, BLK_A_FF, BLK_A_FB, BLK_A_Z, BLK_B_Q, BLK_B_Z, BLK_C_QK, BLK_C_V, BLK_C_O,
 BLK_C_Z, BLK_D_Q, BLK_D_Z, BLK_MISC) = range(14)
N_BLK = 14
MISC_K, MISC_V, MISC_G = 0, 128, 256
N_GATE = 4 * C_HEADS


def _dot(a, b):
    return jnp.dot(a, b, preferred_element_type=F32)


def _dot_nt(a, b):
    return lax.dot_general(a, b, (((1,), (1,)), ((), ())), preferred_element_type=F32)


def _dot_tn(a, b):
    return lax.dot_general(a, b, (((0,), (0,)), ((), ())), preferred_element_type=F32)


def _sigmoid(x):
    return 1.0 / (1.0 + jnp.exp(-x))


def _silu(x):
    return x * _sigmoid(x)


def _log_sigmoid(x):
    return jnp.minimum(x, 0.0) - jnp.log(1.0 + jnp.exp(-jnp.abs(x)))


def _cumsum_rows(x, reverse):
    n = x.shape[0]
    row = lax.broadcasted_iota(jnp.int32, x.shape, 0)
    s = 1
    while s < n:
        if reverse:
            x = x + jnp.where(row < n - s, pltpu.roll(x, n - s, axis=0), 0.0)
        else:
            x = x + jnp.where(row >= s, pltpu.roll(x, s, axis=0), 0.0)
        s *= 2
    return x


def _cumsum_lanes(x, reverse):
    n = x.shape[1]
    col = lax.broadcasted_iota(jnp.int32, x.shape, 1)
    s = 1
    while s < n:
        if reverse:
            x = x + jnp.where(col < n - s, pltpu.roll(x, n - s, axis=1), 0.0)
        else:
            x = x + jnp.where(col >= s, pltpu.roll(x, s, axis=1), 0.0)
        s *= 2
    return x


def _in_proj_kernel(x_ref, g_ref, w_ref, wgt_ref, u_ref, gt_ref):
    x = x_ref[...]
    ms = jnp.mean(x * x, axis=-1, keepdims=True)
    h = (x * lax.rsqrt(ms + EPS) * g_ref[...]).astype(BF16)
    gt_ref[...] = _dot_nt(wgt_ref[...], h)
    for j in range(N_BLK):
        sl = slice(j * D_GROUP, (j + 1) * D_GROUP)
        u_ref[:, sl] = _dot(h, w_ref[:, sl])


def _in_proj(xf, g, w_cat, w_gt, batch):
    rows, d = xf.shape
    tm = IN_ROW_TILE
    n_out = N_BLK * D_GROUP
    per_batch = rows // batch // tm
    return pl.pallas_call(
        _in_proj_kernel,
        out_shape=(jax.ShapeDtypeStruct((rows, n_out), F32),
                   jax.ShapeDtypeStruct((batch, N_GATE, rows // batch), F32)),
        grid=(rows // tm,),
        in_specs=[
            pl.BlockSpec((tm, d), lambda i: (i, 0)),
            pl.BlockSpec((1, d), lambda i: (0, 0)),
            pl.BlockSpec((d, n_out), lambda i: (0, 0), pipeline_mode=pl.Buffered(1)),
            pl.BlockSpec((N_GATE, d), lambda i: (0, 0)),
        ],
        out_specs=(
            pl.BlockSpec((tm, n_out), lambda i: (i, 0)),
            pl.BlockSpec((None, N_GATE, tm), lambda i: (i // per_batch, 0, i % per_batch)),
        ),
        compiler_params=pltpu.CompilerParams(
            dimension_semantics=("parallel",), vmem_limit_bytes=VMEM_LIMIT),
        name="in_proj",
    )(xf, g.reshape(1, d), w_cat, w_gt)


def _prep_w_in(w):
    (a_q, a_i, a_ff, a_fb, a_z, b_q, b_k, b_v, b_z,
     c_q, c_k, c_v, c_o, c_z, c_ig, c_fg, d_q, d_z) = jnp.split(w, SPLIT_POINTS, axis=1)
    gates = jnp.concatenate([c_ig, c_fg], axis=1)
    used = b_k.shape[1] + b_v.shape[1] + gates.shape[1]
    pad = jnp.zeros((w.shape[0], D_GROUP - used), w.dtype)
    w_cat = jnp.concatenate(
        [a_q, a_i, a_ff, a_fb, a_z, b_q, b_z, c_q, c_k, c_v, c_o, c_z, d_q, d_z,
         b_k, b_v, gates, pad], axis=1)
    return w_cat.astype(BF16), gates.T.astype(BF16)


def _out_proj_kernel(ya_ref, yb_ref, yc_ref, yd_ref, w_ref, x_ref, g_ref, o_ref, *, final):
    acc = x_ref[...]
    for i, y_ref in enumerate((ya_ref, yb_ref, yc_ref, yd_ref)):
        acc = acc + _dot(y_ref[...], w_ref[i * D_GROUP:(i + 1) * D_GROUP, :])
    if final:
        ms = jnp.mean(acc * acc, axis=-1, keepdims=True)
        acc = acc * lax.rsqrt(ms + EPS) * g_ref[...]
    o_ref[...] = acc


def _out_proj(ya, yb, yc, yd, w, xf, g, final):
    rows, d = xf.shape
    tm = ROW_TILE
    yspec = pl.BlockSpec((tm, D_GROUP), lambda i: (i, 0))
    return pl.pallas_call(
        functools.partial(_out_proj_kernel, final=final),
        out_shape=jax.ShapeDtypeStruct((rows, d), F32),
        grid=(rows // tm,),
        in_specs=[yspec, yspec, yspec, yspec,
                  pl.BlockSpec((4 * D_GROUP, d), lambda i: (0, 0)),
                  pl.BlockSpec((tm, d), lambda i: (i, 0)),
                  pl.BlockSpec((1, d), lambda i: (0, 0))],
        out_specs=pl.BlockSpec((tm, d), lambda i: (i, 0)),
        compiler_params=pltpu.CompilerParams(
            dimension_semantics=("parallel",), vmem_limit_bytes=VMEM_LIMIT),
        name="out_proj",
    )(ya, yb, yc, yd, w, xf, g.reshape(1, d))


def _level_ref(b, h, reverse):
    n_rows, n = b.shape
    blk = 2 * h
    r0 = h if reverse else h - 1
    if blk >= SUBLANE:
        pieces = [jnp.broadcast_to(b[s + r0:s + r0 + 1, :], (blk, n)) for s in range(0, n_rows, blk)]
        return pieces[0] if len(pieces) == 1 else jnp.concatenate(pieces, axis=0)
    res = lax.broadcasted_iota(jnp.int32, b.shape, 0) & (blk - 1)
    out = None
    for m in range(blk):
        shift = (m - r0) % n_rows
        cand = b if shift == 0 else pltpu.roll(b, shift, axis=0)
        out = cand if out is None else jnp.where(res == m, cand, out)
    return out


def _hgrn_lower_bound(lbl_ref, layer, depth):
    rows = [lbl_ref[j:j + 1, :] for j in range(depth)]
    mx = functools.reduce(jnp.maximum, rows)
    es = [jnp.exp(r - mx) for r in rows]
    tot = functools.reduce(lambda a, c: a + c, es)
    lb = jnp.zeros_like(rows[0])
    for j in range(1, layer + 1):
        lb = lb + es[j] / tot
    return lb


def _cumsum_rows_mxu(x, reverse):
    n = x.shape[0]
    r = lax.broadcasted_iota(jnp.int32, (n, n), 0)
    c = lax.broadcasted_iota(jnp.int32, (n, n), 1)
    tri = jnp.where((c >= r) if reverse else (c <= r), 1.0, 0.0).astype(BF16)
    hi = x.astype(BF16)
    r1 = x - hi.astype(F32)
    mid = r1.astype(BF16)
    lo = (r1 - mid.astype(F32)).astype(BF16)
    return _dot(jnp.concatenate([tri, tri, tri], axis=1), jnp.concatenate([hi, mid, lo], axis=0))


def _hgrn_level_operand(q, k, f, b2, h, reverse):
    n_rows = q.shape[0]
    blk = 2 * h
    r0 = h if reverse else h - 1
    if h >= SUBLANE:
        pieces = []
        for s0 in range(0, n_rows, blk):
            ref = b2[s0 + r0:s0 + r0 + 1, :]
            lo, up = slice(s0, s0 + h), slice(s0 + h, s0 + blk)
            if reverse:
                pieces += [q[lo] * jnp.exp2(b2[lo] - ref), k[up] * jnp.exp2(ref - b2[up])]
            else:
                pieces += [k[lo] * jnp.exp2(ref - b2[lo]), q[up] * jnp.exp2(b2[up] - ref)]
        return jnp.concatenate(pieces, axis=0)
    row = lax.broadcasted_iota(jnp.int32, q.shape, 0)
    is_q = ((row & h) == 0) if reverse else ((row & h) != 0)
    qk = jnp.where(is_q, q, k)
    if h == 1:
        return jnp.where(is_q, q * f, k)
    if h == 2:
        res = row & 3
        f_dn = pltpu.roll(f, 1, axis=0)
        f_up = pltpu.roll(f, n_rows - 1, axis=0)
        if reverse:
            e = jnp.where(res == 0, f * f_up, jnp.where(res == 1, f, jnp.where(res == 2, 1.0, f_dn)))
        else:
            e = jnp.where(res == 0, f_up, jnp.where(res == 1, 1.0, jnp.where(res == 2, f, f_dn * f)))
        return qk * e
    return qk * jnp.exp2(-jnp.abs(b2 - _level_ref(b2, h, reverse)))


def _hgrn_kernel(*refs, layer, depth, reverse, final):
    if final:
        q_ref, v_ref, zg_ref, lbl_ref, z_ref, ob_ref, ng_ref, out_ref, st_ref = refs
    else:
        q_ref, v_ref, zg_ref, lbl_ref, out_ref, st_ref = refs

    @pl.when(pl.program_id(1) == 0)
    def _():
        st_ref[...] = jnp.zeros_like(st_ref)

    for g in range(q_ref.shape[0]):
        fin = (z_ref.at[g], ob_ref.at[g], ng_ref) if final else None
        _hgrn_chunk(q_ref.at[g], v_ref.at[g], zg_ref.at[g], lbl_ref, fin, out_ref.at[g],
                    st_ref.at[g], layer=layer, depth=depth, reverse=reverse)


def _hgrn_chunk(q_ref, v_ref, zg_ref, lbl_ref, fin, out_ref, st_ref, *, layer, depth, reverse):
    n_rows = q_ref.shape[0]
    q = q_ref[...]
    zg = zg_ref[...]
    e = jnp.exp(-jnp.abs(zg))
    r = 1.0 / (1.0 + e)
    er = e * r
    pos = zg >= 0.0
    f = jnp.where(pos, r, er)
    k = jnp.where(pos, er, r)
    if layer > 0:
        lb = _hgrn_lower_bound(lbl_ref, layer, depth)
        f = lb + (1.0 - lb) * f
        k = (1.0 - lb) * k
    b2 = _cumsum_rows_mxu(jnp.log2(f), reverse)
    tot = b2[0:1, :] if reverse else b2[n_rows - 1:n_rows, :]
    v_bf = v_ref[...].astype(BF16)
    q_in = (q * jnp.exp2(b2)).astype(BF16)
    k_out = (k * jnp.exp2(tot - b2)).astype(BF16)
    dec = jnp.exp2(tot)

    ti = lax.broadcasted_iota(jnp.int32, (n_rows, n_rows), 0)
    si = lax.broadcasted_iota(jnp.int32, (n_rows, n_rows), 1)
    ahead = (ti < si) if reverse else (ti > si)
    level = jnp.where(ahead, 31 - lax.clz(ti ^ si), -1)
    operands = [(q.astype(BF16), k.astype(BF16), ti == si)]
    h = n_rows // 2
    while h >= 1:
        c = _hgrn_level_operand(q, k, f, b2, h, reverse).astype(BF16)
        operands.append((c, c, level == (h.bit_length() - 1)))
        h //= 2

    outs = []
    for hd in range(A_HEADS):
        sl = slice(hd * A_HD, (hd + 1) * A_HD)
        att = jnp.zeros((n_rows, n_rows), F32)
        for lhs, rhs, msk in operands:
            att = jnp.where(msk, _dot_nt(lhs[:, sl], rhs[:, sl]), att)
        st = st_ref[hd]
        o = _dot(att.astype(BF16), v_bf[:, sl]) + _dot_nt(q_in[:, sl], st.astype(BF16))
        st_ref[hd] = dec[:, sl] * st + _dot_tn(v_bf[:, sl], k_out[:, sl])
        outs.append(o)
    o = jnp.concatenate(outs, axis=1)

    if fin is not None:
        z_ref, ob_ref, ng_ref = fin
        o = o + ob_ref[...]
        ys = []
        for hd in range(A_HEADS):
            sl = slice(hd * A_HD, (hd + 1) * A_HD)
            oh = o[:, sl]
            ms = jnp.mean(oh * oh, axis=-1, keepdims=True)
            ys.append(oh * lax.rsqrt(ms + EPS) * ng_ref[:, sl])
        y = jnp.concatenate(ys, axis=1) * _silu(z_ref[...])
        out_ref[...] = y.astype(out_ref.dtype)
    else:
        out_ref[...] = o


def _chunk_index(c, n_chunks, reverse):
    return n_chunks - 1 - c if reverse else c


def _hgrn(u3, lb_logits, layer, *, reverse, ob=None, norm_g=None):
    batch, seq, _ = u3.shape
    n_chunks = seq // CHUNK
    depth = lb_logits.shape[0]
    final = ob is not None
    grp = REC_GROUP

    def ublk(blk):
        return pl.BlockSpec((grp, CHUNK, D_GROUP),
                            lambda bg, c: (bg, _chunk_index(c, n_chunks, reverse), blk))

    row_spec = ublk(0)
    in_specs = [ublk(BLK_A_Q), ublk(BLK_A_I), ublk(BLK_A_FB if reverse else BLK_A_FF),
                pl.BlockSpec((depth, D_GROUP), lambda bg, c: (0, 0))]
    args = [u3, u3, u3, lb_logits]
    if final:
        in_specs += [ublk(BLK_A_Z), row_spec, pl.BlockSpec((1, D_GROUP), lambda bg, c: (0, 0))]
        args += [u3, ob, norm_g.reshape(1, D_GROUP)]
    return pl.pallas_call(
        functools.partial(_hgrn_kernel, layer=layer, depth=depth, reverse=reverse, final=final),
        out_shape=jax.ShapeDtypeStruct((batch, seq, D_GROUP), BF16 if final else F32),
        grid=(batch // grp, n_chunks),
        in_specs=in_specs,
        out_specs=row_spec,
        scratch_shapes=[pltpu.VMEM((grp, A_HEADS, A_HD, A_HD), F32)],
        compiler_params=pltpu.CompilerParams(
            dimension_semantics=("parallel", "arbitrary"), vmem_limit_bytes=VMEM_LIMIT),
        name="hgrn_fwd" if final else "hgrn_bwd",
    )(*args)


def _split3(x):
    hi = x.astype(BF16).astype(F32)
    r1 = x - hi
    mid = r1.astype(BF16).astype(F32)
    lo = (r1 - mid).astype(BF16).astype(F32)
    return hi, mid, lo


def _cumsum_lanes_mxu(x, reverse):
    n = x.shape[1]
    j = lax.broadcasted_iota(jnp.int32, (n, n), 0)
    t = lax.broadcasted_iota(jnp.int32, (n, n), 1)
    tri = jnp.where((j >= t) if reverse else (j <= t), 1.0, 0.0).astype(BF16)
    pieces = jnp.concatenate(list(_split3(x)), axis=1).astype(BF16)
    return _dot(pieces, jnp.concatenate([tri, tri, tri], axis=0))


def _conv_silu_qk(qkp_ref, qkc_ref, qkn_ref, cw_ref, reverse):
    n_rows = qkc_ref.shape[0]
    c = pl.program_id(1)
    n_chunks = pl.num_programs(1)
    cc = _chunk_index(c, n_chunks, reverse)
    halo = SUBLANE
    prev = qkp_ref[n_rows - halo:n_rows, :] * (cc > 0).astype(F32)
    nxt = qkn_ref[0:halo, :] * (cc < n_chunks - 1).astype(F32)
    xcat = jnp.concatenate([prev, qkc_ref[...], nxt], axis=0)
    acc = None
    for j in range(CONV_W):
        off = halo + j - CONV_W // 2
        term = cw_ref[j:j + 1, :] * xcat[off:off + n_rows, :]
        acc = term if acc is None else acc + term
    qk = _silu(acc)
    n_qk = C_HEADS * C_QK
    return jnp.concatenate([qk[:, :n_qk] * (C_QK ** -0.5), qk[:, n_qk:]], axis=1).astype(BF16)


def _mlstm_kernel(*refs, reverse, final):
    if final:
        (qk_ref, v_ref, gt_ref, gbc_ref, og_ref, z_ref, hb_ref, ng_ref,
         out_ref, cst_ref, m_ref) = refs
    else:
        (qkp_ref, qkc_ref, qkn_ref, cw_ref, v_ref, gt_ref, gbc_ref,
         out_ref, qko_ref, cst_ref, m_ref) = refs

    @pl.when(pl.program_id(1) == 0)
    def _():
        cst_ref[...] = jnp.zeros_like(cst_ref)
        m_ref[...] = jnp.zeros_like(m_ref)

    for g in range(v_ref.shape[0]):
        if final:
            qk = qk_ref[g]
            fin = (og_ref.at[g], z_ref.at[g], hb_ref.at[g], ng_ref)
        else:
            qk = _conv_silu_qk(qkp_ref.at[g], qkc_ref.at[g], qkn_ref.at[g], cw_ref, reverse)
            qko_ref[g] = qk
            fin = None
        _mlstm_chunk(qk, v_ref.at[g], gt_ref.at[g], gbc_ref, fin, out_ref.at[g],
                     cst_ref.at[g], m_ref.at[g], reverse=reverse)


def _mlstm_chunk(qk, v_ref, gt_ref, gbc_ref, fin, out_ref, cst_ref, m_ref, *, reverse):
    n_rows = v_ref.shape[0]
    n_qk = C_HEADS * C_QK
    n_dir = 2 * C_HEADS

    gr = gt_ref[...] + gbc_ref[...]
    li = gr[0:n_dir, :]
    b = _cumsum_lanes_mxu(_log_sigmoid(gr[n_dir:, :]), reverse)
    b_last = b[:, 0:1] if reverse else b[:, n_rows - 1:n_rows]
    m_prev = m_ref[...]
    d_inter = b + m_prev
    a = (b_last - b) + li
    m_new = jnp.maximum(b_last + m_prev, jnp.max(a, axis=1, keepdims=True))
    ws = jnp.exp(a - m_new)
    decay = jnp.exp((b_last + m_prev) - m_new)
    m_ref[...] = m_new

    one8 = jnp.ones((n_dir, n_rows), F32)
    lhs = jnp.concatenate(list(_split3(li - b)) + [one8, one8, one8], axis=0).astype(BF16)
    rhs_all = jnp.concatenate([one8, one8, one8] + list(_split3(b)), axis=0)
    gate_row = lax.broadcasted_iota(jnp.int32, rhs_all.shape, 0) & (n_dir - 1)

    si = lax.broadcasted_iota(jnp.int32, (n_rows, n_rows), 0)
    ti = lax.broadcasted_iota(jnp.int32, (n_rows, n_rows), 1)
    causal = (si >= ti) if reverse else (si <= ti)
    v_all = v_ref[...]
    ones_t = jnp.ones((C_V, n_rows), F32)

    outs = []
    for hd in range(C_HEADS):
        gi = (C_HEADS if reverse else 0) + hd
        q_h = qk[:, hd * C_QK:(hd + 1) * C_QK]
        k_h = qk[:, n_qk + hd * C_QK:n_qk + (hd + 1) * C_QK]
        rhs = jnp.where(gate_row == gi, rhs_all, 0.0).astype(BF16)
        d = jnp.where(causal, _dot_tn(lhs, rhs), NEG)
        d_int = d_inter[gi:gi + 1, :]
        m_t = jnp.maximum(jnp.max(d, axis=0, keepdims=True), d_int)
        w = (_dot_nt(k_h, q_h) * jnp.exp(d - m_t)).astype(BF16)
        g = jnp.exp(d_int - m_t)
        v_t = jnp.concatenate([v_all[:, hd * C_V:(hd + 1) * C_V].T, ones_t], axis=0)
        cst = cst_ref[hd]
        num = _dot(v_t.astype(BF16), w) + g * _dot_nt(cst.astype(BF16), q_h)
        den = jnp.maximum(jnp.abs(num[C_V:C_V + 1, :]), jnp.exp(-m_t))
        outs.append((num[:C_V, :] / den).T)
        cst_ref[hd] = (decay[gi:gi + 1, :C_QK] * cst
                       + _dot((v_t * ws[gi:gi + 1, :]).astype(BF16), k_h))
    hcat = jnp.concatenate(outs, axis=1)

    if fin is not None:
        og_ref, z_ref, hb_ref, ng_ref = fin
        hsum = (hcat + hb_ref[...]) * _sigmoid(og_ref[...])
        ys = []
        for hd in range(C_HEADS):
            sl = slice(hd * C_V, (hd + 1) * C_V)
            hh = hsum[:, sl]
            mu = jnp.mean(hh, axis=-1, keepdims=True)
            var = jnp.mean(jnp.square(hh - mu), axis=-1, keepdims=True)
            ys.append((hh - mu) * lax.rsqrt(var + EPS) * ng_ref[:, sl])
        y = jnp.concatenate(ys, axis=1) * _silu(z_ref[...])
        out_ref[...] = y.astype(out_ref.dtype)
    else:
        out_ref[...] = hcat


def _mlstm(u3, gt, conv_w, gate_b, *, reverse, qk=None, hb=None, norm_g=None):
    batch, seq, _ = u3.shape
    n_chunks = seq // CHUNK
    final = hb is not None
    grp = REC_GROUP
    assert CHUNK == LANE

    def chunk(c, delta=0):
        return jnp.clip(_chunk_index(c, n_chunks, reverse) + delta, 0, n_chunks - 1)

    def ublk(blk, delta=0):
        return pl.BlockSpec((grp, CHUNK, D_GROUP), lambda bg, c: (bg, chunk(c, delta), blk))

    row_spec = ublk(0)
    gate_specs = [pl.BlockSpec((grp, N_GATE, CHUNK), lambda bg, c: (bg, 0, chunk(c))),
                  pl.BlockSpec((N_GATE, 1), lambda bg, c: (0, 0))]
    gate_args = [gt, gate_b.reshape(N_GATE, 1)]
    if final:
        in_specs = [row_spec, ublk(BLK_C_V)] + gate_specs + [
            ublk(BLK_C_O), ublk(BLK_C_Z), row_spec,
            pl.BlockSpec((1, D_GROUP), lambda bg, c: (0, 0))]
        args = [qk, u3] + gate_args + [u3, u3, hb, norm_g.reshape(1, D_GROUP)]
        out_shape = jax.ShapeDtypeStruct((batch, seq, D_GROUP), BF16)
        out_specs = row_spec
    else:
        cw = jnp.zeros((SUBLANE, D_GROUP), F32).at[:CONV_W].set(conv_w)
        in_specs = [ublk(BLK_C_QK, -1), ublk(BLK_C_QK), ublk(BLK_C_QK, 1),
                    pl.BlockSpec((SUBLANE, D_GROUP), lambda bg, c: (0, 0)),
                    ublk(BLK_C_V)] + gate_specs
        args = [u3, u3, u3, cw, u3] + gate_args
        out_shape = (jax.ShapeDtypeStruct((batch, seq, D_GROUP), F32),
                     jax.ShapeDtypeStruct((batch, seq, D_GROUP), BF16))
        out_specs = (row_spec, row_spec)
    return pl.pallas_call(
        functools.partial(_mlstm_kernel, reverse=reverse, final=final),
        out_shape=out_shape,
        grid=(batch // grp, n_chunks),
        in_specs=in_specs,
        out_specs=out_specs,
        scratch_shapes=[pltpu.VMEM((grp, C_HEADS, 2 * C_V, C_QK), F32),
                        pltpu.VMEM((grp, 2 * C_HEADS, CHUNK), F32)],
        compiler_params=pltpu.CompilerParams(
            dimension_semantics=("parallel", "arbitrary"), vmem_limit_bytes=VMEM_LIMIT),
        name="mlstm_fwd" if final else "mlstm_bwd",
    )(*args)


def _rope_table_kernel(pos_ref, invf_ref, cos_ref, sin_ref):
    ang = pos_ref[...].astype(F32) * invf_ref[...]
    r = lax.broadcasted_iota(jnp.int32, ang.shape, 1) & (B_HD - 1)
    half = ROPE_DIM // 2
    sn = jnp.sin(ang)
    cos_ref[...] = jnp.where(r < ROPE_DIM, jnp.cos(ang), 1.0)
    sin_ref[...] = jnp.where(r < half, -sn, jnp.where(r < ROPE_DIM, sn, 0.0))


def _rope_tables(positions):
    rows = positions.size
    half = ROPE_DIM // 2
    inv_freq = ROPE_THETA ** (-jnp.arange(0, ROPE_DIM, 2, dtype=F32) / ROPE_DIM)
    lane = np.arange(LANE) % B_HD
    invf = jnp.where(lane < ROPE_DIM, inv_freq[lane % half], 0.0).astype(F32).reshape(1, LANE)
    tm = ROW_TILE
    return pl.pallas_call(
        _rope_table_kernel,
        out_shape=(jax.ShapeDtypeStruct((rows, LANE), F32),) * 2,
        grid=(rows // tm,),
        in_specs=[pl.BlockSpec((tm, 1), lambda i: (i, 0)),
                  pl.BlockSpec((1, LANE), lambda i: (0, 0))],
        out_specs=(pl.BlockSpec((tm, LANE), lambda i: (i, 0)),) * 2,
        compiler_params=pltpu.CompilerParams(dimension_semantics=("parallel",)),
        name="rope_tables",
    )(positions.reshape(rows, 1), invf)


def _rope(x, cos, sin):
    width = x.shape[1]
    if width > LANE:
        cos = jnp.concatenate([cos] * (width // LANE), axis=1)
        sin = jnp.concatenate([sin] * (width // LANE), axis=1)
    half = ROPE_DIM // 2
    r = lax.broadcasted_iota(jnp.int32, x.shape, 1) & (B_HD - 1)
    partner = jnp.where(r < half, pltpu.roll(x, width - half, axis=1), pltpu.roll(x, half, axis=1))
    return x * cos + partner * sin


def _swa_kernel(q_ref, z_ref, mp_ref, mc_ref, mn_ref, cp_ref, cc_ref, cn_ref,
                sp_ref, sc_ref, sn_ref, sink_ref, out_ref):
    blk = q_ref.shape[0]
    i = pl.program_id(1)
    n_blocks = pl.num_programs(1)
    kv_w = B_KV_HEADS * B_HD
    grp = B_HEADS // B_KV_HEADS

    q = (_rope(q_ref[...], cc_ref[...], sc_ref[...]) * (B_HD ** -0.5)).astype(BF16)
    k = jnp.concatenate([
        _rope(mp_ref[:, MISC_K:MISC_K + kv_w], cp_ref[...], sp_ref[...]),
        _rope(mc_ref[:, MISC_K:MISC_K + kv_w], cc_ref[...], sc_ref[...]),
        _rope(mn_ref[:, MISC_K:MISC_K + kv_w], cn_ref[...], sn_ref[...])], axis=0).astype(BF16)
    v = jnp.concatenate([mp_ref[:, MISC_V:MISC_V + kv_w], mc_ref[:, MISC_V:MISC_V + kv_w],
                         mn_ref[:, MISC_V:MISC_V + kv_w]], axis=0)

    kr = lax.broadcasted_iota(jnp.int32, (3 * blk, blk), 0)
    qc = lax.broadcasted_iota(jnp.int32, (3 * blk, blk), 1)
    lo = jnp.where(i > 0, 0, blk)
    hi = jnp.where(i < n_blocks - 1, 3 * blk, 2 * blk)
    valid = (jnp.abs(qc - (kr - blk)) <= WINDOW) & (kr >= lo) & (kr < hi)
    bias = jnp.where(valid, 0.0, NEG)
    ones = jnp.ones((3 * blk, B_HD), F32)

    tiles = []
    pair = []
    for hq in range(B_HEADS):
        kvh = hq // grp
        ksl = slice(kvh * B_HD, (kvh + 1) * B_HD)
        if hq % grp == 0:
            vt = jnp.concatenate([v[:, ksl], ones], axis=1).T.astype(BF16)
        s = _dot_nt(k[:, ksl], q[:, hq * B_HD:(hq + 1) * B_HD]) + bias
        sk = sink_ref[hq:hq + 1, 0:1]
        m = jnp.maximum(jnp.max(s, axis=0, keepdims=True), sk)
        p = jnp.exp(s - m).astype(BF16)
        o = _dot(vt, p)
        denom = o[B_HD:B_HD + 1, :] + jnp.exp(sk - m)
        pair.append(o[:B_HD, :] / denom)
        if len(pair) == LANE // B_HD:
            tiles.append(jnp.concatenate(pair, axis=0).T)
            pair = []
    y = jnp.concatenate(tiles, axis=1) * _silu(z_ref[...])
    out_ref[...] = y.astype(out_ref.dtype)


def _swa(u, cos_t, sin_t, sink, batch):
    rows = u.shape[0]
    n_blocks = rows // batch // CHUNK

    def blk_row(bi, i, delta=0):
        return bi * n_blocks + jnp.clip(i + delta, 0, n_blocks - 1)

    def ublk(blk, delta=0):
        return pl.BlockSpec((CHUNK, D_GROUP), lambda bi, i: (blk_row(bi, i, delta), blk))

    def tspec(delta):
        return pl.BlockSpec((CHUNK, LANE), lambda bi, i: (blk_row(bi, i, delta), 0))

    sink_t = jnp.broadcast_to(sink.astype(F32).reshape(B_HEADS, 1), (B_HEADS, LANE))
    return pl.pallas_call(
        _swa_kernel,
        out_shape=jax.ShapeDtypeStruct((rows, D_GROUP), BF16),
        grid=(batch, n_blocks),
        in_specs=[ublk(BLK_B_Q), ublk(BLK_B_Z), ublk(BLK_MISC, -1), ublk(BLK_MISC), ublk(BLK_MISC, 1),
                  tspec(-1), tspec(0), tspec(1), tspec(-1), tspec(0), tspec(1),
                  pl.BlockSpec((B_HEADS, LANE), lambda bi, i: (0, 0))],
        out_specs=pl.BlockSpec((CHUNK, D_GROUP), lambda bi, i: (blk_row(bi, i), 0)),
        compiler_params=pltpu.CompilerParams(
            dimension_semantics=("parallel", "parallel"), vmem_limit_bytes=VMEM_LIMIT),
        name="window_attn",
    )(u, u, u, u, u, cos_t, cos_t, cos_t, sin_t, sin_t, sin_t, sink_t)


def _mem_kv_kernel(mem_ref, g_ref, w_ref, kv_ref):
    x = mem_ref[...]
    ms = jnp.mean(x * x, axis=-1, keepdims=True)
    h = (x * lax.rsqrt(ms + EPS) * g_ref[...]).astype(BF16)
    kv_ref[...] = _dot(h, w_ref[...]).astype(kv_ref.dtype)


def _mem_kv(mem, g, w):
    batch, m_len, d = mem.shape
    n_out = w.shape[1]
    return pl.pallas_call(
        _mem_kv_kernel,
        out_shape=jax.ShapeDtypeStruct((batch * m_len, n_out), BF16),
        grid=(batch,),
        in_specs=[pl.BlockSpec((m_len, d), lambda bi: (bi, 0)),
                  pl.BlockSpec((1, d), lambda bi: (0, 0)),
                  pl.BlockSpec((d, n_out), lambda bi: (0, 0))],
        out_specs=pl.BlockSpec((m_len, n_out), lambda bi: (bi, 0)),
        compiler_params=pltpu.CompilerParams(
            dimension_semantics=("parallel",), vmem_limit_bytes=VMEM_LIMIT),
        name="mem_kv",
    )(mem.reshape(batch * m_len, d), g.reshape(1, d), w)


def _mem_attn_kernel(q_ref, z_ref, kv_ref, out_ref):
    q = q_ref[...].astype(BF16)
    outs = []
    for hd in range(D_HEADS):
        sl = slice(hd * D_HD, (hd + 1) * D_HD)
        k = kv_ref[:, hd * D_HD:(hd + 1) * D_HD]
        v = kv_ref[:, D_GROUP + hd * D_HD:D_GROUP + (hd + 1) * D_HD]
        s = _dot_nt(q[:, sl], k) * (D_HD ** -0.5)
        m = jnp.max(s, axis=-1, keepdims=True)
        p = jnp.exp(s - m)
        p = p / jnp.sum(p, axis=-1, keepdims=True)
        outs.append(_dot(p.astype(BF16), v))
    y = jnp.concatenate(outs, axis=1) * _silu(z_ref[...])
    out_ref[...] = y.astype(out_ref.dtype)


def _mem_attn(u, kv, batch):
    rows = u.shape[0]
    tm = ROW_TILE
    per_batch = rows // batch // tm
    m_len = kv.shape[0] // batch
    return pl.pallas_call(
        _mem_attn_kernel,
        out_shape=jax.ShapeDtypeStruct((rows, D_GROUP), BF16),
        grid=(batch, per_batch),
        in_specs=[pl.BlockSpec((tm, D_GROUP), lambda bi, i: (bi * per_batch + i, BLK_D_Q)),
                  pl.BlockSpec((tm, D_GROUP), lambda bi, i: (bi * per_batch + i, BLK_D_Z)),
                  pl.BlockSpec((m_len, kv.shape[1]), lambda bi, i: (bi, 0))],
        out_specs=pl.BlockSpec((tm, D_GROUP), lambda bi, i: (bi * per_batch + i, 0)),
        compiler_params=pltpu.CompilerParams(
            dimension_semantics=("parallel", "parallel"), vmem_limit_bytes=VMEM_LIMIT),
        name="mem_attn",
    )(u, u, kv)


def kernel(x, mem, positions, norm_g, w_in, hgrn_lb_logits, hgrn_norm_g, attn_sink, mlstm_conv_w,
           mlstm_gate_b, mlstm_norm_g, mem_norm_g, w_mem_kv, w_out, final_norm_g):
    batch, seq, d = x.shape
    depth = w_in.shape[0]
    assert seq % CHUNK == 0 and seq % ROW_TILE == 0 and seq % IN_ROW_TILE == 0
    assert batch % REC_GROUP == 0
    xf = x.reshape(batch * seq, d)
    cos_t, sin_t = _rope_tables(positions)
    lbl = hgrn_lb_logits.astype(F32)
    for layer in range(depth):
        w_cat, w_gt = _prep_w_in(w_in[layer])
        u, gt = _in_proj(xf, norm_g[layer], w_cat, w_gt, batch)
        u3 = u.reshape(batch, seq, u.shape[1])
        ob = _hgrn(u3, lbl, layer, reverse=True)
        ya = _hgrn(u3, lbl, layer, reverse=False, ob=ob, norm_g=hgrn_norm_g[layer])
        yb = _swa(u, cos_t, sin_t, attn_sink[layer], batch)
        hb, qk_c = _mlstm(u3, gt, mlstm_conv_w[layer], mlstm_gate_b[layer], reverse=True)
        yc = _mlstm(u3, gt, None, mlstm_gate_b[layer], reverse=False,
                    qk=qk_c, hb=hb, norm_g=mlstm_norm_g[layer])
        kv = _mem_kv(mem, mem_norm_g[layer], w_mem_kv[layer].astype(BF16))
        yd = _mem_attn(u, kv, batch)
        xf = _out_proj(ya.reshape(batch * seq, D_GROUP), yb, yc.reshape(batch * seq, D_GROUP), yd,
                       w_out[layer].astype(BF16), xf, final_norm_g, final=(layer == depth - 1))
    return xf.reshape(batch, seq, d)
```

```python
import functools

import jax
import jax.numpy as jnp
import numpy as np
from jax import lax
from jax.experimental import pallas as pl
from jax.experimental.pallas import tpu as pltpu

F32 = jnp.float32
BF16 = jnp.bfloat16

D_GROUP = 512
A_HEADS, A_HD = 4, 128
B_HEADS, B_KV_HEADS, B_HD = 8, 2, 64
WINDOW = 128
ROPE_THETA = 500000.0
ROPE_DIM = 16
C_HEADS, C_QK, C_V = 4, 64, 128
CONV_W = 5
D_HEADS, D_HD = 4, 128
EPS = 1e-6
IN_SIZES = (
    D_GROUP, D_GROUP, D_GROUP, D_GROUP, D_GROUP,
    B_HEADS * B_HD, B_KV_HEADS * B_HD, B_KV_HEADS * B_HD, D_GROUP,
    C_HEADS * C_QK, C_HEADS * C_QK, C_HEADS * C_V, D_GROUP, D_GROUP,
    2 * C_HEADS, 2 * C_HEADS,
    D_GROUP, D_GROUP,
)
SPLIT_POINTS = tuple(int(s) for s in np.cumsum(IN_SIZES)[:-1])

LANE = 128
SUBLANE = 8
VMEM_LIMIT = 48 * 1024 * 1024

CHUNK = 128
ROW_TILE = 512
IN_ROW_TILE = 256
SWA_TILE = 256
REC_GROUP = 4
NEG = -1e30

(BLK_A_Q, BLK_A_I, BLK_A_FF, BLK_A_FB, BLK_A_Z, BLK_B_Q, BLK_B_Z, BLK_C_QK, BLK_C_V, BLK_C_O,
 BLK_C_Z, BLK_D_Q, BLK_D_Z, BLK_MISC) = range(14)
N_BLK = 14
MISC_K, MISC_V, MISC_G = 0, 128, 256
N_GATE = 4 * C_HEADS
C_ONES = 16


def _dot(a, b):
    return jnp.dot(a, b, preferred_element_type=F32)


def _dot_nt(a, b):
    return lax.dot_general(a, b, (((1,), (1,)), ((), ())), preferred_element_type=F32)


def _dot_tn(a, b):
    return lax.dot_general(a, b, (((0,), (0,)), ((), ())), preferred_element_type=F32)


def _sigmoid(x):
    return 1.0 / (1.0 + jnp.exp(-x))


def _silu(x):
    return x * _sigmoid(x)


def _log_sigmoid(x):
    return jnp.minimum(x, 0.0) - jnp.log(1.0 + jnp.exp(-jnp.abs(x)))


def _cumsum_rows(x, reverse):
    n = x.shape[0]
    row = lax.broadcasted_iota(jnp.int32, x.shape, 0)
    s = 1
    while s < n:
        if reverse:
            x = x + jnp.where(row < n - s, pltpu.roll(x, n - s, axis=0), 0.0)
        else:
            x = x + jnp.where(row >= s, pltpu.roll(x, s, axis=0), 0.0)
        s *= 2
    return x


def _cumsum_lanes(x, reverse):
    n = x.shape[1]
    col = lax.broadcasted_iota(jnp.int32, x.shape, 1)
    s = 1
    while s < n:
        if reverse:
            x = x + jnp.where(col < n - s, pltpu.roll(x, n - s, axis=1), 0.0)
        else:
            x = x + jnp.where(col >= s, pltpu.roll(x, s, axis=1), 0.0)
        s *= 2
    return x


def _in_proj_kernel(x_ref, g_ref, w_ref, wgt_ref, u_ref, gt_ref, *, layer):
    x = x_ref[...]
    ms = jnp.mean(x * x, axis=-1, keepdims=True)
    h = (x * lax.rsqrt(ms + EPS) * g_ref[layer:layer + 1, :]).astype(BF16)
    gt_ref[...] = _dot_nt(wgt_ref[...], h)
    for j in range(N_BLK):
        sl = slice(j * D_GROUP, (j + 1) * D_GROUP)
        u_ref[:, sl] = _dot(h, w_ref[:, sl])


def _in_proj(xf, g_all, w_cat_all, w_gt_all, layer, batch):
    rows, d = xf.shape
    tm = IN_ROW_TILE
    n_out = N_BLK * D_GROUP
    per_batch = rows // batch // tm
    return pl.pallas_call(
        functools.partial(_in_proj_kernel, layer=layer),
        out_shape=(jax.ShapeDtypeStruct((rows, n_out), F32),
                   jax.ShapeDtypeStruct((batch, N_GATE, rows // batch), F32)),
        grid=(rows // tm,),
        in_specs=[
            pl.BlockSpec((tm, d), lambda i: (i, 0)),
            pl.BlockSpec(g_all.shape, lambda i: (0, 0)),
            pl.BlockSpec((None, d, n_out), lambda i: (layer, 0, 0), pipeline_mode=pl.Buffered(1)),
            pl.BlockSpec((None, N_GATE, d), lambda i: (layer, 0, 0)),
        ],
        out_specs=(
            pl.BlockSpec((tm, n_out), lambda i: (i, 0)),
            pl.BlockSpec((None, N_GATE, tm), lambda i: (i // per_batch, 0, i % per_batch)),
        ),
        compiler_params=pltpu.CompilerParams(
            dimension_semantics=("parallel",), vmem_limit_bytes=VMEM_LIMIT),
        name="in_proj",
    )(xf, g_all, w_cat_all, w_gt_all)


def _prep_w_in(w):
    (a_q, a_i, a_ff, a_fb, a_z, b_q, b_k, b_v, b_z,
     c_q, c_k, c_v, c_o, c_z, c_ig, c_fg, d_q, d_z) = jnp.split(w.astype(BF16), SPLIT_POINTS, axis=-1)
    gates = jnp.concatenate([c_ig, c_fg], axis=-1)
    used = b_k.shape[-1] + b_v.shape[-1] + gates.shape[-1]
    pad = jnp.zeros(w.shape[:-1] + (D_GROUP - used,), BF16)
    w_cat = jnp.concatenate(
        [a_q, a_i, a_ff, a_fb, a_z, b_q, b_z, c_q, c_k, c_v, c_o, c_z, d_q, d_z,
         b_k, b_v, gates, pad], axis=-1)
    return w_cat, jnp.swapaxes(gates, -1, -2)


def _out_proj_kernel(ya_ref, yb_ref, yc_ref, yd_ref, w_ref, x_ref, g_ref, o_ref, *, final):
    acc = x_ref[...]
    for i, y_ref in enumerate((ya_ref, yb_ref, yc_ref, yd_ref)):
        acc = acc + _dot(y_ref[...], w_ref[i * D_GROUP:(i + 1) * D_GROUP, :])
    if final:
        ms = jnp.mean(acc * acc, axis=-1, keepdims=True)
        acc = acc * lax.rsqrt(ms + EPS) * g_ref[...]
    o_ref[...] = acc


def _out_proj(ya, yb, yc, yd, w_all, xf, g, layer, final):
    rows, d = xf.shape
    tm = ROW_TILE
    yspec = pl.BlockSpec((tm, D_GROUP), lambda i: (i, 0))
    return pl.pallas_call(
        functools.partial(_out_proj_kernel, final=final),
        out_shape=jax.ShapeDtypeStruct((rows, d), F32),
        grid=(rows // tm,),
        in_specs=[yspec, yspec, yspec, yspec,
                  pl.BlockSpec((None, 4 * D_GROUP, d), lambda i: (layer, 0, 0)),
                  pl.BlockSpec((tm, d), lambda i: (i, 0)),
                  pl.BlockSpec((1, d), lambda i: (0, 0))],
        out_specs=pl.BlockSpec((tm, d), lambda i: (i, 0)),
        compiler_params=pltpu.CompilerParams(
            dimension_semantics=("parallel",), vmem_limit_bytes=VMEM_LIMIT),
        name="out_proj",
    )(ya, yb, yc, yd, w_all, xf, g.reshape(1, d))


def _level_ref(b, h, reverse):
    n_rows, n = b.shape
    blk = 2 * h
    r0 = h if reverse else h - 1
    if blk >= SUBLANE:
        pieces = [jnp.broadcast_to(b[s + r0:s + r0 + 1, :], (blk, n)) for s in range(0, n_rows, blk)]
        return pieces[0] if len(pieces) == 1 else jnp.concatenate(pieces, axis=0)
    res = lax.broadcasted_iota(jnp.int32, b.shape, 0) & (blk - 1)
    out = None
    for m in range(blk):
        shift = (m - r0) % n_rows
        cand = b if shift == 0 else pltpu.roll(b, shift, axis=0)
        out = cand if out is None else jnp.where(res == m, cand, out)
    return out


def _hgrn_lower_bound(lbl_ref, layer, depth):
    rows = [lbl_ref[j:j + 1, :] for j in range(depth)]
    mx = functools.reduce(jnp.maximum, rows)
    es = [jnp.exp(r - mx) for r in rows]
    tot = functools.reduce(lambda a, c: a + c, es)
    lb = jnp.zeros_like(rows[0])
    for j in range(1, layer + 1):
        lb = lb + es[j] / tot
    return lb


def _cumsum_rows_mxu(x, reverse):
    n = x.shape[0]
    r = lax.broadcasted_iota(jnp.int32, (n, n), 0)
    c = lax.broadcasted_iota(jnp.int32, (n, n), 1)
    tri = jnp.where((c >= r) if reverse else (c <= r), 1.0, 0.0).astype(BF16)
    hi = x.astype(BF16)
    r1 = x - hi.astype(F32)
    mid = r1.astype(BF16)
    lo = (r1 - mid.astype(F32)).astype(BF16)
    return _dot(jnp.concatenate([tri, tri, tri], axis=1), jnp.concatenate([hi, mid, lo], axis=0))


def _hgrn_level_operand(q, k, f, b2, h, reverse):
    n_rows = q.shape[0]
    blk = 2 * h
    r0 = h if reverse else h - 1
    if h >= SUBLANE:
        pieces = []
        for s0 in range(0, n_rows, blk):
            ref = b2[s0 + r0:s0 + r0 + 1, :]
            lo, up = slice(s0, s0 + h), slice(s0 + h, s0 + blk)
            if reverse:
                pieces += [q[lo] * jnp.exp2(b2[lo] - ref), k[up] * jnp.exp2(ref - b2[up])]
            else:
                pieces += [k[lo] * jnp.exp2(ref - b2[lo]), q[up] * jnp.exp2(b2[up] - ref)]
        return jnp.concatenate(pieces, axis=0)
    row = lax.broadcasted_iota(jnp.int32, q.shape, 0)
    is_q = ((row & h) == 0) if reverse else ((row & h) != 0)
    qk = jnp.where(is_q, q, k)
    if h == 1:
        return jnp.where(is_q, q * f, k)
    if h == 2:
        res = row & 3
        f_dn = pltpu.roll(f, 1, axis=0)
        f_up = pltpu.roll(f, n_rows - 1, axis=0)
        if reverse:
            e = jnp.where(res == 0, f * f_up, jnp.where(res == 1, f, jnp.where(res == 2, 1.0, f_dn)))
        else:
            e = jnp.where(res == 0, f_up, jnp.where(res == 1, 1.0, jnp.where(res == 2, f, f_dn * f)))
        return qk * e
    return qk * jnp.exp2(-jnp.abs(b2 - _level_ref(b2, h, reverse)))


def _hgrn_kernel(*refs, layer, depth, reverse, final):
    if final:
        q_ref, v_ref, zg_ref, lbl_ref, z_ref, ob_ref, ng_ref, out_ref, st_ref = refs
    else:
        q_ref, v_ref, zg_ref, lbl_ref, out_ref, st_ref = refs

    @pl.when(pl.program_id(1) == 0)
    def _():
        st_ref[...] = jnp.zeros_like(st_ref)

    rows = range(q_ref.shape[0])
    n_rows = q_ref.shape[1]
    ti = lax.broadcasted_iota(jnp.int32, (n_rows, n_rows), 0)
    si = lax.broadcasted_iota(jnp.int32, (n_rows, n_rows), 1)
    ahead = (ti < si) if reverse else (ti > si)
    level = jnp.where(ahead, 31 - lax.clz(ti ^ si), jnp.where(ti == si, -1, -2))
    preps = [_hgrn_prep(q_ref.at[g], zg_ref.at[g], lbl_ref, layer, depth, reverse) for g in rows]
    outs = [[] for _ in rows]
    for hd in range(A_HEADS):
        for g in rows:
            outs[g].append(_hgrn_head(hd, preps[g], v_ref.at[g], st_ref.at[g], level, reverse))
    for g in rows:
        o = jnp.concatenate(outs[g], axis=1)
        if final:
            _hgrn_finish(o, z_ref.at[g], ob_ref.at[g], ng_ref.at[layer:layer + 1], out_ref.at[g])
        else:
            out_ref[g] = o


def _hgrn_prep(q_ref, zg_ref, lbl_ref, layer, depth, reverse):
    n_rows = q_ref.shape[0]
    q = q_ref[...]
    zg = zg_ref[...]
    e = jnp.exp(-jnp.abs(zg))
    r = 1.0 / (1.0 + e)
    er = e * r
    pos = zg >= 0.0
    f = jnp.where(pos, r, er)
    k = jnp.where(pos, er, r)
    if layer > 0:
        lb = _hgrn_lower_bound(lbl_ref, layer, depth)
        f = lb + (1.0 - lb) * f
        k = (1.0 - lb) * k
    b2 = _cumsum_rows_mxu(jnp.log2(f), reverse)
    return q, k, f, b2


def _hgrn_head(hd, prep, v_ref, st_ref, level, reverse):
    sl = slice(hd * A_HD, (hd + 1) * A_HD)
    q, k, f, b2 = (a[:, sl] for a in prep)
    n_rows = q.shape[0]
    tot = b2[0:1, :] if reverse else b2[n_rows - 1:n_rows, :]
    v_bf = v_ref[:, sl].astype(BF16)
    q_in = (q * jnp.exp2(b2)).astype(BF16)
    k_out = (k * jnp.exp2(tot - b2)).astype(BF16)

    att = jnp.where(level == -1, _dot_nt(q.astype(BF16), k.astype(BF16)), 0.0)
    h = n_rows // 2
    while h >= 1:
        c = _hgrn_level_operand(q, k, f, b2, h, reverse).astype(BF16)
        att = jnp.where(level == (h.bit_length() - 1), _dot_nt(c, c), att)
        h //= 2
    st = st_ref[hd]
    o = _dot(att.astype(BF16), v_bf) + _dot_nt(q_in, st.astype(BF16))
    st_ref[hd] = jnp.exp2(tot) * st + _dot_tn(v_bf, k_out)
    return o


def _hgrn_finish(o, z_ref, ob_ref, ng_ref, out_ref):
    o = o + ob_ref[...]
    ys = []
    for hd in range(A_HEADS):
        sl = slice(hd * A_HD, (hd + 1) * A_HD)
        oh = o[:, sl]
        ms = jnp.mean(oh * oh, axis=-1, keepdims=True)
        ys.append(oh * lax.rsqrt(ms + EPS) * ng_ref[:, sl])
    y = jnp.concatenate(ys, axis=1) * _silu(z_ref[...])
    out_ref[...] = y.astype(out_ref.dtype)


def _chunk_index(c, n_chunks, reverse):
    return n_chunks - 1 - c if reverse else c


def _hgrn(u3, lb_logits, layer, *, reverse, ob=None, norm_g=None):
    batch, seq, _ = u3.shape
    n_chunks = seq // CHUNK
    depth = lb_logits.shape[0]
    final = ob is not None
    grp = REC_GROUP

    def ublk(blk):
        return pl.BlockSpec((grp, CHUNK, D_GROUP),
                            lambda bg, c: (bg, _chunk_index(c, n_chunks, reverse), blk))

    row_spec = ublk(0)
    in_specs = [ublk(BLK_A_Q), ublk(BLK_A_I), ublk(BLK_A_FB if reverse else BLK_A_FF),
                pl.BlockSpec((depth, D_GROUP), lambda bg, c: (0, 0))]
    args = [u3, u3, u3, lb_logits]
    if final:
        in_specs += [ublk(BLK_A_Z), row_spec, pl.BlockSpec(norm_g.shape, lambda bg, c: (0, 0))]
        args += [u3, ob, norm_g]
    return pl.pallas_call(
        functools.partial(_hgrn_kernel, layer=layer, depth=depth, reverse=reverse, final=final),
        out_shape=jax.ShapeDtypeStruct((batch, seq, D_GROUP), BF16 if final else F32),
        grid=(batch // grp, n_chunks),
        in_specs=in_specs,
        out_specs=row_spec,
        scratch_shapes=[pltpu.VMEM((grp, A_HEADS, A_HD, A_HD), F32)],
        compiler_params=pltpu.CompilerParams(
            dimension_semantics=("parallel", "arbitrary"), vmem_limit_bytes=VMEM_LIMIT),
        name="hgrn_fwd" if final else "hgrn_bwd",
    )(*args)


def _split3(x):
    hi = x.astype(BF16).astype(F32)
    r1 = x - hi
    mid = r1.astype(BF16).astype(F32)
    lo = (r1 - mid).astype(BF16).astype(F32)
    return hi, mid, lo


def _cumsum_lanes_mxu(x, reverse):
    n = x.shape[1]
    j = lax.broadcasted_iota(jnp.int32, (n, n), 0)
    t = lax.broadcasted_iota(jnp.int32, (n, n), 1)
    tri = jnp.where((j >= t) if reverse else (j <= t), 1.0, 0.0).astype(BF16)
    pieces = jnp.concatenate(list(_split3(x)), axis=1).astype(BF16)
    return _dot(pieces, jnp.concatenate([tri, tri, tri], axis=0))


def _conv_silu_qk(qkp_ref, qkc_ref, qkn_ref, cw_ref, reverse):
    n_rows = qkc_ref.shape[0]
    c = pl.program_id(1)
    n_chunks = pl.num_programs(1)
    cc = _chunk_index(c, n_chunks, reverse)
    halo = SUBLANE
    prev = qkp_ref[n_rows - halo:n_rows, :] * (cc > 0).astype(F32)
    nxt = qkn_ref[0:halo, :] * (cc < n_chunks - 1).astype(F32)
    xcat = jnp.concatenate([prev, qkc_ref[...], nxt], axis=0)
    acc = None
    for j in range(CONV_W):
        off = halo + j - CONV_W // 2
        term = cw_ref[j:j + 1, :] * xcat[off:off + n_rows, :]
        acc = term if acc is None else acc + term
    qk = _silu(acc)
    n_qk = C_HEADS * C_QK
    return jnp.concatenate([qk[:, :n_qk] * (C_QK ** -0.5), qk[:, n_qk:]], axis=1).astype(BF16)


def _mlstm_kernel(*refs, layer, reverse, final):
    if final:
        (qk_ref, v_ref, gt_ref, gbc_ref, og_ref, z_ref, hb_ref, ng_ref,
         out_ref, cst_ref, m_ref) = refs
    else:
        (qkp_ref, qkc_ref, qkn_ref, cw_ref, v_ref, gt_ref, gbc_ref,
         out_ref, qko_ref, cst_ref, m_ref) = refs

    @pl.when(pl.program_id(1) == 0)
    def _():
        cst_ref[...] = jnp.zeros_like(cst_ref)
        m_ref[...] = jnp.zeros_like(m_ref)

    rows = range(v_ref.shape[0])
    if final:
        qks = [qk_ref[g] for g in rows]
    else:
        qks = [_conv_silu_qk(qkp_ref.at[g], qkc_ref.at[g], qkn_ref.at[g], cw_ref.at[layer], reverse)
               for g in rows]
        for g in rows:
            qko_ref[g] = qks[g]
    gates = [_mlstm_gates(gt_ref.at[g], gbc_ref.at[layer], m_ref.at[g], reverse) for g in rows]
    outs = [[] for _ in rows]
    for hd in range(C_HEADS):
        for g in rows:
            outs[g].append(_mlstm_head(hd, qks[g], v_ref.at[g], gates[g], cst_ref.at[g], reverse))
    for g in rows:
        hcat = jnp.concatenate(outs[g], axis=1)
        if final:
            _mlstm_finish(hcat, og_ref.at[g], z_ref.at[g], hb_ref.at[g],
                          ng_ref.at[layer:layer + 1], out_ref.at[g])
        else:
            out_ref[g] = hcat


def _mlstm_gates(gt_ref, gbc_ref, m_ref, reverse):
    n_rows = gt_ref.shape[1]
    n_dir = 2 * C_HEADS

    gr = gt_ref[...] + gbc_ref[...]
    li = gr[0:n_dir, :]
    b = _cumsum_lanes_mxu(_log_sigmoid(gr[n_dir:, :]), reverse)
    b_last = b[:, 0:1] if reverse else b[:, n_rows - 1:n_rows]
    m_prev = m_ref[...]
    d_inter = b + m_prev
    a = (b_last - b) + li
    m_new = jnp.maximum(b_last + m_prev, jnp.max(a, axis=1, keepdims=True))
    ws = jnp.exp(a - m_new)
    decay = jnp.exp((b_last + m_prev) - m_new)
    m_ref[...] = m_new

    one8 = jnp.ones((n_dir, n_rows), F32)
    lhs = jnp.concatenate(list(_split3(li - b)) + [one8, one8, one8], axis=0).T.astype(BF16)
    rhs_all = jnp.concatenate([one8, one8, one8] + list(_split3(b)), axis=0)
    return lhs, rhs_all, d_inter, ws, decay


def _mlstm_head(hd, qk, v_ref, gates, cst_ref, reverse):
    lhs, rhs_all, d_inter, ws, decay = gates
    n_rows = v_ref.shape[0]
    n_qk = C_HEADS * C_QK
    n_dir = 2 * C_HEADS
    gi = (C_HEADS if reverse else 0) + hd
    gate_row = lax.broadcasted_iota(jnp.int32, rhs_all.shape, 0) & (n_dir - 1)
    si = lax.broadcasted_iota(jnp.int32, (n_rows, n_rows), 0)
    ti = lax.broadcasted_iota(jnp.int32, (n_rows, n_rows), 1)
    causal = (si >= ti) if reverse else (si <= ti)

    q_h = qk[:, hd * C_QK:(hd + 1) * C_QK]
    k_h = qk[:, n_qk + hd * C_QK:n_qk + (hd + 1) * C_QK]
    rhs = jnp.where(gate_row == gi, rhs_all, 0.0).astype(BF16)
    d = jnp.where(causal, _dot(lhs, rhs), NEG)
    d_int = d_inter[gi:gi + 1, :]
    m_t = jnp.maximum(jnp.max(d, axis=0, keepdims=True), d_int)
    w = (_dot_nt(k_h, q_h) * jnp.exp(d - m_t)).astype(BF16)
    g = jnp.exp(d_int - m_t)
    v_t = jnp.concatenate([v_ref[:, hd * C_V:(hd + 1) * C_V].T, jnp.ones((C_ONES, n_rows), F32)],
                          axis=0)
    cst = cst_ref[hd]
    num = _dot(v_t.astype(BF16), w) + g * _dot_nt(cst.astype(BF16), q_h)
    den = jnp.maximum(jnp.abs(num[C_V:C_V + 1, :]), jnp.exp(-m_t))
    cst_ref[hd] = (decay[gi:gi + 1, :C_QK] * cst
                   + _dot((v_t * ws[gi:gi + 1, :]).astype(BF16), k_h))
    return (num[:C_V, :] / den).T


def _mlstm_finish(hcat, og_ref, z_ref, hb_ref, ng_ref, out_ref):
    hsum = (hcat + hb_ref[...]) * _sigmoid(og_ref[...])
    ys = []
    for hd in range(C_HEADS):
        sl = slice(hd * C_V, (hd + 1) * C_V)
        hh = hsum[:, sl]
        mu = jnp.mean(hh, axis=-1, keepdims=True)
        var = jnp.mean(jnp.square(hh - mu), axis=-1, keepdims=True)
        ys.append((hh - mu) * lax.rsqrt(var + EPS) * ng_ref[:, sl])
    y = jnp.concatenate(ys, axis=1) * _silu(z_ref[...])
    out_ref[...] = y.astype(out_ref.dtype)


def _mlstm(u3, gt, conv_w, gate_b, layer, *, reverse, qk=None, hb=None, norm_g=None):
    batch, seq, _ = u3.shape
    n_chunks = seq // CHUNK
    final = hb is not None
    grp = REC_GROUP
    assert CHUNK == LANE

    def chunk(c, delta=0):
        return jnp.clip(_chunk_index(c, n_chunks, reverse) + delta, 0, n_chunks - 1)

    def ublk(blk, delta=0):
        return pl.BlockSpec((grp, CHUNK, D_GROUP), lambda bg, c: (bg, chunk(c, delta), blk))

    row_spec = ublk(0)
    gate_specs = [pl.BlockSpec((grp, N_GATE, CHUNK), lambda bg, c: (bg, 0, chunk(c))),
                  pl.BlockSpec(gate_b.shape, lambda bg, c: (0, 0, 0))]
    gate_args = [gt, gate_b]
    if final:
        in_specs = [row_spec, ublk(BLK_C_V)] + gate_specs + [
            ublk(BLK_C_O), ublk(BLK_C_Z), row_spec,
            pl.BlockSpec(norm_g.shape, lambda bg, c: (0, 0))]
        args = [qk, u3] + gate_args + [u3, u3, hb, norm_g]
        out_shape = jax.ShapeDtypeStruct((batch, seq, D_GROUP), BF16)
        out_specs = row_spec
    else:
        in_specs = [ublk(BLK_C_QK, -1), ublk(BLK_C_QK), ublk(BLK_C_QK, 1),
                    pl.BlockSpec(conv_w.shape, lambda bg, c: (0, 0, 0)),
                    ublk(BLK_C_V)] + gate_specs
        args = [u3, u3, u3, conv_w, u3] + gate_args
        out_shape = (jax.ShapeDtypeStruct((batch, seq, D_GROUP), F32),
                     jax.ShapeDtypeStruct((batch, seq, D_GROUP), BF16))
        out_specs = (row_spec, row_spec)
    return pl.pallas_call(
        functools.partial(_mlstm_kernel, layer=layer, reverse=reverse, final=final),
        out_shape=out_shape,
        grid=(batch // grp, n_chunks),
        in_specs=in_specs,
        out_specs=out_specs,
        scratch_shapes=[pltpu.VMEM((grp, C_HEADS, C_V + C_ONES, C_QK), F32),
                        pltpu.VMEM((grp, 2 * C_HEADS, CHUNK), F32)],
        compiler_params=pltpu.CompilerParams(
            dimension_semantics=("parallel", "arbitrary"), vmem_limit_bytes=VMEM_LIMIT),
        name="mlstm_fwd" if final else "mlstm_bwd",
    )(*args)


def _rope_table_kernel(pos_ref, invf_ref, cos_ref, sin_ref):
    ang = pos_ref[...].astype(F32) * invf_ref[...]
    r = lax.broadcasted_iota(jnp.int32, ang.shape, 1) & (B_HD - 1)
    half = ROPE_DIM // 2
    sn = jnp.sin(ang)
    cos_ref[...] = jnp.where(r < ROPE_DIM, jnp.cos(ang), 1.0)
    sin_ref[...] = jnp.where(r < half, -sn, jnp.where(r < ROPE_DIM, sn, 0.0))


def _rope_tables(positions):
    rows = positions.size
    half = ROPE_DIM // 2
    inv_freq = ROPE_THETA ** (-jnp.arange(0, ROPE_DIM, 2, dtype=F32) / ROPE_DIM)
    lane = np.arange(LANE) % B_HD
    invf = jnp.where(lane < ROPE_DIM, inv_freq[lane % half], 0.0).astype(F32).reshape(1, LANE)
    tm = ROW_TILE
    return pl.pallas_call(
        _rope_table_kernel,
        out_shape=(jax.ShapeDtypeStruct((rows, LANE), F32),) * 2,
        grid=(rows // tm,),
        in_specs=[pl.BlockSpec((tm, 1), lambda i: (i, 0)),
                  pl.BlockSpec((1, LANE), lambda i: (0, 0))],
        out_specs=(pl.BlockSpec((tm, LANE), lambda i: (i, 0)),) * 2,
        compiler_params=pltpu.CompilerParams(dimension_semantics=("parallel",)),
        name="rope_tables",
    )(positions.reshape(rows, 1), invf)


def _rope(x, cos, sin):
    width = x.shape[1]
    if width > LANE:
        cos = jnp.concatenate([cos] * (width // LANE), axis=1)
        sin = jnp.concatenate([sin] * (width // LANE), axis=1)
    half = ROPE_DIM // 2
    r = lax.broadcasted_iota(jnp.int32, x.shape, 1) & (B_HD - 1)
    partner = jnp.where(r < half, pltpu.roll(x, width - half, axis=1), pltpu.roll(x, half, axis=1))
    return x * cos + partner * sin


def _swa_kernel(q_ref, z_ref, mp_ref, mc_ref, mn_ref, cp_ref, cc_ref, cn_ref,
                sp_ref, sc_ref, sn_ref, sink_ref, out_ref, *, layer):
    tile = q_ref.shape[0]
    blk = mp_ref.shape[0]
    n_sub = tile // blk
    i = pl.program_id(1)
    n_tiles = pl.num_programs(1)
    kv_w = B_KV_HEADS * B_HD
    grp = B_HEADS // B_KV_HEADS

    q = (_rope(q_ref[...], cc_ref[...], sc_ref[...]) * (B_HD ** -0.5)).astype(BF16)
    k = jnp.concatenate([
        _rope(mp_ref[:, MISC_K:MISC_K + kv_w], cp_ref[...], sp_ref[...]),
        _rope(mc_ref[:, MISC_K:MISC_K + kv_w], cc_ref[...], sc_ref[...]),
        _rope(mn_ref[:, MISC_K:MISC_K + kv_w], cn_ref[...], sn_ref[...])], axis=0).astype(BF16)
    v = jnp.concatenate([mp_ref[:, MISC_V:MISC_V + kv_w], mc_ref[:, MISC_V:MISC_V + kv_w],
                         mn_ref[:, MISC_V:MISC_V + kv_w]], axis=0)
    n_keys = v.shape[0]

    kr = lax.broadcasted_iota(jnp.int32, (3 * blk, blk), 0)
    qc = lax.broadcasted_iota(jnp.int32, (3 * blk, blk), 1)
    in_window = jnp.abs(qc - (kr - blk)) <= WINDOW
    biases = []
    for j in range(n_sub):
        valid = in_window
        if j == 0:
            valid = valid & (kr >= jnp.where(i > 0, 0, blk))
        if j == n_sub - 1:
            valid = valid & (kr < jnp.where(i < n_tiles - 1, 3 * blk, 2 * blk))
        biases.append(jnp.where(valid, 0.0, NEG))

    ones = jnp.ones((n_keys, B_HD), F32)
    vts = [jnp.concatenate([v[:, kvh * B_HD:(kvh + 1) * B_HD], ones], axis=1).T.astype(BF16)
           for kvh in range(B_KV_HEADS)]

    tiles = [[] for _ in range(n_sub)]
    pairs = [[] for _ in range(n_sub)]
    for hq in range(B_HEADS):
        kvh = hq // grp
        ksl = slice(kvh * B_HD, (kvh + 1) * B_HD)
        sk = sink_ref[layer:layer + 1, hq:hq + 1]
        for j in range(n_sub):
            keys = slice(j * blk, (j + 3) * blk)
            qh = q[j * blk:(j + 1) * blk, hq * B_HD:(hq + 1) * B_HD]
            s = _dot_nt(k[keys, ksl], qh) + biases[j]
            m = jnp.maximum(jnp.max(s, axis=0, keepdims=True), sk)
            p = jnp.exp(s - m).astype(BF16)
            o = _dot(vts[kvh][:, keys], p)
            denom = o[B_HD:B_HD + 1, :] + jnp.exp(sk - m)
            pairs[j].append(o[:B_HD, :] / denom)
            if len(pairs[j]) == LANE // B_HD:
                tiles[j].append(jnp.concatenate(pairs[j], axis=0).T)
                pairs[j] = []
    for j in range(n_sub):
        rows = slice(j * blk, (j + 1) * blk)
        y = jnp.concatenate(tiles[j], axis=1) * _silu(z_ref[rows, :])
        out_ref[rows, :] = y.astype(out_ref.dtype)


def _swa(u, cos_t, sin_t, sink, layer, batch):
    rows = u.shape[0]
    n_blocks = rows // batch // CHUNK
    n_sub = SWA_TILE // CHUNK
    n_tiles = n_blocks // n_sub

    def tile_spec(width, blk):
        return pl.BlockSpec((SWA_TILE, width), lambda bi, i: (bi * n_tiles + i, blk))

    def halo_spec(width, blk, after):
        def index(bi, i):
            nb = (i + 1) * n_sub if after else i * n_sub - 1
            return (bi * n_blocks + jnp.clip(nb, 0, n_blocks - 1), blk)
        return pl.BlockSpec((CHUNK, width), index)

    def table_specs():
        return [halo_spec(LANE, 0, False), tile_spec(LANE, 0), halo_spec(LANE, 0, True)]

    return pl.pallas_call(
        functools.partial(_swa_kernel, layer=layer),
        out_shape=jax.ShapeDtypeStruct((rows, D_GROUP), BF16),
        grid=(batch, n_tiles),
        in_specs=[tile_spec(D_GROUP, BLK_B_Q), tile_spec(D_GROUP, BLK_B_Z),
                  halo_spec(D_GROUP, BLK_MISC, False), tile_spec(D_GROUP, BLK_MISC),
                  halo_spec(D_GROUP, BLK_MISC, True)] + table_specs() + table_specs() + [
                  pl.BlockSpec(sink.shape, lambda bi, i: (0, 0))],
        out_specs=tile_spec(D_GROUP, 0),
        compiler_params=pltpu.CompilerParams(
            dimension_semantics=("parallel", "parallel"), vmem_limit_bytes=VMEM_LIMIT),
        name="window_attn",
    )(u, u, u, u, u, cos_t, cos_t, cos_t, sin_t, sin_t, sin_t, sink)


def _mem_kv_kernel(mem_ref, g_ref, w_ref, kv_ref, *, layer):
    x = mem_ref[...]
    ms = jnp.mean(x * x, axis=-1, keepdims=True)
    h = (x * lax.rsqrt(ms + EPS) * g_ref[layer:layer + 1, :]).astype(BF16)
    kv_ref[...] = _dot(h, w_ref[...]).astype(kv_ref.dtype)


def _mem_kv(mem, g_all, w_all, layer):
    batch, m_len, d = mem.shape
    n_out = w_all.shape[2]
    return pl.pallas_call(
        functools.partial(_mem_kv_kernel, layer=layer),
        out_shape=jax.ShapeDtypeStruct((batch * m_len, n_out), BF16),
        grid=(batch,),
        in_specs=[pl.BlockSpec((m_len, d), lambda bi: (bi, 0)),
                  pl.BlockSpec(g_all.shape, lambda bi: (0, 0)),
                  pl.BlockSpec((None, d, n_out), lambda bi: (layer, 0, 0))],
        out_specs=pl.BlockSpec((m_len, n_out), lambda bi: (bi, 0)),
        compiler_params=pltpu.CompilerParams(
            dimension_semantics=("parallel",), vmem_limit_bytes=VMEM_LIMIT),
        name="mem_kv",
    )(mem.reshape(batch * m_len, d), g_all, w_all)


def _mem_attn_kernel(q_ref, z_ref, kv_ref, out_ref):
    q = q_ref[...].astype(BF16)
    outs = []
    for hd in range(D_HEADS):
        sl = slice(hd * D_HD, (hd + 1) * D_HD)
        k = kv_ref[:, hd * D_HD:(hd + 1) * D_HD]
        v = kv_ref[:, D_GROUP + hd * D_HD:D_GROUP + (hd + 1) * D_HD]
        s = _dot_nt(q[:, sl], k) * (D_HD ** -0.5)
        m = jnp.max(s, axis=-1, keepdims=True)
        p = jnp.exp(s - m)
        p = p / jnp.sum(p, axis=-1, keepdims=True)
        outs.append(_dot(p.astype(BF16), v))
    y = jnp.concatenate(outs, axis=1) * _silu(z_ref[...])
    out_ref[...] = y.astype(out_ref.dtype)


def _mem_attn(u, kv, batch):
    rows = u.shape[0]
    tm = ROW_TILE
    per_batch = rows // batch // tm
    m_len = kv.shape[0] // batch
    return pl.pallas_call(
        _mem_attn_kernel,
        out_shape=jax.ShapeDtypeStruct((rows, D_GROUP), BF16),
        grid=(batch, per_batch),
        in_specs=[pl.BlockSpec((tm, D_GROUP), lambda bi, i: (bi * per_batch + i, BLK_D_Q)),
                  pl.BlockSpec((tm, D_GROUP), lambda bi, i: (bi * per_batch + i, BLK_D_Z)),
                  pl.BlockSpec((m_len, kv.shape[1]), lambda bi, i: (bi, 0))],
        out_specs=pl.BlockSpec((tm, D_GROUP), lambda bi, i: (bi * per_batch + i, 0)),
        compiler_params=pltpu.CompilerParams(
            dimension_semantics=("parallel", "parallel"), vmem_limit_bytes=VMEM_LIMIT),
        name="mem_attn",
    )(u, u, kv)


def kernel(x, mem, positions, norm_g, w_in, hgrn_lb_logits, hgrn_norm_g, attn_sink, mlstm_conv_w,
           mlstm_gate_b, mlstm_norm_g, mem_norm_g, w_mem_kv, w_out, final_norm_g):
    batch, seq, d = x.shape
    depth = w_in.shape[0]
    assert seq % CHUNK == 0 and seq % ROW_TILE == 0 and seq % IN_ROW_TILE == 0
    assert batch % REC_GROUP == 0
    xf = x.reshape(batch * seq, d)
    cos_t, sin_t = _rope_tables(positions)
    w_cat_all, w_gt_all = _prep_w_in(w_in)
    w_out_all = w_out.astype(BF16)
    w_kv_all = w_mem_kv.astype(BF16)
    gate_b = mlstm_gate_b.reshape(depth, N_GATE, 1)
    for layer in range(depth):
        u, gt = _in_proj(xf, norm_g, w_cat_all, w_gt_all, layer, batch)
        u3 = u.reshape(batch, seq, u.shape[1])
        ob = _hgrn(u3, hgrn_lb_logits, layer, reverse=True)
        ya = _hgrn(u3, hgrn_lb_logits, layer, reverse=False, ob=ob, norm_g=hgrn_norm_g)
        yb = _swa(u, cos_t, sin_t, attn_sink, layer, batch)
        hb, qk_c = _mlstm(u3, gt, mlstm_conv_w, gate_b, layer, reverse=True)
        yc = _mlstm(u3, gt, None, gate_b, layer, reverse=False,
                    qk=qk_c, hb=hb, norm_g=mlstm_norm_g)
        kv = _mem_kv(mem, mem_norm_g, w_kv_all, layer)
        yd = _mem_attn(u, kv, batch)
        xf = _out_proj(ya.reshape(batch * seq, D_GROUP), yb, yc.reshape(batch * seq, D_GROUP), yd,
                       w_out_all, xf, final_norm_g, layer, final=(layer == depth - 1))
    return xf.reshape(batch, seq, d)
```

```python
import functools

import jax
import jax.numpy as jnp
import numpy as np
from jax import lax
from jax.experimental import pallas as pl
from jax.experimental.pallas import tpu as pltpu

F32 = jnp.float32
BF16 = jnp.bfloat16

D_GROUP = 512
A_HEADS, A_HD = 4, 128
B_HEADS, B_KV_HEADS, B_HD = 8, 2, 64
WINDOW = 128
ROPE_THETA = 500000.0
ROPE_DIM = 16
C_HEADS, C_QK, C_V = 4, 64, 128
CONV_W = 5
D_HEADS, D_HD = 4, 128
EPS = 1e-6
IN_SIZES = (
    D_GROUP, D_GROUP, D_GROUP, D_GROUP, D_GROUP,
    B_HEADS * B_HD, B_KV_HEADS * B_HD, B_KV_HEADS * B_HD, D_GROUP,
    C_HEADS * C_QK, C_HEADS * C_QK, C_HEADS * C_V, D_GROUP, D_GROUP,
    2 * C_HEADS, 2 * C_HEADS,
    D_GROUP, D_GROUP,
)
SPLIT_POINTS = tuple(int(s) for s in np.cumsum(IN_SIZES)[:-1])

LANE = 128
SUBLANE = 8
VMEM_LIMIT = 48 * 1024 * 1024

CHUNK = 128
ROW_TILE = 512
IN_ROW_TILE = 256
SWA_TILE = 512
REC_GROUP = 4
NEG = -1e30
LOG2_E = 1.4426950408889634

(BLK_A_Q, BLK_A_I, BLK_A_FF, BLK_A_FB, BLK_A_Z, BLK_B_Q, BLK_B_Z, BLK_C_QK, BLK_C_V, BLK_C_O,
 BLK_C_Z, BLK_D_Q, BLK_D_Z, BLK_MISC) = range(14)
N_BLK = 14
MISC_K, MISC_V, MISC_G = 0, 128, 256
N_GATE = 4 * C_HEADS
C_ONES = 16


def _dot(a, b):
    return jnp.dot(a, b, preferred_element_type=F32)


def _dot_nt(a, b):
    return lax.dot_general(a, b, (((1,), (1,)), ((), ())), preferred_element_type=F32)


def _dot_tn(a, b):
    return lax.dot_general(a, b, (((0,), (0,)), ((), ())), preferred_element_type=F32)


def _sigmoid(x):
    return 1.0 / (1.0 + jnp.exp(-x))


def _silu(x):
    return x * _sigmoid(x)


def _log_sigmoid(x):
    return jnp.minimum(x, 0.0) - jnp.log(1.0 + jnp.exp(-jnp.abs(x)))


def _cumsum_rows(x, reverse):
    n = x.shape[0]
    row = lax.broadcasted_iota(jnp.int32, x.shape, 0)
    s = 1
    while s < n:
        if reverse:
            x = x + jnp.where(row < n - s, pltpu.roll(x, n - s, axis=0), 0.0)
        else:
            x = x + jnp.where(row >= s, pltpu.roll(x, s, axis=0), 0.0)
        s *= 2
    return x


def _cumsum_lanes(x, reverse):
    n = x.shape[1]
    col = lax.broadcasted_iota(jnp.int32, x.shape, 1)
    s = 1
    while s < n:
        if reverse:
            x = x + jnp.where(col < n - s, pltpu.roll(x, n - s, axis=1), 0.0)
        else:
            x = x + jnp.where(col >= s, pltpu.roll(x, s, axis=1), 0.0)
        s *= 2
    return x


def _in_proj_kernel(x_ref, g_ref, w_ref, wgt_ref, u_ref, gt_ref, *, layer):
    x = x_ref[...]
    ms = jnp.mean(x * x, axis=-1, keepdims=True)
    h = (x * lax.rsqrt(ms + EPS) * g_ref[layer:layer + 1, :]).astype(BF16)
    gt_ref[...] = _dot_nt(wgt_ref[...], h)
    for j in range(N_BLK):
        sl = slice(j * D_GROUP, (j + 1) * D_GROUP)
        u_ref[:, sl] = _dot(h, w_ref[:, sl])


def _in_proj(xf, g_all, w_cat_all, w_gt_all, layer, batch):
    rows, d = xf.shape
    tm = IN_ROW_TILE
    n_out = N_BLK * D_GROUP
    per_batch = rows // batch // tm
    return pl.pallas_call(
        functools.partial(_in_proj_kernel, layer=layer),
        out_shape=(jax.ShapeDtypeStruct((rows, n_out), F32),
                   jax.ShapeDtypeStruct((batch, N_GATE, rows // batch), F32)),
        grid=(rows // tm,),
        in_specs=[
            pl.BlockSpec((tm, d), lambda i: (i, 0)),
            pl.BlockSpec(g_all.shape, lambda i: (0, 0)),
            pl.BlockSpec((None, d, n_out), lambda i: (layer, 0, 0), pipeline_mode=pl.Buffered(1)),
            pl.BlockSpec((None, N_GATE, d), lambda i: (layer, 0, 0)),
        ],
        out_specs=(
            pl.BlockSpec((tm, n_out), lambda i: (i, 0)),
            pl.BlockSpec((None, N_GATE, tm), lambda i: (i // per_batch, 0, i % per_batch)),
        ),
        compiler_params=pltpu.CompilerParams(
            dimension_semantics=("parallel",), vmem_limit_bytes=VMEM_LIMIT),
        name="in_proj",
    )(xf, g_all, w_cat_all, w_gt_all)


def _w_in_segments():
    names = ("a_q", "a_i", "a_ff", "a_fb", "a_z", "b_q", "b_k", "b_v", "b_z",
             "c_q", "c_k", "c_v", "c_o", "c_z", "c_ig", "c_fg", "d_q", "d_z")
    src = dict(zip(names, zip((0,) + SPLIT_POINTS, IN_SIZES)))
    order = ("a_q", "a_i", "a_ff", "a_fb", "a_z", "b_q", "b_z", "c_q", "c_k", "c_v", "c_o", "c_z",
             "d_q", "d_z", "b_k", "b_v", "c_ig", "c_fg")
    segs, dst = [], 0
    for name in order:
        off, width = src[name]
        if segs and segs[-1][0] + segs[-1][1] == off:
            segs[-1] = (segs[-1][0], segs[-1][1] + width, segs[-1][2])
        else:
            segs.append((off, width, dst))
        dst += width
    return tuple(segs), dst, src["c_ig"][0]


def _w_prep_kernel(w_ref, wc_ref, wg_ref):
    segs, used, gate_off = _w_in_segments()
    for off, width, dst in segs:
        wc_ref[:, dst:dst + width] = w_ref[:, off:off + width].astype(BF16)
    wc_ref[:, used:] = jnp.zeros((wc_ref.shape[0], wc_ref.shape[1] - used), BF16)
    win = (gate_off // LANE) * LANE
    g_t = w_ref[:, win:win + LANE].T
    wg_ref[...] = g_t[gate_off - win:gate_off - win + N_GATE, :].astype(BF16)


def _prep_w_in(w):
    depth, d, d_in = w.shape
    tr = IN_ROW_TILE
    n_out = N_BLK * D_GROUP
    return pl.pallas_call(
        _w_prep_kernel,
        out_shape=(jax.ShapeDtypeStruct((depth, d, n_out), BF16),
                   jax.ShapeDtypeStruct((depth, N_GATE, d), BF16)),
        grid=(depth, d // tr),
        in_specs=[pl.BlockSpec((None, tr, d_in), lambda l, i: (l, i, 0))],
        out_specs=(pl.BlockSpec((None, tr, n_out), lambda l, i: (l, i, 0)),
                   pl.BlockSpec((None, N_GATE, tr), lambda l, i: (l, 0, i))),
        compiler_params=pltpu.CompilerParams(
            dimension_semantics=("parallel", "parallel"), vmem_limit_bytes=VMEM_LIMIT),
        name="w_prep",
    )(w)


def _out_proj_kernel(ya_ref, yb_ref, yc_ref, yd_ref, w_ref, x_ref, g_ref, o_ref, *, final):
    acc = x_ref[...]
    for i, y_ref in enumerate((ya_ref, yb_ref, yc_ref, yd_ref)):
        acc = acc + _dot(y_ref[...], w_ref[i * D_GROUP:(i + 1) * D_GROUP, :].astype(BF16))
    if final:
        ms = jnp.mean(acc * acc, axis=-1, keepdims=True)
        acc = acc * lax.rsqrt(ms + EPS) * g_ref[...]
    o_ref[...] = acc


def _out_proj(ya, yb, yc, yd, w_all, xf, g, layer, final):
    rows, d = xf.shape
    tm = ROW_TILE
    yspec = pl.BlockSpec((tm, D_GROUP), lambda i: (i, 0))
    return pl.pallas_call(
        functools.partial(_out_proj_kernel, final=final),
        out_shape=jax.ShapeDtypeStruct((rows, d), F32),
        grid=(rows // tm,),
        in_specs=[yspec, yspec, yspec, yspec,
                  pl.BlockSpec((None, 4 * D_GROUP, d), lambda i: (layer, 0, 0)),
                  pl.BlockSpec((tm, d), lambda i: (i, 0)),
                  pl.BlockSpec((1, d), lambda i: (0, 0))],
        out_specs=pl.BlockSpec((tm, d), lambda i: (i, 0)),
        compiler_params=pltpu.CompilerParams(
            dimension_semantics=("parallel",), vmem_limit_bytes=VMEM_LIMIT),
        name="out_proj",
    )(ya, yb, yc, yd, w_all, xf, g.reshape(1, d))


def _level_ref(b, h, reverse):
    n_rows, n = b.shape
    blk = 2 * h
    r0 = h if reverse else h - 1
    if blk >= SUBLANE:
        pieces = [jnp.broadcast_to(b[s + r0:s + r0 + 1, :], (blk, n)) for s in range(0, n_rows, blk)]
        return pieces[0] if len(pieces) == 1 else jnp.concatenate(pieces, axis=0)
    res = lax.broadcasted_iota(jnp.int32, b.shape, 0) & (blk - 1)
    out = None
    for m in range(blk):
        shift = (m - r0) % n_rows
        cand = b if shift == 0 else pltpu.roll(b, shift, axis=0)
        out = cand if out is None else jnp.where(res == m, cand, out)
    return out


def _hgrn_lower_bound(lbl_ref, layer, depth):
    rows = [lbl_ref[j:j + 1, :] for j in range(depth)]
    mx = functools.reduce(jnp.maximum, rows)
    es = [jnp.exp(r - mx) for r in rows]
    tot = functools.reduce(lambda a, c: a + c, es)
    lb = jnp.zeros_like(rows[0])
    for j in range(1, layer + 1):
        lb = lb + es[j] / tot
    return lb


def _cumsum_rows_mxu(x, reverse):
    n = x.shape[0]
    r = lax.broadcasted_iota(jnp.int32, (n, n), 0)
    c = lax.broadcasted_iota(jnp.int32, (n, n), 1)
    tri = jnp.where((c >= r) if reverse else (c <= r), 1.0, 0.0).astype(BF16)
    hi = x.astype(BF16)
    r1 = x - hi.astype(F32)
    mid = r1.astype(BF16)
    lo = (r1 - mid.astype(F32)).astype(BF16)
    return _dot(jnp.concatenate([tri, tri, tri], axis=1), jnp.concatenate([hi, mid, lo], axis=0))


def _hgrn_level_operand(q, k, f, b2, h, reverse):
    n_rows = q.shape[0]
    blk = 2 * h
    r0 = h if reverse else h - 1
    if h >= SUBLANE:
        pieces = []
        for s0 in range(0, n_rows, blk):
            ref = b2[s0 + r0:s0 + r0 + 1, :]
            lo, up = slice(s0, s0 + h), slice(s0 + h, s0 + blk)
            if reverse:
                pieces += [q[lo] * jnp.exp2(b2[lo] - ref), k[up] * jnp.exp2(ref - b2[up])]
            else:
                pieces += [k[lo] * jnp.exp2(ref - b2[lo]), q[up] * jnp.exp2(b2[up] - ref)]
        return jnp.concatenate(pieces, axis=0)
    row = lax.broadcasted_iota(jnp.int32, q.shape, 0)
    is_q = ((row & h) == 0) if reverse else ((row & h) != 0)
    qk = jnp.where(is_q, q, k)
    if h == 1:
        return jnp.where(is_q, q * f, k)
    if h == 2:
        res = row & 3
        f_dn = pltpu.roll(f, 1, axis=0)
        f_up = pltpu.roll(f, n_rows - 1, axis=0)
        if reverse:
            e = jnp.where(res == 0, f * f_up, jnp.where(res == 1, f, jnp.where(res == 2, 1.0, f_dn)))
        else:
            e = jnp.where(res == 0, f_up, jnp.where(res == 1, 1.0, jnp.where(res == 2, f, f_dn * f)))
        return qk * e
    return qk * jnp.exp2(-jnp.abs(b2 - _level_ref(b2, h, reverse)))


def _hgrn_kernel(*refs, layer, depth, reverse, final):
    if final:
        q_ref, v_ref, zg_ref, lbl_ref, z_ref, ob_ref, ng_ref, out_ref, st_ref = refs
    else:
        q_ref, v_ref, zg_ref, lbl_ref, out_ref, st_ref = refs

    @pl.when(pl.program_id(1) == 0)
    def _():
        st_ref[...] = jnp.zeros_like(st_ref)

    rows = range(q_ref.shape[0])
    n_rows = q_ref.shape[1]
    ti = lax.broadcasted_iota(jnp.int32, (n_rows, n_rows), 0)
    si = lax.broadcasted_iota(jnp.int32, (n_rows, n_rows), 1)
    ahead = (ti < si) if reverse else (ti > si)
    level = jnp.where(ahead, 31 - lax.clz(ti ^ si), jnp.where(ti == si, -1, -2))
    preps = [_hgrn_prep(q_ref.at[g], zg_ref.at[g], lbl_ref, layer, depth, reverse) for g in rows]
    outs = [[] for _ in rows]
    for hd in range(A_HEADS):
        for g in rows:
            outs[g].append(_hgrn_head(hd, preps[g], v_ref.at[g], st_ref.at[g], level, reverse))
    for g in rows:
        o = jnp.concatenate(outs[g], axis=1)
        if final:
            _hgrn_finish(o, z_ref.at[g], ob_ref.at[g], ng_ref.at[layer:layer + 1], out_ref.at[g])
        else:
            out_ref[g] = o


def _hgrn_prep(q_ref, zg_ref, lbl_ref, layer, depth, reverse):
    n_rows = q_ref.shape[0]
    q = q_ref[...]
    zg = zg_ref[...]
    e = jnp.exp(-jnp.abs(zg))
    r = 1.0 / (1.0 + e)
    er = e * r
    pos = zg >= 0.0
    f = jnp.where(pos, r, er)
    k = jnp.where(pos, er, r)
    if layer > 0:
        lb = _hgrn_lower_bound(lbl_ref, layer, depth)
        f = lb + (1.0 - lb) * f
        k = (1.0 - lb) * k
    b2 = _cumsum_rows_mxu(jnp.log2(f), reverse)
    return q, k, f, b2


def _hgrn_head(hd, prep, v_ref, st_ref, level, reverse):
    sl = slice(hd * A_HD, (hd + 1) * A_HD)
    q, k, f, b2 = (a[:, sl] for a in prep)
    n_rows = q.shape[0]
    tot = b2[0:1, :] if reverse else b2[n_rows - 1:n_rows, :]
    v_bf = v_ref[:, sl].astype(BF16)
    q_in = (q * jnp.exp2(b2)).astype(BF16)
    k_out = (k * jnp.exp2(tot - b2)).astype(BF16)

    att = jnp.where(level == -1, _dot_nt(q.astype(BF16), k.astype(BF16)), 0.0)
    h = n_rows // 2
    while h >= 1:
        c = _hgrn_level_operand(q, k, f, b2, h, reverse).astype(BF16)
        att = jnp.where(level == (h.bit_length() - 1), _dot_nt(c, c), att)
        h //= 2
    st = st_ref[hd]
    o = _dot(att.astype(BF16), v_bf) + _dot_nt(q_in, st.astype(BF16))
    st_ref[hd] = jnp.exp2(tot) * st + _dot_tn(v_bf, k_out)
    return o


def _hgrn_finish(o, z_ref, ob_ref, ng_ref, out_ref):
    o = o + ob_ref[...]
    ys = []
    for hd in range(A_HEADS):
        sl = slice(hd * A_HD, (hd + 1) * A_HD)
        oh = o[:, sl]
        ms = jnp.mean(oh * oh, axis=-1, keepdims=True)
        ys.append(oh * lax.rsqrt(ms + EPS) * ng_ref[:, sl])
    y = jnp.concatenate(ys, axis=1) * _silu(z_ref[...])
    out_ref[...] = y.astype(out_ref.dtype)


def _chunk_index(c, n_chunks, reverse):
    return n_chunks - 1 - c if reverse else c


def _hgrn(u3, lb_logits, layer, *, reverse, ob=None, norm_g=None):
    batch, seq, _ = u3.shape
    n_chunks = seq // CHUNK
    depth = lb_logits.shape[0]
    final = ob is not None
    grp = REC_GROUP

    def ublk(blk):
        return pl.BlockSpec((grp, CHUNK, D_GROUP),
                            lambda bg, c: (bg, _chunk_index(c, n_chunks, reverse), blk))

    row_spec = ublk(0)
    in_specs = [ublk(BLK_A_Q), ublk(BLK_A_I), ublk(BLK_A_FB if reverse else BLK_A_FF),
                pl.BlockSpec((depth, D_GROUP), lambda bg, c: (0, 0))]
    args = [u3, u3, u3, lb_logits]
    if final:
        in_specs += [ublk(BLK_A_Z), row_spec, pl.BlockSpec(norm_g.shape, lambda bg, c: (0, 0))]
        args += [u3, ob, norm_g]
    return pl.pallas_call(
        functools.partial(_hgrn_kernel, layer=layer, depth=depth, reverse=reverse, final=final),
        out_shape=jax.ShapeDtypeStruct((batch, seq, D_GROUP), BF16 if final else F32),
        grid=(batch // grp, n_chunks),
        in_specs=in_specs,
        out_specs=row_spec,
        scratch_shapes=[pltpu.VMEM((grp, A_HEADS, A_HD, A_HD), F32)],
        compiler_params=pltpu.CompilerParams(
            dimension_semantics=("parallel", "arbitrary"), vmem_limit_bytes=VMEM_LIMIT),
        name="hgrn_fwd" if final else "hgrn_bwd",
    )(*args)


def _split3(x):
    hi = x.astype(BF16).astype(F32)
    r1 = x - hi
    mid = r1.astype(BF16).astype(F32)
    lo = (r1 - mid).astype(BF16).astype(F32)
    return hi, mid, lo


def _cumsum_lanes_mxu(x, reverse):
    n = x.shape[1]
    j = lax.broadcasted_iota(jnp.int32, (n, n), 0)
    t = lax.broadcasted_iota(jnp.int32, (n, n), 1)
    tri = jnp.where((j >= t) if reverse else (j <= t), 1.0, 0.0).astype(BF16)
    pieces = jnp.concatenate(list(_split3(x)), axis=1).astype(BF16)
    return _dot(pieces, jnp.concatenate([tri, tri, tri], axis=0))


def _conv_silu_qk(qkp_ref, qkc_ref, qkn_ref, cw_ref, reverse):
    n_rows = qkc_ref.shape[0]
    c = pl.program_id(1)
    n_chunks = pl.num_programs(1)
    cc = _chunk_index(c, n_chunks, reverse)
    halo = SUBLANE
    prev = qkp_ref[n_rows - halo:n_rows, :] * (cc > 0).astype(F32)
    nxt = qkn_ref[0:halo, :] * (cc < n_chunks - 1).astype(F32)
    xcat = jnp.concatenate([prev, qkc_ref[...], nxt], axis=0)
    acc = None
    for j in range(CONV_W):
        off = halo + j - CONV_W // 2
        term = cw_ref[j:j + 1, :] * xcat[off:off + n_rows, :]
        acc = term if acc is None else acc + term
    qk = _silu(acc)
    n_qk = C_HEADS * C_QK
    return jnp.concatenate([qk[:, :n_qk] * (C_QK ** -0.5), qk[:, n_qk:]], axis=1).astype(BF16)


def _mlstm_kernel(*refs, layer, reverse, final):
    if final:
        (qk_ref, v_ref, gt_ref, gbc_ref, og_ref, z_ref, hb_ref, ng_ref,
         out_ref, cst_ref, m_ref) = refs
    else:
        (qkp_ref, qkc_ref, qkn_ref, cw_ref, v_ref, gt_ref, gbc_ref,
         out_ref, qko_ref, cst_ref, m_ref) = refs

    @pl.when(pl.program_id(1) == 0)
    def _():
        cst_ref[...] = jnp.zeros_like(cst_ref)
        m_ref[...] = jnp.zeros_like(m_ref)

    rows = range(v_ref.shape[0])
    if final:
        qks = [qk_ref[g] for g in rows]
    else:
        qks = [_conv_silu_qk(qkp_ref.at[g], qkc_ref.at[g], qkn_ref.at[g], cw_ref.at[layer], reverse)
               for g in rows]
        for g in rows:
            qko_ref[g] = qks[g]
    gates = [_mlstm_gates(gt_ref.at[g], gbc_ref.at[layer], m_ref.at[g], reverse) for g in rows]
    outs = [[] for _ in rows]
    for hd in range(C_HEADS):
        for g in rows:
            outs[g].append(_mlstm_head(hd, qks[g], v_ref.at[g], gates[g], cst_ref.at[g], reverse))
    for g in rows:
        hcat = jnp.concatenate(outs[g], axis=1)
        if final:
            _mlstm_finish(hcat, og_ref.at[g], z_ref.at[g], hb_ref.at[g],
                          ng_ref.at[layer:layer + 1], out_ref.at[g])
        else:
            out_ref[g] = hcat


def _mlstm_gates(gt_ref, gbc_ref, m_ref, reverse):
    n_rows = gt_ref.shape[1]
    n_dir = 2 * C_HEADS

    gr = gt_ref[...] + gbc_ref[...]
    li = gr[0:n_dir, :]
    b = _cumsum_lanes_mxu(_log_sigmoid(gr[n_dir:, :]), reverse)
    b_last = b[:, 0:1] if reverse else b[:, n_rows - 1:n_rows]
    m_prev = m_ref[...]
    d_inter = b + m_prev
    a = (b_last - b) + li
    m_new = jnp.maximum(b_last + m_prev, jnp.max(a, axis=1, keepdims=True))
    ws = jnp.exp(a - m_new)
    decay = jnp.exp((b_last + m_prev) - m_new)
    m_ref[...] = m_new

    one8 = jnp.ones((n_dir, n_rows), F32)
    lhs = jnp.concatenate(list(_split3(li - b)) + [one8, one8, one8], axis=0).T.astype(BF16)
    rhs_all = jnp.concatenate([one8, one8, one8] + list(_split3(b)), axis=0)
    return lhs, rhs_all, d_inter, ws, decay


def _mlstm_head(hd, qk, v_ref, gates, cst_ref, reverse):
    lhs, rhs_all, d_inter, ws, decay = gates
    n_rows = v_ref.shape[0]
    n_qk = C_HEADS * C_QK
    n_dir = 2 * C_HEADS
    gi = (C_HEADS if reverse else 0) + hd
    gate_row = lax.broadcasted_iota(jnp.int32, rhs_all.shape, 0) & (n_dir - 1)
    si = lax.broadcasted_iota(jnp.int32, (n_rows, n_rows), 0)
    ti = lax.broadcasted_iota(jnp.int32, (n_rows, n_rows), 1)
    causal = (si >= ti) if reverse else (si <= ti)

    q_h = qk[:, hd * C_QK:(hd + 1) * C_QK]
    k_h = qk[:, n_qk + hd * C_QK:n_qk + (hd + 1) * C_QK]
    rhs = jnp.where(gate_row == gi, rhs_all, 0.0).astype(BF16)
    d = jnp.where(causal, _dot(lhs, rhs), NEG)
    d_int = d_inter[gi:gi + 1, :]
    m_t = jnp.maximum(jnp.max(d, axis=0, keepdims=True), d_int)
    w = (_dot_nt(k_h, q_h) * jnp.exp(d - m_t)).astype(BF16)
    g = jnp.exp(d_int - m_t)
    v_t = jnp.concatenate([v_ref[:, hd * C_V:(hd + 1) * C_V].T, jnp.ones((C_ONES, n_rows), F32)],
                          axis=0)
    cst = cst_ref[hd]
    num = _dot(v_t.astype(BF16), w) + g * _dot_nt(cst.astype(BF16), q_h)
    den = jnp.maximum(jnp.abs(num[C_V:C_V + 1, :]), jnp.exp(-m_t))
    cst_ref[hd] = (decay[gi:gi + 1, :C_QK] * cst
                   + _dot((v_t * ws[gi:gi + 1, :]).astype(BF16), k_h))
    return (num[:C_V, :] / den).T


def _mlstm_finish(hcat, og_ref, z_ref, hb_ref, ng_ref, out_ref):
    hsum = (hcat + hb_ref[...]) * _sigmoid(og_ref[...])
    ys = []
    for hd in range(C_HEADS):
        sl = slice(hd * C_V, (hd + 1) * C_V)
        hh = hsum[:, sl]
        mu = jnp.mean(hh, axis=-1, keepdims=True)
        var = jnp.mean(jnp.square(hh - mu), axis=-1, keepdims=True)
        ys.append((hh - mu) * lax.rsqrt(var + EPS) * ng_ref[:, sl])
    y = jnp.concatenate(ys, axis=1) * _silu(z_ref[...])
    out_ref[...] = y.astype(out_ref.dtype)


def _mlstm(u3, gt, conv_w, gate_b, layer, *, reverse, qk=None, hb=None, norm_g=None):
    batch, seq, _ = u3.shape
    n_chunks = seq // CHUNK
    final = hb is not None
    grp = REC_GROUP
    assert CHUNK == LANE

    def chunk(c, delta=0):
        return jnp.clip(_chunk_index(c, n_chunks, reverse) + delta, 0, n_chunks - 1)

    def ublk(blk, delta=0):
        return pl.BlockSpec((grp, CHUNK, D_GROUP), lambda bg, c: (bg, chunk(c, delta), blk))

    row_spec = ublk(0)
    gate_specs = [pl.BlockSpec((grp, N_GATE, CHUNK), lambda bg, c: (bg, 0, chunk(c))),
                  pl.BlockSpec(gate_b.shape, lambda bg, c: (0, 0, 0))]
    gate_args = [gt, gate_b]
    if final:
        in_specs = [row_spec, ublk(BLK_C_V)] + gate_specs + [
            ublk(BLK_C_O), ublk(BLK_C_Z), row_spec,
            pl.BlockSpec(norm_g.shape, lambda bg, c: (0, 0))]
        args = [qk, u3] + gate_args + [u3, u3, hb, norm_g]
        out_shape = jax.ShapeDtypeStruct((batch, seq, D_GROUP), BF16)
        out_specs = row_spec
    else:
        in_specs = [ublk(BLK_C_QK, -1), ublk(BLK_C_QK), ublk(BLK_C_QK, 1),
                    pl.BlockSpec(conv_w.shape, lambda bg, c: (0, 0, 0)),
                    ublk(BLK_C_V)] + gate_specs
        args = [u3, u3, u3, conv_w, u3] + gate_args
        out_shape = (jax.ShapeDtypeStruct((batch, seq, D_GROUP), F32),
                     jax.ShapeDtypeStruct((batch, seq, D_GROUP), BF16))
        out_specs = (row_spec, row_spec)
    return pl.pallas_call(
        functools.partial(_mlstm_kernel, layer=layer, reverse=reverse, final=final),
        out_shape=out_shape,
        grid=(batch // grp, n_chunks),
        in_specs=in_specs,
        out_specs=out_specs,
        scratch_shapes=[pltpu.VMEM((grp, C_HEADS, C_V + C_ONES, C_QK), F32),
                        pltpu.VMEM((grp, 2 * C_HEADS, CHUNK), F32)],
        compiler_params=pltpu.CompilerParams(
            dimension_semantics=("parallel", "arbitrary"), vmem_limit_bytes=VMEM_LIMIT),
        name="mlstm_fwd" if final else "mlstm_bwd",
    )(*args)


def _rope_table_kernel(pos_ref, invf_ref, cos_ref, sin_ref):
    ang = invf_ref[...] * pos_ref[...].astype(F32)
    n_rows = ang.shape[1]
    half = ROPE_DIM // 2
    r = lax.broadcasted_iota(jnp.int32, ang.shape, 0)
    sn = jnp.sin(ang)
    sn = jnp.where(r < half, -sn, sn)
    pad = jnp.zeros((LANE - ROPE_DIM, n_rows), F32)

    def expand(x):
        t = jnp.concatenate([x, pad], axis=0).T
        return t + pltpu.roll(t, B_HD, axis=1)

    lane = lax.broadcasted_iota(jnp.int32, (n_rows, LANE), 1) & (B_HD - 1)
    cos_ref[...] = jnp.where(lane < ROPE_DIM, expand(jnp.cos(ang)), 1.0)
    sin_ref[...] = expand(sn)


def _rope_tables(positions):
    rows = positions.size
    half = ROPE_DIM // 2
    inv_freq = ROPE_THETA ** (-jnp.arange(0, ROPE_DIM, 2, dtype=F32) / ROPE_DIM)
    invf = jnp.concatenate([inv_freq, inv_freq]).reshape(ROPE_DIM, 1)
    tm = ROW_TILE
    return pl.pallas_call(
        _rope_table_kernel,
        out_shape=(jax.ShapeDtypeStruct((rows, LANE), F32),) * 2,
        grid=(rows // tm,),
        in_specs=[pl.BlockSpec((1, tm), lambda i: (0, i)),
                  pl.BlockSpec((ROPE_DIM, 1), lambda i: (0, 0))],
        out_specs=(pl.BlockSpec((tm, LANE), lambda i: (i, 0)),) * 2,
        compiler_params=pltpu.CompilerParams(dimension_semantics=("parallel",)),
        name="rope_tables",
    )(positions.reshape(1, rows), invf)


def _rope(x, cos, sin):
    width = x.shape[1]
    if width > LANE:
        cos = jnp.concatenate([cos] * (width // LANE), axis=1)
        sin = jnp.concatenate([sin] * (width // LANE), axis=1)
    half = ROPE_DIM // 2
    r = lax.broadcasted_iota(jnp.int32, x.shape, 1) & (B_HD - 1)
    partner = jnp.where(r < half, pltpu.roll(x, width - half, axis=1), pltpu.roll(x, half, axis=1))
    return x * cos + partner * sin


def _swa_kernel(q_ref, z_ref, mp_ref, mc_ref, mn_ref, cp_ref, cc_ref, cn_ref,
                sp_ref, sc_ref, sn_ref, sink_ref, out_ref, *, layer):
    tile = q_ref.shape[0]
    blk = mp_ref.shape[0]
    n_sub = tile // blk
    i = pl.program_id(1)
    n_tiles = pl.num_programs(1)
    kv_w = B_KV_HEADS * B_HD
    grp = B_HEADS // B_KV_HEADS

    q = (_rope(q_ref[...], cc_ref[...], sc_ref[...]) * (B_HD ** -0.5)).astype(BF16)
    k = jnp.concatenate([
        _rope(mp_ref[:, MISC_K:MISC_K + kv_w], cp_ref[...], sp_ref[...]),
        _rope(mc_ref[:, MISC_K:MISC_K + kv_w], cc_ref[...], sc_ref[...]),
        _rope(mn_ref[:, MISC_K:MISC_K + kv_w], cn_ref[...], sn_ref[...])], axis=0).astype(BF16)
    v = jnp.concatenate([mp_ref[:, MISC_V:MISC_V + kv_w], mc_ref[:, MISC_V:MISC_V + kv_w],
                         mn_ref[:, MISC_V:MISC_V + kv_w]], axis=0)
    n_keys = v.shape[0]

    kr = lax.broadcasted_iota(jnp.int32, (3 * blk, blk), 0)
    qc = lax.broadcasted_iota(jnp.int32, (3 * blk, blk), 1)
    in_window = jnp.abs(qc - (kr - blk)) <= WINDOW
    biases = []
    for j in range(n_sub):
        valid = in_window
        if j == 0:
            valid = valid & (kr >= jnp.where(i > 0, 0, blk))
        if j == n_sub - 1:
            valid = valid & (kr < jnp.where(i < n_tiles - 1, 3 * blk, 2 * blk))
        biases.append(jnp.where(valid, 0.0, NEG))

    ones = jnp.ones((n_keys, B_HD), F32)
    vts = [jnp.concatenate([v[:, kvh * B_HD:(kvh + 1) * B_HD], ones], axis=1).T.astype(BF16)
           for kvh in range(B_KV_HEADS)]

    tiles = [[] for _ in range(n_sub)]
    pairs = [[] for _ in range(n_sub)]
    for hq in range(B_HEADS):
        kvh = hq // grp
        ksl = slice(kvh * B_HD, (kvh + 1) * B_HD)
        sk = sink_ref[layer:layer + 1, hq:hq + 1]
        for j in range(n_sub):
            keys = slice(j * blk, (j + 3) * blk)
            qh = q[j * blk:(j + 1) * blk, hq * B_HD:(hq + 1) * B_HD]
            s = _dot_nt(k[keys, ksl], qh) + biases[j]
            m = jnp.maximum(jnp.max(s, axis=0, keepdims=True), sk)
            p = jnp.exp(s - m).astype(BF16)
            o = _dot(vts[kvh][:, keys], p)
            denom = o[B_HD:B_HD + 1, :] + jnp.exp(sk - m)
            pairs[j].append(o[:B_HD, :] / denom)
            if len(pairs[j]) == LANE // B_HD:
                tiles[j].append(jnp.concatenate(pairs[j], axis=0).T)
                pairs[j] = []
    for j in range(n_sub):
        rows = slice(j * blk, (j + 1) * blk)
        y = jnp.concatenate(tiles[j], axis=1) * _silu(z_ref[rows, :])
        out_ref[rows, :] = y.astype(out_ref.dtype)


def _swa(u, cos_t, sin_t, sink, layer, batch):
    rows = u.shape[0]
    n_blocks = rows // batch // CHUNK
    n_sub = SWA_TILE // CHUNK
    n_tiles = n_blocks // n_sub

    def tile_spec(width, blk):
        return pl.BlockSpec((SWA_TILE, width), lambda bi, i: (bi * n_tiles + i, blk))

    def halo_spec(width, blk, after):
        def index(bi, i):
            nb = (i + 1) * n_sub if after else i * n_sub - 1
            return (bi * n_blocks + jnp.clip(nb, 0, n_blocks - 1), blk)
        return pl.BlockSpec((CHUNK, width), index)

    def table_specs():
        return [halo_spec(LANE, 0, False), tile_spec(LANE, 0), halo_spec(LANE, 0, True)]

    return pl.pallas_call(
        functools.partial(_swa_kernel, layer=layer),
        out_shape=jax.ShapeDtypeStruct((rows, D_GROUP), BF16),
        grid=(batch, n_tiles),
        in_specs=[tile_spec(D_GROUP, BLK_B_Q), tile_spec(D_GROUP, BLK_B_Z),
                  halo_spec(D_GROUP, BLK_MISC, False), tile_spec(D_GROUP, BLK_MISC),
                  halo_spec(D_GROUP, BLK_MISC, True)] + table_specs() + table_specs() + [
                  pl.BlockSpec(sink.shape, lambda bi, i: (0, 0))],
        out_specs=tile_spec(D_GROUP, 0),
        compiler_params=pltpu.CompilerParams(
            dimension_semantics=("parallel", "parallel"), vmem_limit_bytes=VMEM_LIMIT),
        name="window_attn",
    )(u, u, u, u, u, cos_t, cos_t, cos_t, sin_t, sin_t, sin_t, sink)


def _mem_kv_kernel(mem_ref, g_ref, w_ref, k_ref, vt_ref, *, layer):
    x = mem_ref[...]
    ms = jnp.mean(x * x, axis=-1, keepdims=True)
    h = (x * lax.rsqrt(ms + EPS) * g_ref[layer:layer + 1, :]).astype(BF16)
    kv = _dot(h, w_ref[...].astype(BF16))
    k_ref[...] = kv[:, :D_GROUP].astype(k_ref.dtype)
    vt_ref[...] = kv[:, D_GROUP:].T.astype(vt_ref.dtype)


def _mem_kv(mem, g_all, w_all, layer):
    batch, m_len, d = mem.shape
    n_out = w_all.shape[2]
    return pl.pallas_call(
        functools.partial(_mem_kv_kernel, layer=layer),
        out_shape=(jax.ShapeDtypeStruct((batch, m_len, D_GROUP), BF16),
                   jax.ShapeDtypeStruct((batch, D_GROUP, m_len), BF16)),
        grid=(batch,),
        in_specs=[pl.BlockSpec((None, m_len, d), lambda bi: (bi, 0, 0)),
                  pl.BlockSpec(g_all.shape, lambda bi: (0, 0)),
                  pl.BlockSpec((None, d, n_out), lambda bi: (layer, 0, 0))],
        out_specs=(pl.BlockSpec((None, m_len, D_GROUP), lambda bi: (bi, 0, 0)),
                   pl.BlockSpec((None, D_GROUP, m_len), lambda bi: (bi, 0, 0))),
        compiler_params=pltpu.CompilerParams(
            dimension_semantics=("parallel",), vmem_limit_bytes=VMEM_LIMIT),
        name="mem_kv",
    )(mem, g_all, w_all)


def _mem_attn_kernel(q_ref, z_ref, k_ref, vt_ref, out_ref):
    q = q_ref[...].astype(BF16)
    outs = []
    for hd in range(D_HEADS):
        sl = slice(hd * D_HD, (hd + 1) * D_HD)
        s = _dot_nt(q[:, sl], k_ref[:, sl])
        m = jnp.max(s, axis=-1, keepdims=True)
        p = jnp.exp2((s - m) * (D_HD ** -0.5 * LOG2_E))
        den = jnp.sum(p, axis=-1, keepdims=True)
        outs.append(_dot_nt(p.astype(BF16), vt_ref[sl, :]) / den)
    y = jnp.concatenate(outs, axis=1) * _silu(z_ref[...])
    out_ref[...] = y.astype(out_ref.dtype)


def _mem_attn(u, k, vt, batch):
    rows = u.shape[0]
    tm = ROW_TILE
    per_batch = rows // batch // tm
    return pl.pallas_call(
        _mem_attn_kernel,
        out_shape=jax.ShapeDtypeStruct((rows, D_GROUP), BF16),
        grid=(batch, per_batch),
        in_specs=[pl.BlockSpec((tm, D_GROUP), lambda bi, i: (bi * per_batch + i, BLK_D_Q)),
                  pl.BlockSpec((tm, D_GROUP), lambda bi, i: (bi * per_batch + i, BLK_D_Z)),
                  pl.BlockSpec((None,) + k.shape[1:], lambda bi, i: (bi, 0, 0)),
                  pl.BlockSpec((None,) + vt.shape[1:], lambda bi, i: (bi, 0, 0))],
        out_specs=pl.BlockSpec((tm, D_GROUP), lambda bi, i: (bi * per_batch + i, 0)),
        compiler_params=pltpu.CompilerParams(
            dimension_semantics=("parallel", "parallel"), vmem_limit_bytes=VMEM_LIMIT),
        name="mem_attn",
    )(u, u, k, vt)


def kernel(x, mem, positions, norm_g, w_in, hgrn_lb_logits, hgrn_norm_g, attn_sink, mlstm_conv_w,
           mlstm_gate_b, mlstm_norm_g, mem_norm_g, w_mem_kv, w_out, final_norm_g):
    batch, seq, d = x.shape
    depth = w_in.shape[0]
    assert seq % CHUNK == 0 and seq % ROW_TILE == 0 and seq % IN_ROW_TILE == 0
    assert batch % REC_GROUP == 0
    xf = x.reshape(batch * seq, d)
    cos_t, sin_t = _rope_tables(positions)
    w_cat_all, w_gt_all = _prep_w_in(w_in)
    gate_b = mlstm_gate_b.reshape(depth, N_GATE, 1)
    for layer in range(depth):
        u, gt = _in_proj(xf, norm_g, w_cat_all, w_gt_all, layer, batch)
        u3 = u.reshape(batch, seq, u.shape[1])
        ob = _hgrn(u3, hgrn_lb_logits, layer, reverse=True)
        ya = _hgrn(u3, hgrn_lb_logits, layer, reverse=False, ob=ob, norm_g=hgrn_norm_g)
        yb = _swa(u, cos_t, sin_t, attn_sink, layer, batch)
        hb, qk_c = _mlstm(u3, gt, mlstm_conv_w, gate_b, layer, reverse=True)
        yc = _mlstm(u3, gt, None, gate_b, layer, reverse=False,
                    qk=qk_c, hb=hb, norm_g=mlstm_norm_g)
        mem_k, mem_vt = _mem_kv(mem, mem_norm_g, w_mem_kv, layer)
        yd = _mem_attn(u, mem_k, mem_vt, batch)
        xf = _out_proj(ya.reshape(batch * seq, D_GROUP), yb, yc.reshape(batch * seq, D_GROUP), yd,
                       w_out, xf, final_norm_g, layer, final=(layer == depth - 1))
    return xf.reshape(batch, seq, d)
```

```python
import functools

import jax
import jax.numpy as jnp
import numpy as np
from jax import lax
from jax.experimental import pallas as pl
from jax.experimental.pallas import tpu as pltpu

F32 = jnp.float32
BF16 = jnp.bfloat16

D_GROUP = 512
A_HEADS, A_HD = 4, 128
B_HEADS, B_KV_HEADS, B_HD = 8, 2, 64
WINDOW = 128
ROPE_THETA = 500000.0
ROPE_DIM = 16
C_HEADS, C_QK, C_V = 4, 64, 128
CONV_W = 5
D_HEADS, D_HD = 4, 128
EPS = 1e-6
IN_SIZES = (
    D_GROUP, D_GROUP, D_GROUP, D_GROUP, D_GROUP,
    B_HEADS * B_HD, B_KV_HEADS * B_HD, B_KV_HEADS * B_HD, D_GROUP,
    C_HEADS * C_QK, C_HEADS * C_QK, C_HEADS * C_V, D_GROUP, D_GROUP,
    2 * C_HEADS, 2 * C_HEADS,
    D_GROUP, D_GROUP,
)
SPLIT_POINTS = tuple(int(s) for s in np.cumsum(IN_SIZES)[:-1])

LANE = 128
SUBLANE = 8
VMEM_LIMIT = 48 * 1024 * 1024

CHUNK = 128
ROW_TILE = 512
IN_ROW_TILE = 256
SWA_TILE = 512
REC_GROUP = 4
NEG = -1e30
LOG2_E = 1.4426950408889634

(BLK_A_Q, BLK_A_I, BLK_A_FF, BLK_A_FB, BLK_A_Z, BLK_B_Q, BLK_B_Z, BLK_C_QK, BLK_C_O,
 BLK_C_Z, BLK_D_Q, BLK_D_Z, BLK_MISC) = range(13)
N_BLK = 13
W_BLK_C_V = N_BLK
N_W_BLK = N_BLK + 1
MISC_K, MISC_V, MISC_G = 0, 128, 256
N_GATE = 4 * C_HEADS
C_ONES = 16


def _dot(a, b):
    return jnp.dot(a, b, preferred_element_type=F32)


def _dot_nt(a, b):
    return lax.dot_general(a, b, (((1,), (1,)), ((), ())), preferred_element_type=F32)


def _dot_tn(a, b):
    return lax.dot_general(a, b, (((0,), (0,)), ((), ())), preferred_element_type=F32)


def _sigmoid(x):
    return 1.0 / (1.0 + jnp.exp(-x))


def _silu(x):
    return x * _sigmoid(x)


def _log_sigmoid(x):
    return jnp.minimum(x, 0.0) - jnp.log(1.0 + jnp.exp(-jnp.abs(x)))


def _cumsum_rows(x, reverse):
    n = x.shape[0]
    row = lax.broadcasted_iota(jnp.int32, x.shape, 0)
    s = 1
    while s < n:
        if reverse:
            x = x + jnp.where(row < n - s, pltpu.roll(x, n - s, axis=0), 0.0)
        else:
            x = x + jnp.where(row >= s, pltpu.roll(x, s, axis=0), 0.0)
        s *= 2
    return x


def _cumsum_lanes(x, reverse):
    n = x.shape[1]
    col = lax.broadcasted_iota(jnp.int32, x.shape, 1)
    s = 1
    while s < n:
        if reverse:
            x = x + jnp.where(col < n - s, pltpu.roll(x, n - s, axis=1), 0.0)
        else:
            x = x + jnp.where(col >= s, pltpu.roll(x, s, axis=1), 0.0)
        s *= 2
    return x


def _in_proj_kernel(x_ref, g_ref, w_ref, wgt_ref, u_ref, gt_ref, vt_ref, *, layer):
    x = x_ref[...]
    ms = jnp.mean(x * x, axis=-1, keepdims=True)
    h = (x * lax.rsqrt(ms + EPS) * g_ref[layer:layer + 1, :]).astype(BF16)
    gt_ref[...] = _dot_nt(wgt_ref[...], h)
    vt_ref[...] = _dot_nt(w_ref[W_BLK_C_V * D_GROUP:(W_BLK_C_V + 1) * D_GROUP, :], h).astype(BF16)
    for j in range(N_BLK):
        sl = slice(j * D_GROUP, (j + 1) * D_GROUP)
        u_ref[:, sl] = _dot_nt(h, w_ref[sl, :])


def _in_proj(xf, g_all, w_cat_all, w_gt_all, layer, batch):
    rows, d = xf.shape
    tm = IN_ROW_TILE
    n_out = N_BLK * D_GROUP
    per_batch = rows // batch // tm
    return pl.pallas_call(
        functools.partial(_in_proj_kernel, layer=layer),
        out_shape=(jax.ShapeDtypeStruct((rows, n_out), F32),
                   jax.ShapeDtypeStruct((batch, N_GATE, rows // batch), F32),
                   jax.ShapeDtypeStruct((batch, D_GROUP, rows // batch), BF16)),
        grid=(rows // tm,),
        in_specs=[
            pl.BlockSpec((tm, d), lambda i: (i, 0)),
            pl.BlockSpec(g_all.shape, lambda i: (0, 0)),
            pl.BlockSpec((None, N_W_BLK * D_GROUP, d), lambda i: (layer, 0, 0),
                         pipeline_mode=pl.Buffered(1)),
            pl.BlockSpec((None, N_GATE, d), lambda i: (layer, 0, 0)),
        ],
        out_specs=(
            pl.BlockSpec((tm, n_out), lambda i: (i, 0)),
            pl.BlockSpec((None, N_GATE, tm), lambda i: (i // per_batch, 0, i % per_batch)),
            pl.BlockSpec((None, D_GROUP, tm), lambda i: (i // per_batch, 0, i % per_batch)),
        ),
        compiler_params=pltpu.CompilerParams(
            dimension_semantics=("parallel",), vmem_limit_bytes=VMEM_LIMIT),
        name="in_proj",
    )(xf, g_all, w_cat_all, w_gt_all)


def _w_in_segments():
    names = ("a_q", "a_i", "a_ff", "a_fb", "a_z", "b_q", "b_k", "b_v", "b_z",
             "c_q", "c_k", "c_v", "c_o", "c_z", "c_ig", "c_fg", "d_q", "d_z")
    src = dict(zip(names, zip((0,) + SPLIT_POINTS, IN_SIZES)))
    order = ("a_q", "a_i", "a_ff", "a_fb", "a_z", "b_q", "b_z", "c_q", "c_k", "c_o", "c_z",
             "d_q", "d_z", "b_k", "b_v", "c_ig", "c_fg")
    segs, dst = [], 0
    for name in order + ("c_v",):
        off, width = src[name]
        if name == "c_v":
            pad = (dst, W_BLK_C_V * D_GROUP)
            dst = pad[1]
        if segs and segs[-1][0] + segs[-1][1] == off and segs[-1][2] + segs[-1][1] == dst:
            segs[-1] = (segs[-1][0], segs[-1][1] + width, segs[-1][2])
        else:
            segs.append((off, width, dst))
        dst += width
    return tuple(segs), pad, src["c_ig"][0]


def _w_prep_kernel(wt_ref, wc_ref, wg_ref):
    segs, pad, gate_off = _w_in_segments()
    for off, width, dst in segs:
        wc_ref[dst:dst + width, :] = wt_ref[off:off + width, :].astype(BF16)
    wc_ref[pad[0]:pad[1], :] = jnp.zeros((pad[1] - pad[0], wc_ref.shape[1]), BF16)
    wg_ref[...] = wt_ref[gate_off:gate_off + N_GATE, :].astype(BF16)


def _prep_w_in(w):
    depth, d, d_in = w.shape
    wt = jnp.swapaxes(w, 1, 2)
    tk = IN_ROW_TILE
    n_out = N_W_BLK * D_GROUP
    return pl.pallas_call(
        _w_prep_kernel,
        out_shape=(jax.ShapeDtypeStruct((depth, n_out, d), BF16),
                   jax.ShapeDtypeStruct((depth, N_GATE, d), BF16)),
        grid=(depth, d // tk),
        in_specs=[pl.BlockSpec((None, d_in, tk), lambda l, i: (l, 0, i))],
        out_specs=(pl.BlockSpec((None, n_out, tk), lambda l, i: (l, 0, i)),
                   pl.BlockSpec((None, N_GATE, tk), lambda l, i: (l, 0, i))),
        compiler_params=pltpu.CompilerParams(
            dimension_semantics=("parallel", "parallel"), vmem_limit_bytes=VMEM_LIMIT),
        name="w_prep",
    )(wt)


def _out_proj_kernel(ya_ref, yb_ref, yc_ref, yd_ref, w_ref, x_ref, g_ref, o_ref, *, final):
    acc = x_ref[...]
    for i, y_ref in enumerate((ya_ref, yb_ref, yc_ref, yd_ref)):
        acc = acc + _dot(y_ref[...], w_ref[i * D_GROUP:(i + 1) * D_GROUP, :].astype(BF16))
    if final:
        ms = jnp.mean(acc * acc, axis=-1, keepdims=True)
        acc = acc * lax.rsqrt(ms + EPS) * g_ref[...]
    o_ref[...] = acc


def _out_proj(ya, yb, yc, yd, w_all, xf, g, layer, final):
    rows, d = xf.shape
    tm = ROW_TILE
    yspec = pl.BlockSpec((tm, D_GROUP), lambda i: (i, 0))
    return pl.pallas_call(
        functools.partial(_out_proj_kernel, final=final),
        out_shape=jax.ShapeDtypeStruct((rows, d), F32),
        grid=(rows // tm,),
        in_specs=[yspec, yspec, yspec, yspec,
                  pl.BlockSpec((None, 4 * D_GROUP, d), lambda i: (layer, 0, 0)),
                  pl.BlockSpec((tm, d), lambda i: (i, 0)),
                  pl.BlockSpec((1, d), lambda i: (0, 0))],
        out_specs=pl.BlockSpec((tm, d), lambda i: (i, 0)),
        compiler_params=pltpu.CompilerParams(
            dimension_semantics=("parallel",), vmem_limit_bytes=VMEM_LIMIT),
        name="out_proj",
    )(ya, yb, yc, yd, w_all, xf, g.reshape(1, d))


def _level_ref(b, h, reverse):
    n_rows, n = b.shape
    blk = 2 * h
    r0 = h if reverse else h - 1
    if blk >= SUBLANE:
        pieces = [jnp.broadcast_to(b[s + r0:s + r0 + 1, :], (blk, n)) for s in range(0, n_rows, blk)]
        return pieces[0] if len(pieces) == 1 else jnp.concatenate(pieces, axis=0)
    res = lax.broadcasted_iota(jnp.int32, b.shape, 0) & (blk - 1)
    out = None
    for m in range(blk):
        shift = (m - r0) % n_rows
        cand = b if shift == 0 else pltpu.roll(b, shift, axis=0)
        out = cand if out is None else jnp.where(res == m, cand, out)
    return out


def _hgrn_lower_bound(lbl_ref, layer, depth):
    rows = [lbl_ref[j:j + 1, :] for j in range(depth)]
    mx = functools.reduce(jnp.maximum, rows)
    es = [jnp.exp(r - mx) for r in rows]
    tot = functools.reduce(lambda a, c: a + c, es)
    lb = jnp.zeros_like(rows[0])
    for j in range(1, layer + 1):
        lb = lb + es[j] / tot
    return lb


def _cumsum_rows_mxu(x, reverse):
    n = x.shape[0]
    r = lax.broadcasted_iota(jnp.int32, (n, n), 0)
    c = lax.broadcasted_iota(jnp.int32, (n, n), 1)
    tri = jnp.where((c >= r) if reverse else (c <= r), 1.0, 0.0).astype(BF16)
    hi = x.astype(BF16)
    r1 = x - hi.astype(F32)
    mid = r1.astype(BF16)
    lo = (r1 - mid.astype(F32)).astype(BF16)
    return _dot(jnp.concatenate([tri, tri, tri], axis=1), jnp.concatenate([hi, mid, lo], axis=0))


def _hgrn_level_operand(q, k, f, b2, h, reverse):
    n_rows = q.shape[0]
    blk = 2 * h
    r0 = h if reverse else h - 1
    if h >= SUBLANE:
        pieces = []
        for s0 in range(0, n_rows, blk):
            ref = b2[s0 + r0:s0 + r0 + 1, :]
            lo, up = slice(s0, s0 + h), slice(s0 + h, s0 + blk)
            if reverse:
                pieces += [q[lo] * jnp.exp2(b2[lo] - ref), k[up] * jnp.exp2(ref - b2[up])]
            else:
                pieces += [k[lo] * jnp.exp2(ref - b2[lo]), q[up] * jnp.exp2(b2[up] - ref)]
        return jnp.concatenate(pieces, axis=0)
    row = lax.broadcasted_iota(jnp.int32, q.shape, 0)
    is_q = ((row & h) == 0) if reverse else ((row & h) != 0)
    qk = jnp.where(is_q, q, k)
    if h == 1:
        return jnp.where(is_q, q * f, k)
    if h == 2:
        res = row & 3
        f_dn = pltpu.roll(f, 1, axis=0)
        f_up = pltpu.roll(f, n_rows - 1, axis=0)
        if reverse:
            e = jnp.where(res == 0, f * f_up, jnp.where(res == 1, f, jnp.where(res == 2, 1.0, f_dn)))
        else:
            e = jnp.where(res == 0, f_up, jnp.where(res == 1, 1.0, jnp.where(res == 2, f, f_dn * f)))
        return qk * e
    return qk * jnp.exp2(-jnp.abs(b2 - _level_ref(b2, h, reverse)))


def _hgrn_kernel(*refs, layer, depth, reverse, final):
    if final:
        q_ref, v_ref, zg_ref, lbl_ref, z_ref, ob_ref, ng_ref, out_ref, st_ref = refs
    else:
        q_ref, v_ref, zg_ref, lbl_ref, out_ref, st_ref = refs

    @pl.when(pl.program_id(1) == 0)
    def _():
        st_ref[...] = jnp.zeros_like(st_ref)

    rows = range(q_ref.shape[0])
    n_rows = q_ref.shape[1]
    ti = lax.broadcasted_iota(jnp.int32, (n_rows, n_rows), 0)
    si = lax.broadcasted_iota(jnp.int32, (n_rows, n_rows), 1)
    ahead = (ti < si) if reverse else (ti > si)
    level = jnp.where(ahead, 31 - lax.clz(ti ^ si), jnp.where(ti == si, -1, -2))
    preps = [_hgrn_prep(q_ref.at[g], zg_ref.at[g], lbl_ref, layer, depth, reverse) for g in rows]
    outs = [[] for _ in rows]
    for hd in range(A_HEADS):
        sl = slice(hd * A_HD, (hd + 1) * A_HD)
        heads = [tuple(a[:, sl] for a in preps[g]) for g in rows]
        atts = [_hgrn_scores(heads[g], level, n_rows, None, reverse) for g in rows]
        h = n_rows // 2
        while h >= 1:
            atts = [_hgrn_scores(heads[g], level, h, atts[g], reverse) for g in rows]
            h //= 2
        for g in rows:
            outs[g].append(_hgrn_apply(heads[g], atts[g], v_ref.at[g, :, sl], st_ref.at[g, hd],
                                       reverse))
    for g in rows:
        o = jnp.concatenate(outs[g], axis=1)
        if final:
            _hgrn_finish(o, z_ref.at[g], ob_ref.at[g], ng_ref.at[layer:layer + 1], out_ref.at[g])
        else:
            out_ref[g] = o


def _hgrn_prep(q_ref, zg_ref, lbl_ref, layer, depth, reverse):
    n_rows = q_ref.shape[0]
    q = q_ref[...]
    zg = zg_ref[...]
    e = jnp.exp(-jnp.abs(zg))
    r = 1.0 / (1.0 + e)
    er = e * r
    pos = zg >= 0.0
    f = jnp.where(pos, r, er)
    k = jnp.where(pos, er, r)
    if layer > 0:
        lb = _hgrn_lower_bound(lbl_ref, layer, depth)
        f = lb + (1.0 - lb) * f
        k = (1.0 - lb) * k
    b2 = _cumsum_rows_mxu(jnp.log2(f), reverse)
    return q, k, f, b2


def _hgrn_scores(head, level, h, att, reverse):
    q, k, f, b2 = head
    if att is None:
        return jnp.where(level == -1, _dot_nt(q.astype(BF16), k.astype(BF16)), 0.0)
    c = _hgrn_level_operand(q, k, f, b2, h, reverse).astype(BF16)
    return jnp.where(level == (h.bit_length() - 1), _dot_nt(c, c), att)


def _hgrn_apply(head, att, v_ref, st_ref, reverse):
    q, k, _, b2 = head
    n_rows = q.shape[0]
    tot = b2[0:1, :] if reverse else b2[n_rows - 1:n_rows, :]
    v_bf = v_ref[...].astype(BF16)
    q_in = (q * jnp.exp2(b2)).astype(BF16)
    k_out = (k * jnp.exp2(tot - b2)).astype(BF16)
    st = st_ref[...]
    o = _dot(att.astype(BF16), v_bf) + _dot_nt(q_in, st.astype(BF16))
    st_ref[...] = jnp.exp2(tot) * st + _dot_tn(v_bf, k_out)
    return o


def _hgrn_finish(o, z_ref, ob_ref, ng_ref, out_ref):
    o = o + ob_ref[...]
    ys = []
    for hd in range(A_HEADS):
        sl = slice(hd * A_HD, (hd + 1) * A_HD)
        oh = o[:, sl]
        ms = jnp.mean(oh * oh, axis=-1, keepdims=True)
        ys.append(oh * lax.rsqrt(ms + EPS) * ng_ref[:, sl])
    y = jnp.concatenate(ys, axis=1) * _silu(z_ref[...])
    out_ref[...] = y.astype(out_ref.dtype)


def _chunk_index(c, n_chunks, reverse):
    return n_chunks - 1 - c if reverse else c


def _hgrn(u3, lb_logits, layer, *, reverse, ob=None, norm_g=None):
    batch, seq, _ = u3.shape
    n_chunks = seq // CHUNK
    depth = lb_logits.shape[0]
    final = ob is not None
    grp = REC_GROUP

    def ublk(blk):
        return pl.BlockSpec((grp, CHUNK, D_GROUP),
                            lambda bg, c: (bg, _chunk_index(c, n_chunks, reverse), blk))

    row_spec = ublk(0)
    in_specs = [ublk(BLK_A_Q), ublk(BLK_A_I), ublk(BLK_A_FB if reverse else BLK_A_FF),
                pl.BlockSpec((depth, D_GROUP), lambda bg, c: (0, 0))]
    args = [u3, u3, u3, lb_logits]
    if final:
        in_specs += [ublk(BLK_A_Z), row_spec, pl.BlockSpec(norm_g.shape, lambda bg, c: (0, 0))]
        args += [u3, ob, norm_g]
    return pl.pallas_call(
        functools.partial(_hgrn_kernel, layer=layer, depth=depth, reverse=reverse, final=final),
        out_shape=jax.ShapeDtypeStruct((batch, seq, D_GROUP), BF16 if final else F32),
        grid=(batch // grp, n_chunks),
        in_specs=in_specs,
        out_specs=row_spec,
        scratch_shapes=[pltpu.VMEM((grp, A_HEADS, A_HD, A_HD), F32)],
        compiler_params=pltpu.CompilerParams(
            dimension_semantics=("parallel", "arbitrary"), vmem_limit_bytes=VMEM_LIMIT),
        name="hgrn_fwd" if final else "hgrn_bwd",
    )(*args)


def _split3(x):
    hi = x.astype(BF16).astype(F32)
    r1 = x - hi
    mid = r1.astype(BF16).astype(F32)
    lo = (r1 - mid).astype(BF16).astype(F32)
    return hi, mid, lo


def _cumsum_lanes_mxu(x, reverse):
    n = x.shape[1]
    j = lax.broadcasted_iota(jnp.int32, (n, n), 0)
    t = lax.broadcasted_iota(jnp.int32, (n, n), 1)
    tri = jnp.where((j >= t) if reverse else (j <= t), 1.0, 0.0).astype(BF16)
    pieces = jnp.concatenate(list(_split3(x)), axis=1).astype(BF16)
    return _dot(pieces, jnp.concatenate([tri, tri, tri], axis=0))


def _conv_silu_qk(qkp_ref, qkc_ref, qkn_ref, cw_ref, reverse):
    n_rows = qkc_ref.shape[0]
    c = pl.program_id(1)
    n_chunks = pl.num_programs(1)
    cc = _chunk_index(c, n_chunks, reverse)
    halo = SUBLANE
    prev = qkp_ref[n_rows - halo:n_rows, :] * (cc > 0).astype(F32)
    nxt = qkn_ref[0:halo, :] * (cc < n_chunks - 1).astype(F32)
    xcat = jnp.concatenate([prev, qkc_ref[...], nxt], axis=0)
    acc = None
    for j in range(CONV_W):
        off = halo + j - CONV_W // 2
        term = cw_ref[j:j + 1, :] * xcat[off:off + n_rows, :]
        acc = term if acc is None else acc + term
    qk = _silu(acc)
    n_qk = C_HEADS * C_QK
    return jnp.concatenate([qk[:, :n_qk] * (C_QK ** -0.5), qk[:, n_qk:]], axis=1).astype(BF16)


def _mlstm_kernel(*refs, layer, reverse, final):
    if final:
        (qk_ref, v_ref, gt_ref, gbc_ref, og_ref, z_ref, hb_ref, ng_ref,
         out_ref, cst_ref, m_ref) = refs
    else:
        (qkp_ref, qkc_ref, qkn_ref, cw_ref, v_ref, gt_ref, gbc_ref,
         out_ref, qko_ref, cst_ref, m_ref) = refs

    @pl.when(pl.program_id(1) == 0)
    def _():
        cst_ref[...] = jnp.zeros_like(cst_ref)
        m_ref[...] = jnp.zeros_like(m_ref)

    rows = range(v_ref.shape[0])
    if final:
        qks = [qk_ref[g] for g in rows]
    else:
        qks = [_conv_silu_qk(qkp_ref.at[g], qkc_ref.at[g], qkn_ref.at[g], cw_ref.at[layer], reverse)
               for g in rows]
        for g in rows:
            qko_ref[g] = qks[g]
    gates = [_mlstm_gates(gt_ref.at[g], gbc_ref.at[layer], m_ref.at[g], reverse) for g in rows]
    outs = [[] for _ in rows]
    for hd in range(C_HEADS):
        weights = [_mlstm_weights(hd, qks[g], gates[g], reverse) for g in rows]
        for g in rows:
            h_t = _mlstm_apply(hd, qks[g], v_ref.at[g], gates[g], weights[g], cst_ref.at[g], reverse)
            if final:
                outs[g].append((h_t + hb_ref[g, hd * C_V:(hd + 1) * C_V, :]).T)
            else:
                out_ref[g, hd * C_V:(hd + 1) * C_V, :] = h_t
    if final:
        for g in rows:
            _mlstm_finish(jnp.concatenate(outs[g], axis=1), og_ref.at[g], z_ref.at[g],
                          ng_ref.at[layer:layer + 1], out_ref.at[g])


def _mlstm_gates(gt_ref, gbc_ref, m_ref, reverse):
    n_rows = gt_ref.shape[1]
    n_dir = 2 * C_HEADS

    gr = gt_ref[...] + gbc_ref[...]
    li = gr[0:n_dir, :]
    b = _cumsum_lanes_mxu(_log_sigmoid(gr[n_dir:, :]), reverse)
    b_last = b[:, 0:1] if reverse else b[:, n_rows - 1:n_rows]
    m_prev = m_ref[...]
    d_inter = b + m_prev
    a = (b_last - b) + li
    m_new = jnp.maximum(b_last + m_prev, jnp.max(a, axis=1, keepdims=True))
    ws = jnp.exp(a - m_new)
    decay = jnp.exp((b_last + m_prev) - m_new)
    m_ref[...] = m_new

    one8 = jnp.ones((n_dir, n_rows), F32)
    lhs = jnp.concatenate(list(_split3(li - b)) + [one8, one8, one8], axis=0).T.astype(BF16)
    rhs_all = jnp.concatenate([one8, one8, one8] + list(_split3(b)), axis=0)
    return lhs, rhs_all, d_inter, ws, decay


def _mlstm_weights(hd, qk, gates, reverse):
    lhs, rhs_all, d_inter, _, _ = gates
    n_rows = qk.shape[0]
    n_qk = C_HEADS * C_QK
    n_dir = 2 * C_HEADS
    gi = (C_HEADS if reverse else 0) + hd
    gate_row = lax.broadcasted_iota(jnp.int32, rhs_all.shape, 0) & (n_dir - 1)
    si = lax.broadcasted_iota(jnp.int32, (n_rows, n_rows), 0)
    ti = lax.broadcasted_iota(jnp.int32, (n_rows, n_rows), 1)
    causal = (si >= ti) if reverse else (si <= ti)

    q_h = qk[:, hd * C_QK:(hd + 1) * C_QK]
    k_h = qk[:, n_qk + hd * C_QK:n_qk + (hd + 1) * C_QK]
    rhs = jnp.where(gate_row == gi, rhs_all, 0.0).astype(BF16)
    d = jnp.where(causal, _dot(lhs, rhs), NEG)
    d_int = d_inter[gi:gi + 1, :]
    m_t = jnp.maximum(jnp.max(d, axis=0, keepdims=True), d_int)
    w = (_dot_nt(k_h, q_h) * jnp.exp(d - m_t)).astype(BF16)
    return w, m_t, jnp.exp(d_int - m_t)


def _mlstm_apply(hd, qk, vt_ref, gates, weights, cst_ref, reverse):
    _, _, _, ws, decay = gates
    w, m_t, g = weights
    n_rows = qk.shape[0]
    n_qk = C_HEADS * C_QK
    gi = (C_HEADS if reverse else 0) + hd
    q_h = qk[:, hd * C_QK:(hd + 1) * C_QK]
    k_h = qk[:, n_qk + hd * C_QK:n_qk + (hd + 1) * C_QK]
    v_t = vt_ref[hd * C_V:(hd + 1) * C_V, :]
    v_ext = jnp.concatenate([v_t, jnp.ones((C_ONES, n_rows), BF16)], axis=0)
    ws_row = ws[gi:gi + 1, :]
    v_scaled = jnp.concatenate([(v_t.astype(F32) * ws_row).astype(BF16),
                                jnp.broadcast_to(ws_row, (C_ONES, n_rows)).astype(BF16)], axis=0)
    cst = cst_ref[hd]
    num = _dot(v_ext, w) + g * _dot_nt(cst.astype(BF16), q_h)
    den = jnp.maximum(jnp.abs(num[C_V:C_V + 1, :]), jnp.exp(-m_t))
    cst_ref[hd] = decay[gi:gi + 1, :C_QK] * cst + _dot(v_scaled, k_h)
    return num[:C_V, :] / den


def _mlstm_finish(hsum, og_ref, z_ref, ng_ref, out_ref):
    hsum = hsum * _sigmoid(og_ref[...])
    ys = []
    for hd in range(C_HEADS):
        sl = slice(hd * C_V, (hd + 1) * C_V)
        hh = hsum[:, sl]
        mu = jnp.mean(hh, axis=-1, keepdims=True)
        var = jnp.mean(jnp.square(hh - mu), axis=-1, keepdims=True)
        ys.append((hh - mu) * lax.rsqrt(var + EPS) * ng_ref[:, sl])
    y = jnp.concatenate(ys, axis=1) * _silu(z_ref[...])
    out_ref[...] = y.astype(out_ref.dtype)


def _mlstm(u3, gt, vt, conv_w, gate_b, layer, *, reverse, qk=None, hb=None, norm_g=None):
    batch, seq, _ = u3.shape
    n_chunks = seq // CHUNK
    final = hb is not None
    grp = REC_GROUP
    assert CHUNK == LANE

    def chunk(c, delta=0):
        return jnp.clip(_chunk_index(c, n_chunks, reverse) + delta, 0, n_chunks - 1)

    def ublk(blk, delta=0):
        return pl.BlockSpec((grp, CHUNK, D_GROUP), lambda bg, c: (bg, chunk(c, delta), blk))

    row_spec = ublk(0)
    t_spec = lambda ch: pl.BlockSpec((grp, ch, CHUNK), lambda bg, c: (bg, 0, chunk(c)))
    gate_specs = [t_spec(N_GATE), pl.BlockSpec(gate_b.shape, lambda bg, c: (0, 0, 0))]
    gate_args = [gt, gate_b]
    if final:
        in_specs = [row_spec, t_spec(D_GROUP)] + gate_specs + [
            ublk(BLK_C_O), ublk(BLK_C_Z), t_spec(D_GROUP),
            pl.BlockSpec(norm_g.shape, lambda bg, c: (0, 0))]
        args = [qk, vt] + gate_args + [u3, u3, hb, norm_g]
        out_shape = jax.ShapeDtypeStruct((batch, seq, D_GROUP), BF16)
        out_specs = row_spec
    else:
        in_specs = [ublk(BLK_C_QK, -1), ublk(BLK_C_QK), ublk(BLK_C_QK, 1),
                    pl.BlockSpec(conv_w.shape, lambda bg, c: (0, 0, 0)),
                    t_spec(D_GROUP)] + gate_specs
        args = [u3, u3, u3, conv_w, vt] + gate_args
        out_shape = (jax.ShapeDtypeStruct((batch, D_GROUP, seq), F32),
                     jax.ShapeDtypeStruct((batch, seq, D_GROUP), BF16))
        out_specs = (t_spec(D_GROUP), row_spec)
    return pl.pallas_call(
        functools.partial(_mlstm_kernel, layer=layer, reverse=reverse, final=final),
        out_shape=out_shape,
        grid=(batch // grp, n_chunks),
        in_specs=in_specs,
        out_specs=out_specs,
        scratch_shapes=[pltpu.VMEM((grp, C_HEADS, C_V + C_ONES, C_QK), F32),
                        pltpu.VMEM((grp, 2 * C_HEADS, CHUNK), F32)],
        compiler_params=pltpu.CompilerParams(
            dimension_semantics=("parallel", "arbitrary"), vmem_limit_bytes=VMEM_LIMIT),
        name="mlstm_fwd" if final else "mlstm_bwd",
    )(*args)


def _rope_table_kernel(pos_ref, invf_ref, cos_ref, sin_ref):
    ang = invf_ref[...] * pos_ref[...].astype(F32)
    n_rows = ang.shape[1]
    half = ROPE_DIM // 2
    r = lax.broadcasted_iota(jnp.int32, ang.shape, 0)
    sn = jnp.sin(ang)
    sn = jnp.where(r < half, -sn, sn)
    pad = jnp.zeros((LANE - ROPE_DIM, n_rows), F32)

    def expand(x):
        t = jnp.concatenate([x, pad], axis=0).T
        return t + pltpu.roll(t, B_HD, axis=1)

    lane = lax.broadcasted_iota(jnp.int32, (n_rows, LANE), 1) & (B_HD - 1)
    cos_ref[...] = jnp.where(lane < ROPE_DIM, expand(jnp.cos(ang)), 1.0)
    sin_ref[...] = expand(sn)


def _rope_tables(positions):
    rows = positions.size
    half = ROPE_DIM // 2
    inv_freq = ROPE_THETA ** (-jnp.arange(0, ROPE_DIM, 2, dtype=F32) / ROPE_DIM)
    invf = jnp.concatenate([inv_freq, inv_freq]).reshape(ROPE_DIM, 1)
    tm = ROW_TILE
    return pl.pallas_call(
        _rope_table_kernel,
        out_shape=(jax.ShapeDtypeStruct((rows, LANE), F32),) * 2,
        grid=(rows // tm,),
        in_specs=[pl.BlockSpec((1, tm), lambda i: (0, i)),
                  pl.BlockSpec((ROPE_DIM, 1), lambda i: (0, 0))],
        out_specs=(pl.BlockSpec((tm, LANE), lambda i: (i, 0)),) * 2,
        compiler_params=pltpu.CompilerParams(dimension_semantics=("parallel",)),
        name="rope_tables",
    )(positions.reshape(1, rows), invf)


def _rope(x, cos, sin):
    width = x.shape[1]
    if width > LANE:
        cos = jnp.concatenate([cos] * (width // LANE), axis=1)
        sin = jnp.concatenate([sin] * (width // LANE), axis=1)
    half = ROPE_DIM // 2
    r = lax.broadcasted_iota(jnp.int32, x.shape, 1) & (B_HD - 1)
    partner = jnp.where(r < half, pltpu.roll(x, width - half, axis=1), pltpu.roll(x, half, axis=1))
    return x * cos + partner * sin


def _swa_kernel(q_ref, z_ref, mp_ref, mc_ref, mn_ref, cp_ref, cc_ref, cn_ref,
                sp_ref, sc_ref, sn_ref, sink_ref, out_ref, *, layer):
    tile = q_ref.shape[0]
    blk = mp_ref.shape[0]
    n_sub = tile // blk
    i = pl.program_id(1)
    n_tiles = pl.num_programs(1)
    kv_w = B_KV_HEADS * B_HD
    grp = B_HEADS // B_KV_HEADS

    q = (_rope(q_ref[...], cc_ref[...], sc_ref[...]) * (B_HD ** -0.5)).astype(BF16)
    k = jnp.concatenate([
        _rope(mp_ref[:, MISC_K:MISC_K + kv_w], cp_ref[...], sp_ref[...]),
        _rope(mc_ref[:, MISC_K:MISC_K + kv_w], cc_ref[...], sc_ref[...]),
        _rope(mn_ref[:, MISC_K:MISC_K + kv_w], cn_ref[...], sn_ref[...])], axis=0).astype(BF16)
    v = jnp.concatenate([mp_ref[:, MISC_V:MISC_V + kv_w], mc_ref[:, MISC_V:MISC_V + kv_w],
                         mn_ref[:, MISC_V:MISC_V + kv_w]], axis=0)
    n_keys = v.shape[0]

    kr = lax.broadcasted_iota(jnp.int32, (3 * blk, blk), 0)
    qc = lax.broadcasted_iota(jnp.int32, (3 * blk, blk), 1)
    in_window = jnp.abs(qc - (kr - blk)) <= WINDOW
    biases = []
    for j in range(n_sub):
        valid = in_window
        if j == 0:
            valid = valid & (kr >= jnp.where(i > 0, 0, blk))
        if j == n_sub - 1:
            valid = valid & (kr < jnp.where(i < n_tiles - 1, 3 * blk, 2 * blk))
        biases.append(jnp.where(valid, 0.0, NEG))

    ones = jnp.ones((n_keys, B_HD), F32)
    vts = [jnp.concatenate([v[:, kvh * B_HD:(kvh + 1) * B_HD], ones], axis=1).T.astype(BF16)
           for kvh in range(B_KV_HEADS)]

    tiles = [[] for _ in range(n_sub)]
    pairs = [[] for _ in range(n_sub)]
    for hq in range(B_HEADS):
        kvh = hq // grp
        ksl = slice(kvh * B_HD, (kvh + 1) * B_HD)
        sk = sink_ref[layer:layer + 1, hq:hq + 1]
        for j in range(n_sub):
            keys = slice(j * blk, (j + 3) * blk)
            qh = q[j * blk:(j + 1) * blk, hq * B_HD:(hq + 1) * B_HD]
            s = _dot_nt(k[keys, ksl], qh) + biases[j]
            m = jnp.maximum(jnp.max(s, axis=0, keepdims=True), sk)
            p = jnp.exp(s - m).astype(BF16)
            o = _dot(vts[kvh][:, keys], p)
            denom = o[B_HD:B_HD + 1, :] + jnp.exp(sk - m)
            pairs[j].append(o[:B_HD, :] / denom)
            if len(pairs[j]) == LANE // B_HD:
                tiles[j].append(jnp.concatenate(pairs[j], axis=0).T)
                pairs[j] = []
    for j in range(n_sub):
        rows = slice(j * blk, (j + 1) * blk)
        y = jnp.concatenate(tiles[j], axis=1) * _silu(z_ref[rows, :])
        out_ref[rows, :] = y.astype(out_ref.dtype)


def _swa(u, cos_t, sin_t, sink, layer, batch):
    rows = u.shape[0]
    n_blocks = rows // batch // CHUNK
    n_sub = SWA_TILE // CHUNK
    n_tiles = n_blocks // n_sub

    def tile_spec(width, blk):
        return pl.BlockSpec((SWA_TILE, width), lambda bi, i: (bi * n_tiles + i, blk))

    def halo_spec(width, blk, after):
        def index(bi, i):
            nb = (i + 1) * n_sub if after else i * n_sub - 1
            return (bi * n_blocks + jnp.clip(nb, 0, n_blocks - 1), blk)
        return pl.BlockSpec((CHUNK, width), index)

    def table_specs():
        return [halo_spec(LANE, 0, False), tile_spec(LANE, 0), halo_spec(LANE, 0, True)]

    return pl.pallas_call(
        functools.partial(_swa_kernel, layer=layer),
        out_shape=jax.ShapeDtypeStruct((rows, D_GROUP), BF16),
        grid=(batch, n_tiles),
        in_specs=[tile_spec(D_GROUP, BLK_B_Q), tile_spec(D_GROUP, BLK_B_Z),
                  halo_spec(D_GROUP, BLK_MISC, False), tile_spec(D_GROUP, BLK_MISC),
                  halo_spec(D_GROUP, BLK_MISC, True)] + table_specs() + table_specs() + [
                  pl.BlockSpec(sink.shape, lambda bi, i: (0, 0))],
        out_specs=tile_spec(D_GROUP, 0),
        compiler_params=pltpu.CompilerParams(
            dimension_semantics=("parallel", "parallel"), vmem_limit_bytes=VMEM_LIMIT),
        name="window_attn",
    )(u, u, u, u, u, cos_t, cos_t, cos_t, sin_t, sin_t, sin_t, sink)


def _mem_kv_kernel(mem_ref, g_ref, w_ref, k_ref, vt_ref, *, layer):
    x = mem_ref[...]
    ms = jnp.mean(x * x, axis=-1, keepdims=True)
    h = (x * lax.rsqrt(ms + EPS) * g_ref[layer:layer + 1, :]).astype(BF16)
    kv = _dot(h, w_ref[...].astype(BF16))
    k_ref[...] = kv[:, :D_GROUP].astype(k_ref.dtype)
    vt_ref[...] = kv[:, D_GROUP:].T.astype(vt_ref.dtype)


def _mem_kv(mem, g_all, w_all, layer):
    batch, m_len, d = mem.shape
    n_out = w_all.shape[2]
    return pl.pallas_call(
        functools.partial(_mem_kv_kernel, layer=layer),
        out_shape=(jax.ShapeDtypeStruct((batch, m_len, D_GROUP), BF16),
                   jax.ShapeDtypeStruct((batch, D_GROUP, m_len), BF16)),
        grid=(batch,),
        in_specs=[pl.BlockSpec((None, m_len, d), lambda bi: (bi, 0, 0)),
                  pl.BlockSpec(g_all.shape, lambda bi: (0, 0)),
                  pl.BlockSpec((None, d, n_out), lambda bi: (layer, 0, 0))],
        out_specs=(pl.BlockSpec((None, m_len, D_GROUP), lambda bi: (bi, 0, 0)),
                   pl.BlockSpec((None, D_GROUP, m_len), lambda bi: (bi, 0, 0))),
        compiler_params=pltpu.CompilerParams(
            dimension_semantics=("parallel",), vmem_limit_bytes=VMEM_LIMIT),
        name="mem_kv",
    )(mem, g_all, w_all)


def _mem_attn_kernel(q_ref, z_ref, k_ref, vt_ref, out_ref):
    q = q_ref[...].astype(BF16)
    outs = []
    for hd in range(D_HEADS):
        sl = slice(hd * D_HD, (hd + 1) * D_HD)
        s = _dot_nt(q[:, sl], k_ref[:, sl])
        m = jnp.max(s, axis=-1, keepdims=True)
        p = jnp.exp2((s - m) * (D_HD ** -0.5 * LOG2_E))
        den = jnp.sum(p, axis=-1, keepdims=True)
        outs.append(_dot_nt(p.astype(BF16), vt_ref[sl, :]) / den)
    y = jnp.concatenate(outs, axis=1) * _silu(z_ref[...])
    out_ref[...] = y.astype(out_ref.dtype)


def _mem_attn(u, k, vt, batch):
    rows = u.shape[0]
    tm = ROW_TILE
    per_batch = rows // batch // tm
    return pl.pallas_call(
        _mem_attn_kernel,
        out_shape=jax.ShapeDtypeStruct((rows, D_GROUP), BF16),
        grid=(batch, per_batch),
        in_specs=[pl.BlockSpec((tm, D_GROUP), lambda bi, i: (bi * per_batch + i, BLK_D_Q)),
                  pl.BlockSpec((tm, D_GROUP), lambda bi, i: (bi * per_batch + i, BLK_D_Z)),
                  pl.BlockSpec((None,) + k.shape[1:], lambda bi, i: (bi, 0, 0)),
                  pl.BlockSpec((None,) + vt.shape[1:], lambda bi, i: (bi, 0, 0))],
        out_specs=pl.BlockSpec((tm, D_GROUP), lambda bi, i: (bi * per_batch + i, 0)),
        compiler_params=pltpu.CompilerParams(
            dimension_semantics=("parallel", "parallel"), vmem_limit_bytes=VMEM_LIMIT),
        name="mem_attn",
    )(u, u, k, vt)


def kernel(x, mem, positions, norm_g, w_in, hgrn_lb_logits, hgrn_norm_g, attn_sink, mlstm_conv_w,
           mlstm_gate_b, mlstm_norm_g, mem_norm_g, w_mem_kv, w_out, final_norm_g):
    batch, seq, d = x.shape
    depth = w_in.shape[0]
    assert seq % CHUNK == 0 and seq % ROW_TILE == 0 and seq % IN_ROW_TILE == 0
    assert batch % REC_GROUP == 0
    xf = x.reshape(batch * seq, d)
    cos_t, sin_t = _rope_tables(positions)
    w_cat_all, w_gt_all = _prep_w_in(w_in)
    gate_b = mlstm_gate_b.reshape(depth, N_GATE, 1)
    for layer in range(depth):
        u, gt, vt = _in_proj(xf, norm_g, w_cat_all, w_gt_all, layer, batch)
        u3 = u.reshape(batch, seq, u.shape[1])
        ob = _hgrn(u3, hgrn_lb_logits, layer, reverse=True)
        ya = _hgrn(u3, hgrn_lb_logits, layer, reverse=False, ob=ob, norm_g=hgrn_norm_g)
        yb = _swa(u, cos_t, sin_t, attn_sink, layer, batch)
        hb, qk_c = _mlstm(u3, gt, vt, mlstm_conv_w, gate_b, layer, reverse=True)
        yc = _mlstm(u3, gt, vt, None, gate_b, layer, reverse=False,
                    qk=qk_c, hb=hb, norm_g=mlstm_norm_g)
        mem_k, mem_vt = _mem_kv(mem, mem_norm_g, w_mem_kv, layer)
        yd = _mem_attn(u, mem_k, mem_vt, batch)
        xf = _out_proj(ya.reshape(batch * seq, D_GROUP), yb, yc.reshape(batch * seq, D_GROUP), yd,
                       w_out, xf, final_norm_g, layer, final=(layer == depth - 1))
    return xf.reshape(batch, seq, d)
```

```python
import functools

import jax
import jax.numpy as jnp
import numpy as np
from jax import lax
from jax.experimental import pallas as pl
from jax.experimental.pallas import tpu as pltpu

F32 = jnp.float32
BF16 = jnp.bfloat16

D_GROUP = 512
A_HEADS, A_HD = 4, 128
B_HEADS, B_KV_HEADS, B_HD = 8, 2, 64
WINDOW = 128
ROPE_THETA = 500000.0
ROPE_DIM = 16
C_HEADS, C_QK, C_V = 4, 64, 128
CONV_W = 5
D_HEADS, D_HD = 4, 128
EPS = 1e-6
IN_SIZES = (
    D_GROUP, D_GROUP, D_GROUP, D_GROUP, D_GROUP,
    B_HEADS * B_HD, B_KV_HEADS * B_HD, B_KV_HEADS * B_HD, D_GROUP,
    C_HEADS * C_QK, C_HEADS * C_QK, C_HEADS * C_V, D_GROUP, D_GROUP,
    2 * C_HEADS, 2 * C_HEADS,
    D_GROUP, D_GROUP,
)
SPLIT_POINTS = tuple(int(s) for s in np.cumsum(IN_SIZES)[:-1])

LANE = 128
SUBLANE = 8
VMEM_LIMIT = 48 * 1024 * 1024

CHUNK = 128
ROW_TILE = 512
IN_ROW_TILE = 512
IN_PROJ_VMEM_LIMIT = 56 * 1024 * 1024
W_PREP_TILE = 256
HGRN_SUB = 32
HGRN_SAFE_LOG2 = 100.0
SWA_TILE = 512
REC_GROUP = 4
NEG = -1e30
LOG2_E = 1.4426950408889634

(BLK_A_Q, BLK_A_I, BLK_A_FF, BLK_A_FB, BLK_A_Z, BLK_B_Q, BLK_B_Z, BLK_C_QK, BLK_C_O,
 BLK_C_Z, BLK_D_Q, BLK_D_Z, BLK_MISC) = range(13)
N_BLK = 13
W_BLK_C_V = N_BLK
N_W_BLK = N_BLK + 1
MISC_K, MISC_V, MISC_G = 0, 128, 256
N_GATE = 4 * C_HEADS
C_ONES = 16


def _dot(a, b):
    return jnp.dot(a, b, preferred_element_type=F32)


def _dot_nt(a, b):
    return lax.dot_general(a, b, (((1,), (1,)), ((), ())), preferred_element_type=F32)


def _dot_tn(a, b):
    return lax.dot_general(a, b, (((0,), (0,)), ((), ())), preferred_element_type=F32)


def _sigmoid(x):
    return 1.0 / (1.0 + jnp.exp(-x))


def _silu(x):
    return x * _sigmoid(x)


def _log_sigmoid(x):
    return jnp.minimum(x, 0.0) - jnp.log(1.0 + jnp.exp(-jnp.abs(x)))


def _cumsum_rows(x, reverse):
    n = x.shape[0]
    row = lax.broadcasted_iota(jnp.int32, x.shape, 0)
    s = 1
    while s < n:
        if reverse:
            x = x + jnp.where(row < n - s, pltpu.roll(x, n - s, axis=0), 0.0)
        else:
            x = x + jnp.where(row >= s, pltpu.roll(x, s, axis=0), 0.0)
        s *= 2
    return x


def _cumsum_lanes(x, reverse):
    n = x.shape[1]
    col = lax.broadcasted_iota(jnp.int32, x.shape, 1)
    s = 1
    while s < n:
        if reverse:
            x = x + jnp.where(col < n - s, pltpu.roll(x, n - s, axis=1), 0.0)
        else:
            x = x + jnp.where(col >= s, pltpu.roll(x, s, axis=1), 0.0)
        s *= 2
    return x


def _in_proj_kernel(x_ref, g_ref, w_ref, wgt_ref, u_ref, gt_ref, vt_ref, *, layer):
    x = x_ref[...]
    ms = jnp.mean(x * x, axis=-1, keepdims=True)
    h = (x * lax.rsqrt(ms + EPS) * g_ref[layer:layer + 1, :]).astype(BF16)
    gt_ref[...] = _dot_nt(wgt_ref[...], h)
    vt_ref[...] = _dot_nt(w_ref[W_BLK_C_V * D_GROUP:(W_BLK_C_V + 1) * D_GROUP, :], h).astype(BF16)
    for j in range(N_BLK):
        sl = slice(j * D_GROUP, (j + 1) * D_GROUP)
        u_ref[:, sl] = _dot_nt(h, w_ref[sl, :])


def _in_proj(xf, g_all, w_cat_all, w_gt_all, layer, batch):
    rows, d = xf.shape
    tm = IN_ROW_TILE
    n_out = N_BLK * D_GROUP
    per_batch = rows // batch // tm
    return pl.pallas_call(
        functools.partial(_in_proj_kernel, layer=layer),
        out_shape=(jax.ShapeDtypeStruct((rows, n_out), F32),
                   jax.ShapeDtypeStruct((batch, N_GATE, rows // batch), F32),
                   jax.ShapeDtypeStruct((batch, D_GROUP, rows // batch), BF16)),
        grid=(rows // tm,),
        in_specs=[
            pl.BlockSpec((tm, d), lambda i: (i, 0)),
            pl.BlockSpec(g_all.shape, lambda i: (0, 0)),
            pl.BlockSpec((None, N_W_BLK * D_GROUP, d), lambda i: (layer, 0, 0),
                         pipeline_mode=pl.Buffered(1)),
            pl.BlockSpec((None, N_GATE, d), lambda i: (layer, 0, 0)),
        ],
        out_specs=(
            pl.BlockSpec((tm, n_out), lambda i: (i, 0)),
            pl.BlockSpec((None, N_GATE, tm), lambda i: (i // per_batch, 0, i % per_batch)),
            pl.BlockSpec((None, D_GROUP, tm), lambda i: (i // per_batch, 0, i % per_batch)),
        ),
        compiler_params=pltpu.CompilerParams(
            dimension_semantics=("parallel",), vmem_limit_bytes=IN_PROJ_VMEM_LIMIT),
        name="in_proj",
    )(xf, g_all, w_cat_all, w_gt_all)


def _w_in_segments():
    names = ("a_q", "a_i", "a_ff", "a_fb", "a_z", "b_q", "b_k", "b_v", "b_z",
             "c_q", "c_k", "c_v", "c_o", "c_z", "c_ig", "c_fg", "d_q", "d_z")
    src = dict(zip(names, zip((0,) + SPLIT_POINTS, IN_SIZES)))
    order = ("a_q", "a_i", "a_ff", "a_fb", "a_z", "b_q", "b_z", "c_q", "c_k", "c_o", "c_z",
             "d_q", "d_z", "b_k", "b_v", "c_ig", "c_fg")
    segs, dst = [], 0
    for name in order + ("c_v",):
        off, width = src[name]
        if name == "c_v":
            pad = (dst, W_BLK_C_V * D_GROUP)
            dst = pad[1]
        if segs and segs[-1][0] + segs[-1][1] == off and segs[-1][2] + segs[-1][1] == dst:
            segs[-1] = (segs[-1][0], segs[-1][1] + width, segs[-1][2])
        else:
            segs.append((off, width, dst))
        dst += width
    return tuple(segs), pad, src["c_ig"][0]


def _w_prep_kernel(wt_ref, wc_ref, wg_ref):
    segs, pad, gate_off = _w_in_segments()
    for off, width, dst in segs:
        wc_ref[dst:dst + width, :] = wt_ref[off:off + width, :].astype(BF16)
    wc_ref[pad[0]:pad[1], :] = jnp.zeros((pad[1] - pad[0], wc_ref.shape[1]), BF16)
    wg_ref[...] = wt_ref[gate_off:gate_off + N_GATE, :].astype(BF16)


def _prep_w_in(w):
    depth, d, d_in = w.shape
    wt = jnp.swapaxes(w, 1, 2)
    tk = W_PREP_TILE
    n_out = N_W_BLK * D_GROUP
    return pl.pallas_call(
        _w_prep_kernel,
        out_shape=(jax.ShapeDtypeStruct((depth, n_out, d), BF16),
                   jax.ShapeDtypeStruct((depth, N_GATE, d), BF16)),
        grid=(depth, d // tk),
        in_specs=[pl.BlockSpec((None, d_in, tk), lambda l, i: (l, 0, i))],
        out_specs=(pl.BlockSpec((None, n_out, tk), lambda l, i: (l, 0, i)),
                   pl.BlockSpec((None, N_GATE, tk), lambda l, i: (l, 0, i))),
        compiler_params=pltpu.CompilerParams(
            dimension_semantics=("parallel", "parallel"), vmem_limit_bytes=VMEM_LIMIT),
        name="w_prep",
    )(wt)


def _out_proj_kernel(ya_ref, yb_ref, yc_ref, yd_ref, w_ref, x_ref, g_ref, o_ref, *, final):
    acc = x_ref[...]
    for i, y_ref in enumerate((ya_ref, yb_ref, yc_ref, yd_ref)):
        acc = acc + _dot(y_ref[...], w_ref[i * D_GROUP:(i + 1) * D_GROUP, :].astype(BF16))
    if final:
        ms = jnp.mean(acc * acc, axis=-1, keepdims=True)
        acc = acc * lax.rsqrt(ms + EPS) * g_ref[...]
    o_ref[...] = acc


def _out_proj(ya, yb, yc, yd, w_all, xf, g, layer, final):
    rows, d = xf.shape
    tm = ROW_TILE
    yspec = pl.BlockSpec((tm, D_GROUP), lambda i: (i, 0))
    return pl.pallas_call(
        functools.partial(_out_proj_kernel, final=final),
        out_shape=jax.ShapeDtypeStruct((rows, d), F32),
        grid=(rows // tm,),
        in_specs=[yspec, yspec, yspec, yspec,
                  pl.BlockSpec((None, 4 * D_GROUP, d), lambda i: (layer, 0, 0)),
                  pl.BlockSpec((tm, d), lambda i: (i, 0)),
                  pl.BlockSpec((1, d), lambda i: (0, 0))],
        out_specs=pl.BlockSpec((tm, d), lambda i: (i, 0)),
        compiler_params=pltpu.CompilerParams(
            dimension_semantics=("parallel",), vmem_limit_bytes=VMEM_LIMIT),
        name="out_proj",
    )(ya, yb, yc, yd, w_all, xf, g.reshape(1, d))


def _level_ref(b, h, reverse):
    n_rows, n = b.shape
    blk = 2 * h
    r0 = h if reverse else h - 1
    if blk >= SUBLANE:
        pieces = [jnp.broadcast_to(b[s + r0:s + r0 + 1, :], (blk, n)) for s in range(0, n_rows, blk)]
        return pieces[0] if len(pieces) == 1 else jnp.concatenate(pieces, axis=0)
    res = lax.broadcasted_iota(jnp.int32, b.shape, 0) & (blk - 1)
    out = None
    for m in range(blk):
        shift = (m - r0) % n_rows
        cand = b if shift == 0 else pltpu.roll(b, shift, axis=0)
        out = cand if out is None else jnp.where(res == m, cand, out)
    return out


def _hgrn_lower_bound(lbl_ref, layer, depth):
    rows = [lbl_ref[j:j + 1, :] for j in range(depth)]
    mx = functools.reduce(jnp.maximum, rows)
    es = [jnp.exp(r - mx) for r in rows]
    tot = functools.reduce(lambda a, c: a + c, es)
    lb = jnp.zeros_like(rows[0])
    for j in range(1, layer + 1):
        lb = lb + es[j] / tot
    return lb


def _cumsum_rows_mxu(x, reverse):
    n = x.shape[0]
    r = lax.broadcasted_iota(jnp.int32, (n, n), 0)
    c = lax.broadcasted_iota(jnp.int32, (n, n), 1)
    tri = jnp.where((c >= r) if reverse else (c <= r), 1.0, 0.0).astype(BF16)
    hi = x.astype(BF16)
    r1 = x - hi.astype(F32)
    mid = r1.astype(BF16)
    lo = (r1 - mid.astype(F32)).astype(BF16)
    return _dot(jnp.concatenate([tri, tri, tri], axis=1), jnp.concatenate([hi, mid, lo], axis=0))


def _hgrn_level_operand(q, k, f, b2, h, reverse):
    n_rows = q.shape[0]
    blk = 2 * h
    r0 = h if reverse else h - 1
    if h >= SUBLANE:
        pieces = []
        for s0 in range(0, n_rows, blk):
            ref = b2[s0 + r0:s0 + r0 + 1, :]
            lo, up = slice(s0, s0 + h), slice(s0 + h, s0 + blk)
            if reverse:
                pieces += [q[lo] * jnp.exp2(b2[lo] - ref), k[up] * jnp.exp2(ref - b2[up])]
            else:
                pieces += [k[lo] * jnp.exp2(ref - b2[lo]), q[up] * jnp.exp2(b2[up] - ref)]
        return jnp.concatenate(pieces, axis=0)
    row = lax.broadcasted_iota(jnp.int32, q.shape, 0)
    is_q = ((row & h) == 0) if reverse else ((row & h) != 0)
    qk = jnp.where(is_q, q, k)
    if h == 1:
        return jnp.where(is_q, q * f, k)
    if h == 2:
        res = row & 3
        f_dn = pltpu.roll(f, 1, axis=0)
        f_up = pltpu.roll(f, n_rows - 1, axis=0)
        if reverse:
            e = jnp.where(res == 0, f * f_up, jnp.where(res == 1, f, jnp.where(res == 2, 1.0, f_dn)))
        else:
            e = jnp.where(res == 0, f_up, jnp.where(res == 1, 1.0, jnp.where(res == 2, f, f_dn * f)))
        return qk * e
    return qk * jnp.exp2(-jnp.abs(b2 - _level_ref(b2, h, reverse)))


def _hgrn_kernel(*refs, layer, depth, reverse, final):
    if final:
        q_ref, v_ref, zg_ref, lbl_ref, z_ref, ob_ref, ng_ref, out_ref, st_ref = refs
    else:
        q_ref, v_ref, zg_ref, lbl_ref, out_ref, st_ref = refs

    @pl.when(pl.program_id(1) == 0)
    def _():
        st_ref[...] = jnp.zeros_like(st_ref)

    rows = range(q_ref.shape[0])
    n_rows = q_ref.shape[1]
    ti = lax.broadcasted_iota(jnp.int32, (n_rows, n_rows), 0)
    si = lax.broadcasted_iota(jnp.int32, (n_rows, n_rows), 1)
    ahead = (ti < si) if reverse else (ti > si)
    level = jnp.where(ahead, 31 - lax.clz(ti ^ si), jnp.where(ti == si, -1, -2))
    preps = [_hgrn_prep(q_ref.at[g], zg_ref.at[g], lbl_ref, layer, depth, reverse) for g in rows]

    def run(fast):
        outs = [[] for _ in rows]
        for hd in range(A_HEADS):
            sl = slice(hd * A_HD, (hd + 1) * A_HD)
            heads = [tuple(a[:, sl] for a in preps[g]) for g in rows]
            if fast:
                atts = [_hgrn_scores_fast(heads[g], level, reverse) for g in rows]
            else:
                atts = [_hgrn_scores(heads[g], level, n_rows, None, reverse) for g in rows]
                h = n_rows // 2
                while h >= 1:
                    atts = [_hgrn_scores(heads[g], level, h, atts[g], reverse) for g in rows]
                    h //= 2
            for g in rows:
                outs[g].append(_hgrn_apply(heads[g], atts[g], v_ref.at[g, :, sl],
                                           st_ref.at[g, hd], reverse))
        for g in rows:
            o = jnp.concatenate(outs[g], axis=1)
            if final:
                _hgrn_finish(o, z_ref.at[g], ob_ref.at[g], ng_ref.at[layer:layer + 1],
                             out_ref.at[g])
            else:
                out_ref[g] = o

    worst = None
    for g in rows:
        b2 = preps[g][3]
        for s0 in range(0, n_rows, HGRN_SUB):
            span = jnp.abs(b2[s0 + HGRN_SUB - 1:s0 + HGRN_SUB, :] - b2[s0:s0 + 1, :])
            worst = span if worst is None else jnp.maximum(worst, span)
    safe = jnp.max(worst) <= HGRN_SAFE_LOG2

    @pl.when(safe)
    def _():
        run(True)

    @pl.when(jnp.logical_not(safe))
    def _():
        run(False)


def _hgrn_prep(q_ref, zg_ref, lbl_ref, layer, depth, reverse):
    n_rows = q_ref.shape[0]
    q = q_ref[...]
    zg = zg_ref[...]
    e = jnp.exp(-jnp.abs(zg))
    r = 1.0 / (1.0 + e)
    er = e * r
    pos = zg >= 0.0
    f = jnp.where(pos, r, er)
    k = jnp.where(pos, er, r)
    if layer > 0:
        lb = _hgrn_lower_bound(lbl_ref, layer, depth)
        f = lb + (1.0 - lb) * f
        k = (1.0 - lb) * k
    b2 = _cumsum_rows_mxu(jnp.log2(f), reverse)
    return q, k, f, b2


def _hgrn_scores(head, level, h, att, reverse):
    q, k, f, b2 = head
    if att is None:
        return jnp.where(level == -1, _dot_nt(q.astype(BF16), k.astype(BF16)), 0.0)
    c = _hgrn_level_operand(q, k, f, b2, h, reverse).astype(BF16)
    return jnp.where(level == (h.bit_length() - 1), _dot_nt(c, c), att)


def _hgrn_scores_fast(head, level, reverse):
    q, k, _, b2 = head
    n_rows = q.shape[0]
    blocks = []
    for s0 in range(0, n_rows, HGRN_SUB):
        rows = slice(s0, s0 + HGRN_SUB)
        ref = b2[s0 + HGRN_SUB - 1:s0 + HGRN_SUB, :] if reverse else b2[s0:s0 + 1, :]
        qs = (q[rows] * jnp.exp2(b2[rows] - ref)).astype(BF16)
        ks = (k * jnp.exp2(jnp.minimum(ref - b2, HGRN_SAFE_LOG2))).astype(BF16)
        blocks.append(_dot_nt(qs, ks))
    return jnp.where(level >= -1, jnp.concatenate(blocks, axis=0), 0.0)


def _hgrn_apply(head, att, v_ref, st_ref, reverse):
    q, k, _, b2 = head
    n_rows = q.shape[0]
    tot = b2[0:1, :] if reverse else b2[n_rows - 1:n_rows, :]
    v_bf = v_ref[...].astype(BF16)
    q_in = (q * jnp.exp2(b2)).astype(BF16)
    k_out = (k * jnp.exp2(tot - b2)).astype(BF16)
    st = st_ref[...]
    o = _dot(att.astype(BF16), v_bf) + _dot_nt(q_in, st.astype(BF16))
    st_ref[...] = jnp.exp2(tot) * st + _dot_tn(v_bf, k_out)
    return o


def _hgrn_finish(o, z_ref, ob_ref, ng_ref, out_ref):
    o = o + ob_ref[...]
    ys = []
    for hd in range(A_HEADS):
        sl = slice(hd * A_HD, (hd + 1) * A_HD)
        oh = o[:, sl]
        ms = jnp.mean(oh * oh, axis=-1, keepdims=True)
        ys.append(oh * lax.rsqrt(ms + EPS) * ng_ref[:, sl])
    y = jnp.concatenate(ys, axis=1) * _silu(z_ref[...])
    out_ref[...] = y.astype(out_ref.dtype)


def _chunk_index(c, n_chunks, reverse):
    return n_chunks - 1 - c if reverse else c


def _hgrn(u3, lb_logits, layer, *, reverse, ob=None, norm_g=None):
    batch, seq, _ = u3.shape
    n_chunks = seq // CHUNK
    depth = lb_logits.shape[0]
    final = ob is not None
    grp = REC_GROUP

    def ublk(blk):
        return pl.BlockSpec((grp, CHUNK, D_GROUP),
                            lambda bg, c: (bg, _chunk_index(c, n_chunks, reverse), blk))

    row_spec = ublk(0)
    in_specs = [ublk(BLK_A_Q), ublk(BLK_A_I), ublk(BLK_A_FB if reverse else BLK_A_FF),
                pl.BlockSpec((depth, D_GROUP), lambda bg, c: (0, 0))]
    args = [u3, u3, u3, lb_logits]
    if final:
        in_specs += [ublk(BLK_A_Z), row_spec, pl.BlockSpec(norm_g.shape, lambda bg, c: (0, 0))]
        args += [u3, ob, norm_g]
    return pl.pallas_call(
        functools.partial(_hgrn_kernel, layer=layer, depth=depth, reverse=reverse, final=final),
        out_shape=jax.ShapeDtypeStruct((batch, seq, D_GROUP), BF16 if final else F32),
        grid=(batch // grp, n_chunks),
        in_specs=in_specs,
        out_specs=row_spec,
        scratch_shapes=[pltpu.VMEM((grp, A_HEADS, A_HD, A_HD), F32)],
        compiler_params=pltpu.CompilerParams(
            dimension_semantics=("parallel", "arbitrary"), vmem_limit_bytes=VMEM_LIMIT),
        name="hgrn_fwd" if final else "hgrn_bwd",
    )(*args)


def _split3(x):
    hi = x.astype(BF16).astype(F32)
    r1 = x - hi
    mid = r1.astype(BF16).astype(F32)
    lo = (r1 - mid).astype(BF16).astype(F32)
    return hi, mid, lo


def _cumsum_lanes_mxu(x, reverse):
    n = x.shape[1]
    j = lax.broadcasted_iota(jnp.int32, (n, n), 0)
    t = lax.broadcasted_iota(jnp.int32, (n, n), 1)
    tri = jnp.where((j >= t) if reverse else (j <= t), 1.0, 0.0).astype(BF16)
    pieces = jnp.concatenate(list(_split3(x)), axis=1).astype(BF16)
    return _dot(pieces, jnp.concatenate([tri, tri, tri], axis=0))


def _conv_silu_qk(qkp_ref, qkc_ref, qkn_ref, cw_ref, reverse):
    n_rows = qkc_ref.shape[0]
    c = pl.program_id(1)
    n_chunks = pl.num_programs(1)
    cc = _chunk_index(c, n_chunks, reverse)
    halo = SUBLANE
    prev = qkp_ref[n_rows - halo:n_rows, :] * (cc > 0).astype(F32)
    nxt = qkn_ref[0:halo, :] * (cc < n_chunks - 1).astype(F32)
    xcat = jnp.concatenate([prev, qkc_ref[...], nxt], axis=0)
    acc = None
    for j in range(CONV_W):
        off = halo + j - CONV_W // 2
        term = cw_ref[j:j + 1, :] * xcat[off:off + n_rows, :]
        acc = term if acc is None else acc + term
    qk = _silu(acc)
    n_qk = C_HEADS * C_QK
    return jnp.concatenate([qk[:, :n_qk] * (C_QK ** -0.5), qk[:, n_qk:]], axis=1).astype(BF16)


def _mlstm_kernel(*refs, layer, reverse, final):
    if final:
        (qk_ref, v_ref, gt_ref, gbc_ref, og_ref, z_ref, hb_ref, ng_ref,
         out_ref, cst_ref, m_ref) = refs
    else:
        (qkp_ref, qkc_ref, qkn_ref, cw_ref, v_ref, gt_ref, gbc_ref,
         out_ref, qko_ref, cst_ref, m_ref) = refs

    @pl.when(pl.program_id(1) == 0)
    def _():
        cst_ref[...] = jnp.zeros_like(cst_ref)
        m_ref[...] = jnp.zeros_like(m_ref)

    rows = range(v_ref.shape[0])
    if final:
        qks = [qk_ref[g] for g in rows]
    else:
        qks = [_conv_silu_qk(qkp_ref.at[g], qkc_ref.at[g], qkn_ref.at[g], cw_ref.at[layer], reverse)
               for g in rows]
        for g in rows:
            qko_ref[g] = qks[g]
    gates = [_mlstm_gates(gt_ref.at[g], gbc_ref.at[layer], m_ref.at[g], reverse) for g in rows]
    outs = [[] for _ in rows]
    for hd in range(C_HEADS):
        weights = [_mlstm_weights(hd, qks[g], gates[g], reverse) for g in rows]
        for g in rows:
            h_t = _mlstm_apply(hd, qks[g], v_ref.at[g], gates[g], weights[g], cst_ref.at[g], reverse)
            if final:
                outs[g].append((h_t + hb_ref[g, hd * C_V:(hd + 1) * C_V, :]).T)
            else:
                out_ref[g, hd * C_V:(hd + 1) * C_V, :] = h_t
    if final:
        for g in rows:
            _mlstm_finish(jnp.concatenate(outs[g], axis=1), og_ref.at[g], z_ref.at[g],
                          ng_ref.at[layer:layer + 1], out_ref.at[g])


def _mlstm_gates(gt_ref, gbc_ref, m_ref, reverse):
    n_rows = gt_ref.shape[1]
    n_dir = 2 * C_HEADS

    gr = gt_ref[...] + gbc_ref[...]
    li = gr[0:n_dir, :]
    b = _cumsum_lanes_mxu(_log_sigmoid(gr[n_dir:, :]), reverse)
    b_last = b[:, 0:1] if reverse else b[:, n_rows - 1:n_rows]
    m_prev = m_ref[...]
    d_inter = b + m_prev
    a = (b_last - b) + li
    m_new = jnp.maximum(b_last + m_prev, jnp.max(a, axis=1, keepdims=True))
    ws = jnp.exp(a - m_new)
    decay = jnp.exp((b_last + m_prev) - m_new)
    m_ref[...] = m_new

    one8 = jnp.ones((n_dir, n_rows), F32)
    lhs = jnp.concatenate(list(_split3(li - b)) + [one8, one8, one8], axis=0).T.astype(BF16)
    rhs_all = jnp.concatenate([one8, one8, one8] + list(_split3(b)), axis=0)
    return lhs, rhs_all, d_inter, ws, decay


def _mlstm_weights(hd, qk, gates, reverse):
    lhs, rhs_all, d_inter, _, _ = gates
    n_rows = qk.shape[0]
    n_qk = C_HEADS * C_QK
    n_dir = 2 * C_HEADS
    gi = (C_HEADS if reverse else 0) + hd
    gate_row = lax.broadcasted_iota(jnp.int32, rhs_all.shape, 0) & (n_dir - 1)
    si = lax.broadcasted_iota(jnp.int32, (n_rows, n_rows), 0)
    ti = lax.broadcasted_iota(jnp.int32, (n_rows, n_rows), 1)
    causal = (si >= ti) if reverse else (si <= ti)

    q_h = qk[:, hd * C_QK:(hd + 1) * C_QK]
    k_h = qk[:, n_qk + hd * C_QK:n_qk + (hd + 1) * C_QK]
    rhs = jnp.where(gate_row == gi, rhs_all, 0.0).astype(BF16)
    d = jnp.where(causal, _dot(lhs, rhs), NEG)
    d_int = d_inter[gi:gi + 1, :]
    m_t = jnp.maximum(jnp.max(d, axis=0, keepdims=True), d_int)
    w = (_dot_nt(k_h, q_h) * jnp.exp(d - m_t)).astype(BF16)
    return w, m_t, jnp.exp(d_int - m_t)


def _mlstm_apply(hd, qk, vt_ref, gates, weights, cst_ref, reverse):
    _, _, _, ws, decay = gates
    w, m_t, g = weights
    n_rows = qk.shape[0]
    n_qk = C_HEADS * C_QK
    gi = (C_HEADS if reverse else 0) + hd
    q_h = qk[:, hd * C_QK:(hd + 1) * C_QK]
    k_h = qk[:, n_qk + hd * C_QK:n_qk + (hd + 1) * C_QK]
    v_t = vt_ref[hd * C_V:(hd + 1) * C_V, :]
    v_ext = jnp.concatenate([v_t, jnp.ones((C_ONES, n_rows), BF16)], axis=0)
    ws_row = ws[gi:gi + 1, :]
    v_scaled = jnp.concatenate([(v_t.astype(F32) * ws_row).astype(BF16),
                                jnp.broadcast_to(ws_row, (C_ONES, n_rows)).astype(BF16)], axis=0)
    cst = cst_ref[hd]
    num = _dot(v_ext, w) + g * _dot_nt(cst.astype(BF16), q_h)
    den = jnp.maximum(jnp.abs(num[C_V:C_V + 1, :]), jnp.exp(-m_t))
    cst_ref[hd] = decay[gi:gi + 1, :C_QK] * cst + _dot(v_scaled, k_h)
    return num[:C_V, :] / den


def _mlstm_finish(hsum, og_ref, z_ref, ng_ref, out_ref):
    hsum = hsum * _sigmoid(og_ref[...])
    ys = []
    for hd in range(C_HEADS):
        sl = slice(hd * C_V, (hd + 1) * C_V)
        hh = hsum[:, sl]
        mu = jnp.mean(hh, axis=-1, keepdims=True)
        var = jnp.mean(jnp.square(hh - mu), axis=-1, keepdims=True)
        ys.append((hh - mu) * lax.rsqrt(var + EPS) * ng_ref[:, sl])
    y = jnp.concatenate(ys, axis=1) * _silu(z_ref[...])
    out_ref[...] = y.astype(out_ref.dtype)


def _mlstm(u3, gt, vt, conv_w, gate_b, layer, *, reverse, qk=None, hb=None, norm_g=None):
    batch, seq, _ = u3.shape
    n_chunks = seq // CHUNK
    final = hb is not None
    grp = REC_GROUP
    assert CHUNK == LANE

    def chunk(c, delta=0):
        return jnp.clip(_chunk_index(c, n_chunks, reverse) + delta, 0, n_chunks - 1)

    def ublk(blk, delta=0):
        return pl.BlockSpec((grp, CHUNK, D_GROUP), lambda bg, c: (bg, chunk(c, delta), blk))

    row_spec = ublk(0)
    t_spec = lambda ch: pl.BlockSpec((grp, ch, CHUNK), lambda bg, c: (bg, 0, chunk(c)))
    gate_specs = [t_spec(N_GATE), pl.BlockSpec(gate_b.shape, lambda bg, c: (0, 0, 0))]
    gate_args = [gt, gate_b]
    if final:
        in_specs = [row_spec, t_spec(D_GROUP)] + gate_specs + [
            ublk(BLK_C_O), ublk(BLK_C_Z), t_spec(D_GROUP),
            pl.BlockSpec(norm_g.shape, lambda bg, c: (0, 0))]
        args = [qk, vt] + gate_args + [u3, u3, hb, norm_g]
        out_shape = jax.ShapeDtypeStruct((batch, seq, D_GROUP), BF16)
        out_specs = row_spec
    else:
        in_specs = [ublk(BLK_C_QK, -1), ublk(BLK_C_QK), ublk(BLK_C_QK, 1),
                    pl.BlockSpec(conv_w.shape, lambda bg, c: (0, 0, 0)),
                    t_spec(D_GROUP)] + gate_specs
        args = [u3, u3, u3, conv_w, vt] + gate_args
        out_shape = (jax.ShapeDtypeStruct((batch, D_GROUP, seq), F32),
                     jax.ShapeDtypeStruct((batch, seq, D_GROUP), BF16))
        out_specs = (t_spec(D_GROUP), row_spec)
    return pl.pallas_call(
        functools.partial(_mlstm_kernel, layer=layer, reverse=reverse, final=final),
        out_shape=out_shape,
        grid=(batch // grp, n_chunks),
        in_specs=in_specs,
        out_specs=out_specs,
        scratch_shapes=[pltpu.VMEM((grp, C_HEADS, C_V + C_ONES, C_QK), F32),
                        pltpu.VMEM((grp, 2 * C_HEADS, CHUNK), F32)],
        compiler_params=pltpu.CompilerParams(
            dimension_semantics=("parallel", "arbitrary"), vmem_limit_bytes=VMEM_LIMIT),
        name="mlstm_fwd" if final else "mlstm_bwd",
    )(*args)


def _rope_table_kernel(pos_ref, invf_ref, cos_ref, sin_ref):
    ang = invf_ref[...] * pos_ref[...].astype(F32)
    n_rows = ang.shape[1]
    half = ROPE_DIM // 2
    r = lax.broadcasted_iota(jnp.int32, ang.shape, 0)
    sn = jnp.sin(ang)
    sn = jnp.where(r < half, -sn, sn)
    pad = jnp.zeros((LANE - ROPE_DIM, n_rows), F32)

    def expand(x):
        t = jnp.concatenate([x, pad], axis=0).T
        return t + pltpu.roll(t, B_HD, axis=1)

    lane = lax.broadcasted_iota(jnp.int32, (n_rows, LANE), 1) & (B_HD - 1)
    cos_ref[...] = jnp.where(lane < ROPE_DIM, expand(jnp.cos(ang)), 1.0)
    sin_ref[...] = expand(sn)


def _rope_tables(positions):
    rows = positions.size
    half = ROPE_DIM // 2
    inv_freq = ROPE_THETA ** (-jnp.arange(0, ROPE_DIM, 2, dtype=F32) / ROPE_DIM)
    invf = jnp.concatenate([inv_freq, inv_freq]).reshape(ROPE_DIM, 1)
    tm = ROW_TILE
    return pl.pallas_call(
        _rope_table_kernel,
        out_shape=(jax.ShapeDtypeStruct((rows, LANE), F32),) * 2,
        grid=(rows // tm,),
        in_specs=[pl.BlockSpec((1, tm), lambda i: (0, i)),
                  pl.BlockSpec((ROPE_DIM, 1), lambda i: (0, 0))],
        out_specs=(pl.BlockSpec((tm, LANE), lambda i: (i, 0)),) * 2,
        compiler_params=pltpu.CompilerParams(dimension_semantics=("parallel",)),
        name="rope_tables",
    )(positions.reshape(1, rows), invf)


def _rope(x, cos, sin):
    width = x.shape[1]
    if width > LANE:
        cos = jnp.concatenate([cos] * (width // LANE), axis=1)
        sin = jnp.concatenate([sin] * (width // LANE), axis=1)
    half = ROPE_DIM // 2
    r = lax.broadcasted_iota(jnp.int32, x.shape, 1) & (B_HD - 1)
    partner = jnp.where(r < half, pltpu.roll(x, width - half, axis=1), pltpu.roll(x, half, axis=1))
    return x * cos + partner * sin


def _swa_kernel(q_ref, z_ref, mp_ref, mc_ref, mn_ref, cp_ref, cc_ref, cn_ref,
                sp_ref, sc_ref, sn_ref, sink_ref, out_ref, *, layer):
    tile = q_ref.shape[0]
    blk = mp_ref.shape[0]
    n_sub = tile // blk
    i = pl.program_id(1)
    n_tiles = pl.num_programs(1)
    kv_w = B_KV_HEADS * B_HD
    grp = B_HEADS // B_KV_HEADS

    q = (_rope(q_ref[...], cc_ref[...], sc_ref[...]) * (B_HD ** -0.5)).astype(BF16)
    k = jnp.concatenate([
        _rope(mp_ref[:, MISC_K:MISC_K + kv_w], cp_ref[...], sp_ref[...]),
        _rope(mc_ref[:, MISC_K:MISC_K + kv_w], cc_ref[...], sc_ref[...]),
        _rope(mn_ref[:, MISC_K:MISC_K + kv_w], cn_ref[...], sn_ref[...])], axis=0).astype(BF16)
    v = jnp.concatenate([mp_ref[:, MISC_V:MISC_V + kv_w], mc_ref[:, MISC_V:MISC_V + kv_w],
                         mn_ref[:, MISC_V:MISC_V + kv_w]], axis=0)
    n_keys = v.shape[0]

    kr = lax.broadcasted_iota(jnp.int32, (3 * blk, blk), 0)
    qc = lax.broadcasted_iota(jnp.int32, (3 * blk, blk), 1)
    in_window = jnp.abs(qc - (kr - blk)) <= WINDOW
    biases = []
    for j in range(n_sub):
        valid = in_window
        if j == 0:
            valid = valid & (kr >= jnp.where(i > 0, 0, blk))
        if j == n_sub - 1:
            valid = valid & (kr < jnp.where(i < n_tiles - 1, 3 * blk, 2 * blk))
        biases.append(jnp.where(valid, 0.0, NEG))

    ones = jnp.ones((n_keys, B_HD), F32)
    vts = [jnp.concatenate([v[:, kvh * B_HD:(kvh + 1) * B_HD], ones], axis=1).T.astype(BF16)
           for kvh in range(B_KV_HEADS)]

    tiles = [[] for _ in range(n_sub)]
    pairs = [[] for _ in range(n_sub)]
    for hq in range(B_HEADS):
        kvh = hq // grp
        ksl = slice(kvh * B_HD, (kvh + 1) * B_HD)
        sk = sink_ref[layer:layer + 1, hq:hq + 1]
        for j in range(n_sub):
            keys = slice(j * blk, (j + 3) * blk)
            qh = q[j * blk:(j + 1) * blk, hq * B_HD:(hq + 1) * B_HD]
            s = _dot_nt(k[keys, ksl], qh) + biases[j]
            m = jnp.maximum(jnp.max(s, axis=0, keepdims=True), sk)
            p = jnp.exp(s - m).astype(BF16)
            o = _dot(vts[kvh][:, keys], p)
            denom = o[B_HD:B_HD + 1, :] + jnp.exp(sk - m)
            pairs[j].append(o[:B_HD, :] / denom)
            if len(pairs[j]) == LANE // B_HD:
                tiles[j].append(jnp.concatenate(pairs[j], axis=0).T)
                pairs[j] = []
    for j in range(n_sub):
        rows = slice(j * blk, (j + 1) * blk)
        y = jnp.concatenate(tiles[j], axis=1) * _silu(z_ref[rows, :])
        out_ref[rows, :] = y.astype(out_ref.dtype)


def _swa(u, cos_t, sin_t, sink, layer, batch):
    rows = u.shape[0]
    n_blocks = rows // batch // CHUNK
    n_sub = SWA_TILE // CHUNK
    n_tiles = n_blocks // n_sub

    def tile_spec(width, blk):
        return pl.BlockSpec((SWA_TILE, width), lambda bi, i: (bi * n_tiles + i, blk))

    def halo_spec(width, blk, after):
        def index(bi, i):
            nb = (i + 1) * n_sub if after else i * n_sub - 1
            return (bi * n_blocks + jnp.clip(nb, 0, n_blocks - 1), blk)
        return pl.BlockSpec((CHUNK, width), index)

    def table_specs():
        return [halo_spec(LANE, 0, False), tile_spec(LANE, 0), halo_spec(LANE, 0, True)]

    return pl.pallas_call(
        functools.partial(_swa_kernel, layer=layer),
        out_shape=jax.ShapeDtypeStruct((rows, D_GROUP), BF16),
        grid=(batch, n_tiles),
        in_specs=[tile_spec(D_GROUP, BLK_B_Q), tile_spec(D_GROUP, BLK_B_Z),
                  halo_spec(D_GROUP, BLK_MISC, False), tile_spec(D_GROUP, BLK_MISC),
                  halo_spec(D_GROUP, BLK_MISC, True)] + table_specs() + table_specs() + [
                  pl.BlockSpec(sink.shape, lambda bi, i: (0, 0))],
        out_specs=tile_spec(D_GROUP, 0),
        compiler_params=pltpu.CompilerParams(
            dimension_semantics=("parallel", "parallel"), vmem_limit_bytes=VMEM_LIMIT),
        name="window_attn",
    )(u, u, u, u, u, cos_t, cos_t, cos_t, sin_t, sin_t, sin_t, sink)


def _mem_kv_kernel(mem_ref, g_ref, w_ref, k_ref, vt_ref, *, layer):
    x = mem_ref[...]
    ms = jnp.mean(x * x, axis=-1, keepdims=True)
    h = (x * lax.rsqrt(ms + EPS) * g_ref[layer:layer + 1, :]).astype(BF16)
    kv = _dot(h, w_ref[...].astype(BF16))
    k_ref[...] = kv[:, :D_GROUP].astype(k_ref.dtype)
    vt_ref[...] = kv[:, D_GROUP:].T.astype(vt_ref.dtype)


def _mem_kv(mem, g_all, w_all, layer):
    batch, m_len, d = mem.shape
    n_out = w_all.shape[2]
    return pl.pallas_call(
        functools.partial(_mem_kv_kernel, layer=layer),
        out_shape=(jax.ShapeDtypeStruct((batch, m_len, D_GROUP), BF16),
                   jax.ShapeDtypeStruct((batch, D_GROUP, m_len), BF16)),
        grid=(batch,),
        in_specs=[pl.BlockSpec((None, m_len, d), lambda bi: (bi, 0, 0)),
                  pl.BlockSpec(g_all.shape, lambda bi: (0, 0)),
                  pl.BlockSpec((None, d, n_out), lambda bi: (layer, 0, 0))],
        out_specs=(pl.BlockSpec((None, m_len, D_GROUP), lambda bi: (bi, 0, 0)),
                   pl.BlockSpec((None, D_GROUP, m_len), lambda bi: (bi, 0, 0))),
        compiler_params=pltpu.CompilerParams(
            dimension_semantics=("parallel",), vmem_limit_bytes=VMEM_LIMIT),
        name="mem_kv",
    )(mem, g_all, w_all)


def _mem_attn_kernel(q_ref, z_ref, k_ref, vt_ref, out_ref):
    q = q_ref[...].astype(BF16)
    outs = []
    for hd in range(D_HEADS):
        sl = slice(hd * D_HD, (hd + 1) * D_HD)
        s = _dot_nt(q[:, sl], k_ref[:, sl])
        m = jnp.max(s, axis=-1, keepdims=True)
        p = jnp.exp2((s - m) * (D_HD ** -0.5 * LOG2_E))
        den = jnp.sum(p, axis=-1, keepdims=True)
        outs.append(_dot_nt(p.astype(BF16), vt_ref[sl, :]) / den)
    y = jnp.concatenate(outs, axis=1) * _silu(z_ref[...])
    out_ref[...] = y.astype(out_ref.dtype)


def _mem_attn(u, k, vt, batch):
    rows = u.shape[0]
    tm = ROW_TILE
    per_batch = rows // batch // tm
    return pl.pallas_call(
        _mem_attn_kernel,
        out_shape=jax.ShapeDtypeStruct((rows, D_GROUP), BF16),
        grid=(batch, per_batch),
        in_specs=[pl.BlockSpec((tm, D_GROUP), lambda bi, i: (bi * per_batch + i, BLK_D_Q)),
                  pl.BlockSpec((tm, D_GROUP), lambda bi, i: (bi * per_batch + i, BLK_D_Z)),
                  pl.BlockSpec((None,) + k.shape[1:], lambda bi, i: (bi, 0, 0)),
                  pl.BlockSpec((None,) + vt.shape[1:], lambda bi, i: (bi, 0, 0))],
        out_specs=pl.BlockSpec((tm, D_GROUP), lambda bi, i: (bi * per_batch + i, 0)),
        compiler_params=pltpu.CompilerParams(
            dimension_semantics=("parallel", "parallel"), vmem_limit_bytes=VMEM_LIMIT),
        name="mem_attn",
    )(u, u, k, vt)


def kernel(x, mem, positions, norm_g, w_in, hgrn_lb_logits, hgrn_norm_g, attn_sink, mlstm_conv_w,
           mlstm_gate_b, mlstm_norm_g, mem_norm_g, w_mem_kv, w_out, final_norm_g):
    batch, seq, d = x.shape
    depth = w_in.shape[0]
    assert seq % CHUNK == 0 and seq % ROW_TILE == 0 and seq % IN_ROW_TILE == 0
    assert batch % REC_GROUP == 0
    xf = x.reshape(batch * seq, d)
    cos_t, sin_t = _rope_tables(positions)
    w_cat_all, w_gt_all = _prep_w_in(w_in)
    gate_b = mlstm_gate_b.reshape(depth, N_GATE, 1)
    for layer in range(depth):
        u, gt, vt = _in_proj(xf, norm_g, w_cat_all, w_gt_all, layer, batch)
        u3 = u.reshape(batch, seq, u.shape[1])
        ob = _hgrn(u3, hgrn_lb_logits, layer, reverse=True)
        ya = _hgrn(u3, hgrn_lb_logits, layer, reverse=False, ob=ob, norm_g=hgrn_norm_g)
        yb = _swa(u, cos_t, sin_t, attn_sink, layer, batch)
        hb, qk_c = _mlstm(u3, gt, vt, mlstm_conv_w, gate_b, layer, reverse=True)
        yc = _mlstm(u3, gt, vt, None, gate_b, layer, reverse=False,
                    qk=qk_c, hb=hb, norm_g=mlstm_norm_g)
        mem_k, mem_vt = _mem_kv(mem, mem_norm_g, w_mem_kv, layer)
        yd = _mem_attn(u, mem_k, mem_vt, batch)
        xf = _out_proj(ya.reshape(batch * seq, D_GROUP), yb, yc.reshape(batch * seq, D_GROUP), yd,
                       w_out, xf, final_norm_g, layer, final=(layer == depth - 1))
    return xf.reshape(batch, seq, d)
```

```python
import functools

import jax
import jax.numpy as jnp
import numpy as np
from jax import lax
from jax.experimental import pallas as pl
from jax.experimental.pallas import tpu as pltpu

F32 = jnp.float32
BF16 = jnp.bfloat16

D_GROUP = 512
A_HEADS, A_HD = 4, 128
B_HEADS, B_KV_HEADS, B_HD = 8, 2, 64
WINDOW = 128
ROPE_THETA = 500000.0
ROPE_DIM = 16
C_HEADS, C_QK, C_V = 4, 64, 128
CONV_W = 5
D_HEADS, D_HD = 4, 128
EPS = 1e-6
IN_SIZES = (
    D_GROUP, D_GROUP, D_GROUP, D_GROUP, D_GROUP,
    B_HEADS * B_HD, B_KV_HEADS * B_HD, B_KV_HEADS * B_HD, D_GROUP,
    C_HEADS * C_QK, C_HEADS * C_QK, C_HEADS * C_V, D_GROUP, D_GROUP,
    2 * C_HEADS, 2 * C_HEADS,
    D_GROUP, D_GROUP,
)
SPLIT_POINTS = tuple(int(s) for s in np.cumsum(IN_SIZES)[:-1])

LANE = 128
SUBLANE = 8
VMEM_LIMIT = 48 * 1024 * 1024

CHUNK = 128
ROW_TILE = 512
IN_ROW_TILE = 512
IN_PROJ_VMEM_LIMIT = 56 * 1024 * 1024
W_PREP_TILE = 256
OUT_ROW_TILE = 1024
HGRN_SUB = 32
HGRN_SAFE_LOG2 = 100.0
SWA_TILE = 512
REC_GROUP = 4
NEG = -1e30
LOG2_E = 1.4426950408889634

(BLK_A_Q, BLK_A_I, BLK_A_FF, BLK_A_FB, BLK_A_Z, BLK_B_Q, BLK_B_Z, BLK_C_QK, BLK_C_O,
 BLK_C_Z, BLK_D_Q, BLK_D_Z, BLK_MISC) = range(13)
N_BLK = 13
W_BLK_C_V = N_BLK
N_W_BLK = N_BLK + 1
MISC_K, MISC_V, MISC_G = 0, 128, 256
N_GATE = 4 * C_HEADS
C_ONES = 16


def _dot(a, b):
    return jnp.dot(a, b, preferred_element_type=F32)


def _dot_nt(a, b):
    return lax.dot_general(a, b, (((1,), (1,)), ((), ())), preferred_element_type=F32)


def _dot_tn(a, b):
    return lax.dot_general(a, b, (((0,), (0,)), ((), ())), preferred_element_type=F32)


def _sigmoid(x):
    return 1.0 / (1.0 + jnp.exp(-x))


def _silu(x):
    return x * _sigmoid(x)


def _log_sigmoid(x):
    return jnp.minimum(x, 0.0) - jnp.log(1.0 + jnp.exp(-jnp.abs(x)))


def _chunk_cumsum_lanes(x, reverse):
    n = x.shape[1]
    pos = lax.broadcasted_iota(jnp.int32, x.shape, 1) & (CHUNK - 1)
    s = 1
    while s < CHUNK:
        if reverse:
            x = x + jnp.where(pos < CHUNK - s, pltpu.roll(x, n - s, axis=1), 0.0)
        else:
            x = x + jnp.where(pos >= s, pltpu.roll(x, s, axis=1), 0.0)
        s *= 2
    return x


def _in_proj_kernel(x_ref, g_ref, w_ref, wgt_ref, gb_ref, u_ref, gt_ref, vt_ref, *, layer):
    x = x_ref[...]
    ms = jnp.mean(x * x, axis=-1, keepdims=True)
    h = (x * lax.rsqrt(ms + EPS) * g_ref[layer:layer + 1, :]).astype(BF16)
    gates = _dot_nt(wgt_ref[...], h) + gb_ref[layer]
    n_dir = 2 * C_HEADS
    log_f = _log_sigmoid(gates[n_dir:, :])
    is_fwd = lax.broadcasted_iota(jnp.int32, log_f.shape, 0) < C_HEADS
    cum = jnp.where(is_fwd, _chunk_cumsum_lanes(log_f, False), _chunk_cumsum_lanes(log_f, True))
    gt_ref[...] = jnp.concatenate([gates[:n_dir, :], cum], axis=0)
    vt_ref[...] = _dot_nt(w_ref[W_BLK_C_V * D_GROUP:(W_BLK_C_V + 1) * D_GROUP, :], h).astype(BF16)
    for j in range(N_BLK):
        sl = slice(j * D_GROUP, (j + 1) * D_GROUP)
        u_ref[:, sl] = _dot_nt(h, w_ref[sl, :])


def _in_proj(xf, g_all, w_cat_all, w_gt_all, gate_b, layer, batch):
    rows, d = xf.shape
    tm = IN_ROW_TILE
    n_out = N_BLK * D_GROUP
    per_batch = rows // batch // tm
    return pl.pallas_call(
        functools.partial(_in_proj_kernel, layer=layer),
        out_shape=(jax.ShapeDtypeStruct((rows, n_out), F32),
                   jax.ShapeDtypeStruct((batch, N_GATE, rows // batch), F32),
                   jax.ShapeDtypeStruct((batch, D_GROUP, rows // batch), BF16)),
        grid=(rows // tm,),
        in_specs=[
            pl.BlockSpec((tm, d), lambda i: (i, 0)),
            pl.BlockSpec(g_all.shape, lambda i: (0, 0)),
            pl.BlockSpec((None, N_W_BLK * D_GROUP, d), lambda i: (layer, 0, 0),
                         pipeline_mode=pl.Buffered(1)),
            pl.BlockSpec((None, N_GATE, d), lambda i: (layer, 0, 0)),
            pl.BlockSpec(gate_b.shape, lambda i: (0, 0, 0)),
        ],
        out_specs=(
            pl.BlockSpec((tm, n_out), lambda i: (i, 0)),
            pl.BlockSpec((None, N_GATE, tm), lambda i: (i // per_batch, 0, i % per_batch)),
            pl.BlockSpec((None, D_GROUP, tm), lambda i: (i // per_batch, 0, i % per_batch)),
        ),
        compiler_params=pltpu.CompilerParams(
            dimension_semantics=("parallel",), vmem_limit_bytes=IN_PROJ_VMEM_LIMIT),
        name="in_proj",
    )(xf, g_all, w_cat_all, w_gt_all, gate_b)


def _w_in_segments():
    names = ("a_q", "a_i", "a_ff", "a_fb", "a_z", "b_q", "b_k", "b_v", "b_z",
             "c_q", "c_k", "c_v", "c_o", "c_z", "c_ig", "c_fg", "d_q", "d_z")
    src = dict(zip(names, zip((0,) + SPLIT_POINTS, IN_SIZES)))
    order = ("a_q", "a_i", "a_ff", "a_fb", "a_z", "b_q", "b_z", "c_q", "c_k", "c_o", "c_z",
             "d_q", "d_z", "b_k", "b_v", "c_ig", "c_fg")
    segs, dst = [], 0
    for name in order + ("c_v",):
        off, width = src[name]
        if name == "c_v":
            pad = (dst, W_BLK_C_V * D_GROUP)
            dst = pad[1]
        if segs and segs[-1][0] + segs[-1][1] == off and segs[-1][2] + segs[-1][1] == dst:
            segs[-1] = (segs[-1][0], segs[-1][1] + width, segs[-1][2])
        else:
            segs.append((off, width, dst))
        dst += width
    return tuple(segs), pad, src["c_ig"][0]


def _w_prep_kernel(wt_ref, wc_ref, wg_ref):
    segs, pad, gate_off = _w_in_segments()
    for off, width, dst in segs:
        wc_ref[dst:dst + width, :] = wt_ref[off:off + width, :].astype(BF16)
    wc_ref[pad[0]:pad[1], :] = jnp.zeros((pad[1] - pad[0], wc_ref.shape[1]), BF16)
    wg_ref[...] = wt_ref[gate_off:gate_off + N_GATE, :].astype(BF16)


def _prep_w_in(w):
    depth, d, d_in = w.shape
    wt = jnp.swapaxes(w, 1, 2)
    tk = W_PREP_TILE
    n_out = N_W_BLK * D_GROUP
    return pl.pallas_call(
        _w_prep_kernel,
        out_shape=(jax.ShapeDtypeStruct((depth, n_out, d), BF16),
                   jax.ShapeDtypeStruct((depth, N_GATE, d), BF16)),
        grid=(depth, d // tk),
        in_specs=[pl.BlockSpec((None, d_in, tk), lambda l, i: (l, 0, i))],
        out_specs=(pl.BlockSpec((None, n_out, tk), lambda l, i: (l, 0, i)),
                   pl.BlockSpec((None, N_GATE, tk), lambda l, i: (l, 0, i))),
        compiler_params=pltpu.CompilerParams(
            dimension_semantics=("parallel", "parallel"), vmem_limit_bytes=VMEM_LIMIT),
        name="w_prep",
    )(wt)


def _out_proj_kernel(ya_ref, yb_ref, yc_ref, yd_ref, w_ref, x_ref, g_ref, o_ref, *, final):
    acc = x_ref[...]
    for i, y_ref in enumerate((ya_ref, yb_ref, yc_ref, yd_ref)):
        acc = acc + _dot(y_ref[...], w_ref[i * D_GROUP:(i + 1) * D_GROUP, :].astype(BF16))
    if final:
        ms = jnp.mean(acc * acc, axis=-1, keepdims=True)
        acc = acc * lax.rsqrt(ms + EPS) * g_ref[...]
    o_ref[...] = acc


def _out_proj(ya, yb, yc, yd, w_all, xf, g, layer, final):
    rows, d = xf.shape
    tm = OUT_ROW_TILE
    yspec = pl.BlockSpec((tm, D_GROUP), lambda i: (i, 0))
    return pl.pallas_call(
        functools.partial(_out_proj_kernel, final=final),
        out_shape=jax.ShapeDtypeStruct((rows, d), F32),
        grid=(rows // tm,),
        in_specs=[yspec, yspec, yspec, yspec,
                  pl.BlockSpec((None, 4 * D_GROUP, d), lambda i: (layer, 0, 0)),
                  pl.BlockSpec((tm, d), lambda i: (i, 0)),
                  pl.BlockSpec((1, d), lambda i: (0, 0))],
        out_specs=pl.BlockSpec((tm, d), lambda i: (i, 0)),
        compiler_params=pltpu.CompilerParams(
            dimension_semantics=("parallel",), vmem_limit_bytes=VMEM_LIMIT),
        name="out_proj",
    )(ya, yb, yc, yd, w_all, xf, g.reshape(1, d))


def _level_ref(b, h, reverse):
    n_rows, n = b.shape
    blk = 2 * h
    r0 = h if reverse else h - 1
    if blk >= SUBLANE:
        pieces = [jnp.broadcast_to(b[s + r0:s + r0 + 1, :], (blk, n)) for s in range(0, n_rows, blk)]
        return pieces[0] if len(pieces) == 1 else jnp.concatenate(pieces, axis=0)
    res = lax.broadcasted_iota(jnp.int32, b.shape, 0) & (blk - 1)
    out = None
    for m in range(blk):
        shift = (m - r0) % n_rows
        cand = b if shift == 0 else pltpu.roll(b, shift, axis=0)
        out = cand if out is None else jnp.where(res == m, cand, out)
    return out


def _hgrn_lower_bound(lbl_ref, layer, depth):
    rows = [lbl_ref[j:j + 1, :] for j in range(depth)]
    mx = functools.reduce(jnp.maximum, rows)
    es = [jnp.exp(r - mx) for r in rows]
    tot = functools.reduce(lambda a, c: a + c, es)
    lb = jnp.zeros_like(rows[0])
    for j in range(1, layer + 1):
        lb = lb + es[j] / tot
    return lb


def _cumsum_rows_mxu(x, reverse):
    n = x.shape[0]
    r = lax.broadcasted_iota(jnp.int32, (n, n), 0)
    c = lax.broadcasted_iota(jnp.int32, (n, n), 1)
    tri = jnp.where((c >= r) if reverse else (c <= r), 1.0, 0.0).astype(BF16)
    hi = x.astype(BF16)
    r1 = x - hi.astype(F32)
    mid = r1.astype(BF16)
    lo = (r1 - mid.astype(F32)).astype(BF16)
    return _dot(jnp.concatenate([tri, tri, tri], axis=1), jnp.concatenate([hi, mid, lo], axis=0))


def _hgrn_level_operand(q, k, f, b2, h, reverse):
    n_rows = q.shape[0]
    blk = 2 * h
    r0 = h if reverse else h - 1
    if h >= SUBLANE:
        pieces = []
        for s0 in range(0, n_rows, blk):
            ref = b2[s0 + r0:s0 + r0 + 1, :]
            lo, up = slice(s0, s0 + h), slice(s0 + h, s0 + blk)
            if reverse:
                pieces += [q[lo] * jnp.exp2(b2[lo] - ref), k[up] * jnp.exp2(ref - b2[up])]
            else:
                pieces += [k[lo] * jnp.exp2(ref - b2[lo]), q[up] * jnp.exp2(b2[up] - ref)]
        return jnp.concatenate(pieces, axis=0)
    row = lax.broadcasted_iota(jnp.int32, q.shape, 0)
    is_q = ((row & h) == 0) if reverse else ((row & h) != 0)
    qk = jnp.where(is_q, q, k)
    if h == 1:
        return jnp.where(is_q, q * f, k)
    if h == 2:
        res = row & 3
        f_dn = pltpu.roll(f, 1, axis=0)
        f_up = pltpu.roll(f, n_rows - 1, axis=0)
        if reverse:
            e = jnp.where(res == 0, f * f_up, jnp.where(res == 1, f, jnp.where(res == 2, 1.0, f_dn)))
        else:
            e = jnp.where(res == 0, f_up, jnp.where(res == 1, 1.0, jnp.where(res == 2, f, f_dn * f)))
        return qk * e
    return qk * jnp.exp2(-jnp.abs(b2 - _level_ref(b2, h, reverse)))


def _hgrn_kernel(*refs, layer, depth, reverse, final):
    if final:
        q_ref, v_ref, zg_ref, lbl_ref, z_ref, ob_ref, ng_ref, out_ref, st_ref = refs
    else:
        q_ref, v_ref, zg_ref, lbl_ref, out_ref, st_ref = refs

    @pl.when(pl.program_id(1) == 0)
    def _():
        st_ref[...] = jnp.zeros_like(st_ref)

    rows = range(q_ref.shape[0])
    n_rows = q_ref.shape[1]
    ti = lax.broadcasted_iota(jnp.int32, (n_rows, n_rows), 0)
    si = lax.broadcasted_iota(jnp.int32, (n_rows, n_rows), 1)
    ahead = (ti < si) if reverse else (ti > si)
    level = jnp.where(ahead, 31 - lax.clz(ti ^ si), jnp.where(ti == si, -1, -2))
    preps = [_hgrn_prep(q_ref.at[g], zg_ref.at[g], lbl_ref, layer, depth, reverse) for g in rows]

    def run(fast):
        outs = [[] for _ in rows]
        for hd in range(A_HEADS):
            sl = slice(hd * A_HD, (hd + 1) * A_HD)
            heads = [tuple(a[:, sl] for a in preps[g]) for g in rows]
            if fast:
                atts = [_hgrn_scores_fast(heads[g], level, reverse) for g in rows]
            else:
                atts = [_hgrn_scores(heads[g], level, n_rows, None, reverse) for g in rows]
                h = n_rows // 2
                while h >= 1:
                    atts = [_hgrn_scores(heads[g], level, h, atts[g], reverse) for g in rows]
                    h //= 2
            for g in rows:
                outs[g].append(_hgrn_apply(heads[g], atts[g], v_ref.at[g, :, sl],
                                           st_ref.at[g, hd], reverse))
        for g in rows:
            o = jnp.concatenate(outs[g], axis=1)
            if final:
                _hgrn_finish(o, z_ref.at[g], ob_ref.at[g], ng_ref.at[layer:layer + 1],
                             out_ref.at[g])
            else:
                out_ref[g] = o

    worst = None
    for g in rows:
        b2 = preps[g][3]
        for s0 in range(0, n_rows, HGRN_SUB):
            span = jnp.abs(b2[s0 + HGRN_SUB - 1:s0 + HGRN_SUB, :] - b2[s0:s0 + 1, :])
            worst = span if worst is None else jnp.maximum(worst, span)
    safe = jnp.max(worst) <= HGRN_SAFE_LOG2

    @pl.when(safe)
    def _():
        run(True)

    @pl.when(jnp.logical_not(safe))
    def _():
        run(False)


def _hgrn_prep(q_ref, zg_ref, lbl_ref, layer, depth, reverse):
    n_rows = q_ref.shape[0]
    q = q_ref[...]
    zg = zg_ref[...]
    e = jnp.exp(-jnp.abs(zg))
    r = 1.0 / (1.0 + e)
    er = e * r
    pos = zg >= 0.0
    f = jnp.where(pos, r, er)
    k = jnp.where(pos, er, r)
    if layer > 0:
        lb = _hgrn_lower_bound(lbl_ref, layer, depth)
        f = lb + (1.0 - lb) * f
        k = (1.0 - lb) * k
    b2 = _cumsum_rows_mxu(jnp.log2(f), reverse)
    return q, k, f, b2


def _hgrn_scores(head, level, h, att, reverse):
    q, k, f, b2 = head
    if att is None:
        return jnp.where(level == -1, _dot_nt(q.astype(BF16), k.astype(BF16)), 0.0)
    c = _hgrn_level_operand(q, k, f, b2, h, reverse).astype(BF16)
    return jnp.where(level == (h.bit_length() - 1), _dot_nt(c, c), att)


def _hgrn_scores_fast(head, level, reverse):
    q, k, _, b2 = head
    n_rows = q.shape[0]
    blocks = []
    for s0 in range(0, n_rows, HGRN_SUB):
        rows = slice(s0, s0 + HGRN_SUB)
        ref = b2[s0 + HGRN_SUB - 1:s0 + HGRN_SUB, :] if reverse else b2[s0:s0 + 1, :]
        qs = (q[rows] * jnp.exp2(b2[rows] - ref)).astype(BF16)
        ks = (k * jnp.exp2(jnp.minimum(ref - b2, HGRN_SAFE_LOG2))).astype(BF16)
        blocks.append(_dot_nt(qs, ks))
    return jnp.where(level >= -1, jnp.concatenate(blocks, axis=0), 0.0)


def _hgrn_apply(head, att, v_ref, st_ref, reverse):
    q, k, _, b2 = head
    n_rows = q.shape[0]
    tot = b2[0:1, :] if reverse else b2[n_rows - 1:n_rows, :]
    v_bf = v_ref[...].astype(BF16)
    q_in = (q * jnp.exp2(b2)).astype(BF16)
    k_out = (k * jnp.exp2(tot - b2)).astype(BF16)
    st = st_ref[...]
    o = _dot(att.astype(BF16), v_bf) + _dot_nt(q_in, st.astype(BF16))
    st_ref[...] = jnp.exp2(tot) * st + _dot_tn(v_bf, k_out)
    return o


def _hgrn_finish(o, z_ref, ob_ref, ng_ref, out_ref):
    o = o + ob_ref[...]
    ys = []
    for hd in range(A_HEADS):
        sl = slice(hd * A_HD, (hd + 1) * A_HD)
        oh = o[:, sl]
        ms = jnp.mean(oh * oh, axis=-1, keepdims=True)
        ys.append(oh * lax.rsqrt(ms + EPS) * ng_ref[:, sl])
    y = jnp.concatenate(ys, axis=1) * _silu(z_ref[...])
    out_ref[...] = y.astype(out_ref.dtype)


def _chunk_index(c, n_chunks, reverse):
    return n_chunks - 1 - c if reverse else c


def _hgrn(u3, lb_logits, layer, *, reverse, ob=None, norm_g=None):
    batch, seq, _ = u3.shape
    n_chunks = seq // CHUNK
    depth = lb_logits.shape[0]
    final = ob is not None
    grp = REC_GROUP

    def ublk(blk):
        return pl.BlockSpec((grp, CHUNK, D_GROUP),
                            lambda bg, c: (bg, _chunk_index(c, n_chunks, reverse), blk))

    row_spec = ublk(0)
    in_specs = [ublk(BLK_A_Q), ublk(BLK_A_I), ublk(BLK_A_FB if reverse else BLK_A_FF),
                pl.BlockSpec((depth, D_GROUP), lambda bg, c: (0, 0))]
    args = [u3, u3, u3, lb_logits]
    if final:
        in_specs += [ublk(BLK_A_Z), row_spec, pl.BlockSpec(norm_g.shape, lambda bg, c: (0, 0))]
        args += [u3, ob, norm_g]
    return pl.pallas_call(
        functools.partial(_hgrn_kernel, layer=layer, depth=depth, reverse=reverse, final=final),
        out_shape=jax.ShapeDtypeStruct((batch, seq, D_GROUP), BF16 if final else F32),
        grid=(batch // grp, n_chunks),
        in_specs=in_specs,
        out_specs=row_spec,
        scratch_shapes=[pltpu.VMEM((grp, A_HEADS, A_HD, A_HD), F32)],
        compiler_params=pltpu.CompilerParams(
            dimension_semantics=("parallel", "arbitrary"), vmem_limit_bytes=VMEM_LIMIT),
        name="hgrn_fwd" if final else "hgrn_bwd",
    )(*args)


def _split3(x):
    hi = x.astype(BF16).astype(F32)
    r1 = x - hi
    mid = r1.astype(BF16).astype(F32)
    lo = (r1 - mid).astype(BF16).astype(F32)
    return hi, mid, lo


def _conv_silu_qk(qkp_ref, qkc_ref, qkn_ref, cw_ref, reverse):
    n_rows = qkc_ref.shape[0]
    c = pl.program_id(1)
    n_chunks = pl.num_programs(1)
    cc = _chunk_index(c, n_chunks, reverse)
    halo = SUBLANE
    prev = qkp_ref[n_rows - halo:n_rows, :] * (cc > 0).astype(F32)
    nxt = qkn_ref[0:halo, :] * (cc < n_chunks - 1).astype(F32)
    xcat = jnp.concatenate([prev, qkc_ref[...], nxt], axis=0)
    acc = None
    for j in range(CONV_W):
        off = halo + j - CONV_W // 2
        term = cw_ref[j:j + 1, :] * xcat[off:off + n_rows, :]
        acc = term if acc is None else acc + term
    qk = _silu(acc)
    n_qk = C_HEADS * C_QK
    return jnp.concatenate([qk[:, :n_qk] * (C_QK ** -0.5), qk[:, n_qk:]], axis=1).astype(BF16)


def _mlstm_kernel(*refs, layer, reverse, final):
    if final:
        (qk_ref, v_ref, gt_ref, og_ref, z_ref, hb_ref, ng_ref,
         out_ref, cst_ref, m_ref) = refs
    else:
        (qkp_ref, qkc_ref, qkn_ref, cw_ref, v_ref, gt_ref,
         out_ref, qko_ref, cst_ref, m_ref) = refs

    @pl.when(pl.program_id(1) == 0)
    def _():
        cst_ref[...] = jnp.zeros_like(cst_ref)
        m_ref[...] = jnp.zeros_like(m_ref)

    rows = range(v_ref.shape[0])
    if final:
        qks = [qk_ref[g] for g in rows]
    else:
        qks = [_conv_silu_qk(qkp_ref.at[g], qkc_ref.at[g], qkn_ref.at[g], cw_ref.at[layer], reverse)
               for g in rows]
        for g in rows:
            qko_ref[g] = qks[g]
    gates = [_mlstm_gates(gt_ref.at[g], m_ref.at[g], reverse) for g in rows]
    outs = [[] for _ in rows]
    for hd in range(C_HEADS):
        weights = [_mlstm_weights(hd, qks[g], gates[g], reverse) for g in rows]
        for g in rows:
            h_t = _mlstm_apply(hd, qks[g], v_ref.at[g], gates[g], weights[g], cst_ref.at[g], reverse)
            if final:
                outs[g].append((h_t + hb_ref[g, hd * C_V:(hd + 1) * C_V, :]).T)
            else:
                out_ref[g, hd * C_V:(hd + 1) * C_V, :] = h_t
    if final:
        for g in rows:
            _mlstm_finish(jnp.concatenate(outs[g], axis=1), og_ref.at[g], z_ref.at[g],
                          ng_ref.at[layer:layer + 1], out_ref.at[g])


def _mlstm_gates(gt_ref, m_ref, reverse):
    n_rows = gt_ref.shape[1]
    n_dir = 2 * C_HEADS

    li = gt_ref[0:n_dir, :]
    b = gt_ref[n_dir:, :]
    b_last = b[:, 0:1] if reverse else b[:, n_rows - 1:n_rows]
    m_prev = m_ref[...]
    d_inter = b + m_prev
    a = (b_last - b) + li
    m_new = jnp.maximum(b_last + m_prev, jnp.max(a, axis=1, keepdims=True))
    ws = jnp.exp(a - m_new)
    decay = jnp.exp((b_last + m_prev) - m_new)
    m_ref[...] = m_new

    one8 = jnp.ones((n_dir, n_rows), F32)
    lhs = jnp.concatenate(list(_split3(li - b)) + [one8, one8, one8], axis=0).T.astype(BF16)
    rhs_all = jnp.concatenate([one8, one8, one8] + list(_split3(b)), axis=0)
    return lhs, rhs_all, d_inter, ws, decay


def _mlstm_weights(hd, qk, gates, reverse):
    lhs, rhs_all, d_inter, _, _ = gates
    n_rows = qk.shape[0]
    n_qk = C_HEADS * C_QK
    n_dir = 2 * C_HEADS
    gi = (C_HEADS if reverse else 0) + hd
    gate_row = lax.broadcasted_iota(jnp.int32, rhs_all.shape, 0) & (n_dir - 1)
    si = lax.broadcasted_iota(jnp.int32, (n_rows, n_rows), 0)
    ti = lax.broadcasted_iota(jnp.int32, (n_rows, n_rows), 1)
    causal = (si >= ti) if reverse else (si <= ti)

    q_h = qk[:, hd * C_QK:(hd + 1) * C_QK]
    k_h = qk[:, n_qk + hd * C_QK:n_qk + (hd + 1) * C_QK]
    rhs = jnp.where(gate_row == gi, rhs_all, 0.0).astype(BF16)
    d = jnp.where(causal, _dot(lhs, rhs), NEG)
    d_int = d_inter[gi:gi + 1, :]
    m_t = jnp.maximum(jnp.max(d, axis=0, keepdims=True), d_int)
    w = (_dot_nt(k_h, q_h) * jnp.exp(d - m_t)).astype(BF16)
    return w, m_t, jnp.exp(d_int - m_t)


def _mlstm_apply(hd, qk, vt_ref, gates, weights, cst_ref, reverse):
    _, _, _, ws, decay = gates
    w, m_t, g = weights
    n_rows = qk.shape[0]
    n_qk = C_HEADS * C_QK
    gi = (C_HEADS if reverse else 0) + hd
    q_h = qk[:, hd * C_QK:(hd + 1) * C_QK]
    k_h = qk[:, n_qk + hd * C_QK:n_qk + (hd + 1) * C_QK]
    v_t = vt_ref[hd * C_V:(hd + 1) * C_V, :]
    v_ext = jnp.concatenate([v_t, jnp.ones((C_ONES, n_rows), BF16)], axis=0)
    ws_row = ws[gi:gi + 1, :]
    v_scaled = jnp.concatenate([(v_t.astype(F32) * ws_row).astype(BF16),
                                jnp.broadcast_to(ws_row, (C_ONES, n_rows)).astype(BF16)], axis=0)
    cst = cst_ref[hd]
    num = _dot(v_ext, w) + g * _dot_nt(cst.astype(BF16), q_h)
    den = jnp.maximum(jnp.abs(num[C_V:C_V + 1, :]), jnp.exp(-m_t))
    cst_ref[hd] = decay[gi:gi + 1, :C_QK] * cst + _dot(v_scaled, k_h)
    return num[:C_V, :] / den


def _mlstm_finish(hsum, og_ref, z_ref, ng_ref, out_ref):
    hsum = hsum * _sigmoid(og_ref[...])
    ys = []
    for hd in range(C_HEADS):
        sl = slice(hd * C_V, (hd + 1) * C_V)
        hh = hsum[:, sl]
        mu = jnp.mean(hh, axis=-1, keepdims=True)
        var = jnp.mean(jnp.square(hh - mu), axis=-1, keepdims=True)
        ys.append((hh - mu) * lax.rsqrt(var + EPS) * ng_ref[:, sl])
    y = jnp.concatenate(ys, axis=1) * _silu(z_ref[...])
    out_ref[...] = y.astype(out_ref.dtype)


def _mlstm(u3, gt, vt, conv_w, layer, *, reverse, qk=None, hb=None, norm_g=None):
    batch, seq, _ = u3.shape
    n_chunks = seq // CHUNK
    final = hb is not None
    grp = REC_GROUP
    assert CHUNK == LANE

    def chunk(c, delta=0):
        return jnp.clip(_chunk_index(c, n_chunks, reverse) + delta, 0, n_chunks - 1)

    def ublk(blk, delta=0):
        return pl.BlockSpec((grp, CHUNK, D_GROUP), lambda bg, c: (bg, chunk(c, delta), blk))

    row_spec = ublk(0)
    t_spec = lambda ch: pl.BlockSpec((grp, ch, CHUNK), lambda bg, c: (bg, 0, chunk(c)))
    gate_specs = [t_spec(N_GATE)]
    gate_args = [gt]
    if final:
        in_specs = [row_spec, t_spec(D_GROUP)] + gate_specs + [
            ublk(BLK_C_O), ublk(BLK_C_Z), t_spec(D_GROUP),
            pl.BlockSpec(norm_g.shape, lambda bg, c: (0, 0))]
        args = [qk, vt] + gate_args + [u3, u3, hb, norm_g]
        out_shape = jax.ShapeDtypeStruct((batch, seq, D_GROUP), BF16)
        out_specs = row_spec
    else:
        in_specs = [ublk(BLK_C_QK, -1), ublk(BLK_C_QK), ublk(BLK_C_QK, 1),
                    pl.BlockSpec(conv_w.shape, lambda bg, c: (0, 0, 0)),
                    t_spec(D_GROUP)] + gate_specs
        args = [u3, u3, u3, conv_w, vt] + gate_args
        out_shape = (jax.ShapeDtypeStruct((batch, D_GROUP, seq), F32),
                     jax.ShapeDtypeStruct((batch, seq, D_GROUP), BF16))
        out_specs = (t_spec(D_GROUP), row_spec)
    return pl.pallas_call(
        functools.partial(_mlstm_kernel, layer=layer, reverse=reverse, final=final),
        out_shape=out_shape,
        grid=(batch // grp, n_chunks),
        in_specs=in_specs,
        out_specs=out_specs,
        scratch_shapes=[pltpu.VMEM((grp, C_HEADS, C_V + C_ONES, C_QK), F32),
                        pltpu.VMEM((grp, 2 * C_HEADS, CHUNK), F32)],
        compiler_params=pltpu.CompilerParams(
            dimension_semantics=("parallel", "arbitrary"), vmem_limit_bytes=VMEM_LIMIT),
        name="mlstm_fwd" if final else "mlstm_bwd",
    )(*args)


def _rope_table_kernel(pos_ref, invf_ref, cos_ref, sin_ref):
    ang = invf_ref[...] * pos_ref[...].astype(F32)
    n_rows = ang.shape[1]
    half = ROPE_DIM // 2
    r = lax.broadcasted_iota(jnp.int32, ang.shape, 0)
    sn = jnp.sin(ang)
    sn = jnp.where(r < half, -sn, sn)
    pad = jnp.zeros((LANE - ROPE_DIM, n_rows), F32)

    def expand(x):
        t = jnp.concatenate([x, pad], axis=0).T
        return t + pltpu.roll(t, B_HD, axis=1)

    lane = lax.broadcasted_iota(jnp.int32, (n_rows, LANE), 1) & (B_HD - 1)
    cos_ref[...] = jnp.where(lane < ROPE_DIM, expand(jnp.cos(ang)), 1.0)
    sin_ref[...] = expand(sn)


def _rope_tables(positions):
    rows = positions.size
    half = ROPE_DIM // 2
    inv_freq = ROPE_THETA ** (-jnp.arange(0, ROPE_DIM, 2, dtype=F32) / ROPE_DIM)
    invf = jnp.concatenate([inv_freq, inv_freq]).reshape(ROPE_DIM, 1)
    tm = ROW_TILE
    return pl.pallas_call(
        _rope_table_kernel,
        out_shape=(jax.ShapeDtypeStruct((rows, LANE), F32),) * 2,
        grid=(rows // tm,),
        in_specs=[pl.BlockSpec((1, tm), lambda i: (0, i)),
                  pl.BlockSpec((ROPE_DIM, 1), lambda i: (0, 0))],
        out_specs=(pl.BlockSpec((tm, LANE), lambda i: (i, 0)),) * 2,
        compiler_params=pltpu.CompilerParams(dimension_semantics=("parallel",)),
        name="rope_tables",
    )(positions.reshape(1, rows), invf)


def _rope(x, cos, sin):
    width = x.shape[1]
    if width > LANE:
        cos = jnp.concatenate([cos] * (width // LANE), axis=1)
        sin = jnp.concatenate([sin] * (width // LANE), axis=1)
    half = ROPE_DIM // 2
    r = lax.broadcasted_iota(jnp.int32, x.shape, 1) & (B_HD - 1)
    partner = jnp.where(r < half, pltpu.roll(x, width - half, axis=1), pltpu.roll(x, half, axis=1))
    return x * cos + partner * sin


def _swa_kernel(q_ref, z_ref, mp_ref, mc_ref, mn_ref, cp_ref, cc_ref, cn_ref,
                sp_ref, sc_ref, sn_ref, sink_ref, out_ref, *, layer):
    tile = q_ref.shape[0]
    blk = mp_ref.shape[0]
    n_sub = tile // blk
    i = pl.program_id(1)
    n_tiles = pl.num_programs(1)
    kv_w = B_KV_HEADS * B_HD
    grp = B_HEADS // B_KV_HEADS

    q = (_rope(q_ref[...], cc_ref[...], sc_ref[...]) * (B_HD ** -0.5)).astype(BF16)
    k = jnp.concatenate([
        _rope(mp_ref[:, MISC_K:MISC_K + kv_w], cp_ref[...], sp_ref[...]),
        _rope(mc_ref[:, MISC_K:MISC_K + kv_w], cc_ref[...], sc_ref[...]),
        _rope(mn_ref[:, MISC_K:MISC_K + kv_w], cn_ref[...], sn_ref[...])], axis=0).astype(BF16)
    v = jnp.concatenate([mp_ref[:, MISC_V:MISC_V + kv_w], mc_ref[:, MISC_V:MISC_V + kv_w],
                         mn_ref[:, MISC_V:MISC_V + kv_w]], axis=0)
    n_keys = v.shape[0]

    kr = lax.broadcasted_iota(jnp.int32, (3 * blk, blk), 0)
    qc = lax.broadcasted_iota(jnp.int32, (3 * blk, blk), 1)
    in_window = jnp.abs(qc - (kr - blk)) <= WINDOW
    biases = []
    for j in range(n_sub):
        valid = in_window
        if j == 0:
            valid = valid & (kr >= jnp.where(i > 0, 0, blk))
        if j == n_sub - 1:
            valid = valid & (kr < jnp.where(i < n_tiles - 1, 3 * blk, 2 * blk))
        biases.append(jnp.where(valid, 0.0, NEG))

    ones = jnp.ones((n_keys, B_HD), F32)
    vts = [jnp.concatenate([v[:, kvh * B_HD:(kvh + 1) * B_HD], ones], axis=1).T.astype(BF16)
           for kvh in range(B_KV_HEADS)]

    tiles = [[] for _ in range(n_sub)]
    pairs = [[] for _ in range(n_sub)]
    for hq in range(B_HEADS):
        kvh = hq // grp
        ksl = slice(kvh * B_HD, (kvh + 1) * B_HD)
        sk = sink_ref[layer:layer + 1, hq:hq + 1]
        for j in range(n_sub):
            keys = slice(j * blk, (j + 3) * blk)
            qh = q[j * blk:(j + 1) * blk, hq * B_HD:(hq + 1) * B_HD]
            s = _dot_nt(k[keys, ksl], qh)
            s = jnp.concatenate([s[:blk] + biases[j][:blk], s[blk:2 * blk],
                                 s[2 * blk:] + biases[j][2 * blk:]], axis=0)
            m = jnp.maximum(jnp.max(s, axis=0, keepdims=True), sk)
            p = jnp.exp(s - m).astype(BF16)
            o = _dot(vts[kvh][:, keys], p)
            denom = o[B_HD:B_HD + 1, :] + jnp.exp(sk - m)
            pairs[j].append(o[:B_HD, :] / denom)
            if len(pairs[j]) == LANE // B_HD:
                tiles[j].append(jnp.concatenate(pairs[j], axis=0).T)
                pairs[j] = []
    for j in range(n_sub):
        rows = slice(j * blk, (j + 1) * blk)
        y = jnp.concatenate(tiles[j], axis=1) * _silu(z_ref[rows, :])
        out_ref[rows, :] = y.astype(out_ref.dtype)


def _swa(u, cos_t, sin_t, sink, layer, batch):
    rows = u.shape[0]
    n_blocks = rows // batch // CHUNK
    n_sub = SWA_TILE // CHUNK
    n_tiles = n_blocks // n_sub

    def tile_spec(width, blk):
        return pl.BlockSpec((SWA_TILE, width), lambda bi, i: (bi * n_tiles + i, blk))

    def halo_spec(width, blk, after):
        def index(bi, i):
            nb = (i + 1) * n_sub if after else i * n_sub - 1
            return (bi * n_blocks + jnp.clip(nb, 0, n_blocks - 1), blk)
        return pl.BlockSpec((CHUNK, width), index)

    def table_specs():
        return [halo_spec(LANE, 0, False), tile_spec(LANE, 0), halo_spec(LANE, 0, True)]

    return pl.pallas_call(
        functools.partial(_swa_kernel, layer=layer),
        out_shape=jax.ShapeDtypeStruct((rows, D_GROUP), BF16),
        grid=(batch, n_tiles),
        in_specs=[tile_spec(D_GROUP, BLK_B_Q), tile_spec(D_GROUP, BLK_B_Z),
                  halo_spec(D_GROUP, BLK_MISC, False), tile_spec(D_GROUP, BLK_MISC),
                  halo_spec(D_GROUP, BLK_MISC, True)] + table_specs() + table_specs() + [
                  pl.BlockSpec(sink.shape, lambda bi, i: (0, 0))],
        out_specs=tile_spec(D_GROUP, 0),
        compiler_params=pltpu.CompilerParams(
            dimension_semantics=("parallel", "parallel"), vmem_limit_bytes=VMEM_LIMIT),
        name="window_attn",
    )(u, u, u, u, u, cos_t, cos_t, cos_t, sin_t, sin_t, sin_t, sink)


def _mem_kv_kernel(mem_ref, g_ref, w_ref, k_ref, vt_ref, *, layer):
    x = mem_ref[...]
    ms = jnp.mean(x * x, axis=-1, keepdims=True)
    h = (x * lax.rsqrt(ms + EPS) * g_ref[layer:layer + 1, :]).astype(BF16)
    kv = _dot(h, w_ref[...].astype(BF16))
    k_ref[...] = kv[:, :D_GROUP].astype(k_ref.dtype)
    vt_ref[...] = kv[:, D_GROUP:].T.astype(vt_ref.dtype)


def _mem_kv(mem, g_all, w_all, layer):
    batch, m_len, d = mem.shape
    n_out = w_all.shape[2]
    return pl.pallas_call(
        functools.partial(_mem_kv_kernel, layer=layer),
        out_shape=(jax.ShapeDtypeStruct((batch, m_len, D_GROUP), BF16),
                   jax.ShapeDtypeStruct((batch, D_GROUP, m_len), BF16)),
        grid=(batch,),
        in_specs=[pl.BlockSpec((None, m_len, d), lambda bi: (bi, 0, 0)),
                  pl.BlockSpec(g_all.shape, lambda bi: (0, 0)),
                  pl.BlockSpec((None, d, n_out), lambda bi: (layer, 0, 0))],
        out_specs=(pl.BlockSpec((None, m_len, D_GROUP), lambda bi: (bi, 0, 0)),
                   pl.BlockSpec((None, D_GROUP, m_len), lambda bi: (bi, 0, 0))),
        compiler_params=pltpu.CompilerParams(
            dimension_semantics=("parallel",), vmem_limit_bytes=VMEM_LIMIT),
        name="mem_kv",
    )(mem, g_all, w_all)


def _mem_attn_kernel(q_ref, z_ref, k_ref, vt_ref, out_ref):
    q = q_ref[...].astype(BF16)
    outs = []
    for hd in range(D_HEADS):
        sl = slice(hd * D_HD, (hd + 1) * D_HD)
        s = _dot_nt(q[:, sl], k_ref[:, sl])
        m = jnp.max(s, axis=-1, keepdims=True)
        p = jnp.exp2((s - m) * (D_HD ** -0.5 * LOG2_E))
        den = jnp.sum(p, axis=-1, keepdims=True)
        outs.append(_dot_nt(p.astype(BF16), vt_ref[sl, :]) / den)
    y = jnp.concatenate(outs, axis=1) * _silu(z_ref[...])
    out_ref[...] = y.astype(out_ref.dtype)


def _mem_attn(u, k, vt, batch):
    rows = u.shape[0]
    tm = ROW_TILE
    per_batch = rows // batch // tm
    return pl.pallas_call(
        _mem_attn_kernel,
        out_shape=jax.ShapeDtypeStruct((rows, D_GROUP), BF16),
        grid=(batch, per_batch),
        in_specs=[pl.BlockSpec((tm, D_GROUP), lambda bi, i: (bi * per_batch + i, BLK_D_Q)),
                  pl.BlockSpec((tm, D_GROUP), lambda bi, i: (bi * per_batch + i, BLK_D_Z)),
                  pl.BlockSpec((None,) + k.shape[1:], lambda bi, i: (bi, 0, 0)),
                  pl.BlockSpec((None,) + vt.shape[1:], lambda bi, i: (bi, 0, 0))],
        out_specs=pl.BlockSpec((tm, D_GROUP), lambda bi, i: (bi * per_batch + i, 0)),
        compiler_params=pltpu.CompilerParams(
            dimension_semantics=("parallel", "parallel"), vmem_limit_bytes=VMEM_LIMIT),
        name="mem_attn",
    )(u, u, k, vt)


def kernel(x, mem, positions, norm_g, w_in, hgrn_lb_logits, hgrn_norm_g, attn_sink, mlstm_conv_w,
           mlstm_gate_b, mlstm_norm_g, mem_norm_g, w_mem_kv, w_out, final_norm_g):
    batch, seq, d = x.shape
    depth = w_in.shape[0]
    assert all(seq % t == 0 for t in (CHUNK, ROW_TILE, IN_ROW_TILE, SWA_TILE))
    assert (batch * seq) % OUT_ROW_TILE == 0
    assert batch % REC_GROUP == 0
    xf = x.reshape(batch * seq, d)
    cos_t, sin_t = _rope_tables(positions)
    w_cat_all, w_gt_all = _prep_w_in(w_in)
    gate_b = mlstm_gate_b.reshape(depth, N_GATE, 1)
    for layer in range(depth):
        u, gt, vt = _in_proj(xf, norm_g, w_cat_all, w_gt_all, gate_b, layer, batch)
        u3 = u.reshape(batch, seq, u.shape[1])
        ob = _hgrn(u3, hgrn_lb_logits, layer, reverse=True)
        ya = _hgrn(u3, hgrn_lb_logits, layer, reverse=False, ob=ob, norm_g=hgrn_norm_g)
        yb = _swa(u, cos_t, sin_t, attn_sink, layer, batch)
        hb, qk_c = _mlstm(u3, gt, vt, mlstm_conv_w, layer, reverse=True)
        yc = _mlstm(u3, gt, vt, None, layer, reverse=False, qk=qk_c, hb=hb, norm_g=mlstm_norm_g)
        mem_k, mem_vt = _mem_kv(mem, mem_norm_g, w_mem_kv, layer)
        yd = _mem_attn(u, mem_k, mem_vt, batch)
        xf = _out_proj(ya.reshape(batch * seq, D_GROUP), yb, yc.reshape(batch * seq, D_GROUP), yd,
                       w_out, xf, final_norm_g, layer, final=(layer == depth - 1))
    return xf.reshape(batch, seq, d)
```

```python
import functools

import jax
import jax.numpy as jnp
import numpy as np
from jax import lax
from jax.experimental import pallas as pl
from jax.experimental.pallas import tpu as pltpu

F32 = jnp.float32
BF16 = jnp.bfloat16

D_GROUP = 512
A_HEADS, A_HD = 4, 128
B_HEADS, B_KV_HEADS, B_HD = 8, 2, 64
WINDOW = 128
ROPE_THETA = 500000.0
ROPE_DIM = 16
C_HEADS, C_QK, C_V = 4, 64, 128
CONV_W = 5
D_HEADS, D_HD = 4, 128
EPS = 1e-6
IN_SIZES = (
    D_GROUP, D_GROUP, D_GROUP, D_GROUP, D_GROUP,
    B_HEADS * B_HD, B_KV_HEADS * B_HD, B_KV_HEADS * B_HD, D_GROUP,
    C_HEADS * C_QK, C_HEADS * C_QK, C_HEADS * C_V, D_GROUP, D_GROUP,
    2 * C_HEADS, 2 * C_HEADS,
    D_GROUP, D_GROUP,
)
SPLIT_POINTS = tuple(int(s) for s in np.cumsum(IN_SIZES)[:-1])

LANE = 128
SUBLANE = 8
VMEM_LIMIT = 48 * 1024 * 1024

CHUNK = 128
ROW_TILE = 512
IN_ROW_TILE = 512
IN_PROJ_VMEM_LIMIT = 56 * 1024 * 1024
W_PREP_TILE = 256
OUT_ROW_TILE = 1024
HGRN_SUB = 64
HGRN_SAFE_LOG2 = 100.0
SWA_TILE = 512
REC_GROUP = 4
NEG = -1e30
LOG2_E = 1.4426950408889634

(BLK_A_Q, BLK_A_I, BLK_A_FF, BLK_A_FB, BLK_A_Z, BLK_B_Q, BLK_B_Z, BLK_C_QK, BLK_C_O,
 BLK_C_Z, BLK_D_Q, BLK_D_Z, BLK_MISC) = range(13)
N_BLK = 13
W_BLK_C_V = N_BLK
N_W_BLK = N_BLK + 1
MISC_K, MISC_V, MISC_G = 0, 128, 256
N_GATE = 4 * C_HEADS
C_ONES = 16


def _dot(a, b):
    return jnp.dot(a, b, preferred_element_type=F32)


def _dot_nt(a, b):
    return lax.dot_general(a, b, (((1,), (1,)), ((), ())), preferred_element_type=F32)


def _dot_tn(a, b):
    return lax.dot_general(a, b, (((0,), (0,)), ((), ())), preferred_element_type=F32)


def _sigmoid(x):
    return 1.0 / (1.0 + jnp.exp(-x))


def _silu(x):
    return x * _sigmoid(x)


def _log_sigmoid(x):
    return jnp.minimum(x, 0.0) - jnp.log(1.0 + jnp.exp(-jnp.abs(x)))


def _chunk_cumsum_lanes(x, reverse):
    n = x.shape[1]
    pos = lax.broadcasted_iota(jnp.int32, x.shape, 1) & (CHUNK - 1)
    s = 1
    while s < CHUNK:
        if reverse:
            x = x + jnp.where(pos < CHUNK - s, pltpu.roll(x, n - s, axis=1), 0.0)
        else:
            x = x + jnp.where(pos >= s, pltpu.roll(x, s, axis=1), 0.0)
        s *= 2
    return x


def _in_proj_kernel(x_ref, g_ref, w_ref, wgt_ref, gb_ref, u_ref, gt_ref, vt_ref, *, layer):
    x = x_ref[...]
    ms = jnp.mean(x * x, axis=-1, keepdims=True)
    h = (x * lax.rsqrt(ms + EPS) * g_ref[layer:layer + 1, :]).astype(BF16)
    gates = _dot_nt(wgt_ref[...], h) + gb_ref[layer]
    n_dir = 2 * C_HEADS
    log_f = _log_sigmoid(gates[n_dir:, :])
    is_fwd = lax.broadcasted_iota(jnp.int32, log_f.shape, 0) < C_HEADS
    cum = jnp.where(is_fwd, _chunk_cumsum_lanes(log_f, False), _chunk_cumsum_lanes(log_f, True))
    gt_ref[...] = jnp.concatenate([gates[:n_dir, :], cum], axis=0)
    vt_ref[...] = _dot_nt(w_ref[W_BLK_C_V * D_GROUP:(W_BLK_C_V + 1) * D_GROUP, :], h).astype(BF16)
    for j in range(N_BLK):
        sl = slice(j * D_GROUP, (j + 1) * D_GROUP)
        u_ref[:, sl] = _dot_nt(h, w_ref[sl, :])


def _in_proj(xf, g_all, w_cat_all, w_gt_all, gate_b, layer, batch):
    rows, d = xf.shape
    tm = IN_ROW_TILE
    n_out = N_BLK * D_GROUP
    per_batch = rows // batch // tm
    return pl.pallas_call(
        functools.partial(_in_proj_kernel, layer=layer),
        out_shape=(jax.ShapeDtypeStruct((rows, n_out), F32),
                   jax.ShapeDtypeStruct((batch, N_GATE, rows // batch), F32),
                   jax.ShapeDtypeStruct((batch, D_GROUP, rows // batch), BF16)),
        grid=(rows // tm,),
        in_specs=[
            pl.BlockSpec((tm, d), lambda i: (i, 0)),
            pl.BlockSpec(g_all.shape, lambda i: (0, 0)),
            pl.BlockSpec((None, N_W_BLK * D_GROUP, d), lambda i: (layer, 0, 0),
                         pipeline_mode=pl.Buffered(1)),
            pl.BlockSpec((None, N_GATE, d), lambda i: (layer, 0, 0)),
            pl.BlockSpec(gate_b.shape, lambda i: (0, 0, 0)),
        ],
        out_specs=(
            pl.BlockSpec((tm, n_out), lambda i: (i, 0)),
            pl.BlockSpec((None, N_GATE, tm), lambda i: (i // per_batch, 0, i % per_batch)),
            pl.BlockSpec((None, D_GROUP, tm), lambda i: (i // per_batch, 0, i % per_batch)),
        ),
        compiler_params=pltpu.CompilerParams(
            dimension_semantics=("parallel",), vmem_limit_bytes=IN_PROJ_VMEM_LIMIT),
        name="in_proj",
    )(xf, g_all, w_cat_all, w_gt_all, gate_b)


def _w_in_segments():
    names = ("a_q", "a_i", "a_ff", "a_fb", "a_z", "b_q", "b_k", "b_v", "b_z",
             "c_q", "c_k", "c_v", "c_o", "c_z", "c_ig", "c_fg", "d_q", "d_z")
    src = dict(zip(names, zip((0,) + SPLIT_POINTS, IN_SIZES)))
    order = ("a_q", "a_i", "a_ff", "a_fb", "a_z", "b_q", "b_z", "c_q", "c_k", "c_o", "c_z",
             "d_q", "d_z", "b_k", "b_v", "c_ig", "c_fg")
    segs, dst = [], 0
    for name in order + ("c_v",):
        off, width = src[name]
        if name == "c_v":
            pad = (dst, W_BLK_C_V * D_GROUP)
            dst = pad[1]
        if segs and segs[-1][0] + segs[-1][1] == off and segs[-1][2] + segs[-1][1] == dst:
            segs[-1] = (segs[-1][0], segs[-1][1] + width, segs[-1][2])
        else:
            segs.append((off, width, dst))
        dst += width
    return tuple(segs), pad, src["c_ig"][0]


def _w_prep_kernel(wt_ref, wc_ref, wg_ref):
    segs, pad, gate_off = _w_in_segments()
    for off, width, dst in segs:
        wc_ref[dst:dst + width, :] = wt_ref[off:off + width, :].astype(BF16)
    wc_ref[pad[0]:pad[1], :] = jnp.zeros((pad[1] - pad[0], wc_ref.shape[1]), BF16)
    wg_ref[...] = wt_ref[gate_off:gate_off + N_GATE, :].astype(BF16)


def _prep_w_in(w):
    depth, d, d_in = w.shape
    wt = jnp.swapaxes(w, 1, 2)
    tk = W_PREP_TILE
    n_out = N_W_BLK * D_GROUP
    return pl.pallas_call(
        _w_prep_kernel,
        out_shape=(jax.ShapeDtypeStruct((depth, n_out, d), BF16),
                   jax.ShapeDtypeStruct((depth, N_GATE, d), BF16)),
        grid=(depth, d // tk),
        in_specs=[pl.BlockSpec((None, d_in, tk), lambda l, i: (l, 0, i))],
        out_specs=(pl.BlockSpec((None, n_out, tk), lambda l, i: (l, 0, i)),
                   pl.BlockSpec((None, N_GATE, tk), lambda l, i: (l, 0, i))),
        compiler_params=pltpu.CompilerParams(
            dimension_semantics=("parallel", "parallel"), vmem_limit_bytes=VMEM_LIMIT),
        name="w_prep",
    )(wt)


def _out_proj_kernel(ya_ref, yb_ref, yc_ref, yd_ref, w_ref, x_ref, g_ref, o_ref, *, final):
    acc = x_ref[...]
    for i, y_ref in enumerate((ya_ref, yb_ref, yc_ref, yd_ref)):
        acc = acc + _dot(y_ref[...], w_ref[i * D_GROUP:(i + 1) * D_GROUP, :].astype(BF16))
    if final:
        ms = jnp.mean(acc * acc, axis=-1, keepdims=True)
        acc = acc * lax.rsqrt(ms + EPS) * g_ref[...]
    o_ref[...] = acc


def _out_proj(ya, yb, yc, yd, w_all, xf, g, layer, final):
    rows, d = xf.shape
    tm = OUT_ROW_TILE
    yspec = pl.BlockSpec((tm, D_GROUP), lambda i: (i, 0))
    return pl.pallas_call(
        functools.partial(_out_proj_kernel, final=final),
        out_shape=jax.ShapeDtypeStruct((rows, d), F32),
        grid=(rows // tm,),
        in_specs=[yspec, yspec, yspec, yspec,
                  pl.BlockSpec((None, 4 * D_GROUP, d), lambda i: (layer, 0, 0)),
                  pl.BlockSpec((tm, d), lambda i: (i, 0)),
                  pl.BlockSpec((1, d), lambda i: (0, 0))],
        out_specs=pl.BlockSpec((tm, d), lambda i: (i, 0)),
        compiler_params=pltpu.CompilerParams(
            dimension_semantics=("parallel",), vmem_limit_bytes=VMEM_LIMIT),
        name="out_proj",
    )(ya, yb, yc, yd, w_all, xf, g.reshape(1, d))


def _level_ref(b, h, reverse):
    n_rows, n = b.shape
    blk = 2 * h
    r0 = h if reverse else h - 1
    if blk >= SUBLANE:
        pieces = [jnp.broadcast_to(b[s + r0:s + r0 + 1, :], (blk, n)) for s in range(0, n_rows, blk)]
        return pieces[0] if len(pieces) == 1 else jnp.concatenate(pieces, axis=0)
    res = lax.broadcasted_iota(jnp.int32, b.shape, 0) & (blk - 1)
    out = None
    for m in range(blk):
        shift = (m - r0) % n_rows
        cand = b if shift == 0 else pltpu.roll(b, shift, axis=0)
        out = cand if out is None else jnp.where(res == m, cand, out)
    return out


def _hgrn_lower_bound(lbl_ref, layer, depth):
    rows = [lbl_ref[j:j + 1, :] for j in range(depth)]
    mx = functools.reduce(jnp.maximum, rows)
    es = [jnp.exp(r - mx) for r in rows]
    tot = functools.reduce(lambda a, c: a + c, es)
    lb = jnp.zeros_like(rows[0])
    for j in range(1, layer + 1):
        lb = lb + es[j] / tot
    return lb


def _cumsum_rows_mxu(x, reverse):
    n = x.shape[0]
    r = lax.broadcasted_iota(jnp.int32, (n, n), 0)
    c = lax.broadcasted_iota(jnp.int32, (n, n), 1)
    tri = jnp.where((c >= r) if reverse else (c <= r), 1.0, 0.0).astype(BF16)
    hi = x.astype(BF16)
    r1 = x - hi.astype(F32)
    mid = r1.astype(BF16)
    lo = (r1 - mid.astype(F32)).astype(BF16)
    return _dot(jnp.concatenate([tri, tri, tri], axis=1), jnp.concatenate([hi, mid, lo], axis=0))


def _hgrn_level_operand(q, k, f, b2, h, reverse):
    n_rows = q.shape[0]
    blk = 2 * h
    r0 = h if reverse else h - 1
    if h >= SUBLANE:
        pieces = []
        for s0 in range(0, n_rows, blk):
            ref = b2[s0 + r0:s0 + r0 + 1, :]
            lo, up = slice(s0, s0 + h), slice(s0 + h, s0 + blk)
            if reverse:
                pieces += [q[lo] * jnp.exp2(b2[lo] - ref), k[up] * jnp.exp2(ref - b2[up])]
            else:
                pieces += [k[lo] * jnp.exp2(ref - b2[lo]), q[up] * jnp.exp2(b2[up] - ref)]
        return jnp.concatenate(pieces, axis=0)
    row = lax.broadcasted_iota(jnp.int32, q.shape, 0)
    is_q = ((row & h) == 0) if reverse else ((row & h) != 0)
    qk = jnp.where(is_q, q, k)
    if h == 1:
        return jnp.where(is_q, q * f, k)
    if h == 2:
        res = row & 3
        f_dn = pltpu.roll(f, 1, axis=0)
        f_up = pltpu.roll(f, n_rows - 1, axis=0)
        if reverse:
            e = jnp.where(res == 0, f * f_up, jnp.where(res == 1, f, jnp.where(res == 2, 1.0, f_dn)))
        else:
            e = jnp.where(res == 0, f_up, jnp.where(res == 1, 1.0, jnp.where(res == 2, f, f_dn * f)))
        return qk * e
    return qk * jnp.exp2(-jnp.abs(b2 - _level_ref(b2, h, reverse)))


def _hgrn_kernel(*refs, layer, depth, reverse, final):
    if final:
        q_ref, v_ref, zg_ref, lbl_ref, z_ref, ob_ref, ng_ref, out_ref, st_ref = refs
    else:
        q_ref, v_ref, zg_ref, lbl_ref, out_ref, st_ref = refs

    @pl.when(pl.program_id(1) == 0)
    def _():
        st_ref[...] = jnp.zeros_like(st_ref)

    rows = range(q_ref.shape[0])
    n_rows = q_ref.shape[1]
    ti = lax.broadcasted_iota(jnp.int32, (n_rows, n_rows), 0)
    si = lax.broadcasted_iota(jnp.int32, (n_rows, n_rows), 1)
    ahead = (ti < si) if reverse else (ti > si)
    level = jnp.where(ahead, 31 - lax.clz(ti ^ si), jnp.where(ti == si, -1, -2))
    preps = [_hgrn_prep(q_ref.at[g], zg_ref.at[g], lbl_ref, layer, depth, reverse) for g in rows]

    def run(fast):
        outs = [[] for _ in rows]
        for hd in range(A_HEADS):
            sl = slice(hd * A_HD, (hd + 1) * A_HD)
            heads = [tuple(a[:, sl] for a in preps[g]) for g in rows]
            if fast:
                atts = [_hgrn_scores_fast(heads[g], level, reverse) for g in rows]
            else:
                atts = [_hgrn_scores(heads[g], level, n_rows, None, reverse) for g in rows]
                h = n_rows // 2
                while h >= 1:
                    atts = [_hgrn_scores(heads[g], level, h, atts[g], reverse) for g in rows]
                    h //= 2
            for g in rows:
                outs[g].append(_hgrn_apply(heads[g], atts[g], v_ref.at[g, :, sl],
                                           st_ref.at[g, hd], reverse))
        for g in rows:
            o = jnp.concatenate(outs[g], axis=1)
            if final:
                _hgrn_finish(o, z_ref.at[g], ob_ref.at[g], ng_ref.at[layer:layer + 1],
                             out_ref.at[g])
            else:
                out_ref[g] = o

    worst = None
    for g in rows:
        b2 = preps[g][3]
        for s0 in range(0, n_rows, HGRN_SUB):
            mid = b2[s0 + HGRN_SUB // 2:s0 + HGRN_SUB // 2 + 1, :]
            for edge in (s0, s0 + HGRN_SUB - 1):
                span = jnp.abs(b2[edge:edge + 1, :] - mid)
                worst = span if worst is None else jnp.maximum(worst, span)
    safe = jnp.max(worst) <= HGRN_SAFE_LOG2

    @pl.when(safe)
    def _():
        run(True)

    @pl.when(jnp.logical_not(safe))
    def _():
        run(False)


def _hgrn_prep(q_ref, zg_ref, lbl_ref, layer, depth, reverse):
    n_rows = q_ref.shape[0]
    q = q_ref[...]
    zg = zg_ref[...]
    e = jnp.exp(-jnp.abs(zg))
    r = 1.0 / (1.0 + e)
    er = e * r
    pos = zg >= 0.0
    f = jnp.where(pos, r, er)
    k = jnp.where(pos, er, r)
    if layer > 0:
        lb = _hgrn_lower_bound(lbl_ref, layer, depth)
        f = lb + (1.0 - lb) * f
        k = (1.0 - lb) * k
    b2 = _cumsum_rows_mxu(jnp.log2(f), reverse)
    return q, k, f, b2


def _hgrn_scores(head, level, h, att, reverse):
    q, k, f, b2 = head
    if att is None:
        return jnp.where(level == -1, _dot_nt(q.astype(BF16), k.astype(BF16)), 0.0)
    c = _hgrn_level_operand(q, k, f, b2, h, reverse).astype(BF16)
    return jnp.where(level == (h.bit_length() - 1), _dot_nt(c, c), att)


def _hgrn_scores_fast(head, level, reverse):
    q, k, _, b2 = head
    n_rows = q.shape[0]
    blocks = []
    for s0 in range(0, n_rows, HGRN_SUB):
        rows = slice(s0, s0 + HGRN_SUB)
        ref = b2[s0 + HGRN_SUB // 2:s0 + HGRN_SUB // 2 + 1, :]
        qs = (q[rows] * jnp.exp2(b2[rows] - ref)).astype(BF16)
        ks = (k * jnp.exp2(jnp.minimum(ref - b2, HGRN_SAFE_LOG2))).astype(BF16)
        blocks.append(_dot_nt(qs, ks))
    return jnp.where(level >= -1, jnp.concatenate(blocks, axis=0), 0.0)


def _hgrn_apply(head, att, v_ref, st_ref, reverse):
    q, k, _, b2 = head
    n_rows = q.shape[0]
    tot = b2[0:1, :] if reverse else b2[n_rows - 1:n_rows, :]
    v_bf = v_ref[...].astype(BF16)
    q_in = (q * jnp.exp2(b2)).astype(BF16)
    k_out = (k * jnp.exp2(tot - b2)).astype(BF16)
    st = st_ref[...]
    o = _dot(att.astype(BF16), v_bf) + _dot_nt(q_in, st.astype(BF16))
    st_ref[...] = jnp.exp2(tot) * st + _dot_tn(v_bf, k_out)
    return o


def _hgrn_finish(o, z_ref, ob_ref, ng_ref, out_ref):
    o = o + ob_ref[...]
    ys = []
    for hd in range(A_HEADS):
        sl = slice(hd * A_HD, (hd + 1) * A_HD)
        oh = o[:, sl]
        ms = jnp.mean(oh * oh, axis=-1, keepdims=True)
        ys.append(oh * lax.rsqrt(ms + EPS) * ng_ref[:, sl])
    y = jnp.concatenate(ys, axis=1) * _silu(z_ref[...])
    out_ref[...] = y.astype(out_ref.dtype)


def _chunk_index(c, n_chunks, reverse):
    return n_chunks - 1 - c if reverse else c


def _hgrn(u3, lb_logits, layer, *, reverse, ob=None, norm_g=None):
    batch, seq, _ = u3.shape
    n_chunks = seq // CHUNK
    depth = lb_logits.shape[0]
    final = ob is not None
    grp = REC_GROUP

    def ublk(blk):
        return pl.BlockSpec((grp, CHUNK, D_GROUP),
                            lambda bg, c: (bg, _chunk_index(c, n_chunks, reverse), blk))

    row_spec = ublk(0)
    in_specs = [ublk(BLK_A_Q), ublk(BLK_A_I), ublk(BLK_A_FB if reverse else BLK_A_FF),
                pl.BlockSpec((depth, D_GROUP), lambda bg, c: (0, 0))]
    args = [u3, u3, u3, lb_logits]
    if final:
        in_specs += [ublk(BLK_A_Z), row_spec, pl.BlockSpec(norm_g.shape, lambda bg, c: (0, 0))]
        args += [u3, ob, norm_g]
    return pl.pallas_call(
        functools.partial(_hgrn_kernel, layer=layer, depth=depth, reverse=reverse, final=final),
        out_shape=jax.ShapeDtypeStruct((batch, seq, D_GROUP), BF16 if final else F32),
        grid=(batch // grp, n_chunks),
        in_specs=in_specs,
        out_specs=row_spec,
        scratch_shapes=[pltpu.VMEM((grp, A_HEADS, A_HD, A_HD), F32)],
        compiler_params=pltpu.CompilerParams(
            dimension_semantics=("parallel", "arbitrary"), vmem_limit_bytes=VMEM_LIMIT),
        name="hgrn_fwd" if final else "hgrn_bwd",
    )(*args)


def _split3(x):
    hi = x.astype(BF16).astype(F32)
    r1 = x - hi
    mid = r1.astype(BF16).astype(F32)
    lo = (r1 - mid).astype(BF16).astype(F32)
    return hi, mid, lo


def _conv_silu_qk(qkp_ref, qkc_ref, qkn_ref, cw_ref, reverse):
    n_rows = qkc_ref.shape[0]
    c = pl.program_id(1)
    n_chunks = pl.num_programs(1)
    cc = _chunk_index(c, n_chunks, reverse)
    halo = SUBLANE
    prev = qkp_ref[n_rows - halo:n_rows, :] * (cc > 0).astype(F32)
    nxt = qkn_ref[0:halo, :] * (cc < n_chunks - 1).astype(F32)
    xcat = jnp.concatenate([prev, qkc_ref[...], nxt], axis=0)
    acc = None
    for j in range(CONV_W):
        off = halo + j - CONV_W // 2
        term = cw_ref[j:j + 1, :] * xcat[off:off + n_rows, :]
        acc = term if acc is None else acc + term
    qk = _silu(acc)
    n_qk = C_HEADS * C_QK
    return jnp.concatenate([qk[:, :n_qk] * (C_QK ** -0.5), qk[:, n_qk:]], axis=1).astype(BF16)


def _mlstm_kernel(*refs, layer, reverse, final):
    if final:
        (qk_ref, v_ref, gt_ref, og_ref, z_ref, hb_ref, ng_ref,
         out_ref, cst_ref, m_ref) = refs
    else:
        (qkp_ref, qkc_ref, qkn_ref, cw_ref, v_ref, gt_ref,
         out_ref, qko_ref, cst_ref, m_ref) = refs

    @pl.when(pl.program_id(1) == 0)
    def _():
        cst_ref[...] = jnp.zeros_like(cst_ref)
        m_ref[...] = jnp.zeros_like(m_ref)

    rows = range(v_ref.shape[0])
    if final:
        qks = [qk_ref[g] for g in rows]
    else:
        qks = [_conv_silu_qk(qkp_ref.at[g], qkc_ref.at[g], qkn_ref.at[g], cw_ref.at[layer], reverse)
               for g in rows]
        for g in rows:
            qko_ref[g] = qks[g]
    gates = [_mlstm_gates(gt_ref.at[g], m_ref.at[g], reverse) for g in rows]
    outs = [[] for _ in rows]
    for hd in range(C_HEADS):
        weights = [_mlstm_weights(hd, qks[g], gates[g], reverse) for g in rows]
        for g in rows:
            h_t = _mlstm_apply(hd, qks[g], v_ref.at[g], gates[g], weights[g], cst_ref.at[g], reverse)
            if final:
                outs[g].append((h_t + hb_ref[g, hd * C_V:(hd + 1) * C_V, :]).T)
            else:
                out_ref[g, hd * C_V:(hd + 1) * C_V, :] = h_t
    if final:
        for g in rows:
            _mlstm_finish(jnp.concatenate(outs[g], axis=1), og_ref.at[g], z_ref.at[g],
                          ng_ref.at[layer:layer + 1], out_ref.at[g])


def _mlstm_gates(gt_ref, m_ref, reverse):
    n_rows = gt_ref.shape[1]
    n_dir = 2 * C_HEADS

    li = gt_ref[0:n_dir, :]
    b = gt_ref[n_dir:, :]
    b_last = b[:, 0:1] if reverse else b[:, n_rows - 1:n_rows]
    m_prev = m_ref[...]
    d_inter = b + m_prev
    a = (b_last - b) + li
    m_new = jnp.maximum(b_last + m_prev, jnp.max(a, axis=1, keepdims=True))
    ws = jnp.exp(a - m_new)
    decay = jnp.exp((b_last + m_prev) - m_new)
    m_ref[...] = m_new

    one8 = jnp.ones((n_dir, n_rows), F32)
    lhs = jnp.concatenate(list(_split3(li - b)) + [one8, one8, one8], axis=0).T.astype(BF16)
    rhs_all = jnp.concatenate([one8, one8, one8] + list(_split3(b)), axis=0)
    return lhs, rhs_all, d_inter, ws, decay


def _mlstm_weights(hd, qk, gates, reverse):
    lhs, rhs_all, d_inter, _, _ = gates
    n_rows = qk.shape[0]
    n_qk = C_HEADS * C_QK
    n_dir = 2 * C_HEADS
    gi = (C_HEADS if reverse else 0) + hd
    gate_row = lax.broadcasted_iota(jnp.int32, rhs_all.shape, 0) & (n_dir - 1)
    si = lax.broadcasted_iota(jnp.int32, (n_rows, n_rows), 0)
    ti = lax.broadcasted_iota(jnp.int32, (n_rows, n_rows), 1)
    causal = (si >= ti) if reverse else (si <= ti)

    q_h = qk[:, hd * C_QK:(hd + 1) * C_QK]
    k_h = qk[:, n_qk + hd * C_QK:n_qk + (hd + 1) * C_QK]
    rhs = jnp.where(gate_row == gi, rhs_all, 0.0).astype(BF16)
    d = jnp.where(causal, _dot(lhs, rhs), NEG)
    d_int = d_inter[gi:gi + 1, :]
    m_t = jnp.maximum(jnp.max(d, axis=0, keepdims=True), d_int)
    w = (_dot_nt(k_h, q_h) * jnp.exp(d - m_t)).astype(BF16)
    return w, m_t, jnp.exp(d_int - m_t)


def _mlstm_apply(hd, qk, vt_ref, gates, weights, cst_ref, reverse):
    _, _, _, ws, decay = gates
    w, m_t, g = weights
    n_rows = qk.shape[0]
    n_qk = C_HEADS * C_QK
    gi = (C_HEADS if reverse else 0) + hd
    q_h = qk[:, hd * C_QK:(hd + 1) * C_QK]
    k_h = qk[:, n_qk + hd * C_QK:n_qk + (hd + 1) * C_QK]
    v_t = vt_ref[hd * C_V:(hd + 1) * C_V, :]
    v_ext = jnp.concatenate([v_t, jnp.ones((C_ONES, n_rows), BF16)], axis=0)
    ws_row = ws[gi:gi + 1, :]
    v_scaled = jnp.concatenate([(v_t.astype(F32) * ws_row).astype(BF16),
                                jnp.broadcast_to(ws_row, (C_ONES, n_rows)).astype(BF16)], axis=0)
    cst = cst_ref[hd]
    num = _dot(v_ext, w) + g * _dot_nt(cst.astype(BF16), q_h)
    den = jnp.maximum(jnp.abs(num[C_V:C_V + 1, :]), jnp.exp(-m_t))
    cst_ref[hd] = decay[gi:gi + 1, :C_QK] * cst + _dot(v_scaled, k_h)
    return num[:C_V, :] / den


def _mlstm_finish(hsum, og_ref, z_ref, ng_ref, out_ref):
    hsum = hsum * _sigmoid(og_ref[...])
    ys = []
    for hd in range(C_HEADS):
        sl = slice(hd * C_V, (hd + 1) * C_V)
        hh = hsum[:, sl]
        mu = jnp.mean(hh, axis=-1, keepdims=True)
        var = jnp.mean(jnp.square(hh - mu), axis=-1, keepdims=True)
        ys.append((hh - mu) * lax.rsqrt(var + EPS) * ng_ref[:, sl])
    y = jnp.concatenate(ys, axis=1) * _silu(z_ref[...])
    out_ref[...] = y.astype(out_ref.dtype)


def _mlstm(u3, gt, vt, conv_w, layer, *, reverse, qk=None, hb=None, norm_g=None):
    batch, seq, _ = u3.shape
    n_chunks = seq // CHUNK
    final = hb is not None
    grp = REC_GROUP
    assert CHUNK == LANE

    def chunk(c, delta=0):
        return jnp.clip(_chunk_index(c, n_chunks, reverse) + delta, 0, n_chunks - 1)

    def ublk(blk, delta=0):
        return pl.BlockSpec((grp, CHUNK, D_GROUP), lambda bg, c: (bg, chunk(c, delta), blk))

    row_spec = ublk(0)
    t_spec = lambda ch: pl.BlockSpec((grp, ch, CHUNK), lambda bg, c: (bg, 0, chunk(c)))
    gate_specs = [t_spec(N_GATE)]
    gate_args = [gt]
    if final:
        in_specs = [row_spec, t_spec(D_GROUP)] + gate_specs + [
            ublk(BLK_C_O), ublk(BLK_C_Z), t_spec(D_GROUP),
            pl.BlockSpec(norm_g.shape, lambda bg, c: (0, 0))]
        args = [qk, vt] + gate_args + [u3, u3, hb, norm_g]
        out_shape = jax.ShapeDtypeStruct((batch, seq, D_GROUP), BF16)
        out_specs = row_spec
    else:
        in_specs = [ublk(BLK_C_QK, -1), ublk(BLK_C_QK), ublk(BLK_C_QK, 1),
                    pl.BlockSpec(conv_w.shape, lambda bg, c: (0, 0, 0)),
                    t_spec(D_GROUP)] + gate_specs
        args = [u3, u3, u3, conv_w, vt] + gate_args
        out_shape = (jax.ShapeDtypeStruct((batch, D_GROUP, seq), F32),
                     jax.ShapeDtypeStruct((batch, seq, D_GROUP), BF16))
        out_specs = (t_spec(D_GROUP), row_spec)
    return pl.pallas_call(
        functools.partial(_mlstm_kernel, layer=layer, reverse=reverse, final=final),
        out_shape=out_shape,
        grid=(batch // grp, n_chunks),
        in_specs=in_specs,
        out_specs=out_specs,
        scratch_shapes=[pltpu.VMEM((grp, C_HEADS, C_V + C_ONES, C_QK), F32),
                        pltpu.VMEM((grp, 2 * C_HEADS, CHUNK), F32)],
        compiler_params=pltpu.CompilerParams(
            dimension_semantics=("parallel", "arbitrary"), vmem_limit_bytes=VMEM_LIMIT),
        name="mlstm_fwd" if final else "mlstm_bwd",
    )(*args)


def _rope_table_kernel(pos_ref, invf_ref, cos_ref, sin_ref):
    ang = invf_ref[...] * pos_ref[...].astype(F32)
    n_rows = ang.shape[1]
    half = ROPE_DIM // 2
    r = lax.broadcasted_iota(jnp.int32, ang.shape, 0)
    sn = jnp.sin(ang)
    sn = jnp.where(r < half, -sn, sn)
    pad = jnp.zeros((LANE - ROPE_DIM, n_rows), F32)

    def expand(x):
        t = jnp.concatenate([x, pad], axis=0).T
        return t + pltpu.roll(t, B_HD, axis=1)

    lane = lax.broadcasted_iota(jnp.int32, (n_rows, LANE), 1) & (B_HD - 1)
    cos_ref[...] = jnp.where(lane < ROPE_DIM, expand(jnp.cos(ang)), 1.0)
    sin_ref[...] = expand(sn)


def _rope_tables(positions):
    rows = positions.size
    half = ROPE_DIM // 2
    inv_freq = ROPE_THETA ** (-jnp.arange(0, ROPE_DIM, 2, dtype=F32) / ROPE_DIM)
    invf = jnp.concatenate([inv_freq, inv_freq]).reshape(ROPE_DIM, 1)
    tm = ROW_TILE
    return pl.pallas_call(
        _rope_table_kernel,
        out_shape=(jax.ShapeDtypeStruct((rows, LANE), F32),) * 2,
        grid=(rows // tm,),
        in_specs=[pl.BlockSpec((1, tm), lambda i: (0, i)),
                  pl.BlockSpec((ROPE_DIM, 1), lambda i: (0, 0))],
        out_specs=(pl.BlockSpec((tm, LANE), lambda i: (i, 0)),) * 2,
        compiler_params=pltpu.CompilerParams(dimension_semantics=("parallel",)),
        name="rope_tables",
    )(positions.reshape(1, rows), invf)


def _rope(x, cos, sin):
    width = x.shape[1]
    if width > LANE:
        cos = jnp.concatenate([cos] * (width // LANE), axis=1)
        sin = jnp.concatenate([sin] * (width // LANE), axis=1)
    half = ROPE_DIM // 2
    r = lax.broadcasted_iota(jnp.int32, x.shape, 1) & (B_HD - 1)
    partner = jnp.where(r < half, pltpu.roll(x, width - half, axis=1), pltpu.roll(x, half, axis=1))
    return x * cos + partner * sin


def _swa_kernel(q_ref, z_ref, mp_ref, mc_ref, mn_ref, cp_ref, cc_ref, cn_ref,
                sp_ref, sc_ref, sn_ref, sink_ref, out_ref, *, layer):
    tile = q_ref.shape[0]
    blk = mp_ref.shape[0]
    n_sub = tile // blk
    i = pl.program_id(1)
    n_tiles = pl.num_programs(1)
    kv_w = B_KV_HEADS * B_HD
    grp = B_HEADS // B_KV_HEADS

    q = (_rope(q_ref[...], cc_ref[...], sc_ref[...]) * (B_HD ** -0.5)).astype(BF16)
    k = jnp.concatenate([
        _rope(mp_ref[:, MISC_K:MISC_K + kv_w], cp_ref[...], sp_ref[...]),
        _rope(mc_ref[:, MISC_K:MISC_K + kv_w], cc_ref[...], sc_ref[...]),
        _rope(mn_ref[:, MISC_K:MISC_K + kv_w], cn_ref[...], sn_ref[...])], axis=0).astype(BF16)
    v = jnp.concatenate([mp_ref[:, MISC_V:MISC_V + kv_w], mc_ref[:, MISC_V:MISC_V + kv_w],
                         mn_ref[:, MISC_V:MISC_V + kv_w]], axis=0)
    n_keys = v.shape[0]

    kr = lax.broadcasted_iota(jnp.int32, (3 * blk, blk), 0)
    qc = lax.broadcasted_iota(jnp.int32, (3 * blk, blk), 1)
    in_window = jnp.abs(qc - (kr - blk)) <= WINDOW
    biases = []
    for j in range(n_sub):
        valid = in_window
        if j == 0:
            valid = valid & (kr >= jnp.where(i > 0, 0, blk))
        if j == n_sub - 1:
            valid = valid & (kr < jnp.where(i < n_tiles - 1, 3 * blk, 2 * blk))
        biases.append(jnp.where(valid, 0.0, NEG))

    ones = jnp.ones((n_keys, B_HD), F32)
    vts = [jnp.concatenate([v[:, kvh * B_HD:(kvh + 1) * B_HD], ones], axis=1).T.astype(BF16)
           for kvh in range(B_KV_HEADS)]

    tiles = [[] for _ in range(n_sub)]
    pairs = [[] for _ in range(n_sub)]
    for hq in range(B_HEADS):
        kvh = hq // grp
        ksl = slice(kvh * B_HD, (kvh + 1) * B_HD)
        sk = sink_ref[layer:layer + 1, hq:hq + 1]
        for j in range(n_sub):
            keys = slice(j * blk, (j + 3) * blk)
            qh = q[j * blk:(j + 1) * blk, hq * B_HD:(hq + 1) * B_HD]
            s = _dot_nt(k[keys, ksl], qh)
            s = jnp.concatenate([s[:blk] + biases[j][:blk], s[blk:2 * blk],
                                 s[2 * blk:] + biases[j][2 * blk:]], axis=0)
            m = jnp.maximum(jnp.max(s, axis=0, keepdims=True), sk)
            p = jnp.exp(s - m).astype(BF16)
            o = _dot(vts[kvh][:, keys], p)
            denom = o[B_HD:B_HD + 1, :] + jnp.exp(sk - m)
            pairs[j].append(o[:B_HD, :] / denom)
            if len(pairs[j]) == LANE // B_HD:
                tiles[j].append(jnp.concatenate(pairs[j], axis=0).T)
                pairs[j] = []
    for j in range(n_sub):
        rows = slice(j * blk, (j + 1) * blk)
        y = jnp.concatenate(tiles[j], axis=1) * _silu(z_ref[rows, :])
        out_ref[rows, :] = y.astype(out_ref.dtype)


def _swa(u, cos_t, sin_t, sink, layer, batch):
    rows = u.shape[0]
    n_blocks = rows // batch // CHUNK
    n_sub = SWA_TILE // CHUNK
    n_tiles = n_blocks // n_sub

    def tile_spec(width, blk):
        return pl.BlockSpec((SWA_TILE, width), lambda bi, i: (bi * n_tiles + i, blk))

    def halo_spec(width, blk, after):
        def index(bi, i):
            nb = (i + 1) * n_sub if after else i * n_sub - 1
            return (bi * n_blocks + jnp.clip(nb, 0, n_blocks - 1), blk)
        return pl.BlockSpec((CHUNK, width), index)

    def table_specs():
        return [halo_spec(LANE, 0, False), tile_spec(LANE, 0), halo_spec(LANE, 0, True)]

    return pl.pallas_call(
        functools.partial(_swa_kernel, layer=layer),
        out_shape=jax.ShapeDtypeStruct((rows, D_GROUP), BF16),
        grid=(batch, n_tiles),
        in_specs=[tile_spec(D_GROUP, BLK_B_Q), tile_spec(D_GROUP, BLK_B_Z),
                  halo_spec(D_GROUP, BLK_MISC, False), tile_spec(D_GROUP, BLK_MISC),
                  halo_spec(D_GROUP, BLK_MISC, True)] + table_specs() + table_specs() + [
                  pl.BlockSpec(sink.shape, lambda bi, i: (0, 0))],
        out_specs=tile_spec(D_GROUP, 0),
        compiler_params=pltpu.CompilerParams(
            dimension_semantics=("parallel", "parallel"), vmem_limit_bytes=VMEM_LIMIT),
        name="window_attn",
    )(u, u, u, u, u, cos_t, cos_t, cos_t, sin_t, sin_t, sin_t, sink)


def _mem_kv_kernel(mem_ref, g_ref, w_ref, kt_ref, v_ref, *, layer):
    x = mem_ref[...]
    ms = jnp.mean(x * x, axis=-1, keepdims=True)
    h = (x * lax.rsqrt(ms + EPS) * g_ref[layer:layer + 1, :]).astype(BF16)
    kv = _dot(h, w_ref[...].astype(BF16))
    kt_ref[...] = kv[:, :D_GROUP].T.astype(kt_ref.dtype)
    v_ref[...] = kv[:, D_GROUP:].astype(v_ref.dtype)


def _mem_kv(mem, g_all, w_all, layer):
    batch, m_len, d = mem.shape
    n_out = w_all.shape[2]
    return pl.pallas_call(
        functools.partial(_mem_kv_kernel, layer=layer),
        out_shape=(jax.ShapeDtypeStruct((batch, D_GROUP, m_len), BF16),
                   jax.ShapeDtypeStruct((batch, m_len, D_GROUP), BF16)),
        grid=(batch,),
        in_specs=[pl.BlockSpec((None, m_len, d), lambda bi: (bi, 0, 0)),
                  pl.BlockSpec(g_all.shape, lambda bi: (0, 0)),
                  pl.BlockSpec((None, d, n_out), lambda bi: (layer, 0, 0))],
        out_specs=(pl.BlockSpec((None, D_GROUP, m_len), lambda bi: (bi, 0, 0)),
                   pl.BlockSpec((None, m_len, D_GROUP), lambda bi: (bi, 0, 0))),
        compiler_params=pltpu.CompilerParams(
            dimension_semantics=("parallel",), vmem_limit_bytes=VMEM_LIMIT),
        name="mem_kv",
    )(mem, g_all, w_all)


def _mem_attn_kernel(q_ref, z_ref, kt_ref, v_ref, out_ref):
    q = q_ref[...].astype(BF16)
    outs = []
    for hd in range(D_HEADS):
        sl = slice(hd * D_HD, (hd + 1) * D_HD)
        s = _dot(q[:, sl], kt_ref[sl, :])
        m = jnp.max(s, axis=-1, keepdims=True)
        p = jnp.exp2((s - m) * (D_HD ** -0.5 * LOG2_E))
        den = jnp.sum(p, axis=-1, keepdims=True)
        outs.append(_dot(p.astype(BF16), v_ref[:, sl]) / den)
    y = jnp.concatenate(outs, axis=1) * _silu(z_ref[...])
    out_ref[...] = y.astype(out_ref.dtype)


def _mem_attn(u, k, vt, batch):
    rows = u.shape[0]
    tm = ROW_TILE
    per_batch = rows // batch // tm
    return pl.pallas_call(
        _mem_attn_kernel,
        out_shape=jax.ShapeDtypeStruct((rows, D_GROUP), BF16),
        grid=(batch, per_batch),
        in_specs=[pl.BlockSpec((tm, D_GROUP), lambda bi, i: (bi * per_batch + i, BLK_D_Q)),
                  pl.BlockSpec((tm, D_GROUP), lambda bi, i: (bi * per_batch + i, BLK_D_Z)),
                  pl.BlockSpec((None,) + k.shape[1:], lambda bi, i: (bi, 0, 0)),
                  pl.BlockSpec((None,) + vt.shape[1:], lambda bi, i: (bi, 0, 0))],
        out_specs=pl.BlockSpec((tm, D_GROUP), lambda bi, i: (bi * per_batch + i, 0)),
        compiler_params=pltpu.CompilerParams(
            dimension_semantics=("parallel", "parallel"), vmem_limit_bytes=VMEM_LIMIT),
        name="mem_attn",
    )(u, u, k, vt)


def kernel(x, mem, positions, norm_g, w_in, hgrn_lb_logits, hgrn_norm_g, attn_sink, mlstm_conv_w,
           mlstm_gate_b, mlstm_norm_g, mem_norm_g, w_mem_kv, w_out, final_norm_g):
    batch, seq, d = x.shape
    depth = w_in.shape[0]
    assert all(seq % t == 0 for t in (CHUNK, ROW_TILE, IN_ROW_TILE, SWA_TILE))
    assert (batch * seq) % OUT_ROW_TILE == 0
    assert batch % REC_GROUP == 0
    xf = x.reshape(batch * seq, d)
    cos_t, sin_t = _rope_tables(positions)
    w_cat_all, w_gt_all = _prep_w_in(w_in)
    gate_b = mlstm_gate_b.reshape(depth, N_GATE, 1)
    for layer in range(depth):
        u, gt, vt = _in_proj(xf, norm_g, w_cat_all, w_gt_all, gate_b, layer, batch)
        u3 = u.reshape(batch, seq, u.shape[1])
        ob = _hgrn(u3, hgrn_lb_logits, layer, reverse=True)
        ya = _hgrn(u3, hgrn_lb_logits, layer, reverse=False, ob=ob, norm_g=hgrn_norm_g)
        yb = _swa(u, cos_t, sin_t, attn_sink, layer, batch)
        hb, qk_c = _mlstm(u3, gt, vt, mlstm_conv_w, layer, reverse=True)
        yc = _mlstm(u3, gt, vt, None, layer, reverse=False, qk=qk_c, hb=hb, norm_g=mlstm_norm_g)
        mem_k, mem_vt = _mem_kv(mem, mem_norm_g, w_mem_kv, layer)
        yd = _mem_attn(u, mem_k, mem_vt, batch)
        xf = _out_proj(ya.reshape(batch * seq, D_GROUP), yb, yc.reshape(batch * seq, D_GROUP), yd,
                       w_out, xf, final_norm_g, layer, final=(layer == depth - 1))
    return xf.reshape(batch, seq, d)
```

```python
import functools

import jax
import jax.numpy as jnp
import numpy as np
from jax import lax
from jax.experimental import pallas as pl
from jax.experimental.pallas import tpu as pltpu

F32 = jnp.float32
BF16 = jnp.bfloat16

D_GROUP = 512
A_HEADS, A_HD = 4, 128
B_HEADS, B_KV_HEADS, B_HD = 8, 2, 64
WINDOW = 128
ROPE_THETA = 500000.0
ROPE_DIM = 16
C_HEADS, C_QK, C_V = 4, 64, 128
CONV_W = 5
D_HEADS, D_HD = 4, 128
EPS = 1e-6
IN_SIZES = (
    D_GROUP, D_GROUP, D_GROUP, D_GROUP, D_GROUP,
    B_HEADS * B_HD, B_KV_HEADS * B_HD, B_KV_HEADS * B_HD, D_GROUP,
    C_HEADS * C_QK, C_HEADS * C_QK, C_HEADS * C_V, D_GROUP, D_GROUP,
    2 * C_HEADS, 2 * C_HEADS,
    D_GROUP, D_GROUP,
)
SPLIT_POINTS = tuple(int(s) for s in np.cumsum(IN_SIZES)[:-1])

LANE = 128
SUBLANE = 8
VMEM_LIMIT = 48 * 1024 * 1024

CHUNK = 128
ROW_TILE = 512
IN_ROW_TILE = 512
IN_PROJ_VMEM_LIMIT = 56 * 1024 * 1024
W_PREP_TILE = 256
OUT_ROW_TILE = 1024
HGRN_SUB = 64
HGRN_SAFE_LOG2 = 100.0
SWA_TILE = 512
REC_GROUP = 4
NEG = -1e30
LOG2_E = 1.4426950408889634

(BLK_A_Q, BLK_A_I, BLK_A_FF, BLK_A_FB, BLK_A_Z, BLK_B_Q, BLK_B_Z, BLK_C_QK, BLK_C_O,
 BLK_C_Z, BLK_D_Q, BLK_D_Z, BLK_MISC) = range(13)
N_BLK = 13
W_BLK_C_V = N_BLK
N_W_BLK = N_BLK + 1
MISC_K, MISC_V, MISC_G = 0, 128, 256
N_GATE = 4 * C_HEADS
C_ONES = 16


def _dot(a, b):
    return jnp.dot(a, b, preferred_element_type=F32)


def _dot_nt(a, b):
    return lax.dot_general(a, b, (((1,), (1,)), ((), ())), preferred_element_type=F32)


def _dot_tn(a, b):
    return lax.dot_general(a, b, (((0,), (0,)), ((), ())), preferred_element_type=F32)


def _sigmoid(x):
    return 1.0 / (1.0 + jnp.exp(-x))


def _silu(x):
    return x * _sigmoid(x)


def _log_sigmoid(x):
    return jnp.minimum(x, 0.0) - jnp.log(1.0 + jnp.exp(-jnp.abs(x)))


def _chunk_cumsum_lanes(x, reverse):
    n = x.shape[1]
    pos = lax.broadcasted_iota(jnp.int32, x.shape, 1) & (CHUNK - 1)
    s = 1
    while s < CHUNK:
        if reverse:
            x = x + jnp.where(pos < CHUNK - s, pltpu.roll(x, n - s, axis=1), 0.0)
        else:
            x = x + jnp.where(pos >= s, pltpu.roll(x, s, axis=1), 0.0)
        s *= 2
    return x


def _in_proj_kernel(x_ref, g_ref, w_ref, wgt_ref, gb_ref, u_ref, gt_ref, vt_ref, *, layer):
    x = x_ref[...]
    ms = jnp.mean(x * x, axis=-1, keepdims=True)
    h = (x * lax.rsqrt(ms + EPS) * g_ref[layer:layer + 1, :]).astype(BF16)
    gates = _dot_nt(wgt_ref[...], h) + gb_ref[layer]
    n_dir = 2 * C_HEADS
    log_f = _log_sigmoid(gates[n_dir:, :])
    is_fwd = lax.broadcasted_iota(jnp.int32, log_f.shape, 0) < C_HEADS
    cum = jnp.where(is_fwd, _chunk_cumsum_lanes(log_f, False), _chunk_cumsum_lanes(log_f, True))
    gt_ref[...] = jnp.concatenate([gates[:n_dir, :], cum], axis=0)
    vt_ref[...] = _dot_nt(w_ref[W_BLK_C_V * D_GROUP:(W_BLK_C_V + 1) * D_GROUP, :], h).astype(BF16)
    for j in range(N_BLK):
        sl = slice(j * D_GROUP, (j + 1) * D_GROUP)
        u_ref[:, sl] = _dot_nt(h, w_ref[sl, :])


def _in_proj(xf, g_all, w_cat_all, w_gt_all, gate_b, layer, batch):
    rows, d = xf.shape
    tm = IN_ROW_TILE
    n_out = N_BLK * D_GROUP
    per_batch = rows // batch // tm
    return pl.pallas_call(
        functools.partial(_in_proj_kernel, layer=layer),
        out_shape=(jax.ShapeDtypeStruct((rows, n_out), F32),
                   jax.ShapeDtypeStruct((batch, N_GATE, rows // batch), F32),
                   jax.ShapeDtypeStruct((batch, D_GROUP, rows // batch), BF16)),
        grid=(rows // tm,),
        in_specs=[
            pl.BlockSpec((tm, d), lambda i: (i, 0)),
            pl.BlockSpec(g_all.shape, lambda i: (0, 0)),
            pl.BlockSpec((None, N_W_BLK * D_GROUP, d), lambda i: (layer, 0, 0),
                         pipeline_mode=pl.Buffered(1)),
            pl.BlockSpec((None, N_GATE, d), lambda i: (layer, 0, 0)),
            pl.BlockSpec(gate_b.shape, lambda i: (0, 0, 0)),
        ],
        out_specs=(
            pl.BlockSpec((tm, n_out), lambda i: (i, 0)),
            pl.BlockSpec((None, N_GATE, tm), lambda i: (i // per_batch, 0, i % per_batch)),
            pl.BlockSpec((None, D_GROUP, tm), lambda i: (i // per_batch, 0, i % per_batch)),
        ),
        compiler_params=pltpu.CompilerParams(
            dimension_semantics=("parallel",), vmem_limit_bytes=IN_PROJ_VMEM_LIMIT),
        name="in_proj",
    )(xf, g_all, w_cat_all, w_gt_all, gate_b)


def _w_in_segments():
    names = ("a_q", "a_i", "a_ff", "a_fb", "a_z", "b_q", "b_k", "b_v", "b_z",
             "c_q", "c_k", "c_v", "c_o", "c_z", "c_ig", "c_fg", "d_q", "d_z")
    src = dict(zip(names, zip((0,) + SPLIT_POINTS, IN_SIZES)))
    order = ("a_q", "a_i", "a_ff", "a_fb", "a_z", "b_q", "b_z", "c_q", "c_k", "c_o", "c_z",
             "d_q", "d_z", "b_k", "b_v", "c_ig", "c_fg")
    segs, dst = [], 0
    for name in order + ("c_v",):
        off, width = src[name]
        if name == "c_v":
            pad = (dst, W_BLK_C_V * D_GROUP)
            dst = pad[1]
        if segs and segs[-1][0] + segs[-1][1] == off and segs[-1][2] + segs[-1][1] == dst:
            segs[-1] = (segs[-1][0], segs[-1][1] + width, segs[-1][2])
        else:
            segs.append((off, width, dst))
        dst += width
    return tuple(segs), pad, src["c_ig"][0]


def _w_prep_kernel(wt_ref, wc_ref, wg_ref):
    segs, pad, gate_off = _w_in_segments()
    for off, width, dst in segs:
        wc_ref[dst:dst + width, :] = wt_ref[off:off + width, :].astype(BF16)
    wc_ref[pad[0]:pad[1], :] = jnp.zeros((pad[1] - pad[0], wc_ref.shape[1]), BF16)
    wg_ref[...] = wt_ref[gate_off:gate_off + N_GATE, :].astype(BF16)


def _prep_w_in(w):
    depth, d, d_in = w.shape
    wt = jnp.swapaxes(w, 1, 2)
    tk = W_PREP_TILE
    n_out = N_W_BLK * D_GROUP
    return pl.pallas_call(
        _w_prep_kernel,
        out_shape=(jax.ShapeDtypeStruct((depth, n_out, d), BF16),
                   jax.ShapeDtypeStruct((depth, N_GATE, d), BF16)),
        grid=(depth, d // tk),
        in_specs=[pl.BlockSpec((None, d_in, tk), lambda l, i: (l, 0, i))],
        out_specs=(pl.BlockSpec((None, n_out, tk), lambda l, i: (l, 0, i)),
                   pl.BlockSpec((None, N_GATE, tk), lambda l, i: (l, 0, i))),
        compiler_params=pltpu.CompilerParams(
            dimension_semantics=("parallel", "parallel"), vmem_limit_bytes=VMEM_LIMIT),
        name="w_prep",
    )(wt)


def _out_proj_kernel(ya_ref, yb_ref, yc_ref, yd_ref, w_ref, x_ref, g_ref, o_ref, *, final):
    acc = x_ref[...]
    for i, y_ref in enumerate((ya_ref, yb_ref, yc_ref, yd_ref)):
        acc = acc + _dot(y_ref[...], w_ref[i * D_GROUP:(i + 1) * D_GROUP, :].astype(BF16))
    if final:
        ms = jnp.mean(acc * acc, axis=-1, keepdims=True)
        acc = acc * lax.rsqrt(ms + EPS) * g_ref[...]
    o_ref[...] = acc


def _out_proj(ya, yb, yc, yd, w_all, xf, g, layer, final):
    rows, d = xf.shape
    tm = OUT_ROW_TILE
    yspec = pl.BlockSpec((tm, D_GROUP), lambda i: (i, 0))
    return pl.pallas_call(
        functools.partial(_out_proj_kernel, final=final),
        out_shape=jax.ShapeDtypeStruct((rows, d), F32),
        grid=(rows // tm,),
        in_specs=[yspec, yspec, yspec, yspec,
                  pl.BlockSpec((None, 4 * D_GROUP, d), lambda i: (layer, 0, 0)),
                  pl.BlockSpec((tm, d), lambda i: (i, 0)),
                  pl.BlockSpec((1, d), lambda i: (0, 0))],
        out_specs=pl.BlockSpec((tm, d), lambda i: (i, 0)),
        compiler_params=pltpu.CompilerParams(
            dimension_semantics=("parallel",), vmem_limit_bytes=VMEM_LIMIT),
        name="out_proj",
    )(ya, yb, yc, yd, w_all, xf, g.reshape(1, d))


def _level_ref(b, h, reverse):
    n_rows, n = b.shape
    blk = 2 * h
    r0 = h if reverse else h - 1
    if blk >= SUBLANE:
        pieces = [jnp.broadcast_to(b[s + r0:s + r0 + 1, :], (blk, n)) for s in range(0, n_rows, blk)]
        return pieces[0] if len(pieces) == 1 else jnp.concatenate(pieces, axis=0)
    res = lax.broadcasted_iota(jnp.int32, b.shape, 0) & (blk - 1)
    out = None
    for m in range(blk):
        shift = (m - r0) % n_rows
        cand = b if shift == 0 else pltpu.roll(b, shift, axis=0)
        out = cand if out is None else jnp.where(res == m, cand, out)
    return out


def _hgrn_lower_bound(lbl_ref, layer, depth):
    rows = [lbl_ref[j:j + 1, :] for j in range(depth)]
    mx = functools.reduce(jnp.maximum, rows)
    es = [jnp.exp(r - mx) for r in rows]
    tot = functools.reduce(lambda a, c: a + c, es)
    lb = jnp.zeros_like(rows[0])
    for j in range(1, layer + 1):
        lb = lb + es[j] / tot
    return lb


def _cumsum_rows_mxu(x, reverse):
    n = x.shape[0]
    r = lax.broadcasted_iota(jnp.int32, (n, n), 0)
    c = lax.broadcasted_iota(jnp.int32, (n, n), 1)
    tri = jnp.where((c >= r) if reverse else (c <= r), 1.0, 0.0).astype(BF16)
    hi = x.astype(BF16)
    r1 = x - hi.astype(F32)
    mid = r1.astype(BF16)
    lo = (r1 - mid.astype(F32)).astype(BF16)
    return _dot(jnp.concatenate([tri, tri, tri], axis=1), jnp.concatenate([hi, mid, lo], axis=0))


def _hgrn_level_operand(q, k, f, b2, h, reverse):
    n_rows = q.shape[0]
    blk = 2 * h
    r0 = h if reverse else h - 1
    if h >= SUBLANE:
        pieces = []
        for s0 in range(0, n_rows, blk):
            ref = b2[s0 + r0:s0 + r0 + 1, :]
            lo, up = slice(s0, s0 + h), slice(s0 + h, s0 + blk)
            if reverse:
                pieces += [q[lo] * jnp.exp2(b2[lo] - ref), k[up] * jnp.exp2(ref - b2[up])]
            else:
                pieces += [k[lo] * jnp.exp2(ref - b2[lo]), q[up] * jnp.exp2(b2[up] - ref)]
        return jnp.concatenate(pieces, axis=0)
    row = lax.broadcasted_iota(jnp.int32, q.shape, 0)
    is_q = ((row & h) == 0) if reverse else ((row & h) != 0)
    qk = jnp.where(is_q, q, k)
    if h == 1:
        return jnp.where(is_q, q * f, k)
    if h == 2:
        res = row & 3
        f_dn = pltpu.roll(f, 1, axis=0)
        f_up = pltpu.roll(f, n_rows - 1, axis=0)
        if reverse:
            e = jnp.where(res == 0, f * f_up, jnp.where(res == 1, f, jnp.where(res == 2, 1.0, f_dn)))
        else:
            e = jnp.where(res == 0, f_up, jnp.where(res == 1, 1.0, jnp.where(res == 2, f, f_dn * f)))
        return qk * e
    return qk * jnp.exp2(-jnp.abs(b2 - _level_ref(b2, h, reverse)))


def _hgrn_kernel(*refs, layer, depth, reverse, final):
    if final:
        q_ref, v_ref, zg_ref, lbl_ref, z_ref, ob_ref, ng_ref, out_ref, st_ref = refs
    else:
        q_ref, v_ref, zg_ref, lbl_ref, out_ref, st_ref = refs

    @pl.when(pl.program_id(1) == 0)
    def _():
        st_ref[...] = jnp.zeros_like(st_ref)

    rows = range(q_ref.shape[0])
    n_rows = q_ref.shape[1]
    ti = lax.broadcasted_iota(jnp.int32, (n_rows, n_rows), 0)
    si = lax.broadcasted_iota(jnp.int32, (n_rows, n_rows), 1)
    ahead = (ti < si) if reverse else (ti > si)
    level = jnp.where(ahead, 31 - lax.clz(ti ^ si), jnp.where(ti == si, -1, -2))
    preps = [_hgrn_prep(q_ref.at[g], zg_ref.at[g], lbl_ref, layer, depth, reverse) for g in rows]

    def run(fast):
        outs = [[] for _ in rows]
        for hd in range(A_HEADS):
            sl = slice(hd * A_HD, (hd + 1) * A_HD)
            heads = [tuple(a[:, sl] for a in preps[g]) for g in rows]
            if fast:
                atts = [_hgrn_scores_fast(heads[g], level, reverse) for g in rows]
            else:
                atts = [_hgrn_scores(heads[g], level, n_rows, None, reverse) for g in rows]
                h = n_rows // 2
                while h >= 1:
                    atts = [_hgrn_scores(heads[g], level, h, atts[g], reverse) for g in rows]
                    h //= 2
            for g in rows:
                outs[g].append(_hgrn_apply(heads[g], atts[g], v_ref.at[g, :, sl],
                                           st_ref.at[g, hd], reverse))
        for g in rows:
            o = jnp.concatenate(outs[g], axis=1)
            if final:
                _hgrn_finish(o, z_ref.at[g], ob_ref.at[g], ng_ref.at[layer:layer + 1],
                             out_ref.at[g])
            else:
                out_ref[g] = o

    worst = None
    for g in rows:
        b2 = preps[g][3]
        for s0 in range(0, n_rows, HGRN_SUB):
            mid = b2[s0 + HGRN_SUB // 2:s0 + HGRN_SUB // 2 + 1, :]
            for edge in (s0, s0 + HGRN_SUB - 1):
                span = jnp.abs(b2[edge:edge + 1, :] - mid)
                worst = span if worst is None else jnp.maximum(worst, span)
    safe = jnp.max(worst) <= HGRN_SAFE_LOG2

    @pl.when(safe)
    def _():
        run(True)

    @pl.when(jnp.logical_not(safe))
    def _():
        run(False)


def _hgrn_prep(q_ref, zg_ref, lbl_ref, layer, depth, reverse):
    n_rows = q_ref.shape[0]
    q = q_ref[...]
    zg = zg_ref[...]
    e = jnp.exp(-jnp.abs(zg))
    r = 1.0 / (1.0 + e)
    er = e * r
    pos = zg >= 0.0
    f = jnp.where(pos, r, er)
    k = jnp.where(pos, er, r)
    if layer > 0:
        lb = _hgrn_lower_bound(lbl_ref, layer, depth)
        f = lb + (1.0 - lb) * f
        k = (1.0 - lb) * k
    b2 = _cumsum_rows_mxu(jnp.log2(f), reverse)
    return q, k, f, b2


def _hgrn_scores(head, level, h, att, reverse):
    q, k, f, b2 = head
    if att is None:
        return jnp.where(level == -1, _dot_nt(q.astype(BF16), k.astype(BF16)), 0.0)
    c = _hgrn_level_operand(q, k, f, b2, h, reverse).astype(BF16)
    return jnp.where(level == (h.bit_length() - 1), _dot_nt(c, c), att)


def _hgrn_scores_fast(head, level, reverse):
    q, k, _, b2 = head
    n_rows = q.shape[0]
    blocks = []
    for s0 in range(0, n_rows, HGRN_SUB):
        rows = slice(s0, s0 + HGRN_SUB)
        ref = b2[s0 + HGRN_SUB // 2:s0 + HGRN_SUB // 2 + 1, :]
        qs = (q[rows] * jnp.exp2(b2[rows] - ref)).astype(BF16)
        ks = (k * jnp.exp2(jnp.minimum(ref - b2, HGRN_SAFE_LOG2))).astype(BF16)
        blocks.append(_dot_nt(qs, ks))
    return jnp.where(level >= -1, jnp.concatenate(blocks, axis=0), 0.0)


def _hgrn_apply(head, att, v_ref, st_ref, reverse):
    q, k, _, b2 = head
    n_rows = q.shape[0]
    tot = b2[0:1, :] if reverse else b2[n_rows - 1:n_rows, :]
    v_bf = v_ref[...].astype(BF16)
    q_in = (q * jnp.exp2(b2)).astype(BF16)
    k_out = (k * jnp.exp2(tot - b2)).astype(BF16)
    st = st_ref[...]
    o = _dot(att.astype(BF16), v_bf) + _dot(q_in, st.astype(BF16))
    dec_col = jnp.broadcast_to(jnp.exp2(tot), (SUBLANE, tot.shape[1])).T[:, 0:1]
    st_ref[...] = dec_col * st + _dot_tn(k_out, v_bf)
    return o


def _hgrn_finish(o, z_ref, ob_ref, ng_ref, out_ref):
    o = o + ob_ref[...]
    ys = []
    for hd in range(A_HEADS):
        sl = slice(hd * A_HD, (hd + 1) * A_HD)
        oh = o[:, sl]
        ms = jnp.mean(oh * oh, axis=-1, keepdims=True)
        ys.append(oh * lax.rsqrt(ms + EPS) * ng_ref[:, sl])
    y = jnp.concatenate(ys, axis=1) * _silu(z_ref[...])
    out_ref[...] = y.astype(out_ref.dtype)


def _chunk_index(c, n_chunks, reverse):
    return n_chunks - 1 - c if reverse else c


def _hgrn(u3, lb_logits, layer, *, reverse, ob=None, norm_g=None):
    batch, seq, _ = u3.shape
    n_chunks = seq // CHUNK
    depth = lb_logits.shape[0]
    final = ob is not None
    grp = REC_GROUP

    def ublk(blk):
        return pl.BlockSpec((grp, CHUNK, D_GROUP),
                            lambda bg, c: (bg, _chunk_index(c, n_chunks, reverse), blk))

    row_spec = ublk(0)
    in_specs = [ublk(BLK_A_Q), ublk(BLK_A_I), ublk(BLK_A_FB if reverse else BLK_A_FF),
                pl.BlockSpec((depth, D_GROUP), lambda bg, c: (0, 0))]
    args = [u3, u3, u3, lb_logits]
    if final:
        in_specs += [ublk(BLK_A_Z), row_spec, pl.BlockSpec(norm_g.shape, lambda bg, c: (0, 0))]
        args += [u3, ob, norm_g]
    return pl.pallas_call(
        functools.partial(_hgrn_kernel, layer=layer, depth=depth, reverse=reverse, final=final),
        out_shape=jax.ShapeDtypeStruct((batch, seq, D_GROUP), BF16 if final else F32),
        grid=(batch // grp, n_chunks),
        in_specs=in_specs,
        out_specs=row_spec,
        scratch_shapes=[pltpu.VMEM((grp, A_HEADS, A_HD, A_HD), F32)],
        compiler_params=pltpu.CompilerParams(
            dimension_semantics=("parallel", "arbitrary"), vmem_limit_bytes=VMEM_LIMIT),
        name="hgrn_fwd" if final else "hgrn_bwd",
    )(*args)


def _split3(x):
    hi = x.astype(BF16).astype(F32)
    r1 = x - hi
    mid = r1.astype(BF16).astype(F32)
    lo = (r1 - mid).astype(BF16).astype(F32)
    return hi, mid, lo


def _conv_silu_qk(qkp_ref, qkc_ref, qkn_ref, cw_ref, reverse):
    n_rows = qkc_ref.shape[0]
    c = pl.program_id(1)
    n_chunks = pl.num_programs(1)
    cc = _chunk_index(c, n_chunks, reverse)
    halo = SUBLANE
    prev = qkp_ref[n_rows - halo:n_rows, :] * (cc > 0).astype(F32)
    nxt = qkn_ref[0:halo, :] * (cc < n_chunks - 1).astype(F32)
    xcat = jnp.concatenate([prev, qkc_ref[...], nxt], axis=0)
    acc = None
    for j in range(CONV_W):
        off = halo + j - CONV_W // 2
        term = cw_ref[j:j + 1, :] * xcat[off:off + n_rows, :]
        acc = term if acc is None else acc + term
    qk = _silu(acc)
    n_qk = C_HEADS * C_QK
    return qk[:, n_qk:].astype(BF16), (qk[:, :n_qk] * (C_QK ** -0.5)).T.astype(BF16)


def _mlstm_kernel(*refs, layer, reverse, final):
    if final:
        (k_ref, qt_ref, v_ref, gt_ref, og_ref, z_ref, hb_ref, ng_ref,
         out_ref, cst_ref, m_ref) = refs
    else:
        (qkp_ref, qkc_ref, qkn_ref, cw_ref, v_ref, gt_ref,
         out_ref, ko_ref, qto_ref, cst_ref, m_ref) = refs

    @pl.when(pl.program_id(1) == 0)
    def _():
        cst_ref[...] = jnp.zeros_like(cst_ref)
        m_ref[...] = jnp.zeros_like(m_ref)

    rows = range(v_ref.shape[0])
    if final:
        qks = [(k_ref[g], qt_ref[g]) for g in rows]
    else:
        qks = [_conv_silu_qk(qkp_ref.at[g], qkc_ref.at[g], qkn_ref.at[g], cw_ref.at[layer], reverse)
               for g in rows]
        for g in rows:
            ko_ref[g], qto_ref[g] = qks[g]
    gates = [_mlstm_gates(gt_ref.at[g], m_ref.at[g], reverse) for g in rows]
    outs = [[] for _ in rows]
    for hd in range(C_HEADS):
        weights = [_mlstm_weights(hd, qks[g], gates[g], reverse) for g in rows]
        for g in rows:
            h_t = _mlstm_apply(hd, qks[g], v_ref.at[g], gates[g], weights[g], cst_ref.at[g], reverse)
            if final:
                outs[g].append((h_t + hb_ref[g, hd * C_V:(hd + 1) * C_V, :]).T)
            else:
                out_ref[g, hd * C_V:(hd + 1) * C_V, :] = h_t
    if final:
        for g in rows:
            _mlstm_finish(jnp.concatenate(outs[g], axis=1), og_ref.at[g], z_ref.at[g],
                          ng_ref.at[layer:layer + 1], out_ref.at[g])


def _mlstm_gates(gt_ref, m_ref, reverse):
    n_rows = gt_ref.shape[1]
    n_dir = 2 * C_HEADS

    li = gt_ref[0:n_dir, :]
    b = gt_ref[n_dir:, :]
    b_last = b[:, 0:1] if reverse else b[:, n_rows - 1:n_rows]
    m_prev = m_ref[...]
    d_inter = b + m_prev
    a = (b_last - b) + li
    m_new = jnp.maximum(b_last + m_prev, jnp.max(a, axis=1, keepdims=True))
    ws = jnp.exp(a - m_new)
    decay = jnp.exp((b_last + m_prev) - m_new)
    m_ref[...] = m_new

    one8 = jnp.ones((n_dir, n_rows), F32)
    lhs = jnp.concatenate(list(_split3(li - b)) + [one8, one8, one8], axis=0).T.astype(BF16)
    rhs_all = jnp.concatenate([one8, one8, one8] + list(_split3(b)), axis=0)
    return lhs, rhs_all, d_inter, ws, decay


def _mlstm_weights(hd, qk, gates, reverse):
    lhs, rhs_all, d_inter, _, _ = gates
    k, q_t = qk
    n_rows = k.shape[0]
    n_dir = 2 * C_HEADS
    gi = (C_HEADS if reverse else 0) + hd
    gate_row = lax.broadcasted_iota(jnp.int32, rhs_all.shape, 0) & (n_dir - 1)
    si = lax.broadcasted_iota(jnp.int32, (n_rows, n_rows), 0)
    ti = lax.broadcasted_iota(jnp.int32, (n_rows, n_rows), 1)
    causal = (si >= ti) if reverse else (si <= ti)

    sl = slice(hd * C_QK, (hd + 1) * C_QK)
    rhs = jnp.where(gate_row == gi, rhs_all, 0.0).astype(BF16)
    d = jnp.where(causal, _dot(lhs, rhs), NEG)
    d_int = d_inter[gi:gi + 1, :]
    m_t = jnp.maximum(jnp.max(d, axis=0, keepdims=True), d_int)
    w = (_dot(k[:, sl], q_t[sl, :]) * jnp.exp(d - m_t)).astype(BF16)
    return w, m_t, jnp.exp(d_int - m_t)


def _mlstm_apply(hd, qk, vt_ref, gates, weights, cst_ref, reverse):
    _, _, _, ws, decay = gates
    w, m_t, g = weights
    k, q_t = qk
    n_rows = k.shape[0]
    gi = (C_HEADS if reverse else 0) + hd
    sl = slice(hd * C_QK, (hd + 1) * C_QK)
    v_t = vt_ref[hd * C_V:(hd + 1) * C_V, :]
    v_ext = jnp.concatenate([v_t, jnp.ones((C_ONES, n_rows), BF16)], axis=0)
    ws_row = ws[gi:gi + 1, :]
    v_scaled = jnp.concatenate([(v_t.astype(F32) * ws_row).astype(BF16),
                                jnp.broadcast_to(ws_row, (C_ONES, n_rows)).astype(BF16)], axis=0)
    cst = cst_ref[hd]
    num = _dot(v_ext, w) + g * _dot(cst.astype(BF16), q_t[sl, :])
    den = jnp.maximum(jnp.abs(num[C_V:C_V + 1, :]), jnp.exp(-m_t))
    cst_ref[hd] = decay[gi:gi + 1, :C_QK] * cst + _dot(v_scaled, k[:, sl])
    return num[:C_V, :] / den


def _mlstm_finish(hsum, og_ref, z_ref, ng_ref, out_ref):
    hsum = hsum * _sigmoid(og_ref[...])
    ys = []
    for hd in range(C_HEADS):
        sl = slice(hd * C_V, (hd + 1) * C_V)
        hh = hsum[:, sl]
        mu = jnp.mean(hh, axis=-1, keepdims=True)
        var = jnp.mean(jnp.square(hh - mu), axis=-1, keepdims=True)
        ys.append((hh - mu) * lax.rsqrt(var + EPS) * ng_ref[:, sl])
    y = jnp.concatenate(ys, axis=1) * _silu(z_ref[...])
    out_ref[...] = y.astype(out_ref.dtype)


def _mlstm(u3, gt, vt, conv_w, layer, *, reverse, qk=None, hb=None, norm_g=None):
    batch, seq, _ = u3.shape
    n_chunks = seq // CHUNK
    final = hb is not None
    grp = REC_GROUP
    assert CHUNK == LANE

    def chunk(c, delta=0):
        return jnp.clip(_chunk_index(c, n_chunks, reverse) + delta, 0, n_chunks - 1)

    def ublk(blk, delta=0):
        return pl.BlockSpec((grp, CHUNK, D_GROUP), lambda bg, c: (bg, chunk(c, delta), blk))

    row_spec = ublk(0)
    t_spec = lambda ch: pl.BlockSpec((grp, ch, CHUNK), lambda bg, c: (bg, 0, chunk(c)))
    gate_specs = [t_spec(N_GATE)]
    gate_args = [gt]
    n_qk = C_HEADS * C_QK
    k_spec = pl.BlockSpec((grp, CHUNK, n_qk), lambda bg, c: (bg, chunk(c), 0))
    if final:
        in_specs = [k_spec, t_spec(n_qk), t_spec(D_GROUP)] + gate_specs + [
            ublk(BLK_C_O), ublk(BLK_C_Z), t_spec(D_GROUP),
            pl.BlockSpec(norm_g.shape, lambda bg, c: (0, 0))]
        args = [qk[0], qk[1], vt] + gate_args + [u3, u3, hb, norm_g]
        out_shape = jax.ShapeDtypeStruct((batch, seq, D_GROUP), BF16)
        out_specs = row_spec
    else:
        in_specs = [ublk(BLK_C_QK, -1), ublk(BLK_C_QK), ublk(BLK_C_QK, 1),
                    pl.BlockSpec(conv_w.shape, lambda bg, c: (0, 0, 0)),
                    t_spec(D_GROUP)] + gate_specs
        args = [u3, u3, u3, conv_w, vt] + gate_args
        out_shape = (jax.ShapeDtypeStruct((batch, D_GROUP, seq), F32),
                     jax.ShapeDtypeStruct((batch, seq, n_qk), BF16),
                     jax.ShapeDtypeStruct((batch, n_qk, seq), BF16))
        out_specs = (t_spec(D_GROUP), k_spec, t_spec(n_qk))
    return pl.pallas_call(
        functools.partial(_mlstm_kernel, layer=layer, reverse=reverse, final=final),
        out_shape=out_shape,
        grid=(batch // grp, n_chunks),
        in_specs=in_specs,
        out_specs=out_specs,
        scratch_shapes=[pltpu.VMEM((grp, C_HEADS, C_V + C_ONES, C_QK), F32),
                        pltpu.VMEM((grp, 2 * C_HEADS, CHUNK), F32)],
        compiler_params=pltpu.CompilerParams(
            dimension_semantics=("parallel", "arbitrary"), vmem_limit_bytes=VMEM_LIMIT),
        name="mlstm_fwd" if final else "mlstm_bwd",
    )(*args)


def _rope_table_kernel(pos_ref, invf_ref, cos_ref, sin_ref):
    ang = invf_ref[...] * pos_ref[...].astype(F32)
    n_rows = ang.shape[1]
    half = ROPE_DIM // 2
    r = lax.broadcasted_iota(jnp.int32, ang.shape, 0)
    sn = jnp.sin(ang)
    sn = jnp.where(r < half, -sn, sn)
    pad = jnp.zeros((LANE - ROPE_DIM, n_rows), F32)

    def expand(x):
        t = jnp.concatenate([x, pad], axis=0).T
        return t + pltpu.roll(t, B_HD, axis=1)

    lane = lax.broadcasted_iota(jnp.int32, (n_rows, LANE), 1) & (B_HD - 1)
    cos_ref[...] = jnp.where(lane < ROPE_DIM, expand(jnp.cos(ang)), 1.0)
    sin_ref[...] = expand(sn)


def _rope_tables(positions):
    rows = positions.size
    half = ROPE_DIM // 2
    inv_freq = ROPE_THETA ** (-jnp.arange(0, ROPE_DIM, 2, dtype=F32) / ROPE_DIM)
    invf = jnp.concatenate([inv_freq, inv_freq]).reshape(ROPE_DIM, 1)
    tm = ROW_TILE
    return pl.pallas_call(
        _rope_table_kernel,
        out_shape=(jax.ShapeDtypeStruct((rows, LANE), F32),) * 2,
        grid=(rows // tm,),
        in_specs=[pl.BlockSpec((1, tm), lambda i: (0, i)),
                  pl.BlockSpec((ROPE_DIM, 1), lambda i: (0, 0))],
        out_specs=(pl.BlockSpec((tm, LANE), lambda i: (i, 0)),) * 2,
        compiler_params=pltpu.CompilerParams(dimension_semantics=("parallel",)),
        name="rope_tables",
    )(positions.reshape(1, rows), invf)


def _rope(x, cos, sin):
    width = x.shape[1]
    if width > LANE:
        cos = jnp.concatenate([cos] * (width // LANE), axis=1)
        sin = jnp.concatenate([sin] * (width // LANE), axis=1)
    half = ROPE_DIM // 2
    r = lax.broadcasted_iota(jnp.int32, x.shape, 1) & (B_HD - 1)
    partner = jnp.where(r < half, pltpu.roll(x, width - half, axis=1), pltpu.roll(x, half, axis=1))
    return x * cos + partner * sin


def _swa_kernel(q_ref, z_ref, mp_ref, mc_ref, mn_ref, cp_ref, cc_ref, cn_ref,
                sp_ref, sc_ref, sn_ref, sink_ref, out_ref, *, layer):
    tile = q_ref.shape[0]
    blk = mp_ref.shape[0]
    n_sub = tile // blk
    i = pl.program_id(1)
    n_tiles = pl.num_programs(1)
    kv_w = B_KV_HEADS * B_HD
    grp = B_HEADS // B_KV_HEADS

    q = (_rope(q_ref[...], cc_ref[...], sc_ref[...]) * (B_HD ** -0.5)).astype(BF16)
    k = jnp.concatenate([
        _rope(mp_ref[:, MISC_K:MISC_K + kv_w], cp_ref[...], sp_ref[...]),
        _rope(mc_ref[:, MISC_K:MISC_K + kv_w], cc_ref[...], sc_ref[...]),
        _rope(mn_ref[:, MISC_K:MISC_K + kv_w], cn_ref[...], sn_ref[...])], axis=0).astype(BF16)
    v = jnp.concatenate([mp_ref[:, MISC_V:MISC_V + kv_w], mc_ref[:, MISC_V:MISC_V + kv_w],
                         mn_ref[:, MISC_V:MISC_V + kv_w]], axis=0)
    n_keys = v.shape[0]

    kr = lax.broadcasted_iota(jnp.int32, (3 * blk, blk), 0)
    qc = lax.broadcasted_iota(jnp.int32, (3 * blk, blk), 1)
    in_window = jnp.abs(qc - (kr - blk)) <= WINDOW
    biases = []
    for j in range(n_sub):
        valid = in_window
        if j == 0:
            valid = valid & (kr >= jnp.where(i > 0, 0, blk))
        if j == n_sub - 1:
            valid = valid & (kr < jnp.where(i < n_tiles - 1, 3 * blk, 2 * blk))
        biases.append(jnp.where(valid, 0.0, NEG))

    ones = jnp.ones((n_keys, B_HD), F32)
    vts = [jnp.concatenate([v[:, kvh * B_HD:(kvh + 1) * B_HD], ones], axis=1).T.astype(BF16)
           for kvh in range(B_KV_HEADS)]

    tiles = [[] for _ in range(n_sub)]
    pairs = [[] for _ in range(n_sub)]
    for hq in range(B_HEADS):
        kvh = hq // grp
        ksl = slice(kvh * B_HD, (kvh + 1) * B_HD)
        sk = sink_ref[layer:layer + 1, hq:hq + 1]
        for j in range(n_sub):
            keys = slice(j * blk, (j + 3) * blk)
            qh = q[j * blk:(j + 1) * blk, hq * B_HD:(hq + 1) * B_HD]
            s = _dot_nt(k[keys, ksl], qh)
            s = jnp.concatenate([s[:blk] + biases[j][:blk], s[blk:2 * blk],
                                 s[2 * blk:] + biases[j][2 * blk:]], axis=0)
            m = jnp.maximum(jnp.max(s, axis=0, keepdims=True), sk)
            p = jnp.exp(s - m).astype(BF16)
            o = _dot(vts[kvh][:, keys], p)
            denom = o[B_HD:B_HD + 1, :] + jnp.exp(sk - m)
            pairs[j].append(o[:B_HD, :] / denom)
            if len(pairs[j]) == LANE // B_HD:
                tiles[j].append(jnp.concatenate(pairs[j], axis=0).T)
                pairs[j] = []
    for j in range(n_sub):
        rows = slice(j * blk, (j + 1) * blk)
        y = jnp.concatenate(tiles[j], axis=1) * _silu(z_ref[rows, :])
        out_ref[rows, :] = y.astype(out_ref.dtype)


def _swa(u, cos_t, sin_t, sink, layer, batch):
    rows = u.shape[0]
    n_blocks = rows // batch // CHUNK
    n_sub = SWA_TILE // CHUNK
    n_tiles = n_blocks // n_sub

    def tile_spec(width, blk):
        return pl.BlockSpec((SWA_TILE, width), lambda bi, i: (bi * n_tiles + i, blk))

    def halo_spec(width, blk, after):
        def index(bi, i):
            nb = (i + 1) * n_sub if after else i * n_sub - 1
            return (bi * n_blocks + jnp.clip(nb, 0, n_blocks - 1), blk)
        return pl.BlockSpec((CHUNK, width), index)

    def table_specs():
        return [halo_spec(LANE, 0, False), tile_spec(LANE, 0), halo_spec(LANE, 0, True)]

    return pl.pallas_call(
        functools.partial(_swa_kernel, layer=layer),
        out_shape=jax.ShapeDtypeStruct((rows, D_GROUP), BF16),
        grid=(batch, n_tiles),
        in_specs=[tile_spec(D_GROUP, BLK_B_Q), tile_spec(D_GROUP, BLK_B_Z),
                  halo_spec(D_GROUP, BLK_MISC, False), tile_spec(D_GROUP, BLK_MISC),
                  halo_spec(D_GROUP, BLK_MISC, True)] + table_specs() + table_specs() + [
                  pl.BlockSpec(sink.shape, lambda bi, i: (0, 0))],
        out_specs=tile_spec(D_GROUP, 0),
        compiler_params=pltpu.CompilerParams(
            dimension_semantics=("parallel", "parallel"), vmem_limit_bytes=VMEM_LIMIT),
        name="window_attn",
    )(u, u, u, u, u, cos_t, cos_t, cos_t, sin_t, sin_t, sin_t, sink)


def _mem_kv_kernel(mem_ref, g_ref, w_ref, kt_ref, v_ref, *, layer):
    x = mem_ref[...]
    ms = jnp.mean(x * x, axis=-1, keepdims=True)
    h = (x * lax.rsqrt(ms + EPS) * g_ref[layer:layer + 1, :]).astype(BF16)
    kv = _dot(h, w_ref[...].astype(BF16))
    kt_ref[...] = kv[:, :D_GROUP].T.astype(kt_ref.dtype)
    v_ref[...] = kv[:, D_GROUP:].astype(v_ref.dtype)


def _mem_kv(mem, g_all, w_all, layer):
    batch, m_len, d = mem.shape
    n_out = w_all.shape[2]
    return pl.pallas_call(
        functools.partial(_mem_kv_kernel, layer=layer),
        out_shape=(jax.ShapeDtypeStruct((batch, D_GROUP, m_len), BF16),
                   jax.ShapeDtypeStruct((batch, m_len, D_GROUP), BF16)),
        grid=(batch,),
        in_specs=[pl.BlockSpec((None, m_len, d), lambda bi: (bi, 0, 0)),
                  pl.BlockSpec(g_all.shape, lambda bi: (0, 0)),
                  pl.BlockSpec((None, d, n_out), lambda bi: (layer, 0, 0))],
        out_specs=(pl.BlockSpec((None, D_GROUP, m_len), lambda bi: (bi, 0, 0)),
                   pl.BlockSpec((None, m_len, D_GROUP), lambda bi: (bi, 0, 0))),
        compiler_params=pltpu.CompilerParams(
            dimension_semantics=("parallel",), vmem_limit_bytes=VMEM_LIMIT),
        name="mem_kv",
    )(mem, g_all, w_all)


def _mem_attn_kernel(q_ref, z_ref, kt_ref, v_ref, out_ref):
    q = q_ref[...].astype(BF16)
    outs = []
    for hd in range(D_HEADS):
        sl = slice(hd * D_HD, (hd + 1) * D_HD)
        s = _dot(q[:, sl], kt_ref[sl, :])
        m = jnp.max(s, axis=-1, keepdims=True)
        p = jnp.exp2((s - m) * (D_HD ** -0.5 * LOG2_E))
        den = jnp.sum(p, axis=-1, keepdims=True)
        outs.append(_dot(p.astype(BF16), v_ref[:, sl]) / den)
    y = jnp.concatenate(outs, axis=1) * _silu(z_ref[...])
    out_ref[...] = y.astype(out_ref.dtype)


def _mem_attn(u, k, vt, batch):
    rows = u.shape[0]
    tm = ROW_TILE
    per_batch = rows // batch // tm
    return pl.pallas_call(
        _mem_attn_kernel,
        out_shape=jax.ShapeDtypeStruct((rows, D_GROUP), BF16),
        grid=(batch, per_batch),
        in_specs=[pl.BlockSpec((tm, D_GROUP), lambda bi, i: (bi * per_batch + i, BLK_D_Q)),
                  pl.BlockSpec((tm, D_GROUP), lambda bi, i: (bi * per_batch + i, BLK_D_Z)),
                  pl.BlockSpec((None,) + k.shape[1:], lambda bi, i: (bi, 0, 0)),
                  pl.BlockSpec((None,) + vt.shape[1:], lambda bi, i: (bi, 0, 0))],
        out_specs=pl.BlockSpec((tm, D_GROUP), lambda bi, i: (bi * per_batch + i, 0)),
        compiler_params=pltpu.CompilerParams(
            dimension_semantics=("parallel", "parallel"), vmem_limit_bytes=VMEM_LIMIT),
        name="mem_attn",
    )(u, u, k, vt)


def kernel(x, mem, positions, norm_g, w_in, hgrn_lb_logits, hgrn_norm_g, attn_sink, mlstm_conv_w,
           mlstm_gate_b, mlstm_norm_g, mem_norm_g, w_mem_kv, w_out, final_norm_g):
    batch, seq, d = x.shape
    depth = w_in.shape[0]
    assert all(seq % t == 0 for t in (CHUNK, ROW_TILE, IN_ROW_TILE, SWA_TILE))
    assert (batch * seq) % OUT_ROW_TILE == 0
    assert batch % REC_GROUP == 0
    xf = x.reshape(batch * seq, d)
    cos_t, sin_t = _rope_tables(positions)
    w_cat_all, w_gt_all = _prep_w_in(w_in)
    gate_b = mlstm_gate_b.reshape(depth, N_GATE, 1)
    for layer in range(depth):
        u, gt, vt = _in_proj(xf, norm_g, w_cat_all, w_gt_all, gate_b, layer, batch)
        u3 = u.reshape(batch, seq, u.shape[1])
        ob = _hgrn(u3, hgrn_lb_logits, layer, reverse=True)
        ya = _hgrn(u3, hgrn_lb_logits, layer, reverse=False, ob=ob, norm_g=hgrn_norm_g)
        yb = _swa(u, cos_t, sin_t, attn_sink, layer, batch)
        hb, k_c, qt_c = _mlstm(u3, gt, vt, mlstm_conv_w, layer, reverse=True)
        yc = _mlstm(u3, gt, vt, None, layer, reverse=False, qk=(k_c, qt_c), hb=hb,
                    norm_g=mlstm_norm_g)
        mem_k, mem_vt = _mem_kv(mem, mem_norm_g, w_mem_kv, layer)
        yd = _mem_attn(u, mem_k, mem_vt, batch)
        xf = _out_proj(ya.reshape(batch * seq, D_GROUP), yb, yc.reshape(batch * seq, D_GROUP), yd,
                       w_out, xf, final_norm_g, layer, final=(layer == depth - 1))
    return xf.reshape(batch, seq, d)
```

```python
import functools

import jax
import jax.numpy as jnp
import numpy as np
from jax import lax
from jax.experimental import pallas as pl
from jax.experimental.pallas import tpu as pltpu

F32 = jnp.float32
BF16 = jnp.bfloat16

D_GROUP = 512
A_HEADS, A_HD = 4, 128
B_HEADS, B_KV_HEADS, B_HD = 8, 2, 64
WINDOW = 128
ROPE_THETA = 500000.0
ROPE_DIM = 16
C_HEADS, C_QK, C_V = 4, 64, 128
CONV_W = 5
D_HEADS, D_HD = 4, 128
EPS = 1e-6
IN_SIZES = (
    D_GROUP, D_GROUP, D_GROUP, D_GROUP, D_GROUP,
    B_HEADS * B_HD, B_KV_HEADS * B_HD, B_KV_HEADS * B_HD, D_GROUP,
    C_HEADS * C_QK, C_HEADS * C_QK, C_HEADS * C_V, D_GROUP, D_GROUP,
    2 * C_HEADS, 2 * C_HEADS,
    D_GROUP, D_GROUP,
)
SPLIT_POINTS = tuple(int(s) for s in np.cumsum(IN_SIZES)[:-1])

LANE = 128
SUBLANE = 8
VMEM_LIMIT = 48 * 1024 * 1024

CHUNK = 128
ROW_TILE = 512
IN_ROW_TILE = 512
IN_PROJ_VMEM_LIMIT = 56 * 1024 * 1024
W_PREP_TILE = 256
OUT_ROW_TILE = 1024
HGRN_SUB = 64
HGRN_SAFE_LOG2 = 100.0
SWA_TILE = 512
REC_GROUP = 4
NEG = -1e30
LOG2_E = 1.4426950408889634

(BLK_A_Q, BLK_A_FF, BLK_A_FB, BLK_A_Z, BLK_B_Q, BLK_B_Z, BLK_C_QK, BLK_C_O,
 BLK_C_Z, BLK_D_Z, BLK_MISC) = range(11)
N_BLK = 11
BLK16_A_I, BLK16_D_Q = range(2)
N_BLK16 = 2
W_ROW_C_V = (N_BLK + N_BLK16) * D_GROUP
N_GATE = 4 * C_HEADS
W_ROW_GATE = W_ROW_C_V + D_GROUP
N_W_ROWS = W_ROW_GATE + N_GATE
MISC_K, MISC_V = 0, 128
C_ONES = 16


def _dot(a, b):
    return jnp.dot(a, b, preferred_element_type=F32)


def _dot_nt(a, b):
    return lax.dot_general(a, b, (((1,), (1,)), ((), ())), preferred_element_type=F32)


def _dot_tn(a, b):
    return lax.dot_general(a, b, (((0,), (0,)), ((), ())), preferred_element_type=F32)


def _sigmoid(x):
    return 1.0 / (1.0 + jnp.exp(-x))


def _silu(x):
    return x * _sigmoid(x)


def _log_sigmoid(x):
    return jnp.minimum(x, 0.0) - jnp.log(1.0 + jnp.exp(-jnp.abs(x)))


def _chunk_cumsum_lanes(x, reverse):
    n = x.shape[1]
    pos = lax.broadcasted_iota(jnp.int32, x.shape, 1) & (CHUNK - 1)
    s = 1
    while s < CHUNK:
        if reverse:
            x = x + jnp.where(pos < CHUNK - s, pltpu.roll(x, n - s, axis=1), 0.0)
        else:
            x = x + jnp.where(pos >= s, pltpu.roll(x, s, axis=1), 0.0)
        s *= 2
    return x


def _in_proj_kernel(x_ref, g_ref, w_ref, gb_ref, u_ref, u16_ref, gt_ref, vt_ref, *, layer):
    x = x_ref[...]
    ms = jnp.mean(x * x, axis=-1, keepdims=True)
    h = (x * lax.rsqrt(ms + EPS) * g_ref[layer:layer + 1, :]).astype(BF16)
    vg = _dot_nt(w_ref[W_ROW_C_V:W_ROW_GATE + N_GATE, :], h)
    vt_ref[...] = vg[:D_GROUP, :].astype(BF16)
    gates = vg[D_GROUP:, :] + gb_ref[layer]
    n_dir = 2 * C_HEADS
    log_f = _log_sigmoid(gates[n_dir:, :])
    is_fwd = lax.broadcasted_iota(jnp.int32, log_f.shape, 0) < C_HEADS
    cum = jnp.where(is_fwd, _chunk_cumsum_lanes(log_f, False), _chunk_cumsum_lanes(log_f, True))
    gt_ref[...] = jnp.concatenate([gates[:n_dir, :], cum], axis=0)
    for j in range(N_BLK):
        u_ref[:, j * D_GROUP:(j + 1) * D_GROUP] = _dot_nt(h, w_ref[j * D_GROUP:(j + 1) * D_GROUP, :])
    for j in range(N_BLK16):
        rows = slice((N_BLK + j) * D_GROUP, (N_BLK + j + 1) * D_GROUP)
        u16_ref[:, j * D_GROUP:(j + 1) * D_GROUP] = _dot_nt(h, w_ref[rows, :]).astype(BF16)


def _in_proj(xf, g_all, w_all, gate_b, layer, batch):
    rows, d = xf.shape
    tm = IN_ROW_TILE
    per_batch = rows // batch // tm
    return pl.pallas_call(
        functools.partial(_in_proj_kernel, layer=layer),
        out_shape=(jax.ShapeDtypeStruct((rows, N_BLK * D_GROUP), F32),
                   jax.ShapeDtypeStruct((rows, N_BLK16 * D_GROUP), BF16),
                   jax.ShapeDtypeStruct((batch, N_GATE, rows // batch), F32),
                   jax.ShapeDtypeStruct((batch, D_GROUP, rows // batch), BF16)),
        grid=(rows // tm,),
        in_specs=[
            pl.BlockSpec((tm, d), lambda i: (i, 0)),
            pl.BlockSpec(g_all.shape, lambda i: (0, 0)),
            pl.BlockSpec((None, N_W_ROWS, d), lambda i: (layer, 0, 0), pipeline_mode=pl.Buffered(1)),
            pl.BlockSpec(gate_b.shape, lambda i: (0, 0, 0)),
        ],
        out_specs=(
            pl.BlockSpec((tm, N_BLK * D_GROUP), lambda i: (i, 0)),
            pl.BlockSpec((tm, N_BLK16 * D_GROUP), lambda i: (i, 0)),
            pl.BlockSpec((None, N_GATE, tm), lambda i: (i // per_batch, 0, i % per_batch)),
            pl.BlockSpec((None, D_GROUP, tm), lambda i: (i // per_batch, 0, i % per_batch)),
        ),
        compiler_params=pltpu.CompilerParams(
            dimension_semantics=("parallel",), vmem_limit_bytes=IN_PROJ_VMEM_LIMIT),
        name="in_proj",
    )(xf, g_all, w_all, gate_b)


def _w_in_segments():
    names = ("a_q", "a_i", "a_ff", "a_fb", "a_z", "b_q", "b_k", "b_v", "b_z",
             "c_q", "c_k", "c_v", "c_o", "c_z", "c_ig", "c_fg", "d_q", "d_z")
    src = dict(zip(names, zip((0,) + SPLIT_POINTS, IN_SIZES)))
    blocks = {BLK_A_Q: ("a_q",), BLK_A_FF: ("a_ff",), BLK_A_FB: ("a_fb",), BLK_A_Z: ("a_z",),
              BLK_B_Q: ("b_q",), BLK_B_Z: ("b_z",), BLK_C_QK: ("c_q", "c_k"), BLK_C_O: ("c_o",),
              BLK_C_Z: ("c_z",), BLK_D_Z: ("d_z",), BLK_MISC: ("b_k", "b_v"),
              N_BLK + BLK16_A_I: ("a_i",), N_BLK + BLK16_D_Q: ("d_q",)}
    placed = []
    for blk, group in sorted(blocks.items()):
        dst = blk * D_GROUP
        for name in group:
            placed.append((name, dst))
            dst += src[name][1]
    placed += [("c_v", W_ROW_C_V), ("c_ig", W_ROW_GATE), ("c_fg", W_ROW_GATE + src["c_ig"][1])]
    segs, gaps, end = [], [], 0
    for name, dst in placed:
        off, width = src[name]
        if dst > end:
            gaps.append((end, dst))
        if segs and segs[-1][0] + segs[-1][1] == off and segs[-1][2] + segs[-1][1] == dst:
            segs[-1] = (segs[-1][0], segs[-1][1] + width, segs[-1][2])
        else:
            segs.append((off, width, dst))
        end = dst + width
    assert end == N_W_ROWS
    return tuple(segs), tuple(gaps)


def _w_prep_kernel(wt_ref, wc_ref):
    segs, gaps = _w_in_segments()
    for off, width, dst in segs:
        wc_ref[dst:dst + width, :] = wt_ref[off:off + width, :].astype(BF16)
    for lo, hi in gaps:
        wc_ref[lo:hi, :] = jnp.zeros((hi - lo, wc_ref.shape[1]), BF16)


def _prep_w_in(w):
    depth, d, d_in = w.shape
    wt = jnp.swapaxes(w, 1, 2)
    tk = W_PREP_TILE
    return pl.pallas_call(
        _w_prep_kernel,
        out_shape=jax.ShapeDtypeStruct((depth, N_W_ROWS, d), BF16),
        grid=(depth, d // tk),
        in_specs=[pl.BlockSpec((None, d_in, tk), lambda l, i: (l, 0, i))],
        out_specs=pl.BlockSpec((None, N_W_ROWS, tk), lambda l, i: (l, 0, i)),
        compiler_params=pltpu.CompilerParams(
            dimension_semantics=("parallel", "parallel"), vmem_limit_bytes=VMEM_LIMIT),
        name="w_prep",
    )(wt)


def _out_proj_kernel(ya_ref, yb_ref, yc_ref, yd_ref, w_ref, x_ref, g_ref, o_ref, *, final):
    acc = x_ref[...]
    for i, y_ref in enumerate((ya_ref, yb_ref, yc_ref, yd_ref)):
        acc = acc + _dot(y_ref[...], w_ref[i * D_GROUP:(i + 1) * D_GROUP, :].astype(BF16))
    if final:
        ms = jnp.mean(acc * acc, axis=-1, keepdims=True)
        acc = acc * lax.rsqrt(ms + EPS) * g_ref[...]
    o_ref[...] = acc


def _out_proj(ya, yb, yc, yd, w_all, xf, g, layer, final):
    rows, d = xf.shape
    tm = OUT_ROW_TILE
    yspec = pl.BlockSpec((tm, D_GROUP), lambda i: (i, 0))
    return pl.pallas_call(
        functools.partial(_out_proj_kernel, final=final),
        out_shape=jax.ShapeDtypeStruct((rows, d), F32),
        grid=(rows // tm,),
        in_specs=[yspec, yspec, yspec, yspec,
                  pl.BlockSpec((None, 4 * D_GROUP, d), lambda i: (layer, 0, 0)),
                  pl.BlockSpec((tm, d), lambda i: (i, 0)),
                  pl.BlockSpec((1, d), lambda i: (0, 0))],
        out_specs=pl.BlockSpec((tm, d), lambda i: (i, 0)),
        compiler_params=pltpu.CompilerParams(
            dimension_semantics=("parallel",), vmem_limit_bytes=VMEM_LIMIT),
        name="out_proj",
    )(ya, yb, yc, yd, w_all, xf, g.reshape(1, d))


def _level_ref(b, h, reverse):
    n_rows, n = b.shape
    blk = 2 * h
    r0 = h if reverse else h - 1
    if blk >= SUBLANE:
        pieces = [jnp.broadcast_to(b[s + r0:s + r0 + 1, :], (blk, n)) for s in range(0, n_rows, blk)]
        return pieces[0] if len(pieces) == 1 else jnp.concatenate(pieces, axis=0)
    res = lax.broadcasted_iota(jnp.int32, b.shape, 0) & (blk - 1)
    out = None
    for m in range(blk):
        shift = (m - r0) % n_rows
        cand = b if shift == 0 else pltpu.roll(b, shift, axis=0)
        out = cand if out is None else jnp.where(res == m, cand, out)
    return out


def _hgrn_lower_bound(lbl_ref, layer, depth):
    rows = [lbl_ref[j:j + 1, :] for j in range(depth)]
    mx = functools.reduce(jnp.maximum, rows)
    es = [jnp.exp(r - mx) for r in rows]
    tot = functools.reduce(lambda a, c: a + c, es)
    lb = jnp.zeros_like(rows[0])
    for j in range(1, layer + 1):
        lb = lb + es[j] / tot
    return lb


def _cumsum_rows_mxu(x, reverse):
    n = x.shape[0]
    r = lax.broadcasted_iota(jnp.int32, (n, n), 0)
    c = lax.broadcasted_iota(jnp.int32, (n, n), 1)
    tri = jnp.where((c >= r) if reverse else (c <= r), 1.0, 0.0).astype(BF16)
    hi = x.astype(BF16)
    r1 = x - hi.astype(F32)
    mid = r1.astype(BF16)
    lo = (r1 - mid.astype(F32)).astype(BF16)
    return _dot(jnp.concatenate([tri, tri, tri], axis=1), jnp.concatenate([hi, mid, lo], axis=0))


def _hgrn_level_operand(q, k, f, b2, h, reverse):
    n_rows = q.shape[0]
    blk = 2 * h
    r0 = h if reverse else h - 1
    if h >= SUBLANE:
        pieces = []
        for s0 in range(0, n_rows, blk):
            ref = b2[s0 + r0:s0 + r0 + 1, :]
            lo, up = slice(s0, s0 + h), slice(s0 + h, s0 + blk)
            if reverse:
                pieces += [q[lo] * jnp.exp2(b2[lo] - ref), k[up] * jnp.exp2(ref - b2[up])]
            else:
                pieces += [k[lo] * jnp.exp2(ref - b2[lo]), q[up] * jnp.exp2(b2[up] - ref)]
        return jnp.concatenate(pieces, axis=0)
    row = lax.broadcasted_iota(jnp.int32, q.shape, 0)
    is_q = ((row & h) == 0) if reverse else ((row & h) != 0)
    qk = jnp.where(is_q, q, k)
    if h == 1:
        return jnp.where(is_q, q * f, k)
    if h == 2:
        res = row & 3
        f_dn = pltpu.roll(f, 1, axis=0)
        f_up = pltpu.roll(f, n_rows - 1, axis=0)
        if reverse:
            e = jnp.where(res == 0, f * f_up, jnp.where(res == 1, f, jnp.where(res == 2, 1.0, f_dn)))
        else:
            e = jnp.where(res == 0, f_up, jnp.where(res == 1, 1.0, jnp.where(res == 2, f, f_dn * f)))
        return qk * e
    return qk * jnp.exp2(-jnp.abs(b2 - _level_ref(b2, h, reverse)))


def _hgrn_kernel(*refs, layer, depth, reverse, final):
    if final:
        q_ref, v_ref, zg_ref, lbl_ref, z_ref, ob_ref, ng_ref, out_ref, st_ref = refs
    else:
        q_ref, v_ref, zg_ref, lbl_ref, out_ref, st_ref = refs

    @pl.when(pl.program_id(1) == 0)
    def _():
        st_ref[...] = jnp.zeros_like(st_ref)

    rows = range(q_ref.shape[0])
    n_rows = q_ref.shape[1]
    ti = lax.broadcasted_iota(jnp.int32, (n_rows, n_rows), 0)
    si = lax.broadcasted_iota(jnp.int32, (n_rows, n_rows), 1)
    ahead = (ti < si) if reverse else (ti > si)
    level = jnp.where(ahead, 31 - lax.clz(ti ^ si), jnp.where(ti == si, -1, -2))
    preps = [_hgrn_prep(q_ref.at[g], zg_ref.at[g], lbl_ref, layer, depth, reverse) for g in rows]

    def run(fast):
        outs = [[] for _ in rows]
        for hd in range(A_HEADS):
            sl = slice(hd * A_HD, (hd + 1) * A_HD)
            heads = [tuple(a[:, sl] for a in preps[g]) for g in rows]
            if fast:
                atts = [_hgrn_scores_fast(heads[g], level, reverse) for g in rows]
            else:
                atts = [_hgrn_scores(heads[g], level, n_rows, None, reverse) for g in rows]
                h = n_rows // 2
                while h >= 1:
                    atts = [_hgrn_scores(heads[g], level, h, atts[g], reverse) for g in rows]
                    h //= 2
            for g in rows:
                outs[g].append(_hgrn_apply(heads[g], atts[g], v_ref.at[g, :, sl],
                                           st_ref.at[g, hd], reverse))
        for g in rows:
            o = jnp.concatenate(outs[g], axis=1)
            if final:
                _hgrn_finish(o, z_ref.at[g], ob_ref.at[g], ng_ref.at[layer:layer + 1],
                             out_ref.at[g])
            else:
                out_ref[g] = o

    worst = None
    for g in rows:
        b2 = preps[g][3]
        for s0 in range(0, n_rows, HGRN_SUB):
            mid = b2[s0 + HGRN_SUB // 2:s0 + HGRN_SUB // 2 + 1, :]
            for edge in (s0, s0 + HGRN_SUB - 1):
                span = jnp.abs(b2[edge:edge + 1, :] - mid)
                worst = span if worst is None else jnp.maximum(worst, span)
    safe = jnp.max(worst) <= HGRN_SAFE_LOG2

    @pl.when(safe)
    def _():
        run(True)

    @pl.when(jnp.logical_not(safe))
    def _():
        run(False)


def _hgrn_prep(q_ref, zg_ref, lbl_ref, layer, depth, reverse):
    n_rows = q_ref.shape[0]
    q = q_ref[...]
    zg = zg_ref[...]
    e = jnp.exp(-jnp.abs(zg))
    r = 1.0 / (1.0 + e)
    er = e * r
    pos = zg >= 0.0
    f = jnp.where(pos, r, er)
    k = jnp.where(pos, er, r)
    if layer > 0:
        lb = _hgrn_lower_bound(lbl_ref, layer, depth)
        f = lb + (1.0 - lb) * f
        k = (1.0 - lb) * k
    b2 = _cumsum_rows_mxu(jnp.log2(f), reverse)
    return q, k, f, b2


def _hgrn_scores(head, level, h, att, reverse):
    q, k, f, b2 = head
    if att is None:
        return jnp.where(level == -1, _dot_nt(q.astype(BF16), k.astype(BF16)), 0.0)
    c = _hgrn_level_operand(q, k, f, b2, h, reverse).astype(BF16)
    return jnp.where(level == (h.bit_length() - 1), _dot_nt(c, c), att)


def _hgrn_scores_fast(head, level, reverse):
    q, k, _, b2 = head
    n_rows = q.shape[0]
    blocks = []
    for s0 in range(0, n_rows, HGRN_SUB):
        rows = slice(s0, s0 + HGRN_SUB)
        ref = b2[s0 + HGRN_SUB // 2:s0 + HGRN_SUB // 2 + 1, :]
        qs = (q[rows] * jnp.exp2(b2[rows] - ref)).astype(BF16)
        ks = (k * jnp.exp2(jnp.minimum(ref - b2, HGRN_SAFE_LOG2))).astype(BF16)
        blocks.append(_dot_nt(qs, ks))
    return jnp.where(level >= -1, jnp.concatenate(blocks, axis=0), 0.0)


def _hgrn_apply(head, att, v_ref, st_ref, reverse):
    q, k, _, b2 = head
    n_rows = q.shape[0]
    tot = b2[0:1, :] if reverse else b2[n_rows - 1:n_rows, :]
    v_bf = v_ref[...].astype(BF16)
    q_in = (q * jnp.exp2(b2)).astype(BF16)
    k_out = (k * jnp.exp2(tot - b2)).astype(BF16)
    st = st_ref[...]
    o = _dot(att.astype(BF16), v_bf) + _dot(q_in, st.astype(BF16))
    dec_col = jnp.broadcast_to(jnp.exp2(tot), (SUBLANE, tot.shape[1])).T[:, 0:1]
    st_ref[...] = dec_col * st + _dot_tn(k_out, v_bf)
    return o


def _hgrn_finish(o, z_ref, ob_ref, ng_ref, out_ref):
    o = o + ob_ref[...]
    ys = []
    for hd in range(A_HEADS):
        sl = slice(hd * A_HD, (hd + 1) * A_HD)
        oh = o[:, sl]
        ms = jnp.mean(oh * oh, axis=-1, keepdims=True)
        ys.append(oh * lax.rsqrt(ms + EPS) * ng_ref[:, sl])
    y = jnp.concatenate(ys, axis=1) * _silu(z_ref[...])
    out_ref[...] = y.astype(out_ref.dtype)


def _chunk_index(c, n_chunks, reverse):
    return n_chunks - 1 - c if reverse else c


def _hgrn(u3, u16, lb_logits, layer, *, reverse, ob=None, norm_g=None):
    batch, seq, _ = u3.shape
    n_chunks = seq // CHUNK
    depth = lb_logits.shape[0]
    final = ob is not None
    grp = REC_GROUP

    def ublk(blk):
        return pl.BlockSpec((grp, CHUNK, D_GROUP),
                            lambda bg, c: (bg, _chunk_index(c, n_chunks, reverse), blk))

    row_spec = ublk(0)
    in_specs = [ublk(BLK_A_Q), ublk(BLK16_A_I), ublk(BLK_A_FB if reverse else BLK_A_FF),
                pl.BlockSpec((depth, D_GROUP), lambda bg, c: (0, 0))]
    args = [u3, u16, u3, lb_logits]
    if final:
        in_specs += [ublk(BLK_A_Z), row_spec, pl.BlockSpec(norm_g.shape, lambda bg, c: (0, 0))]
        args += [u3, ob, norm_g]
    return pl.pallas_call(
        functools.partial(_hgrn_kernel, layer=layer, depth=depth, reverse=reverse, final=final),
        out_shape=jax.ShapeDtypeStruct((batch, seq, D_GROUP), BF16 if final else F32),
        grid=(batch // grp, n_chunks),
        in_specs=in_specs,
        out_specs=row_spec,
        scratch_shapes=[pltpu.VMEM((grp, A_HEADS, A_HD, A_HD), F32)],
        compiler_params=pltpu.CompilerParams(
            dimension_semantics=("parallel", "arbitrary"), vmem_limit_bytes=VMEM_LIMIT),
        name="hgrn_fwd" if final else "hgrn_bwd",
    )(*args)


def _split3(x):
    hi = x.astype(BF16).astype(F32)
    r1 = x - hi
    mid = r1.astype(BF16).astype(F32)
    lo = (r1 - mid).astype(BF16).astype(F32)
    return hi, mid, lo


def _conv_silu_qk(qkp_ref, qkc_ref, qkn_ref, cw_ref, reverse):
    n_rows = qkc_ref.shape[0]
    c = pl.program_id(1)
    n_chunks = pl.num_programs(1)
    cc = _chunk_index(c, n_chunks, reverse)
    halo = SUBLANE
    prev = qkp_ref[n_rows - halo:n_rows, :] * (cc > 0).astype(F32)
    nxt = qkn_ref[0:halo, :] * (cc < n_chunks - 1).astype(F32)
    xcat = jnp.concatenate([prev, qkc_ref[...], nxt], axis=0)
    acc = None
    for j in range(CONV_W):
        off = halo + j - CONV_W // 2
        term = cw_ref[j:j + 1, :] * xcat[off:off + n_rows, :]
        acc = term if acc is None else acc + term
    qk = _silu(acc)
    n_qk = C_HEADS * C_QK
    return qk[:, n_qk:].astype(BF16), (qk[:, :n_qk] * (C_QK ** -0.5)).T.astype(BF16)


def _mlstm_kernel(*refs, layer, reverse, final):
    if final:
        (k_ref, qt_ref, v_ref, gt_ref, og_ref, z_ref, hb_ref, ng_ref,
         out_ref, cst_ref, m_ref) = refs
    else:
        (qkp_ref, qkc_ref, qkn_ref, cw_ref, v_ref, gt_ref,
         out_ref, ko_ref, qto_ref, cst_ref, m_ref) = refs

    @pl.when(pl.program_id(1) == 0)
    def _():
        cst_ref[...] = jnp.zeros_like(cst_ref)
        m_ref[...] = jnp.zeros_like(m_ref)

    rows = range(v_ref.shape[0])
    if final:
        qks = [(k_ref[g], qt_ref[g]) for g in rows]
    else:
        qks = [_conv_silu_qk(qkp_ref.at[g], qkc_ref.at[g], qkn_ref.at[g], cw_ref.at[layer], reverse)
               for g in rows]
        for g in rows:
            ko_ref[g], qto_ref[g] = qks[g]
    gates = [_mlstm_gates(gt_ref.at[g], m_ref.at[g], reverse) for g in rows]
    outs = [[] for _ in rows]
    for hd in range(C_HEADS):
        weights = [_mlstm_weights(hd, qks[g], gates[g], reverse) for g in rows]
        for g in rows:
            h_t = _mlstm_apply(hd, qks[g], v_ref.at[g], gates[g], weights[g], cst_ref.at[g], reverse)
            if final:
                outs[g].append((h_t + hb_ref[g, hd * C_V:(hd + 1) * C_V, :]).T)
            else:
                out_ref[g, hd * C_V:(hd + 1) * C_V, :] = h_t
    if final:
        for g in rows:
            _mlstm_finish(jnp.concatenate(outs[g], axis=1), og_ref.at[g], z_ref.at[g],
                          ng_ref.at[layer:layer + 1], out_ref.at[g])


def _mlstm_gates(gt_ref, m_ref, reverse):
    n_rows = gt_ref.shape[1]
    n_dir = 2 * C_HEADS

    li = gt_ref[0:n_dir, :]
    b = gt_ref[n_dir:, :]
    b_last = b[:, 0:1] if reverse else b[:, n_rows - 1:n_rows]
    m_prev = m_ref[...]
    d_inter = b + m_prev
    a = (b_last - b) + li
    m_new = jnp.maximum(b_last + m_prev, jnp.max(a, axis=1, keepdims=True))
    ws = jnp.exp(a - m_new)
    decay = jnp.exp((b_last + m_prev) - m_new)
    m_ref[...] = m_new

    one8 = jnp.ones((n_dir, n_rows), F32)
    lhs = jnp.concatenate(list(_split3(li - b)) + [one8, one8, one8], axis=0).T.astype(BF16)
    rhs_all = jnp.concatenate([one8, one8, one8] + list(_split3(b)), axis=0)
    return lhs, rhs_all, d_inter, ws, decay


def _mlstm_weights(hd, qk, gates, reverse):
    lhs, rhs_all, d_inter, _, _ = gates
    k, q_t = qk
    n_rows = k.shape[0]
    n_dir = 2 * C_HEADS
    gi = (C_HEADS if reverse else 0) + hd
    gate_row = lax.broadcasted_iota(jnp.int32, rhs_all.shape, 0) & (n_dir - 1)
    si = lax.broadcasted_iota(jnp.int32, (n_rows, n_rows), 0)
    ti = lax.broadcasted_iota(jnp.int32, (n_rows, n_rows), 1)
    causal = (si >= ti) if reverse else (si <= ti)

    sl = slice(hd * C_QK, (hd + 1) * C_QK)
    rhs = jnp.where(gate_row == gi, rhs_all, 0.0).astype(BF16)
    d = jnp.where(causal, _dot(lhs, rhs), NEG)
    d_int = d_inter[gi:gi + 1, :]
    m_t = jnp.maximum(jnp.max(d, axis=0, keepdims=True), d_int)
    w = (_dot(k[:, sl], q_t[sl, :]) * jnp.exp(d - m_t)).astype(BF16)
    return w, m_t, jnp.exp(d_int - m_t)


def _mlstm_apply(hd, qk, vt_ref, gates, weights, cst_ref, reverse):
    _, _, _, ws, decay = gates
    w, m_t, g = weights
    k, q_t = qk
    n_rows = k.shape[0]
    gi = (C_HEADS if reverse else 0) + hd
    sl = slice(hd * C_QK, (hd + 1) * C_QK)
    v_t = vt_ref[hd * C_V:(hd + 1) * C_V, :]
    v_ext = jnp.concatenate([v_t, jnp.ones((C_ONES, n_rows), BF16)], axis=0)
    ws_row = ws[gi:gi + 1, :]
    v_scaled = jnp.concatenate([(v_t.astype(F32) * ws_row).astype(BF16),
                                jnp.broadcast_to(ws_row, (C_ONES, n_rows)).astype(BF16)], axis=0)
    cst = cst_ref[hd]
    num = _dot(v_ext, w) + g * _dot(cst.astype(BF16), q_t[sl, :])
    den = jnp.maximum(jnp.abs(num[C_V:C_V + 1, :]), jnp.exp(-m_t))
    cst_ref[hd] = decay[gi:gi + 1, :C_QK] * cst + _dot(v_scaled, k[:, sl])
    return num[:C_V, :] / den


def _mlstm_finish(hsum, og_ref, z_ref, ng_ref, out_ref):
    hsum = hsum * _sigmoid(og_ref[...])
    ys = []
    for hd in range(C_HEADS):
        sl = slice(hd * C_V, (hd + 1) * C_V)
        hh = hsum[:, sl]
        mu = jnp.mean(hh, axis=-1, keepdims=True)
        var = jnp.mean(jnp.square(hh - mu), axis=-1, keepdims=True)
        ys.append((hh - mu) * lax.rsqrt(var + EPS) * ng_ref[:, sl])
    y = jnp.concatenate(ys, axis=1) * _silu(z_ref[...])
    out_ref[...] = y.astype(out_ref.dtype)


def _mlstm(u3, gt, vt, conv_w, layer, *, reverse, qk=None, hb=None, norm_g=None):
    batch, seq, _ = u3.shape
    n_chunks = seq // CHUNK
    final = hb is not None
    grp = REC_GROUP
    assert CHUNK == LANE

    def chunk(c, delta=0):
        return jnp.clip(_chunk_index(c, n_chunks, reverse) + delta, 0, n_chunks - 1)

    def ublk(blk, delta=0):
        return pl.BlockSpec((grp, CHUNK, D_GROUP), lambda bg, c: (bg, chunk(c, delta), blk))

    row_spec = ublk(0)
    t_spec = lambda ch: pl.BlockSpec((grp, ch, CHUNK), lambda bg, c: (bg, 0, chunk(c)))
    gate_specs = [t_spec(N_GATE)]
    gate_args = [gt]
    n_qk = C_HEADS * C_QK
    k_spec = pl.BlockSpec((grp, CHUNK, n_qk), lambda bg, c: (bg, chunk(c), 0))
    if final:
        in_specs = [k_spec, t_spec(n_qk), t_spec(D_GROUP)] + gate_specs + [
            ublk(BLK_C_O), ublk(BLK_C_Z), t_spec(D_GROUP),
            pl.BlockSpec(norm_g.shape, lambda bg, c: (0, 0))]
        args = [qk[0], qk[1], vt] + gate_args + [u3, u3, hb, norm_g]
        out_shape = jax.ShapeDtypeStruct((batch, seq, D_GROUP), BF16)
        out_specs = row_spec
    else:
        in_specs = [ublk(BLK_C_QK, -1), ublk(BLK_C_QK), ublk(BLK_C_QK, 1),
                    pl.BlockSpec(conv_w.shape, lambda bg, c: (0, 0, 0)),
                    t_spec(D_GROUP)] + gate_specs
        args = [u3, u3, u3, conv_w, vt] + gate_args
        out_shape = (jax.ShapeDtypeStruct((batch, D_GROUP, seq), F32),
                     jax.ShapeDtypeStruct((batch, seq, n_qk), BF16),
                     jax.ShapeDtypeStruct((batch, n_qk, seq), BF16))
        out_specs = (t_spec(D_GROUP), k_spec, t_spec(n_qk))
    return pl.pallas_call(
        functools.partial(_mlstm_kernel, layer=layer, reverse=reverse, final=final),
        out_shape=out_shape,
        grid=(batch // grp, n_chunks),
        in_specs=in_specs,
        out_specs=out_specs,
        scratch_shapes=[pltpu.VMEM((grp, C_HEADS, C_V + C_ONES, C_QK), F32),
                        pltpu.VMEM((grp, 2 * C_HEADS, CHUNK), F32)],
        compiler_params=pltpu.CompilerParams(
            dimension_semantics=("parallel", "arbitrary"), vmem_limit_bytes=VMEM_LIMIT),
        name="mlstm_fwd" if final else "mlstm_bwd",
    )(*args)


def _rope_table_kernel(pos_ref, invf_ref, cos_ref, sin_ref):
    ang = invf_ref[...] * pos_ref[...].astype(F32)
    n_rows = ang.shape[1]
    half = ROPE_DIM // 2
    r = lax.broadcasted_iota(jnp.int32, ang.shape, 0)
    sn = jnp.sin(ang)
    sn = jnp.where(r < half, -sn, sn)
    pad = jnp.zeros((LANE - ROPE_DIM, n_rows), F32)

    def expand(x):
        t = jnp.concatenate([x, pad], axis=0).T
        return t + pltpu.roll(t, B_HD, axis=1)

    lane = lax.broadcasted_iota(jnp.int32, (n_rows, LANE), 1) & (B_HD - 1)
    cos_ref[...] = jnp.where(lane < ROPE_DIM, expand(jnp.cos(ang)), 1.0)
    sin_ref[...] = expand(sn)


def _rope_tables(positions):
    rows = positions.size
    half = ROPE_DIM // 2
    inv_freq = ROPE_THETA ** (-jnp.arange(0, ROPE_DIM, 2, dtype=F32) / ROPE_DIM)
    invf = jnp.concatenate([inv_freq, inv_freq]).reshape(ROPE_DIM, 1)
    tm = ROW_TILE
    return pl.pallas_call(
        _rope_table_kernel,
        out_shape=(jax.ShapeDtypeStruct((rows, LANE), F32),) * 2,
        grid=(rows // tm,),
        in_specs=[pl.BlockSpec((1, tm), lambda i: (0, i)),
                  pl.BlockSpec((ROPE_DIM, 1), lambda i: (0, 0))],
        out_specs=(pl.BlockSpec((tm, LANE), lambda i: (i, 0)),) * 2,
        compiler_params=pltpu.CompilerParams(dimension_semantics=("parallel",)),
        name="rope_tables",
    )(positions.reshape(1, rows), invf)


def _rope(x, cos, sin):
    width = x.shape[1]
    if width > LANE:
        cos = jnp.concatenate([cos] * (width // LANE), axis=1)
        sin = jnp.concatenate([sin] * (width // LANE), axis=1)
    half = ROPE_DIM // 2
    r = lax.broadcasted_iota(jnp.int32, x.shape, 1) & (B_HD - 1)
    partner = jnp.where(r < half, pltpu.roll(x, width - half, axis=1), pltpu.roll(x, half, axis=1))
    return x * cos + partner * sin


def _swa_kernel(q_ref, z_ref, mp_ref, mc_ref, mn_ref, cp_ref, cc_ref, cn_ref,
                sp_ref, sc_ref, sn_ref, sink_ref, out_ref, *, layer):
    tile = q_ref.shape[0]
    blk = mp_ref.shape[0]
    n_sub = tile // blk
    i = pl.program_id(1)
    n_tiles = pl.num_programs(1)
    kv_w = B_KV_HEADS * B_HD
    grp = B_HEADS // B_KV_HEADS

    q = (_rope(q_ref[...], cc_ref[...], sc_ref[...]) * (B_HD ** -0.5)).astype(BF16)
    k = jnp.concatenate([
        _rope(mp_ref[:, MISC_K:MISC_K + kv_w], cp_ref[...], sp_ref[...]),
        _rope(mc_ref[:, MISC_K:MISC_K + kv_w], cc_ref[...], sc_ref[...]),
        _rope(mn_ref[:, MISC_K:MISC_K + kv_w], cn_ref[...], sn_ref[...])], axis=0).astype(BF16)
    v = jnp.concatenate([mp_ref[:, MISC_V:MISC_V + kv_w], mc_ref[:, MISC_V:MISC_V + kv_w],
                         mn_ref[:, MISC_V:MISC_V + kv_w]], axis=0)
    n_keys = v.shape[0]

    kr = lax.broadcasted_iota(jnp.int32, (3 * blk, blk), 0)
    qc = lax.broadcasted_iota(jnp.int32, (3 * blk, blk), 1)
    in_window = jnp.abs(qc - (kr - blk)) <= WINDOW
    biases = []
    for j in range(n_sub):
        valid = in_window
        if j == 0:
            valid = valid & (kr >= jnp.where(i > 0, 0, blk))
        if j == n_sub - 1:
            valid = valid & (kr < jnp.where(i < n_tiles - 1, 3 * blk, 2 * blk))
        biases.append(jnp.where(valid, 0.0, NEG))

    ones = jnp.ones((n_keys, B_HD), F32)
    vts = [jnp.concatenate([v[:, kvh * B_HD:(kvh + 1) * B_HD], ones], axis=1).T.astype(BF16)
           for kvh in range(B_KV_HEADS)]

    tiles = [[] for _ in range(n_sub)]
    pairs = [[] for _ in range(n_sub)]
    for hq in range(B_HEADS):
        kvh = hq // grp
        ksl = slice(kvh * B_HD, (kvh + 1) * B_HD)
        sk = sink_ref[layer:layer + 1, hq:hq + 1]
        for j in range(n_sub):
            keys = slice(j * blk, (j + 3) * blk)
            qh = q[j * blk:(j + 1) * blk, hq * B_HD:(hq + 1) * B_HD]
            s = _dot_nt(k[keys, ksl], qh)
            s = jnp.concatenate([s[:blk] + biases[j][:blk], s[blk:2 * blk],
                                 s[2 * blk:] + biases[j][2 * blk:]], axis=0)
            m = jnp.maximum(jnp.max(s, axis=0, keepdims=True), sk)
            p = jnp.exp(s - m).astype(BF16)
            o = _dot(vts[kvh][:, keys], p)
            denom = o[B_HD:B_HD + 1, :] + jnp.exp(sk - m)
            pairs[j].append(o[:B_HD, :] / denom)
            if len(pairs[j]) == LANE // B_HD:
                tiles[j].append(jnp.concatenate(pairs[j], axis=0).T)
                pairs[j] = []
    for j in range(n_sub):
        rows = slice(j * blk, (j + 1) * blk)
        y = jnp.concatenate(tiles[j], axis=1) * _silu(z_ref[rows, :])
        out_ref[rows, :] = y.astype(out_ref.dtype)


def _swa(u, cos_t, sin_t, sink, layer, batch):
    rows = u.shape[0]
    n_blocks = rows // batch // CHUNK
    n_sub = SWA_TILE // CHUNK
    n_tiles = n_blocks // n_sub

    def tile_spec(width, blk):
        return pl.BlockSpec((SWA_TILE, width), lambda bi, i: (bi * n_tiles + i, blk))

    def halo_spec(width, blk, after):
        def index(bi, i):
            nb = (i + 1) * n_sub if after else i * n_sub - 1
            return (bi * n_blocks + jnp.clip(nb, 0, n_blocks - 1), blk)
        return pl.BlockSpec((CHUNK, width), index)

    def table_specs():
        return [halo_spec(LANE, 0, False), tile_spec(LANE, 0), halo_spec(LANE, 0, True)]

    return pl.pallas_call(
        functools.partial(_swa_kernel, layer=layer),
        out_shape=jax.ShapeDtypeStruct((rows, D_GROUP), BF16),
        grid=(batch, n_tiles),
        in_specs=[tile_spec(D_GROUP, BLK_B_Q), tile_spec(D_GROUP, BLK_B_Z),
                  halo_spec(D_GROUP, BLK_MISC, False), tile_spec(D_GROUP, BLK_MISC),
                  halo_spec(D_GROUP, BLK_MISC, True)] + table_specs() + table_specs() + [
                  pl.BlockSpec(sink.shape, lambda bi, i: (0, 0))],
        out_specs=tile_spec(D_GROUP, 0),
        compiler_params=pltpu.CompilerParams(
            dimension_semantics=("parallel", "parallel"), vmem_limit_bytes=VMEM_LIMIT),
        name="window_attn",
    )(u, u, u, u, u, cos_t, cos_t, cos_t, sin_t, sin_t, sin_t, sink)


def _mem_kv_kernel(mem_ref, g_ref, w_ref, kt_ref, v_ref, *, layer):
    x = mem_ref[...]
    ms = jnp.mean(x * x, axis=-1, keepdims=True)
    h = (x * lax.rsqrt(ms + EPS) * g_ref[layer:layer + 1, :]).astype(BF16)
    kv = _dot(h, w_ref[...].astype(BF16))
    kt_ref[...] = kv[:, :D_GROUP].T.astype(kt_ref.dtype)
    v_ref[...] = kv[:, D_GROUP:].astype(v_ref.dtype)


def _mem_kv(mem, g_all, w_all, layer):
    batch, m_len, d = mem.shape
    n_out = w_all.shape[2]
    return pl.pallas_call(
        functools.partial(_mem_kv_kernel, layer=layer),
        out_shape=(jax.ShapeDtypeStruct((batch, D_GROUP, m_len), BF16),
                   jax.ShapeDtypeStruct((batch, m_len, D_GROUP), BF16)),
        grid=(batch,),
        in_specs=[pl.BlockSpec((None, m_len, d), lambda bi: (bi, 0, 0)),
                  pl.BlockSpec(g_all.shape, lambda bi: (0, 0)),
                  pl.BlockSpec((None, d, n_out), lambda bi: (layer, 0, 0))],
        out_specs=(pl.BlockSpec((None, D_GROUP, m_len), lambda bi: (bi, 0, 0)),
                   pl.BlockSpec((None, m_len, D_GROUP), lambda bi: (bi, 0, 0))),
        compiler_params=pltpu.CompilerParams(
            dimension_semantics=("parallel",), vmem_limit_bytes=VMEM_LIMIT),
        name="mem_kv",
    )(mem, g_all, w_all)


def _mem_attn_kernel(q_ref, z_ref, kt_ref, v_ref, out_ref):
    q = q_ref[...].astype(BF16)
    outs = []
    for hd in range(D_HEADS):
        sl = slice(hd * D_HD, (hd + 1) * D_HD)
        s = _dot(q[:, sl], kt_ref[sl, :])
        m = jnp.max(s, axis=-1, keepdims=True)
        p = jnp.exp2((s - m) * (D_HD ** -0.5 * LOG2_E))
        den = jnp.sum(p, axis=-1, keepdims=True)
        outs.append(_dot(p.astype(BF16), v_ref[:, sl]) / den)
    y = jnp.concatenate(outs, axis=1) * _silu(z_ref[...])
    out_ref[...] = y.astype(out_ref.dtype)


def _mem_attn(u, u16, kt, v, batch):
    rows = u.shape[0]
    tm = ROW_TILE
    per_batch = rows // batch // tm
    return pl.pallas_call(
        _mem_attn_kernel,
        out_shape=jax.ShapeDtypeStruct((rows, D_GROUP), BF16),
        grid=(batch, per_batch),
        in_specs=[pl.BlockSpec((tm, D_GROUP), lambda bi, i: (bi * per_batch + i, BLK16_D_Q)),
                  pl.BlockSpec((tm, D_GROUP), lambda bi, i: (bi * per_batch + i, BLK_D_Z)),
                  pl.BlockSpec((None,) + kt.shape[1:], lambda bi, i: (bi, 0, 0)),
                  pl.BlockSpec((None,) + v.shape[1:], lambda bi, i: (bi, 0, 0))],
        out_specs=pl.BlockSpec((tm, D_GROUP), lambda bi, i: (bi * per_batch + i, 0)),
        compiler_params=pltpu.CompilerParams(
            dimension_semantics=("parallel", "parallel"), vmem_limit_bytes=VMEM_LIMIT),
        name="mem_attn",
    )(u16, u, kt, v)


def kernel(x, mem, positions, norm_g, w_in, hgrn_lb_logits, hgrn_norm_g, attn_sink, mlstm_conv_w,
           mlstm_gate_b, mlstm_norm_g, mem_norm_g, w_mem_kv, w_out, final_norm_g):
    batch, seq, d = x.shape
    depth = w_in.shape[0]
    assert all(seq % t == 0 for t in (CHUNK, ROW_TILE, IN_ROW_TILE, SWA_TILE))
    assert (batch * seq) % OUT_ROW_TILE == 0
    assert batch % REC_GROUP == 0
    xf = x.reshape(batch * seq, d)
    cos_t, sin_t = _rope_tables(positions)
    w_all = _prep_w_in(w_in)
    gate_b = mlstm_gate_b.reshape(depth, N_GATE, 1)
    for layer in range(depth):
        u, u16, gt, vt = _in_proj(xf, norm_g, w_all, gate_b, layer, batch)
        u3 = u.reshape(batch, seq, u.shape[1])
        u16_3 = u16.reshape(batch, seq, u16.shape[1])
        ob = _hgrn(u3, u16_3, hgrn_lb_logits, layer, reverse=True)
        ya = _hgrn(u3, u16_3, hgrn_lb_logits, layer, reverse=False, ob=ob, norm_g=hgrn_norm_g)
        yb = _swa(u, cos_t, sin_t, attn_sink, layer, batch)
        hb, k_c, qt_c = _mlstm(u3, gt, vt, mlstm_conv_w, layer, reverse=True)
        yc = _mlstm(u3, gt, vt, None, layer, reverse=False, qk=(k_c, qt_c), hb=hb,
                    norm_g=mlstm_norm_g)
        mem_kt, mem_v = _mem_kv(mem, mem_norm_g, w_mem_kv, layer)
        yd = _mem_attn(u, u16, mem_kt, mem_v, batch)
        xf = _out_proj(ya.reshape(batch * seq, D_GROUP), yb, yc.reshape(batch * seq, D_GROUP), yd,
                       w_out, xf, final_norm_g, layer, final=(layer == depth - 1))
    return xf.reshape(batch, seq, d)
```

```python
import functools

import jax
import jax.numpy as jnp
import numpy as np
from jax import lax
from jax.experimental import pallas as pl
from jax.experimental.pallas import tpu as pltpu

F32 = jnp.float32
BF16 = jnp.bfloat16

D_GROUP = 512
A_HEADS, A_HD = 4, 128
B_HEADS, B_KV_HEADS, B_HD = 8, 2, 64
WINDOW = 128
ROPE_THETA = 500000.0
ROPE_DIM = 16
C_HEADS, C_QK, C_V = 4, 64, 128
CONV_W = 5
D_HEADS, D_HD = 4, 128
EPS = 1e-6
IN_SIZES = (
    D_GROUP, D_GROUP, D_GROUP, D_GROUP, D_GROUP,
    B_HEADS * B_HD, B_KV_HEADS * B_HD, B_KV_HEADS * B_HD, D_GROUP,
    C_HEADS * C_QK, C_HEADS * C_QK, C_HEADS * C_V, D_GROUP, D_GROUP,
    2 * C_HEADS, 2 * C_HEADS,
    D_GROUP, D_GROUP,
)
SPLIT_POINTS = tuple(int(s) for s in np.cumsum(IN_SIZES)[:-1])

LANE = 128
SUBLANE = 8
VMEM_LIMIT = 48 * 1024 * 1024

CHUNK = 128
ROW_TILE = 512
IN_ROW_TILE = 512
IN_PROJ_VMEM_LIMIT = 56 * 1024 * 1024
W_PREP_TILE = 256
OUT_ROW_TILE = 1024
HGRN_SUB = 64
HGRN_SAFE_LOG2 = 100.0
SWA_TILE = 512
REC_GROUP = 4
NEG = -1e30
LOG2_E = 1.4426950408889634

(BLK_A_Q, BLK_A_FF, BLK_A_FB, BLK_A_Z, BLK_B_Q, BLK_B_Z, BLK_C_QK, BLK_C_O,
 BLK_C_Z, BLK_D_Z, BLK_MISC) = range(11)
N_BLK = 11
BLK16_A_I, BLK16_D_Q = range(2)
N_BLK16 = 2
W_ROW_C_V = (N_BLK + N_BLK16) * D_GROUP
N_GATE = 4 * C_HEADS
W_ROW_GATE = W_ROW_C_V + D_GROUP
N_W_ROWS = W_ROW_GATE + N_GATE
MISC_K, MISC_V = 0, 128
C_ONES = 16


def _dot(a, b):
    return jnp.dot(a, b, preferred_element_type=F32)


def _dot_nt(a, b):
    return lax.dot_general(a, b, (((1,), (1,)), ((), ())), preferred_element_type=F32)


def _dot_tn(a, b):
    return lax.dot_general(a, b, (((0,), (0,)), ((), ())), preferred_element_type=F32)


def _sigmoid(x):
    return 1.0 / (1.0 + jnp.exp(-x))


def _silu(x):
    return x * _sigmoid(x)


def _log_sigmoid(x):
    return jnp.minimum(x, 0.0) - jnp.log(1.0 + jnp.exp(-jnp.abs(x)))


def _chunk_cumsum_lanes(x, reverse):
    n = x.shape[1]
    pos = lax.broadcasted_iota(jnp.int32, x.shape, 1) & (CHUNK - 1)
    s = 1
    while s < CHUNK:
        if reverse:
            x = x + jnp.where(pos < CHUNK - s, pltpu.roll(x, n - s, axis=1), 0.0)
        else:
            x = x + jnp.where(pos >= s, pltpu.roll(x, s, axis=1), 0.0)
        s *= 2
    return x


def _in_proj_kernel(x_ref, g_ref, w_ref, gb_ref, u_ref, u16_ref, gt_ref, vt_ref, *, layer):
    x = x_ref[...]
    ms = jnp.mean(x * x, axis=-1, keepdims=True)
    h = (x * lax.rsqrt(ms + EPS) * g_ref[layer:layer + 1, :]).astype(BF16)
    vg = _dot_nt(w_ref[W_ROW_C_V:W_ROW_GATE + N_GATE, :], h)
    vt_ref[...] = vg[:D_GROUP, :].astype(BF16)
    gates = vg[D_GROUP:, :] + gb_ref[layer]
    n_dir = 2 * C_HEADS
    log_f = _log_sigmoid(gates[n_dir:, :])
    is_fwd = lax.broadcasted_iota(jnp.int32, log_f.shape, 0) < C_HEADS
    cum = jnp.where(is_fwd, _chunk_cumsum_lanes(log_f, False), _chunk_cumsum_lanes(log_f, True))
    gt_ref[...] = jnp.concatenate([gates[:n_dir, :], cum], axis=0)
    for j in range(N_BLK):
        u_ref[:, j * D_GROUP:(j + 1) * D_GROUP] = _dot_nt(h, w_ref[j * D_GROUP:(j + 1) * D_GROUP, :])
    for j in range(N_BLK16):
        rows = slice((N_BLK + j) * D_GROUP, (N_BLK + j + 1) * D_GROUP)
        u16_ref[:, j * D_GROUP:(j + 1) * D_GROUP] = _dot_nt(h, w_ref[rows, :]).astype(BF16)


def _in_proj(xf, g_all, w_all, gate_b, layer, batch):
    rows, d = xf.shape
    tm = IN_ROW_TILE
    per_batch = rows // batch // tm
    return pl.pallas_call(
        functools.partial(_in_proj_kernel, layer=layer),
        out_shape=(jax.ShapeDtypeStruct((rows, N_BLK * D_GROUP), F32),
                   jax.ShapeDtypeStruct((rows, N_BLK16 * D_GROUP), BF16),
                   jax.ShapeDtypeStruct((batch, N_GATE, rows // batch), F32),
                   jax.ShapeDtypeStruct((batch, D_GROUP, rows // batch), BF16)),
        grid=(rows // tm,),
        in_specs=[
            pl.BlockSpec((tm, d), lambda i: (i, 0)),
            pl.BlockSpec(g_all.shape, lambda i: (0, 0)),
            pl.BlockSpec((None, N_W_ROWS, d), lambda i: (layer, 0, 0), pipeline_mode=pl.Buffered(1)),
            pl.BlockSpec(gate_b.shape, lambda i: (0, 0, 0)),
        ],
        out_specs=(
            pl.BlockSpec((tm, N_BLK * D_GROUP), lambda i: (i, 0)),
            pl.BlockSpec((tm, N_BLK16 * D_GROUP), lambda i: (i, 0)),
            pl.BlockSpec((None, N_GATE, tm), lambda i: (i // per_batch, 0, i % per_batch)),
            pl.BlockSpec((None, D_GROUP, tm), lambda i: (i // per_batch, 0, i % per_batch)),
        ),
        compiler_params=pltpu.CompilerParams(
            dimension_semantics=("parallel",), vmem_limit_bytes=IN_PROJ_VMEM_LIMIT),
        name="in_proj",
    )(xf, g_all, w_all, gate_b)


def _w_in_segments():
    names = ("a_q", "a_i", "a_ff", "a_fb", "a_z", "b_q", "b_k", "b_v", "b_z",
             "c_q", "c_k", "c_v", "c_o", "c_z", "c_ig", "c_fg", "d_q", "d_z")
    src = dict(zip(names, zip((0,) + SPLIT_POINTS, IN_SIZES)))
    blocks = {BLK_A_Q: ("a_q",), BLK_A_FF: ("a_ff",), BLK_A_FB: ("a_fb",), BLK_A_Z: ("a_z",),
              BLK_B_Q: ("b_q",), BLK_B_Z: ("b_z",), BLK_C_QK: ("c_q", "c_k"), BLK_C_O: ("c_o",),
              BLK_C_Z: ("c_z",), BLK_D_Z: ("d_z",), BLK_MISC: ("b_k", "b_v"),
              N_BLK + BLK16_A_I: ("a_i",), N_BLK + BLK16_D_Q: ("d_q",)}
    placed = []
    for blk, group in sorted(blocks.items()):
        dst = blk * D_GROUP
        for name in group:
            placed.append((name, dst))
            dst += src[name][1]
    placed += [("c_v", W_ROW_C_V), ("c_ig", W_ROW_GATE), ("c_fg", W_ROW_GATE + src["c_ig"][1])]
    segs, gaps, end = [], [], 0
    for name, dst in placed:
        off, width = src[name]
        if dst > end:
            gaps.append((end, dst))
        if segs and segs[-1][0] + segs[-1][1] == off and segs[-1][2] + segs[-1][1] == dst:
            segs[-1] = (segs[-1][0], segs[-1][1] + width, segs[-1][2])
        else:
            segs.append((off, width, dst))
        end = dst + width
    assert end == N_W_ROWS
    return tuple(segs), tuple(gaps)


def _w_prep_kernel(wt_ref, wc_ref):
    segs, gaps = _w_in_segments()
    for off, width, dst in segs:
        wc_ref[dst:dst + width, :] = wt_ref[off:off + width, :].astype(BF16)
    for lo, hi in gaps:
        wc_ref[lo:hi, :] = jnp.zeros((hi - lo, wc_ref.shape[1]), BF16)


def _prep_w_in(w):
    depth, d, d_in = w.shape
    wt = jnp.swapaxes(w, 1, 2)
    tk = W_PREP_TILE
    return pl.pallas_call(
        _w_prep_kernel,
        out_shape=jax.ShapeDtypeStruct((depth, N_W_ROWS, d), BF16),
        grid=(depth, d // tk),
        in_specs=[pl.BlockSpec((None, d_in, tk), lambda l, i: (l, 0, i))],
        out_specs=pl.BlockSpec((None, N_W_ROWS, tk), lambda l, i: (l, 0, i)),
        compiler_params=pltpu.CompilerParams(
            dimension_semantics=("parallel", "parallel"), vmem_limit_bytes=VMEM_LIMIT),
        name="w_prep",
    )(wt)


def _out_proj_kernel(ya_ref, yb_ref, yc_ref, yd_ref, w_ref, x_ref, g_ref, o_ref, *, final):
    acc = x_ref[...]
    for i, y_ref in enumerate((ya_ref, yb_ref, yc_ref, yd_ref)):
        acc = acc + _dot(y_ref[...], w_ref[i * D_GROUP:(i + 1) * D_GROUP, :].astype(BF16))
    if final:
        ms = jnp.mean(acc * acc, axis=-1, keepdims=True)
        acc = acc * lax.rsqrt(ms + EPS) * g_ref[...]
    o_ref[...] = acc


def _out_proj(ya, yb, yc, yd, w_all, xf, g, layer, final):
    rows, d = xf.shape
    tm = OUT_ROW_TILE
    yspec = pl.BlockSpec((tm, D_GROUP), lambda i: (i, 0))
    return pl.pallas_call(
        functools.partial(_out_proj_kernel, final=final),
        out_shape=jax.ShapeDtypeStruct((rows, d), F32),
        grid=(rows // tm,),
        in_specs=[yspec, yspec, yspec, yspec,
                  pl.BlockSpec((None, 4 * D_GROUP, d), lambda i: (layer, 0, 0)),
                  pl.BlockSpec((tm, d), lambda i: (i, 0)),
                  pl.BlockSpec((1, d), lambda i: (0, 0))],
        out_specs=pl.BlockSpec((tm, d), lambda i: (i, 0)),
        compiler_params=pltpu.CompilerParams(
            dimension_semantics=("parallel",), vmem_limit_bytes=VMEM_LIMIT),
        name="out_proj",
    )(ya, yb, yc, yd, w_all, xf, g.reshape(1, d))


def _level_ref(b, h, reverse):
    n_rows, n = b.shape
    blk = 2 * h
    r0 = h if reverse else h - 1
    if blk >= SUBLANE:
        pieces = [jnp.broadcast_to(b[s + r0:s + r0 + 1, :], (blk, n)) for s in range(0, n_rows, blk)]
        return pieces[0] if len(pieces) == 1 else jnp.concatenate(pieces, axis=0)
    res = lax.broadcasted_iota(jnp.int32, b.shape, 0) & (blk - 1)
    out = None
    for m in range(blk):
        shift = (m - r0) % n_rows
        cand = b if shift == 0 else pltpu.roll(b, shift, axis=0)
        out = cand if out is None else jnp.where(res == m, cand, out)
    return out


def _hgrn_lower_bound(lbl_ref, layer, depth):
    rows = [lbl_ref[j:j + 1, :] for j in range(depth)]
    mx = functools.reduce(jnp.maximum, rows)
    es = [jnp.exp(r - mx) for r in rows]
    tot = functools.reduce(lambda a, c: a + c, es)
    lb = jnp.zeros_like(rows[0])
    for j in range(1, layer + 1):
        lb = lb + es[j] / tot
    return lb


def _cumsum_rows_mxu(x, reverse):
    n = x.shape[0]
    r = lax.broadcasted_iota(jnp.int32, (n, n), 0)
    c = lax.broadcasted_iota(jnp.int32, (n, n), 1)
    tri = jnp.where((c >= r) if reverse else (c <= r), 1.0, 0.0).astype(BF16)
    hi = x.astype(BF16)
    r1 = x - hi.astype(F32)
    mid = r1.astype(BF16)
    lo = (r1 - mid.astype(F32)).astype(BF16)
    return _dot(jnp.concatenate([tri, tri, tri], axis=1), jnp.concatenate([hi, mid, lo], axis=0))


def _hgrn_level_operand(q, k, f, b2, h, reverse):
    n_rows = q.shape[0]
    blk = 2 * h
    r0 = h if reverse else h - 1
    if h >= SUBLANE:
        pieces = []
        for s0 in range(0, n_rows, blk):
            ref = b2[s0 + r0:s0 + r0 + 1, :]
            lo, up = slice(s0, s0 + h), slice(s0 + h, s0 + blk)
            if reverse:
                pieces += [q[lo] * jnp.exp2(b2[lo] - ref), k[up] * jnp.exp2(ref - b2[up])]
            else:
                pieces += [k[lo] * jnp.exp2(ref - b2[lo]), q[up] * jnp.exp2(b2[up] - ref)]
        return jnp.concatenate(pieces, axis=0)
    row = lax.broadcasted_iota(jnp.int32, q.shape, 0)
    is_q = ((row & h) == 0) if reverse else ((row & h) != 0)
    qk = jnp.where(is_q, q, k)
    if h == 1:
        return jnp.where(is_q, q * f, k)
    if h == 2:
        res = row & 3
        f_dn = pltpu.roll(f, 1, axis=0)
        f_up = pltpu.roll(f, n_rows - 1, axis=0)
        if reverse:
            e = jnp.where(res == 0, f * f_up, jnp.where(res == 1, f, jnp.where(res == 2, 1.0, f_dn)))
        else:
            e = jnp.where(res == 0, f_up, jnp.where(res == 1, 1.0, jnp.where(res == 2, f, f_dn * f)))
        return qk * e
    return qk * jnp.exp2(-jnp.abs(b2 - _level_ref(b2, h, reverse)))


def _hgrn_kernel(*refs, layer, depth, reverse, final):
    if final:
        q_ref, v_ref, zg_ref, lbl_ref, z_ref, ob_ref, ng_ref, out_ref, st_ref = refs
    else:
        q_ref, v_ref, zg_ref, lbl_ref, out_ref, st_ref = refs

    @pl.when(pl.program_id(1) == 0)
    def _():
        st_ref[...] = jnp.zeros_like(st_ref)

    rows = range(q_ref.shape[0])
    n_rows = q_ref.shape[1]
    ti = lax.broadcasted_iota(jnp.int32, (n_rows, n_rows), 0)
    si = lax.broadcasted_iota(jnp.int32, (n_rows, n_rows), 1)
    ahead = (ti < si) if reverse else (ti > si)
    level = jnp.where(ahead, 31 - lax.clz(ti ^ si), jnp.where(ti == si, -1, -2))
    preps, bounds = zip(*[_hgrn_prep(q_ref.at[g], zg_ref.at[g], lbl_ref, layer, depth, reverse)
                          for g in rows])

    def run(fast):
        outs = [[] for _ in rows]
        for hd in range(A_HEADS):
            sl = slice(hd * A_HD, (hd + 1) * A_HD)
            heads = [tuple(a[:, sl] for a in preps[g]) for g in rows]
            if fast:
                atts = [_hgrn_scores_fast(heads[g], level, reverse) for g in rows]
            else:
                atts = [_hgrn_scores(heads[g], level, n_rows, None, reverse) for g in rows]
                h = n_rows // 2
                while h >= 1:
                    atts = [_hgrn_scores(heads[g], level, h, atts[g], reverse) for g in rows]
                    h //= 2
            for g in rows:
                outs[g].append(_hgrn_apply(heads[g], atts[g], v_ref.at[g, :, sl],
                                           st_ref.at[g, hd], reverse))
        for g in rows:
            o = jnp.concatenate(outs[g], axis=1)
            if final:
                _hgrn_finish(o, z_ref.at[g], ob_ref.at[g], ng_ref.at[layer:layer + 1],
                             out_ref.at[g])
            else:
                out_ref[g] = o

    safe = jnp.max(functools.reduce(jnp.maximum, bounds)) <= HGRN_SAFE_LOG2

    @pl.when(safe)
    def _():
        run(True)

    @pl.when(jnp.logical_not(safe))
    def _():
        run(False)


def _hgrn_prep(q_ref, zg_ref, lbl_ref, layer, depth, reverse):
    n_rows = q_ref.shape[0]
    q = q_ref[...]
    zg = zg_ref[...]
    e = jnp.exp(-jnp.abs(zg))
    r = 1.0 / (1.0 + e)
    er = e * r
    pos = zg >= 0.0
    f = jnp.where(pos, r, er)
    k = jnp.where(pos, er, r)
    if layer > 0:
        lb = _hgrn_lower_bound(lbl_ref, layer, depth)
        f = lb + (1.0 - lb) * f
        k = (1.0 - lb) * k
    g2 = jnp.log2(f)
    half = HGRN_SUB // 2
    bound = None
    for s0 in range(0, n_rows, HGRN_SUB):
        first = jnp.sum(g2[s0:s0 + half], axis=0, keepdims=True) + g2[s0 + half:s0 + half + 1]
        second = jnp.sum(g2[s0 + half:s0 + HGRN_SUB], axis=0, keepdims=True)
        worst = -jnp.minimum(first, second)
        bound = worst if bound is None else jnp.maximum(bound, worst)
    b2 = _cumsum_rows_mxu(g2, reverse)
    return (q, k, f, b2), bound


def _hgrn_scores(head, level, h, att, reverse):
    q, k, f, b2 = head
    if att is None:
        return jnp.where(level == -1, _dot_nt(q.astype(BF16), k.astype(BF16)), 0.0)
    c = _hgrn_level_operand(q, k, f, b2, h, reverse).astype(BF16)
    return jnp.where(level == (h.bit_length() - 1), _dot_nt(c, c), att)


def _hgrn_scores_fast(head, level, reverse):
    q, k, _, b2 = head
    n_rows = q.shape[0]
    blocks = []
    for s0 in range(0, n_rows, HGRN_SUB):
        rows = slice(s0, s0 + HGRN_SUB)
        ref = b2[s0 + HGRN_SUB // 2:s0 + HGRN_SUB // 2 + 1, :]
        qs = (q[rows] * jnp.exp2(b2[rows] - ref)).astype(BF16)
        ks = (k * jnp.exp2(jnp.minimum(ref - b2, HGRN_SAFE_LOG2))).astype(BF16)
        blocks.append(_dot_nt(qs, ks))
    return jnp.where(level >= -1, jnp.concatenate(blocks, axis=0), 0.0)


def _hgrn_apply(head, att, v_ref, st_ref, reverse):
    q, k, _, b2 = head
    n_rows = q.shape[0]
    tot = b2[0:1, :] if reverse else b2[n_rows - 1:n_rows, :]
    v_bf = v_ref[...].astype(BF16)
    q_in = (q * jnp.exp2(b2)).astype(BF16)
    k_out = (k * jnp.exp2(tot - b2)).astype(BF16)
    st = st_ref[...]
    o = _dot(att.astype(BF16), v_bf) + _dot(q_in, st.astype(BF16))
    dec_col = jnp.broadcast_to(jnp.exp2(tot), (SUBLANE, tot.shape[1])).T[:, 0:1]
    st_ref[...] = dec_col * st + _dot_tn(k_out, v_bf)
    return o


def _hgrn_finish(o, z_ref, ob_ref, ng_ref, out_ref):
    o = o + ob_ref[...]
    ys = []
    for hd in range(A_HEADS):
        sl = slice(hd * A_HD, (hd + 1) * A_HD)
        oh = o[:, sl]
        ms = jnp.mean(oh * oh, axis=-1, keepdims=True)
        ys.append(oh * lax.rsqrt(ms + EPS) * ng_ref[:, sl])
    y = jnp.concatenate(ys, axis=1) * _silu(z_ref[...])
    out_ref[...] = y.astype(out_ref.dtype)


def _chunk_index(c, n_chunks, reverse):
    return n_chunks - 1 - c if reverse else c


def _hgrn(u3, u16, lb_logits, layer, *, reverse, ob=None, norm_g=None):
    batch, seq, _ = u3.shape
    n_chunks = seq // CHUNK
    depth = lb_logits.shape[0]
    final = ob is not None
    grp = REC_GROUP

    def ublk(blk):
        return pl.BlockSpec((grp, CHUNK, D_GROUP),
                            lambda bg, c: (bg, _chunk_index(c, n_chunks, reverse), blk))

    row_spec = ublk(0)
    in_specs = [ublk(BLK_A_Q), ublk(BLK16_A_I), ublk(BLK_A_FB if reverse else BLK_A_FF),
                pl.BlockSpec((depth, D_GROUP), lambda bg, c: (0, 0))]
    args = [u3, u16, u3, lb_logits]
    if final:
        in_specs += [ublk(BLK_A_Z), row_spec, pl.BlockSpec(norm_g.shape, lambda bg, c: (0, 0))]
        args += [u3, ob, norm_g]
    return pl.pallas_call(
        functools.partial(_hgrn_kernel, layer=layer, depth=depth, reverse=reverse, final=final),
        out_shape=jax.ShapeDtypeStruct((batch, seq, D_GROUP), BF16 if final else F32),
        grid=(batch // grp, n_chunks),
        in_specs=in_specs,
        out_specs=row_spec,
        scratch_shapes=[pltpu.VMEM((grp, A_HEADS, A_HD, A_HD), F32)],
        compiler_params=pltpu.CompilerParams(
            dimension_semantics=("parallel", "arbitrary"), vmem_limit_bytes=VMEM_LIMIT),
        name="hgrn_fwd" if final else "hgrn_bwd",
    )(*args)


def _split3(x):
    hi = x.astype(BF16).astype(F32)
    r1 = x - hi
    mid = r1.astype(BF16).astype(F32)
    lo = (r1 - mid).astype(BF16).astype(F32)
    return hi, mid, lo


def _conv_silu_qk(qkp_ref, qkc_ref, qkn_ref, cw_ref, reverse):
    n_rows = qkc_ref.shape[0]
    c = pl.program_id(1)
    n_chunks = pl.num_programs(1)
    cc = _chunk_index(c, n_chunks, reverse)
    halo = SUBLANE
    prev = qkp_ref[n_rows - halo:n_rows, :] * (cc > 0).astype(F32)
    nxt = qkn_ref[0:halo, :] * (cc < n_chunks - 1).astype(F32)
    xcat = jnp.concatenate([prev, qkc_ref[...], nxt], axis=0)
    acc = None
    for j in range(CONV_W):
        off = halo + j - CONV_W // 2
        term = cw_ref[j:j + 1, :] * xcat[off:off + n_rows, :]
        acc = term if acc is None else acc + term
    qk = _silu(acc)
    n_qk = C_HEADS * C_QK
    return qk[:, n_qk:].astype(BF16), (qk[:, :n_qk] * (C_QK ** -0.5)).T.astype(BF16)


def _mlstm_kernel(*refs, layer, reverse, final):
    if final:
        (k_ref, qt_ref, v_ref, gt_ref, og_ref, z_ref, hb_ref, ng_ref,
         out_ref, cst_ref, m_ref) = refs
    else:
        (qkp_ref, qkc_ref, qkn_ref, cw_ref, v_ref, gt_ref,
         out_ref, ko_ref, qto_ref, cst_ref, m_ref) = refs

    @pl.when(pl.program_id(1) == 0)
    def _():
        cst_ref[...] = jnp.zeros_like(cst_ref)
        m_ref[...] = jnp.zeros_like(m_ref)

    rows = range(v_ref.shape[0])
    if final:
        qks = [(k_ref[g], qt_ref[g]) for g in rows]
    else:
        qks = [_conv_silu_qk(qkp_ref.at[g], qkc_ref.at[g], qkn_ref.at[g], cw_ref.at[layer], reverse)
               for g in rows]
        for g in rows:
            ko_ref[g], qto_ref[g] = qks[g]
    gates = [_mlstm_gates(gt_ref.at[g], m_ref.at[g], reverse) for g in rows]
    outs = [[] for _ in rows]
    for hd in range(C_HEADS):
        weights = [_mlstm_weights(hd, qks[g], gates[g], reverse) for g in rows]
        for g in rows:
            h_t = _mlstm_apply(hd, qks[g], v_ref.at[g], gates[g], weights[g], cst_ref.at[g], reverse)
            if final:
                outs[g].append((h_t + hb_ref[g, hd * C_V:(hd + 1) * C_V, :]).T)
            else:
                out_ref[g, hd * C_V:(hd + 1) * C_V, :] = h_t
    if final:
        for g in rows:
            _mlstm_finish(jnp.concatenate(outs[g], axis=1), og_ref.at[g], z_ref.at[g],
                          ng_ref.at[layer:layer + 1], out_ref.at[g])


def _mlstm_gates(gt_ref, m_ref, reverse):
    n_rows = gt_ref.shape[1]
    n_dir = 2 * C_HEADS

    li = gt_ref[0:n_dir, :]
    b = gt_ref[n_dir:, :]
    b_last = b[:, 0:1] if reverse else b[:, n_rows - 1:n_rows]
    m_prev = m_ref[...]
    d_inter = b + m_prev
    a = (b_last - b) + li
    m_new = jnp.maximum(b_last + m_prev, jnp.max(a, axis=1, keepdims=True))
    ws = jnp.exp(a - m_new)
    decay = jnp.exp((b_last + m_prev) - m_new)
    m_ref[...] = m_new

    one8 = jnp.ones((n_dir, n_rows), F32)
    lhs = jnp.concatenate(list(_split3(li - b)) + [one8, one8, one8], axis=0).T.astype(BF16)
    rhs_all = jnp.concatenate([one8, one8, one8] + list(_split3(b)), axis=0)
    return lhs, rhs_all, d_inter, ws, decay


def _mlstm_weights(hd, qk, gates, reverse):
    lhs, rhs_all, d_inter, _, _ = gates
    k, q_t = qk
    n_rows = k.shape[0]
    n_dir = 2 * C_HEADS
    gi = (C_HEADS if reverse else 0) + hd
    gate_row = lax.broadcasted_iota(jnp.int32, rhs_all.shape, 0) & (n_dir - 1)
    si = lax.broadcasted_iota(jnp.int32, (n_rows, n_rows), 0)
    ti = lax.broadcasted_iota(jnp.int32, (n_rows, n_rows), 1)
    causal = (si >= ti) if reverse else (si <= ti)

    sl = slice(hd * C_QK, (hd + 1) * C_QK)
    rhs = jnp.where(gate_row == gi, rhs_all, 0.0).astype(BF16)
    d = jnp.where(causal, _dot(lhs, rhs), NEG)
    d_int = d_inter[gi:gi + 1, :]
    m_t = jnp.maximum(jnp.max(d, axis=0, keepdims=True), d_int)
    w = (_dot(k[:, sl], q_t[sl, :]) * jnp.exp(d - m_t)).astype(BF16)
    return w, m_t, jnp.exp(d_int - m_t)


def _mlstm_apply(hd, qk, vt_ref, gates, weights, cst_ref, reverse):
    _, _, _, ws, decay = gates
    w, m_t, g = weights
    k, q_t = qk
    n_rows = k.shape[0]
    gi = (C_HEADS if reverse else 0) + hd
    sl = slice(hd * C_QK, (hd + 1) * C_QK)
    v_t = vt_ref[hd * C_V:(hd + 1) * C_V, :]
    v_ext = jnp.concatenate([v_t, jnp.ones((C_ONES, n_rows), BF16)], axis=0)
    ws_row = ws[gi:gi + 1, :]
    v_scaled = jnp.concatenate([(v_t.astype(F32) * ws_row).astype(BF16),
                                jnp.broadcast_to(ws_row, (C_ONES, n_rows)).astype(BF16)], axis=0)
    cst = cst_ref[hd]
    num = _dot(v_ext, w) + g * _dot(cst.astype(BF16), q_t[sl, :])
    den = jnp.maximum(jnp.abs(num[C_V:C_V + 1, :]), jnp.exp(-m_t))
    cst_ref[hd] = decay[gi:gi + 1, :C_QK] * cst + _dot(v_scaled, k[:, sl])
    return num[:C_V, :] / den


def _mlstm_finish(hsum, og_ref, z_ref, ng_ref, out_ref):
    hsum = hsum * _sigmoid(og_ref[...])
    ys = []
    for hd in range(C_HEADS):
        sl = slice(hd * C_V, (hd + 1) * C_V)
        hh = hsum[:, sl]
        mu = jnp.mean(hh, axis=-1, keepdims=True)
        var = jnp.mean(jnp.square(hh - mu), axis=-1, keepdims=True)
        ys.append((hh - mu) * lax.rsqrt(var + EPS) * ng_ref[:, sl])
    y = jnp.concatenate(ys, axis=1) * _silu(z_ref[...])
    out_ref[...] = y.astype(out_ref.dtype)


def _mlstm(u3, gt, vt, conv_w, layer, *, reverse, qk=None, hb=None, norm_g=None):
    batch, seq, _ = u3.shape
    n_chunks = seq // CHUNK
    final = hb is not None
    grp = REC_GROUP
    assert CHUNK == LANE

    def chunk(c, delta=0):
        return jnp.clip(_chunk_index(c, n_chunks, reverse) + delta, 0, n_chunks - 1)

    def ublk(blk, delta=0):
        return pl.BlockSpec((grp, CHUNK, D_GROUP), lambda bg, c: (bg, chunk(c, delta), blk))

    row_spec = ublk(0)
    t_spec = lambda ch: pl.BlockSpec((grp, ch, CHUNK), lambda bg, c: (bg, 0, chunk(c)))
    gate_specs = [t_spec(N_GATE)]
    gate_args = [gt]
    n_qk = C_HEADS * C_QK
    k_spec = pl.BlockSpec((grp, CHUNK, n_qk), lambda bg, c: (bg, chunk(c), 0))
    if final:
        in_specs = [k_spec, t_spec(n_qk), t_spec(D_GROUP)] + gate_specs + [
            ublk(BLK_C_O), ublk(BLK_C_Z), t_spec(D_GROUP),
            pl.BlockSpec(norm_g.shape, lambda bg, c: (0, 0))]
        args = [qk[0], qk[1], vt] + gate_args + [u3, u3, hb, norm_g]
        out_shape = jax.ShapeDtypeStruct((batch, seq, D_GROUP), BF16)
        out_specs = row_spec
    else:
        in_specs = [ublk(BLK_C_QK, -1), ublk(BLK_C_QK), ublk(BLK_C_QK, 1),
                    pl.BlockSpec(conv_w.shape, lambda bg, c: (0, 0, 0)),
                    t_spec(D_GROUP)] + gate_specs
        args = [u3, u3, u3, conv_w, vt] + gate_args
        out_shape = (jax.ShapeDtypeStruct((batch, D_GROUP, seq), F32),
                     jax.ShapeDtypeStruct((batch, seq, n_qk), BF16),
                     jax.ShapeDtypeStruct((batch, n_qk, seq), BF16))
        out_specs = (t_spec(D_GROUP), k_spec, t_spec(n_qk))
    return pl.pallas_call(
        functools.partial(_mlstm_kernel, layer=layer, reverse=reverse, final=final),
        out_shape=out_shape,
        grid=(batch // grp, n_chunks),
        in_specs=in_specs,
        out_specs=out_specs,
        scratch_shapes=[pltpu.VMEM((grp, C_HEADS, C_V + C_ONES, C_QK), F32),
                        pltpu.VMEM((grp, 2 * C_HEADS, CHUNK), F32)],
        compiler_params=pltpu.CompilerParams(
            dimension_semantics=("parallel", "arbitrary"), vmem_limit_bytes=VMEM_LIMIT),
        name="mlstm_fwd" if final else "mlstm_bwd",
    )(*args)


def _rope_table_kernel(pos_ref, invf_ref, cos_ref, sin_ref):
    ang = invf_ref[...] * pos_ref[...].astype(F32)
    n_rows = ang.shape[1]
    half = ROPE_DIM // 2
    r = lax.broadcasted_iota(jnp.int32, ang.shape, 0)
    sn = jnp.sin(ang)
    sn = jnp.where(r < half, -sn, sn)
    pad = jnp.zeros((LANE - ROPE_DIM, n_rows), F32)

    def expand(x):
        t = jnp.concatenate([x, pad], axis=0).T
        return t + pltpu.roll(t, B_HD, axis=1)

    lane = lax.broadcasted_iota(jnp.int32, (n_rows, LANE), 1) & (B_HD - 1)
    cos_ref[...] = jnp.where(lane < ROPE_DIM, expand(jnp.cos(ang)), 1.0)
    sin_ref[...] = expand(sn)


def _rope_tables(positions):
    rows = positions.size
    half = ROPE_DIM // 2
    inv_freq = ROPE_THETA ** (-jnp.arange(0, ROPE_DIM, 2, dtype=F32) / ROPE_DIM)
    invf = jnp.concatenate([inv_freq, inv_freq]).reshape(ROPE_DIM, 1)
    tm = ROW_TILE
    return pl.pallas_call(
        _rope_table_kernel,
        out_shape=(jax.ShapeDtypeStruct((rows, LANE), F32),) * 2,
        grid=(rows // tm,),
        in_specs=[pl.BlockSpec((1, tm), lambda i: (0, i)),
                  pl.BlockSpec((ROPE_DIM, 1), lambda i: (0, 0))],
        out_specs=(pl.BlockSpec((tm, LANE), lambda i: (i, 0)),) * 2,
        compiler_params=pltpu.CompilerParams(dimension_semantics=("parallel",)),
        name="rope_tables",
    )(positions.reshape(1, rows), invf)


def _rope(x, cos, sin):
    width = x.shape[1]
    if width > LANE:
        cos = jnp.concatenate([cos] * (width // LANE), axis=1)
        sin = jnp.concatenate([sin] * (width // LANE), axis=1)
    half = ROPE_DIM // 2
    r = lax.broadcasted_iota(jnp.int32, x.shape, 1) & (B_HD - 1)
    partner = jnp.where(r < half, pltpu.roll(x, width - half, axis=1), pltpu.roll(x, half, axis=1))
    return x * cos + partner * sin


def _swa_kernel(q_ref, z_ref, mp_ref, mc_ref, mn_ref, cp_ref, cc_ref, cn_ref,
                sp_ref, sc_ref, sn_ref, sink_ref, out_ref, *, layer):
    tile = q_ref.shape[0]
    blk = mp_ref.shape[0]
    n_sub = tile // blk
    i = pl.program_id(1)
    n_tiles = pl.num_programs(1)
    kv_w = B_KV_HEADS * B_HD
    grp = B_HEADS // B_KV_HEADS

    q = (_rope(q_ref[...], cc_ref[...], sc_ref[...]) * (B_HD ** -0.5)).astype(BF16)
    k = jnp.concatenate([
        _rope(mp_ref[:, MISC_K:MISC_K + kv_w], cp_ref[...], sp_ref[...]),
        _rope(mc_ref[:, MISC_K:MISC_K + kv_w], cc_ref[...], sc_ref[...]),
        _rope(mn_ref[:, MISC_K:MISC_K + kv_w], cn_ref[...], sn_ref[...])], axis=0).astype(BF16)
    v = jnp.concatenate([mp_ref[:, MISC_V:MISC_V + kv_w], mc_ref[:, MISC_V:MISC_V + kv_w],
                         mn_ref[:, MISC_V:MISC_V + kv_w]], axis=0)
    n_keys = v.shape[0]

    kr = lax.broadcasted_iota(jnp.int32, (3 * blk, blk), 0)
    qc = lax.broadcasted_iota(jnp.int32, (3 * blk, blk), 1)
    in_window = jnp.abs(qc - (kr - blk)) <= WINDOW
    biases = []
    for j in range(n_sub):
        valid = in_window
        if j == 0:
            valid = valid & (kr >= jnp.where(i > 0, 0, blk))
        if j == n_sub - 1:
            valid = valid & (kr < jnp.where(i < n_tiles - 1, 3 * blk, 2 * blk))
        biases.append(jnp.where(valid, 0.0, NEG))

    ones = jnp.ones((n_keys, B_HD), F32)
    vts = [jnp.concatenate([v[:, kvh * B_HD:(kvh + 1) * B_HD], ones], axis=1).T.astype(BF16)
           for kvh in range(B_KV_HEADS)]

    tiles = [[] for _ in range(n_sub)]
    pairs = [[] for _ in range(n_sub)]
    for hq in range(B_HEADS):
        kvh = hq // grp
        ksl = slice(kvh * B_HD, (kvh + 1) * B_HD)
        sk = sink_ref[layer:layer + 1, hq:hq + 1]
        for j in range(n_sub):
            keys = slice(j * blk, (j + 3) * blk)
            qh = q[j * blk:(j + 1) * blk, hq * B_HD:(hq + 1) * B_HD]
            s = _dot_nt(k[keys, ksl], qh)
            s = jnp.concatenate([s[:blk] + biases[j][:blk], s[blk:2 * blk],
                                 s[2 * blk:] + biases[j][2 * blk:]], axis=0)
            m = jnp.maximum(jnp.max(s, axis=0, keepdims=True), sk)
            p = jnp.exp(s - m).astype(BF16)
            o = _dot(vts[kvh][:, keys], p)
            denom = o[B_HD:B_HD + 1, :] + jnp.exp(sk - m)
            pairs[j].append(o[:B_HD, :] / denom)
            if len(pairs[j]) == LANE // B_HD:
                tiles[j].append(jnp.concatenate(pairs[j], axis=0).T)
                pairs[j] = []
    for j in range(n_sub):
        rows = slice(j * blk, (j + 1) * blk)
        y = jnp.concatenate(tiles[j], axis=1) * _silu(z_ref[rows, :])
        out_ref[rows, :] = y.astype(out_ref.dtype)


def _swa(u, cos_t, sin_t, sink, layer, batch):
    rows = u.shape[0]
    n_blocks = rows // batch // CHUNK
    n_sub = SWA_TILE // CHUNK
    n_tiles = n_blocks // n_sub

    def tile_spec(width, blk):
        return pl.BlockSpec((SWA_TILE, width), lambda bi, i: (bi * n_tiles + i, blk))

    def halo_spec(width, blk, after):
        def index(bi, i):
            nb = (i + 1) * n_sub if after else i * n_sub - 1
            return (bi * n_blocks + jnp.clip(nb, 0, n_blocks - 1), blk)
        return pl.BlockSpec((CHUNK, width), index)

    def table_specs():
        return [halo_spec(LANE, 0, False), tile_spec(LANE, 0), halo_spec(LANE, 0, True)]

    return pl.pallas_call(
        functools.partial(_swa_kernel, layer=layer),
        out_shape=jax.ShapeDtypeStruct((rows, D_GROUP), BF16),
        grid=(batch, n_tiles),
        in_specs=[tile_spec(D_GROUP, BLK_B_Q), tile_spec(D_GROUP, BLK_B_Z),
                  halo_spec(D_GROUP, BLK_MISC, False), tile_spec(D_GROUP, BLK_MISC),
                  halo_spec(D_GROUP, BLK_MISC, True)] + table_specs() + table_specs() + [
                  pl.BlockSpec(sink.shape, lambda bi, i: (0, 0))],
        out_specs=tile_spec(D_GROUP, 0),
        compiler_params=pltpu.CompilerParams(
            dimension_semantics=("parallel", "parallel"), vmem_limit_bytes=VMEM_LIMIT),
        name="window_attn",
    )(u, u, u, u, u, cos_t, cos_t, cos_t, sin_t, sin_t, sin_t, sink)


def _mem_kv_kernel(mem_ref, g_ref, w_ref, kt_ref, v_ref, *, layer):
    x = mem_ref[...]
    ms = jnp.mean(x * x, axis=-1, keepdims=True)
    h = (x * lax.rsqrt(ms + EPS) * g_ref[layer:layer + 1, :]).astype(BF16)
    kv = _dot(h, w_ref[...].astype(BF16))
    kt_ref[...] = kv[:, :D_GROUP].T.astype(kt_ref.dtype)
    v_ref[...] = kv[:, D_GROUP:].astype(v_ref.dtype)


def _mem_kv(mem, g_all, w_all, layer):
    batch, m_len, d = mem.shape
    n_out = w_all.shape[2]
    return pl.pallas_call(
        functools.partial(_mem_kv_kernel, layer=layer),
        out_shape=(jax.ShapeDtypeStruct((batch, D_GROUP, m_len), BF16),
                   jax.ShapeDtypeStruct((batch, m_len, D_GROUP), BF16)),
        grid=(batch,),
        in_specs=[pl.BlockSpec((None, m_len, d), lambda bi: (bi, 0, 0)),
                  pl.BlockSpec(g_all.shape, lambda bi: (0, 0)),
                  pl.BlockSpec((None, d, n_out), lambda bi: (layer, 0, 0))],
        out_specs=(pl.BlockSpec((None, D_GROUP, m_len), lambda bi: (bi, 0, 0)),
                   pl.BlockSpec((None, m_len, D_GROUP), lambda bi: (bi, 0, 0))),
        compiler_params=pltpu.CompilerParams(
            dimension_semantics=("parallel",), vmem_limit_bytes=VMEM_LIMIT),
        name="mem_kv",
    )(mem, g_all, w_all)


def _mem_attn_kernel(q_ref, z_ref, kt_ref, v_ref, out_ref):
    q = q_ref[...].astype(BF16)
    outs = []
    for hd in range(D_HEADS):
        sl = slice(hd * D_HD, (hd + 1) * D_HD)
        s = _dot(q[:, sl], kt_ref[sl, :])
        m = jnp.max(s, axis=-1, keepdims=True)
        p = jnp.exp2((s - m) * (D_HD ** -0.5 * LOG2_E))
        den = jnp.sum(p, axis=-1, keepdims=True)
        outs.append(_dot(p.astype(BF16), v_ref[:, sl]) / den)
    y = jnp.concatenate(outs, axis=1) * _silu(z_ref[...])
    out_ref[...] = y.astype(out_ref.dtype)


def _mem_attn(u, u16, kt, v, batch):
    rows = u.shape[0]
    tm = ROW_TILE
    per_batch = rows // batch // tm
    return pl.pallas_call(
        _mem_attn_kernel,
        out_shape=jax.ShapeDtypeStruct((rows, D_GROUP), BF16),
        grid=(batch, per_batch),
        in_specs=[pl.BlockSpec((tm, D_GROUP), lambda bi, i: (bi * per_batch + i, BLK16_D_Q)),
                  pl.BlockSpec((tm, D_GROUP), lambda bi, i: (bi * per_batch + i, BLK_D_Z)),
                  pl.BlockSpec((None,) + kt.shape[1:], lambda bi, i: (bi, 0, 0)),
                  pl.BlockSpec((None,) + v.shape[1:], lambda bi, i: (bi, 0, 0))],
        out_specs=pl.BlockSpec((tm, D_GROUP), lambda bi, i: (bi * per_batch + i, 0)),
        compiler_params=pltpu.CompilerParams(
            dimension_semantics=("parallel", "parallel"), vmem_limit_bytes=VMEM_LIMIT),
        name="mem_attn",
    )(u16, u, kt, v)


def kernel(x, mem, positions, norm_g, w_in, hgrn_lb_logits, hgrn_norm_g, attn_sink, mlstm_conv_w,
           mlstm_gate_b, mlstm_norm_g, mem_norm_g, w_mem_kv, w_out, final_norm_g):
    batch, seq, d = x.shape
    depth = w_in.shape[0]
    assert all(seq % t == 0 for t in (CHUNK, ROW_TILE, IN_ROW_TILE, SWA_TILE))
    assert (batch * seq) % OUT_ROW_TILE == 0
    assert batch % REC_GROUP == 0
    xf = x.reshape(batch * seq, d)
    cos_t, sin_t = _rope_tables(positions)
    w_all = _prep_w_in(w_in)
    gate_b = mlstm_gate_b.reshape(depth, N_GATE, 1)
    for layer in range(depth):
        u, u16, gt, vt = _in_proj(xf, norm_g, w_all, gate_b, layer, batch)
        u3 = u.reshape(batch, seq, u.shape[1])
        u16_3 = u16.reshape(batch, seq, u16.shape[1])
        ob = _hgrn(u3, u16_3, hgrn_lb_logits, layer, reverse=True)
        ya = _hgrn(u3, u16_3, hgrn_lb_logits, layer, reverse=False, ob=ob, norm_g=hgrn_norm_g)
        yb = _swa(u, cos_t, sin_t, attn_sink, layer, batch)
        hb, k_c, qt_c = _mlstm(u3, gt, vt, mlstm_conv_w, layer, reverse=True)
        yc = _mlstm(u3, gt, vt, None, layer, reverse=False, qk=(k_c, qt_c), hb=hb,
                    norm_g=mlstm_norm_g)
        mem_kt, mem_v = _mem_kv(mem, mem_norm_g, w_mem_kv, layer)
        yd = _mem_attn(u, u16, mem_kt, mem_v, batch)
        xf = _out_proj(ya.reshape(batch * seq, D_GROUP), yb, yc.reshape(batch * seq, D_GROUP), yd,
                       w_out, xf, final_norm_g, layer, final=(layer == depth - 1))
    return xf.reshape(batch, seq, d)
```

```python
import functools

import jax
import jax.numpy as jnp
import numpy as np
from jax import lax
from jax.experimental import pallas as pl
from jax.experimental.pallas import tpu as pltpu

F32 = jnp.float32
BF16 = jnp.bfloat16

D_GROUP = 512
A_HEADS, A_HD = 4, 128
B_HEADS, B_KV_HEADS, B_HD = 8, 2, 64
WINDOW = 128
ROPE_THETA = 500000.0
ROPE_DIM = 16
C_HEADS, C_QK, C_V = 4, 64, 128
CONV_W = 5
D_HEADS, D_HD = 4, 128
EPS = 1e-6
IN_SIZES = (
    D_GROUP, D_GROUP, D_GROUP, D_GROUP, D_GROUP,
    B_HEADS * B_HD, B_KV_HEADS * B_HD, B_KV_HEADS * B_HD, D_GROUP,
    C_HEADS * C_QK, C_HEADS * C_QK, C_HEADS * C_V, D_GROUP, D_GROUP,
    2 * C_HEADS, 2 * C_HEADS,
    D_GROUP, D_GROUP,
)
SPLIT_POINTS = tuple(int(s) for s in np.cumsum(IN_SIZES)[:-1])

LANE = 128
SUBLANE = 8
VMEM_LIMIT = 48 * 1024 * 1024

CHUNK = 128
ROW_TILE = 512
IN_ROW_TILE = 512
IN_PROJ_VMEM_LIMIT = 56 * 1024 * 1024
W_PREP_TILE = 256
HGRN_SUB = 64
HGRN_SAFE_LOG2 = 100.0
SWA_TILE = 512
REC_GROUP = 4
NEG = -1e30
LOG2_E = 1.4426950408889634

(BLK_A_Q, BLK_A_FF, BLK_A_FB, BLK_A_Z, BLK_B_Q, BLK_B_Z, BLK_C_QK, BLK_C_O,
 BLK_C_Z, BLK_D_Z, BLK_MISC) = range(11)
N_BLK = 11
BLK16_A_I, BLK16_D_Q = range(2)
N_BLK16 = 2
W_ROW_C_V = (N_BLK + N_BLK16) * D_GROUP
N_GATE = 4 * C_HEADS
W_ROW_GATE = W_ROW_C_V + D_GROUP
N_W_ROWS = W_ROW_GATE + N_GATE
MISC_K, MISC_V = 0, 128
C_ONES = 16


def _dot(a, b):
    return jnp.dot(a, b, preferred_element_type=F32)


def _dot_nt(a, b):
    return lax.dot_general(a, b, (((1,), (1,)), ((), ())), preferred_element_type=F32)


def _dot_tn(a, b):
    return lax.dot_general(a, b, (((0,), (0,)), ((), ())), preferred_element_type=F32)


def _sigmoid(x):
    return 1.0 / (1.0 + jnp.exp(-x))


def _silu(x):
    return x * _sigmoid(x)


def _log_sigmoid(x):
    return jnp.minimum(x, 0.0) - jnp.log(1.0 + jnp.exp(-jnp.abs(x)))


def _chunk_cumsum_lanes(x, reverse):
    n = x.shape[1]
    pos = lax.broadcasted_iota(jnp.int32, x.shape, 1) & (CHUNK - 1)
    s = 1
    while s < CHUNK:
        if reverse:
            x = x + jnp.where(pos < CHUNK - s, pltpu.roll(x, n - s, axis=1), 0.0)
        else:
            x = x + jnp.where(pos >= s, pltpu.roll(x, s, axis=1), 0.0)
        s *= 2
    return x


def _in_proj_kernel(x_ref, g_ref, w_ref, gb_ref, u_ref, u16_ref, gt_ref, vt_ref, *, layer):
    x = x_ref[...]
    ms = jnp.mean(x * x, axis=-1, keepdims=True)
    h = (x * lax.rsqrt(ms + EPS) * g_ref[layer:layer + 1, :]).astype(BF16)
    vg = _dot_nt(w_ref[W_ROW_C_V:W_ROW_GATE + N_GATE, :], h)
    vt_ref[...] = vg[:D_GROUP, :].astype(BF16)
    gates = vg[D_GROUP:, :] + gb_ref[layer]
    n_dir = 2 * C_HEADS
    log_f = _log_sigmoid(gates[n_dir:, :])
    is_fwd = lax.broadcasted_iota(jnp.int32, log_f.shape, 0) < C_HEADS
    cum = jnp.where(is_fwd, _chunk_cumsum_lanes(log_f, False), _chunk_cumsum_lanes(log_f, True))
    gt_ref[...] = jnp.concatenate([gates[:n_dir, :], cum], axis=0)
    for j in range(N_BLK):
        u_ref[:, j * D_GROUP:(j + 1) * D_GROUP] = _dot_nt(h, w_ref[j * D_GROUP:(j + 1) * D_GROUP, :])
    for j in range(N_BLK16):
        rows = slice((N_BLK + j) * D_GROUP, (N_BLK + j + 1) * D_GROUP)
        u16_ref[:, j * D_GROUP:(j + 1) * D_GROUP] = _dot_nt(h, w_ref[rows, :]).astype(BF16)


def _in_proj(xf, g_all, w_all, gate_b, layer, batch):
    rows, d = xf.shape
    tm = IN_ROW_TILE
    per_batch = rows // batch // tm
    return pl.pallas_call(
        functools.partial(_in_proj_kernel, layer=layer),
        out_shape=(jax.ShapeDtypeStruct((rows, N_BLK * D_GROUP), F32),
                   jax.ShapeDtypeStruct((rows, N_BLK16 * D_GROUP), BF16),
                   jax.ShapeDtypeStruct((batch, N_GATE, rows // batch), F32),
                   jax.ShapeDtypeStruct((batch, D_GROUP, rows // batch), BF16)),
        grid=(rows // tm,),
        in_specs=[
            pl.BlockSpec((tm, d), lambda i: (i, 0)),
            pl.BlockSpec(g_all.shape, lambda i: (0, 0)),
            pl.BlockSpec((None, N_W_ROWS, d), lambda i: (layer, 0, 0), pipeline_mode=pl.Buffered(1)),
            pl.BlockSpec(gate_b.shape, lambda i: (0, 0, 0)),
        ],
        out_specs=(
            pl.BlockSpec((tm, N_BLK * D_GROUP), lambda i: (i, 0)),
            pl.BlockSpec((tm, N_BLK16 * D_GROUP), lambda i: (i, 0)),
            pl.BlockSpec((None, N_GATE, tm), lambda i: (i // per_batch, 0, i % per_batch)),
            pl.BlockSpec((None, D_GROUP, tm), lambda i: (i // per_batch, 0, i % per_batch)),
        ),
        compiler_params=pltpu.CompilerParams(
            dimension_semantics=("parallel",), vmem_limit_bytes=IN_PROJ_VMEM_LIMIT),
        name="in_proj",
    )(xf, g_all, w_all, gate_b)


def _w_in_segments():
    names = ("a_q", "a_i", "a_ff", "a_fb", "a_z", "b_q", "b_k", "b_v", "b_z",
             "c_q", "c_k", "c_v", "c_o", "c_z", "c_ig", "c_fg", "d_q", "d_z")
    src = dict(zip(names, zip((0,) + SPLIT_POINTS, IN_SIZES)))
    blocks = {BLK_A_Q: ("a_q",), BLK_A_FF: ("a_ff",), BLK_A_FB: ("a_fb",), BLK_A_Z: ("a_z",),
              BLK_B_Q: ("b_q",), BLK_B_Z: ("b_z",), BLK_C_QK: ("c_q", "c_k"), BLK_C_O: ("c_o",),
              BLK_C_Z: ("c_z",), BLK_D_Z: ("d_z",), BLK_MISC: ("b_k", "b_v"),
              N_BLK + BLK16_A_I: ("a_i",), N_BLK + BLK16_D_Q: ("d_q",)}
    placed = []
    for blk, group in sorted(blocks.items()):
        dst = blk * D_GROUP
        for name in group:
            placed.append((name, dst))
            dst += src[name][1]
    placed += [("c_v", W_ROW_C_V), ("c_ig", W_ROW_GATE), ("c_fg", W_ROW_GATE + src["c_ig"][1])]
    segs, gaps, end = [], [], 0
    for name, dst in placed:
        off, width = src[name]
        if dst > end:
            gaps.append((end, dst))
        if segs and segs[-1][0] + segs[-1][1] == off and segs[-1][2] + segs[-1][1] == dst:
            segs[-1] = (segs[-1][0], segs[-1][1] + width, segs[-1][2])
        else:
            segs.append((off, width, dst))
        end = dst + width
    assert end == N_W_ROWS
    return tuple(segs), tuple(gaps)


def _w_prep_kernel(wt_ref, wc_ref):
    segs, gaps = _w_in_segments()
    for off, width, dst in segs:
        wc_ref[dst:dst + width, :] = wt_ref[off:off + width, :].astype(BF16)
    for lo, hi in gaps:
        wc_ref[lo:hi, :] = jnp.zeros((hi - lo, wc_ref.shape[1]), BF16)


def _prep_w_in(w):
    depth, d, d_in = w.shape
    wt = jnp.swapaxes(w, 1, 2)
    tk = W_PREP_TILE
    return pl.pallas_call(
        _w_prep_kernel,
        out_shape=jax.ShapeDtypeStruct((depth, N_W_ROWS, d), BF16),
        grid=(depth, d // tk),
        in_specs=[pl.BlockSpec((None, d_in, tk), lambda l, i: (l, 0, i))],
        out_specs=pl.BlockSpec((None, N_W_ROWS, tk), lambda l, i: (l, 0, i)),
        compiler_params=pltpu.CompilerParams(
            dimension_semantics=("parallel", "parallel"), vmem_limit_bytes=VMEM_LIMIT),
        name="w_prep",
    )(wt)


def _level_ref(b, h, reverse):
    n_rows, n = b.shape
    blk = 2 * h
    r0 = h if reverse else h - 1
    if blk >= SUBLANE:
        pieces = [jnp.broadcast_to(b[s + r0:s + r0 + 1, :], (blk, n)) for s in range(0, n_rows, blk)]
        return pieces[0] if len(pieces) == 1 else jnp.concatenate(pieces, axis=0)
    res = lax.broadcasted_iota(jnp.int32, b.shape, 0) & (blk - 1)
    out = None
    for m in range(blk):
        shift = (m - r0) % n_rows
        cand = b if shift == 0 else pltpu.roll(b, shift, axis=0)
        out = cand if out is None else jnp.where(res == m, cand, out)
    return out


def _hgrn_lower_bound(lbl_ref, layer, depth):
    rows = [lbl_ref[j:j + 1, :] for j in range(depth)]
    mx = functools.reduce(jnp.maximum, rows)
    es = [jnp.exp(r - mx) for r in rows]
    tot = functools.reduce(lambda a, c: a + c, es)
    lb = jnp.zeros_like(rows[0])
    for j in range(1, layer + 1):
        lb = lb + es[j] / tot
    return lb


def _cumsum_rows_mxu(x, reverse):
    n = x.shape[0]
    r = lax.broadcasted_iota(jnp.int32, (n, n), 0)
    c = lax.broadcasted_iota(jnp.int32, (n, n), 1)
    tri = jnp.where((c >= r) if reverse else (c <= r), 1.0, 0.0).astype(BF16)
    hi = x.astype(BF16)
    r1 = x - hi.astype(F32)
    mid = r1.astype(BF16)
    lo = (r1 - mid.astype(F32)).astype(BF16)
    return _dot(jnp.concatenate([tri, tri, tri], axis=1), jnp.concatenate([hi, mid, lo], axis=0))


def _hgrn_level_operand(q, k, f, b2, h, reverse):
    n_rows = q.shape[0]
    blk = 2 * h
    r0 = h if reverse else h - 1
    if h >= SUBLANE:
        pieces = []
        for s0 in range(0, n_rows, blk):
            ref = b2[s0 + r0:s0 + r0 + 1, :]
            lo, up = slice(s0, s0 + h), slice(s0 + h, s0 + blk)
            if reverse:
                pieces += [q[lo] * jnp.exp2(b2[lo] - ref), k[up] * jnp.exp2(ref - b2[up])]
            else:
                pieces += [k[lo] * jnp.exp2(ref - b2[lo]), q[up] * jnp.exp2(b2[up] - ref)]
        return jnp.concatenate(pieces, axis=0)
    row = lax.broadcasted_iota(jnp.int32, q.shape, 0)
    is_q = ((row & h) == 0) if reverse else ((row & h) != 0)
    qk = jnp.where(is_q, q, k)
    if h == 1:
        return jnp.where(is_q, q * f, k)
    if h == 2:
        res = row & 3
        f_dn = pltpu.roll(f, 1, axis=0)
        f_up = pltpu.roll(f, n_rows - 1, axis=0)
        if reverse:
            e = jnp.where(res == 0, f * f_up, jnp.where(res == 1, f, jnp.where(res == 2, 1.0, f_dn)))
        else:
            e = jnp.where(res == 0, f_up, jnp.where(res == 1, 1.0, jnp.where(res == 2, f, f_dn * f)))
        return qk * e
    return qk * jnp.exp2(-jnp.abs(b2 - _level_ref(b2, h, reverse)))


def _hgrn_kernel(*refs, layer, depth, reverse, final):
    if final:
        q_ref, v_ref, zg_ref, lbl_ref, z_ref, ob_ref, ng_ref, out_ref, st_ref = refs
    else:
        q_ref, v_ref, zg_ref, lbl_ref, out_ref, st_ref = refs

    @pl.when(pl.program_id(1) == 0)
    def _():
        st_ref[...] = jnp.zeros_like(st_ref)

    rows = range(q_ref.shape[0])
    n_rows = q_ref.shape[1]
    ti = lax.broadcasted_iota(jnp.int32, (n_rows, n_rows), 0)
    si = lax.broadcasted_iota(jnp.int32, (n_rows, n_rows), 1)
    ahead = (ti < si) if reverse else (ti > si)
    level = jnp.where(ahead, 31 - lax.clz(ti ^ si), jnp.where(ti == si, -1, -2))
    preps, bounds = zip(*[_hgrn_prep(q_ref.at[g], zg_ref.at[g], lbl_ref, layer, depth, reverse)
                          for g in rows])

    def run(fast):
        outs = [[] for _ in rows]
        for hd in range(A_HEADS):
            sl = slice(hd * A_HD, (hd + 1) * A_HD)
            heads = [tuple(a[:, sl] for a in preps[g]) for g in rows]
            if fast:
                atts = [_hgrn_scores_fast(heads[g], level) for g in rows]
            else:
                atts = [_hgrn_scores(heads[g], level, n_rows, None, reverse) for g in rows]
                h = n_rows // 2
                while h >= 1:
                    atts = [_hgrn_scores(heads[g], level, h, atts[g], reverse) for g in rows]
                    h //= 2
            for g in rows:
                outs[g].append(_hgrn_apply(heads[g], atts[g], v_ref.at[g, :, sl],
                                           st_ref.at[g, hd], reverse))
        for g in rows:
            o = jnp.concatenate(outs[g], axis=1)
            if final:
                _hgrn_finish(o, z_ref.at[g], ob_ref.at[g], ng_ref.at[layer:layer + 1],
                             out_ref.at[g])
            else:
                out_ref[g] = o

    safe = jnp.max(functools.reduce(jnp.maximum, bounds)) <= HGRN_SAFE_LOG2

    @pl.when(safe)
    def _():
        run(True)

    @pl.when(jnp.logical_not(safe))
    def _():
        run(False)


def _hgrn_prep(q_ref, zg_ref, lbl_ref, layer, depth, reverse):
    n_rows = q_ref.shape[0]
    q = q_ref[...]
    zg = zg_ref[...]
    e = jnp.exp(-jnp.abs(zg))
    r = 1.0 / (1.0 + e)
    er = e * r
    pos = zg >= 0.0
    f = jnp.where(pos, r, er)
    k = jnp.where(pos, er, r)
    if layer > 0:
        lb = _hgrn_lower_bound(lbl_ref, layer, depth)
        f = lb + (1.0 - lb) * f
        k = (1.0 - lb) * k
    g2 = jnp.log2(f)
    half = HGRN_SUB // 2
    bound = None
    for s0 in range(0, n_rows, HGRN_SUB):
        first = jnp.sum(g2[s0:s0 + half], axis=0, keepdims=True) + g2[s0 + half:s0 + half + 1]
        second = jnp.sum(g2[s0 + half:s0 + HGRN_SUB], axis=0, keepdims=True)
        worst = -jnp.minimum(first, second)
        bound = worst if bound is None else jnp.maximum(bound, worst)
    b2 = _cumsum_rows_mxu(g2, reverse)
    return (q, k, f, b2), bound


def _hgrn_scores(head, level, h, att, reverse):
    q, k, f, b2 = head
    if att is None:
        return jnp.where(level == -1, _dot_nt(q.astype(BF16), k.astype(BF16)), 0.0)
    c = _hgrn_level_operand(q, k, f, b2, h, reverse).astype(BF16)
    return jnp.where(level == (h.bit_length() - 1), _dot_nt(c, c), att)


def _hgrn_scores_fast(head, level):
    q, k, _, b2 = head
    n_rows = q.shape[0]
    blocks = []
    for s0 in range(0, n_rows, HGRN_SUB):
        rows = slice(s0, s0 + HGRN_SUB)
        ref = b2[s0 + HGRN_SUB // 2:s0 + HGRN_SUB // 2 + 1, :]
        qs = (q[rows] * jnp.exp2(b2[rows] - ref)).astype(BF16)
        ks = (k * jnp.exp2(jnp.minimum(ref - b2, HGRN_SAFE_LOG2))).astype(BF16)
        blocks.append(_dot_nt(qs, ks))
    return jnp.where(level >= -1, jnp.concatenate(blocks, axis=0), 0.0)


def _hgrn_apply(head, att, v_ref, st_ref, reverse):
    q, k, _, b2 = head
    n_rows = q.shape[0]
    tot = b2[0:1, :] if reverse else b2[n_rows - 1:n_rows, :]
    v_bf = v_ref[...].astype(BF16)
    q_in = (q * jnp.exp2(b2)).astype(BF16)
    k_out = (k * jnp.exp2(tot - b2)).astype(BF16)
    st = st_ref[...]
    o = _dot(att.astype(BF16), v_bf) + _dot(q_in, st.astype(BF16))
    dec_col = jnp.broadcast_to(jnp.exp2(tot), (SUBLANE, tot.shape[1])).T[:, 0:1]
    st_ref[...] = dec_col * st + _dot_tn(k_out, v_bf)
    return o


def _hgrn_finish(o, z_ref, ob_ref, ng_ref, out_ref):
    o = o + ob_ref[...]
    ys = []
    for hd in range(A_HEADS):
        sl = slice(hd * A_HD, (hd + 1) * A_HD)
        oh = o[:, sl]
        ms = jnp.mean(oh * oh, axis=-1, keepdims=True)
        ys.append(oh * lax.rsqrt(ms + EPS) * ng_ref[:, sl])
    y = jnp.concatenate(ys, axis=1) * _silu(z_ref[...])
    out_ref[...] = y.astype(out_ref.dtype)


def _chunk_index(c, n_chunks, reverse):
    return n_chunks - 1 - c if reverse else c


def _hgrn(u3, u16, lb_logits, layer, *, reverse, ob=None, norm_g=None):
    batch, seq, _ = u3.shape
    n_chunks = seq // CHUNK
    depth = lb_logits.shape[0]
    final = ob is not None
    grp = REC_GROUP

    def ublk(blk):
        return pl.BlockSpec((grp, CHUNK, D_GROUP),
                            lambda bg, c: (bg, _chunk_index(c, n_chunks, reverse), blk))

    row_spec = ublk(0)
    in_specs = [ublk(BLK_A_Q), ublk(BLK16_A_I), ublk(BLK_A_FB if reverse else BLK_A_FF),
                pl.BlockSpec((depth, D_GROUP), lambda bg, c: (0, 0))]
    args = [u3, u16, u3, lb_logits]
    if final:
        in_specs += [ublk(BLK_A_Z), row_spec, pl.BlockSpec(norm_g.shape, lambda bg, c: (0, 0))]
        args += [u3, ob, norm_g]
    return pl.pallas_call(
        functools.partial(_hgrn_kernel, layer=layer, depth=depth, reverse=reverse, final=final),
        out_shape=jax.ShapeDtypeStruct((batch, seq, D_GROUP), BF16 if final else F32),
        grid=(batch // grp, n_chunks),
        in_specs=in_specs,
        out_specs=row_spec,
        scratch_shapes=[pltpu.VMEM((grp, A_HEADS, A_HD, A_HD), F32)],
        compiler_params=pltpu.CompilerParams(
            dimension_semantics=("parallel", "arbitrary"), vmem_limit_bytes=VMEM_LIMIT),
        name="hgrn_fwd" if final else "hgrn_bwd",
    )(*args)


def _split3(x):
    hi = x.astype(BF16).astype(F32)
    r1 = x - hi
    mid = r1.astype(BF16).astype(F32)
    lo = (r1 - mid).astype(BF16).astype(F32)
    return hi, mid, lo


def _conv_silu_qk(qkp_ref, qkc_ref, qkn_ref, cw_ref, reverse):
    n_rows = qkc_ref.shape[0]
    c = pl.program_id(1)
    n_chunks = pl.num_programs(1)
    cc = _chunk_index(c, n_chunks, reverse)
    halo = SUBLANE
    prev = qkp_ref[n_rows - halo:n_rows, :] * (cc > 0).astype(F32)
    nxt = qkn_ref[0:halo, :] * (cc < n_chunks - 1).astype(F32)
    xcat = jnp.concatenate([prev, qkc_ref[...], nxt], axis=0)
    acc = None
    for j in range(CONV_W):
        off = halo + j - CONV_W // 2
        term = cw_ref[j:j + 1, :] * xcat[off:off + n_rows, :]
        acc = term if acc is None else acc + term
    qk = _silu(acc)
    n_qk = C_HEADS * C_QK
    return qk[:, n_qk:].astype(BF16), (qk[:, :n_qk] * (C_QK ** -0.5)).T.astype(BF16)


def _mlstm_kernel(*refs, layer, reverse, final):
    if final:
        (k_ref, qt_ref, v_ref, gt_ref, og_ref, z_ref, hb_ref, ng_ref,
         out_ref, cst_ref, m_ref) = refs
    else:
        (qkp_ref, qkc_ref, qkn_ref, cw_ref, v_ref, gt_ref,
         out_ref, ko_ref, qto_ref, cst_ref, m_ref) = refs

    @pl.when(pl.program_id(1) == 0)
    def _():
        cst_ref[...] = jnp.zeros_like(cst_ref)
        m_ref[...] = jnp.zeros_like(m_ref)

    rows = range(v_ref.shape[0])
    if final:
        qks = [(k_ref[g], qt_ref[g]) for g in rows]
    else:
        qks = [_conv_silu_qk(qkp_ref.at[g], qkc_ref.at[g], qkn_ref.at[g], cw_ref.at[layer], reverse)
               for g in rows]
        for g in rows:
            ko_ref[g], qto_ref[g] = qks[g]
    gates = [_mlstm_gates(gt_ref.at[g], m_ref.at[g], reverse) for g in rows]
    outs = [[] for _ in rows]
    for hd in range(C_HEADS):
        weights = [_mlstm_weights(hd, qks[g], gates[g], reverse) for g in rows]
        for g in rows:
            h_t = _mlstm_apply(hd, qks[g], v_ref.at[g], gates[g], weights[g], cst_ref.at[g], reverse)
            if final:
                outs[g].append((h_t + hb_ref[g, hd * C_V:(hd + 1) * C_V, :]).T)
            else:
                out_ref[g, hd * C_V:(hd + 1) * C_V, :] = h_t
    if final:
        for g in rows:
            _mlstm_finish(jnp.concatenate(outs[g], axis=1), og_ref.at[g], z_ref.at[g],
                          ng_ref.at[layer:layer + 1], out_ref.at[g])


def _mlstm_gates(gt_ref, m_ref, reverse):
    n_rows = gt_ref.shape[1]
    n_dir = 2 * C_HEADS

    li = gt_ref[0:n_dir, :]
    b = gt_ref[n_dir:, :]
    b_last = b[:, 0:1] if reverse else b[:, n_rows - 1:n_rows]
    m_prev = m_ref[...]
    d_inter = b + m_prev
    a = (b_last - b) + li
    m_new = jnp.maximum(b_last + m_prev, jnp.max(a, axis=1, keepdims=True))
    ws = jnp.exp(a - m_new)
    decay = jnp.exp((b_last + m_prev) - m_new)
    m_ref[...] = m_new

    one8 = jnp.ones((n_dir, n_rows), F32)
    lhs = jnp.concatenate(list(_split3(li - b)) + [one8, one8, one8], axis=0).T.astype(BF16)
    rhs_all = jnp.concatenate([one8, one8, one8] + list(_split3(b)), axis=0)
    return lhs, rhs_all, d_inter, ws, decay


def _mlstm_weights(hd, qk, gates, reverse):
    lhs, rhs_all, d_inter, _, _ = gates
    k, q_t = qk
    n_rows = k.shape[0]
    n_dir = 2 * C_HEADS
    gi = (C_HEADS if reverse else 0) + hd
    gate_row = lax.broadcasted_iota(jnp.int32, rhs_all.shape, 0) & (n_dir - 1)
    si = lax.broadcasted_iota(jnp.int32, (n_rows, n_rows), 0)
    ti = lax.broadcasted_iota(jnp.int32, (n_rows, n_rows), 1)
    causal = (si >= ti) if reverse else (si <= ti)

    sl = slice(hd * C_QK, (hd + 1) * C_QK)
    rhs = jnp.where(gate_row == gi, rhs_all, 0.0).astype(BF16)
    d = jnp.where(causal, _dot(lhs, rhs), NEG)
    d_int = d_inter[gi:gi + 1, :]
    m_t = jnp.maximum(jnp.max(d, axis=0, keepdims=True), d_int)
    w = (_dot(k[:, sl], q_t[sl, :]) * jnp.exp(d - m_t)).astype(BF16)
    return w, m_t, jnp.exp(d_int - m_t)


def _mlstm_apply(hd, qk, vt_ref, gates, weights, cst_ref, reverse):
    _, _, _, ws, decay = gates
    w, m_t, g = weights
    k, q_t = qk
    n_rows = k.shape[0]
    gi = (C_HEADS if reverse else 0) + hd
    sl = slice(hd * C_QK, (hd + 1) * C_QK)
    v_t = vt_ref[hd * C_V:(hd + 1) * C_V, :]
    v_ext = jnp.concatenate([v_t, jnp.ones((C_ONES, n_rows), BF16)], axis=0)
    ws_row = ws[gi:gi + 1, :]
    v_scaled = jnp.concatenate([(v_t.astype(F32) * ws_row).astype(BF16),
                                jnp.broadcast_to(ws_row, (C_ONES, n_rows)).astype(BF16)], axis=0)
    cst = cst_ref[hd]
    num = _dot(v_ext, w) + g * _dot(cst.astype(BF16), q_t[sl, :])
    den = jnp.maximum(jnp.abs(num[C_V:C_V + 1, :]), jnp.exp(-m_t))
    cst_ref[hd] = decay[gi:gi + 1, :C_QK] * cst + _dot(v_scaled, k[:, sl])
    return num[:C_V, :] / den


def _mlstm_finish(hsum, og_ref, z_ref, ng_ref, out_ref):
    hsum = hsum * _sigmoid(og_ref[...])
    ys = []
    for hd in range(C_HEADS):
        sl = slice(hd * C_V, (hd + 1) * C_V)
        hh = hsum[:, sl]
        mu = jnp.mean(hh, axis=-1, keepdims=True)
        var = jnp.mean(jnp.square(hh - mu), axis=-1, keepdims=True)
        ys.append((hh - mu) * lax.rsqrt(var + EPS) * ng_ref[:, sl])
    y = jnp.concatenate(ys, axis=1) * _silu(z_ref[...])
    out_ref[...] = y.astype(out_ref.dtype)


def _mlstm(u3, gt, vt, conv_w, layer, *, reverse, qk=None, hb=None, norm_g=None):
    batch, seq, _ = u3.shape
    n_chunks = seq // CHUNK
    final = hb is not None
    grp = REC_GROUP
    assert CHUNK == LANE

    def chunk(c, delta=0):
        return jnp.clip(_chunk_index(c, n_chunks, reverse) + delta, 0, n_chunks - 1)

    def ublk(blk, delta=0):
        return pl.BlockSpec((grp, CHUNK, D_GROUP), lambda bg, c: (bg, chunk(c, delta), blk))

    row_spec = ublk(0)
    t_spec = lambda ch: pl.BlockSpec((grp, ch, CHUNK), lambda bg, c: (bg, 0, chunk(c)))
    gate_specs = [t_spec(N_GATE)]
    gate_args = [gt]
    n_qk = C_HEADS * C_QK
    k_spec = pl.BlockSpec((grp, CHUNK, n_qk), lambda bg, c: (bg, chunk(c), 0))
    if final:
        in_specs = [k_spec, t_spec(n_qk), t_spec(D_GROUP)] + gate_specs + [
            ublk(BLK_C_O), ublk(BLK_C_Z), t_spec(D_GROUP),
            pl.BlockSpec(norm_g.shape, lambda bg, c: (0, 0))]
        args = [qk[0], qk[1], vt] + gate_args + [u3, u3, hb, norm_g]
        out_shape = jax.ShapeDtypeStruct((batch, seq, D_GROUP), BF16)
        out_specs = row_spec
    else:
        in_specs = [ublk(BLK_C_QK, -1), ublk(BLK_C_QK), ublk(BLK_C_QK, 1),
                    pl.BlockSpec(conv_w.shape, lambda bg, c: (0, 0, 0)),
                    t_spec(D_GROUP)] + gate_specs
        args = [u3, u3, u3, conv_w, vt] + gate_args
        out_shape = (jax.ShapeDtypeStruct((batch, D_GROUP, seq), F32),
                     jax.ShapeDtypeStruct((batch, seq, n_qk), BF16),
                     jax.ShapeDtypeStruct((batch, n_qk, seq), BF16))
        out_specs = (t_spec(D_GROUP), k_spec, t_spec(n_qk))
    return pl.pallas_call(
        functools.partial(_mlstm_kernel, layer=layer, reverse=reverse, final=final),
        out_shape=out_shape,
        grid=(batch // grp, n_chunks),
        in_specs=in_specs,
        out_specs=out_specs,
        scratch_shapes=[pltpu.VMEM((grp, C_HEADS, C_V + C_ONES, C_QK), F32),
                        pltpu.VMEM((grp, 2 * C_HEADS, CHUNK), F32)],
        compiler_params=pltpu.CompilerParams(
            dimension_semantics=("parallel", "arbitrary"), vmem_limit_bytes=VMEM_LIMIT),
        name="mlstm_fwd" if final else "mlstm_bwd",
    )(*args)


def _rope_table_kernel(pos_ref, invf_ref, cos_ref, sin_ref):
    ang = invf_ref[...] * pos_ref[...].astype(F32)
    n_rows = ang.shape[1]
    half = ROPE_DIM // 2
    r = lax.broadcasted_iota(jnp.int32, ang.shape, 0)
    sn = jnp.sin(ang)
    sn = jnp.where(r < half, -sn, sn)
    pad = jnp.zeros((LANE - ROPE_DIM, n_rows), F32)

    def expand(x):
        t = jnp.concatenate([x, pad], axis=0).T
        return t + pltpu.roll(t, B_HD, axis=1)

    lane = lax.broadcasted_iota(jnp.int32, (n_rows, LANE), 1) & (B_HD - 1)
    cos_ref[...] = jnp.where(lane < ROPE_DIM, expand(jnp.cos(ang)), 1.0)
    sin_ref[...] = expand(sn)


def _rope_tables(positions):
    rows = positions.size
    half = ROPE_DIM // 2
    inv_freq = ROPE_THETA ** (-jnp.arange(0, ROPE_DIM, 2, dtype=F32) / ROPE_DIM)
    invf = jnp.concatenate([inv_freq, inv_freq]).reshape(ROPE_DIM, 1)
    tm = ROW_TILE
    return pl.pallas_call(
        _rope_table_kernel,
        out_shape=(jax.ShapeDtypeStruct((rows, LANE), F32),) * 2,
        grid=(rows // tm,),
        in_specs=[pl.BlockSpec((1, tm), lambda i: (0, i)),
                  pl.BlockSpec((ROPE_DIM, 1), lambda i: (0, 0))],
        out_specs=(pl.BlockSpec((tm, LANE), lambda i: (i, 0)),) * 2,
        compiler_params=pltpu.CompilerParams(dimension_semantics=("parallel",)),
        name="rope_tables",
    )(positions.reshape(1, rows), invf)


def _rope(x, cos, sin):
    width = x.shape[1]
    if width > LANE:
        cos = jnp.concatenate([cos] * (width // LANE), axis=1)
        sin = jnp.concatenate([sin] * (width // LANE), axis=1)
    half = ROPE_DIM // 2
    r = lax.broadcasted_iota(jnp.int32, x.shape, 1) & (B_HD - 1)
    partner = jnp.where(r < half, pltpu.roll(x, width - half, axis=1), pltpu.roll(x, half, axis=1))
    return x * cos + partner * sin


def _swa_kernel(q_ref, z_ref, mp_ref, mc_ref, mn_ref, cp_ref, cc_ref, cn_ref,
                sp_ref, sc_ref, sn_ref, sink_ref, out_ref, *, layer):
    tile = q_ref.shape[0]
    blk = mp_ref.shape[0]
    n_sub = tile // blk
    i = pl.program_id(1)
    n_tiles = pl.num_programs(1)
    kv_w = B_KV_HEADS * B_HD
    grp = B_HEADS // B_KV_HEADS

    q = (_rope(q_ref[...], cc_ref[...], sc_ref[...]) * (B_HD ** -0.5)).astype(BF16)
    k = jnp.concatenate([
        _rope(mp_ref[:, MISC_K:MISC_K + kv_w], cp_ref[...], sp_ref[...]),
        _rope(mc_ref[:, MISC_K:MISC_K + kv_w], cc_ref[...], sc_ref[...]),
        _rope(mn_ref[:, MISC_K:MISC_K + kv_w], cn_ref[...], sn_ref[...])], axis=0).astype(BF16)
    v = jnp.concatenate([mp_ref[:, MISC_V:MISC_V + kv_w], mc_ref[:, MISC_V:MISC_V + kv_w],
                         mn_ref[:, MISC_V:MISC_V + kv_w]], axis=0)
    n_keys = v.shape[0]

    kr = lax.broadcasted_iota(jnp.int32, (3 * blk, blk), 0)
    qc = lax.broadcasted_iota(jnp.int32, (3 * blk, blk), 1)
    in_window = jnp.abs(qc - (kr - blk)) <= WINDOW
    biases = []
    for j in range(n_sub):
        valid = in_window
        if j == 0:
            valid = valid & (kr >= jnp.where(i > 0, 0, blk))
        if j == n_sub - 1:
            valid = valid & (kr < jnp.where(i < n_tiles - 1, 3 * blk, 2 * blk))
        biases.append(jnp.where(valid, 0.0, NEG))

    ones = jnp.ones((n_keys, B_HD), F32)
    vts = [jnp.concatenate([v[:, kvh * B_HD:(kvh + 1) * B_HD], ones], axis=1).T.astype(BF16)
           for kvh in range(B_KV_HEADS)]

    tiles = [[] for _ in range(n_sub)]
    pairs = [[] for _ in range(n_sub)]
    for hq in range(B_HEADS):
        kvh = hq // grp
        ksl = slice(kvh * B_HD, (kvh + 1) * B_HD)
        sk = sink_ref[layer:layer + 1, hq:hq + 1]
        for j in range(n_sub):
            keys = slice(j * blk, (j + 3) * blk)
            qh = q[j * blk:(j + 1) * blk, hq * B_HD:(hq + 1) * B_HD]
            s = _dot_nt(k[keys, ksl], qh)
            s = jnp.concatenate([s[:blk] + biases[j][:blk], s[blk:2 * blk],
                                 s[2 * blk:] + biases[j][2 * blk:]], axis=0)
            m = jnp.maximum(jnp.max(s, axis=0, keepdims=True), sk)
            p = jnp.exp(s - m).astype(BF16)
            o = _dot(vts[kvh][:, keys], p)
            denom = o[B_HD:B_HD + 1, :] + jnp.exp(sk - m)
            pairs[j].append(o[:B_HD, :] / denom)
            if len(pairs[j]) == LANE // B_HD:
                tiles[j].append(jnp.concatenate(pairs[j], axis=0).T)
                pairs[j] = []
    for j in range(n_sub):
        rows = slice(j * blk, (j + 1) * blk)
        y = jnp.concatenate(tiles[j], axis=1) * _silu(z_ref[rows, :])
        out_ref[rows, :] = y.astype(out_ref.dtype)


def _swa(u, cos_t, sin_t, sink, layer, batch):
    rows = u.shape[0]
    n_blocks = rows // batch // CHUNK
    n_sub = SWA_TILE // CHUNK
    n_tiles = n_blocks // n_sub

    def tile_spec(width, blk):
        return pl.BlockSpec((SWA_TILE, width), lambda bi, i: (bi * n_tiles + i, blk))

    def halo_spec(width, blk, after):
        def index(bi, i):
            nb = (i + 1) * n_sub if after else i * n_sub - 1
            return (bi * n_blocks + jnp.clip(nb, 0, n_blocks - 1), blk)
        return pl.BlockSpec((CHUNK, width), index)

    def table_specs():
        return [halo_spec(LANE, 0, False), tile_spec(LANE, 0), halo_spec(LANE, 0, True)]

    return pl.pallas_call(
        functools.partial(_swa_kernel, layer=layer),
        out_shape=jax.ShapeDtypeStruct((rows, D_GROUP), BF16),
        grid=(batch, n_tiles),
        in_specs=[tile_spec(D_GROUP, BLK_B_Q), tile_spec(D_GROUP, BLK_B_Z),
                  halo_spec(D_GROUP, BLK_MISC, False), tile_spec(D_GROUP, BLK_MISC),
                  halo_spec(D_GROUP, BLK_MISC, True)] + table_specs() + table_specs() + [
                  pl.BlockSpec(sink.shape, lambda bi, i: (0, 0))],
        out_specs=tile_spec(D_GROUP, 0),
        compiler_params=pltpu.CompilerParams(
            dimension_semantics=("parallel", "parallel"), vmem_limit_bytes=VMEM_LIMIT),
        name="window_attn",
    )(u, u, u, u, u, cos_t, cos_t, cos_t, sin_t, sin_t, sin_t, sink)


def _mem_kv_kernel(mem_ref, g_ref, w_ref, kt_ref, v_ref, *, layer):
    x = mem_ref[...]
    ms = jnp.mean(x * x, axis=-1, keepdims=True)
    h = (x * lax.rsqrt(ms + EPS) * g_ref[layer:layer + 1, :]).astype(BF16)
    kv = _dot(h, w_ref[...].astype(BF16))
    kt_ref[...] = kv[:, :D_GROUP].T.astype(kt_ref.dtype)
    v_ref[...] = kv[:, D_GROUP:].astype(v_ref.dtype)


def _mem_kv(mem, g_all, w_all, layer):
    batch, m_len, d = mem.shape
    n_out = w_all.shape[2]
    return pl.pallas_call(
        functools.partial(_mem_kv_kernel, layer=layer),
        out_shape=(jax.ShapeDtypeStruct((batch, D_GROUP, m_len), BF16),
                   jax.ShapeDtypeStruct((batch, m_len, D_GROUP), BF16)),
        grid=(batch,),
        in_specs=[pl.BlockSpec((None, m_len, d), lambda bi: (bi, 0, 0)),
                  pl.BlockSpec(g_all.shape, lambda bi: (0, 0)),
                  pl.BlockSpec((None, d, n_out), lambda bi: (layer, 0, 0))],
        out_specs=(pl.BlockSpec((None, D_GROUP, m_len), lambda bi: (bi, 0, 0)),
                   pl.BlockSpec((None, m_len, D_GROUP), lambda bi: (bi, 0, 0))),
        compiler_params=pltpu.CompilerParams(
            dimension_semantics=("parallel",), vmem_limit_bytes=VMEM_LIMIT),
        name="mem_kv",
    )(mem, g_all, w_all)


def _mem_attn_out_kernel(q_ref, z_ref, kt_ref, v_ref, ya_ref, yb_ref, yc_ref, w_ref, x_ref, g_ref,
                         o_ref, *, final):
    q = q_ref[...].astype(BF16)
    outs = []
    for hd in range(D_HEADS):
        sl = slice(hd * D_HD, (hd + 1) * D_HD)
        s = _dot(q[:, sl], kt_ref[sl, :])
        m = jnp.max(s, axis=-1, keepdims=True)
        p = jnp.exp2((s - m) * (D_HD ** -0.5 * LOG2_E))
        den = jnp.sum(p, axis=-1, keepdims=True)
        outs.append(_dot(p.astype(BF16), v_ref[:, sl]) / den)
    yd = (jnp.concatenate(outs, axis=1) * _silu(z_ref[...])).astype(BF16)

    acc = x_ref[...]
    for i, y in enumerate((ya_ref[...], yb_ref[...], yc_ref[...], yd)):
        acc = acc + _dot(y, w_ref[i * D_GROUP:(i + 1) * D_GROUP, :].astype(BF16))
    if final:
        ms = jnp.mean(acc * acc, axis=-1, keepdims=True)
        acc = acc * lax.rsqrt(ms + EPS) * g_ref[...]
    o_ref[...] = acc


def _mem_attn_out(u, u16, kt, v, ya, yb, yc, w_all, xf, g, layer, batch, final):
    rows, d = xf.shape
    tm = ROW_TILE
    per_batch = rows // batch // tm
    row = lambda width, blk: pl.BlockSpec((tm, width), lambda bi, i: (bi * per_batch + i, blk))
    return pl.pallas_call(
        functools.partial(_mem_attn_out_kernel, final=final),
        out_shape=jax.ShapeDtypeStruct((rows, d), F32),
        grid=(batch, per_batch),
        in_specs=[row(D_GROUP, BLK16_D_Q), row(D_GROUP, BLK_D_Z),
                  pl.BlockSpec((None,) + kt.shape[1:], lambda bi, i: (bi, 0, 0)),
                  pl.BlockSpec((None,) + v.shape[1:], lambda bi, i: (bi, 0, 0)),
                  row(D_GROUP, 0), row(D_GROUP, 0), row(D_GROUP, 0),
                  pl.BlockSpec((None, 4 * D_GROUP, d), lambda bi, i: (layer, 0, 0)),
                  row(d, 0),
                  pl.BlockSpec((1, d), lambda bi, i: (0, 0))],
        out_specs=row(d, 0),
        compiler_params=pltpu.CompilerParams(
            dimension_semantics=("parallel", "parallel"), vmem_limit_bytes=VMEM_LIMIT),
        name="mem_attn_out",
    )(u16, u, kt, v, ya, yb, yc, w_all, xf, g.reshape(1, d))


def kernel(x, mem, positions, norm_g, w_in, hgrn_lb_logits, hgrn_norm_g, attn_sink, mlstm_conv_w,
           mlstm_gate_b, mlstm_norm_g, mem_norm_g, w_mem_kv, w_out, final_norm_g):
    batch, seq, d = x.shape
    depth = w_in.shape[0]
    assert all(seq % t == 0 for t in (CHUNK, ROW_TILE, IN_ROW_TILE, SWA_TILE))
    assert batch % REC_GROUP == 0
    xf = x.reshape(batch * seq, d)
    cos_t, sin_t = _rope_tables(positions)
    w_all = _prep_w_in(w_in)
    gate_b = mlstm_gate_b.reshape(depth, N_GATE, 1)
    for layer in range(depth):
        u, u16, gt, vt = _in_proj(xf, norm_g, w_all, gate_b, layer, batch)
        u3 = u.reshape(batch, seq, u.shape[1])
        u16_3 = u16.reshape(batch, seq, u16.shape[1])
        ob = _hgrn(u3, u16_3, hgrn_lb_logits, layer, reverse=True)
        ya = _hgrn(u3, u16_3, hgrn_lb_logits, layer, reverse=False, ob=ob, norm_g=hgrn_norm_g)
        yb = _swa(u, cos_t, sin_t, attn_sink, layer, batch)
        hb, k_c, qt_c = _mlstm(u3, gt, vt, mlstm_conv_w, layer, reverse=True)
        yc = _mlstm(u3, gt, vt, None, layer, reverse=False, qk=(k_c, qt_c), hb=hb,
                    norm_g=mlstm_norm_g)
        mem_kt, mem_v = _mem_kv(mem, mem_norm_g, w_mem_kv, layer)
        xf = _mem_attn_out(u, u16, mem_kt, mem_v, ya.reshape(batch * seq, D_GROUP), yb,
                           yc.reshape(batch * seq, D_GROUP), w_out, xf, final_norm_g, layer, batch,
                           final=(layer == depth - 1))
    return xf.reshape(batch, seq, d)
```

```python
import functools

import jax
import jax.numpy as jnp
import numpy as np
from jax import lax
from jax.experimental import pallas as pl
from jax.experimental.pallas import tpu as pltpu

F32 = jnp.float32
BF16 = jnp.bfloat16

D_GROUP = 512
A_HEADS, A_HD = 4, 128
B_HEADS, B_KV_HEADS, B_HD = 8, 2, 64
WINDOW = 128
ROPE_THETA = 500000.0
ROPE_DIM = 16
C_HEADS, C_QK, C_V = 4, 64, 128
CONV_W = 5
D_HEADS, D_HD = 4, 128
EPS = 1e-6
IN_SIZES = (
    D_GROUP, D_GROUP, D_GROUP, D_GROUP, D_GROUP,
    B_HEADS * B_HD, B_KV_HEADS * B_HD, B_KV_HEADS * B_HD, D_GROUP,
    C_HEADS * C_QK, C_HEADS * C_QK, C_HEADS * C_V, D_GROUP, D_GROUP,
    2 * C_HEADS, 2 * C_HEADS,
    D_GROUP, D_GROUP,
)
SPLIT_POINTS = tuple(int(s) for s in np.cumsum(IN_SIZES)[:-1])

LANE = 128
SUBLANE = 8
VMEM_LIMIT = 48 * 1024 * 1024

CHUNK = 128
ROW_TILE = 1024
IN_ROW_TILE = 512
IN_PROJ_VMEM_LIMIT = 56 * 1024 * 1024
W_PREP_TILE = 256
HGRN_SUB = 64
HGRN_SAFE_LOG2 = 100.0
SWA_TILE = 1024
REC_GROUP = 4
NEG = -1e30
LOG2_E = 1.4426950408889634

(BLK_A_Q, BLK_A_FF, BLK_A_FB, BLK_A_Z, BLK_B_Q, BLK_B_Z, BLK_C_QK, BLK_C_O,
 BLK_C_Z, BLK_D_Z, BLK_MISC) = range(11)
N_BLK = 11
BLK16_A_I, BLK16_D_Q = range(2)
N_BLK16 = 2
W_ROW_C_V = (N_BLK + N_BLK16) * D_GROUP
N_GATE = 4 * C_HEADS
W_ROW_GATE = W_ROW_C_V + D_GROUP
N_W_ROWS = W_ROW_GATE + N_GATE
MISC_K, MISC_V = 0, 128
C_ONES = 16


def _dot(a, b):
    return jnp.dot(a, b, preferred_element_type=F32)


def _dot_nt(a, b):
    return lax.dot_general(a, b, (((1,), (1,)), ((), ())), preferred_element_type=F32)


def _dot_tn(a, b):
    return lax.dot_general(a, b, (((0,), (0,)), ((), ())), preferred_element_type=F32)


def _sigmoid(x):
    return 1.0 / (1.0 + jnp.exp(-x))


def _silu(x):
    return x * _sigmoid(x)


def _log_sigmoid(x):
    return jnp.minimum(x, 0.0) - jnp.log(1.0 + jnp.exp(-jnp.abs(x)))


def _chunk_cumsum_lanes(x, reverse):
    n = x.shape[1]
    pos = lax.broadcasted_iota(jnp.int32, x.shape, 1) & (CHUNK - 1)
    s = 1
    while s < CHUNK:
        if reverse:
            x = x + jnp.where(pos < CHUNK - s, pltpu.roll(x, n - s, axis=1), 0.0)
        else:
            x = x + jnp.where(pos >= s, pltpu.roll(x, s, axis=1), 0.0)
        s *= 2
    return x


def _in_proj_kernel(x_ref, g_ref, w_ref, gb_ref, u_ref, u16_ref, gt_ref, vt_ref, *, layer):
    x = x_ref[...]
    ms = jnp.mean(x * x, axis=-1, keepdims=True)
    h = (x * lax.rsqrt(ms + EPS) * g_ref[layer:layer + 1, :]).astype(BF16)
    vg = _dot_nt(w_ref[W_ROW_C_V:W_ROW_GATE + N_GATE, :], h)
    vt_ref[...] = vg[:D_GROUP, :].astype(BF16)
    gates = vg[D_GROUP:, :] + gb_ref[layer]
    n_dir = 2 * C_HEADS
    log_f = _log_sigmoid(gates[n_dir:, :])
    is_fwd = lax.broadcasted_iota(jnp.int32, log_f.shape, 0) < C_HEADS
    cum = jnp.where(is_fwd, _chunk_cumsum_lanes(log_f, False), _chunk_cumsum_lanes(log_f, True))
    gt_ref[...] = jnp.concatenate([gates[:n_dir, :], cum], axis=0)
    for j in range(N_BLK):
        u_ref[:, j * D_GROUP:(j + 1) * D_GROUP] = _dot_nt(h, w_ref[j * D_GROUP:(j + 1) * D_GROUP, :])
    for j in range(N_BLK16):
        rows = slice((N_BLK + j) * D_GROUP, (N_BLK + j + 1) * D_GROUP)
        u16_ref[:, j * D_GROUP:(j + 1) * D_GROUP] = _dot_nt(h, w_ref[rows, :]).astype(BF16)


def _in_proj(xf, g_all, w_all, gate_b, layer, batch):
    rows, d = xf.shape
    tm = IN_ROW_TILE
    per_batch = rows // batch // tm
    return pl.pallas_call(
        functools.partial(_in_proj_kernel, layer=layer),
        out_shape=(jax.ShapeDtypeStruct((rows, N_BLK * D_GROUP), F32),
                   jax.ShapeDtypeStruct((rows, N_BLK16 * D_GROUP), BF16),
                   jax.ShapeDtypeStruct((batch, N_GATE, rows // batch), F32),
                   jax.ShapeDtypeStruct((batch, D_GROUP, rows // batch), BF16)),
        grid=(rows // tm,),
        in_specs=[
            pl.BlockSpec((tm, d), lambda i: (i, 0)),
            pl.BlockSpec(g_all.shape, lambda i: (0, 0)),
            pl.BlockSpec((None, N_W_ROWS, d), lambda i: (layer, 0, 0), pipeline_mode=pl.Buffered(1)),
            pl.BlockSpec(gate_b.shape, lambda i: (0, 0, 0)),
        ],
        out_specs=(
            pl.BlockSpec((tm, N_BLK * D_GROUP), lambda i: (i, 0)),
            pl.BlockSpec((tm, N_BLK16 * D_GROUP), lambda i: (i, 0)),
            pl.BlockSpec((None, N_GATE, tm), lambda i: (i // per_batch, 0, i % per_batch)),
            pl.BlockSpec((None, D_GROUP, tm), lambda i: (i // per_batch, 0, i % per_batch)),
        ),
        compiler_params=pltpu.CompilerParams(
            dimension_semantics=("parallel",), vmem_limit_bytes=IN_PROJ_VMEM_LIMIT),
        name="in_proj",
    )(xf, g_all, w_all, gate_b)


def _w_in_segments():
    names = ("a_q", "a_i", "a_ff", "a_fb", "a_z", "b_q", "b_k", "b_v", "b_z",
             "c_q", "c_k", "c_v", "c_o", "c_z", "c_ig", "c_fg", "d_q", "d_z")
    src = dict(zip(names, zip((0,) + SPLIT_POINTS, IN_SIZES)))
    blocks = {BLK_A_Q: ("a_q",), BLK_A_FF: ("a_ff",), BLK_A_FB: ("a_fb",), BLK_A_Z: ("a_z",),
              BLK_B_Q: ("b_q",), BLK_B_Z: ("b_z",), BLK_C_QK: ("c_q", "c_k"), BLK_C_O: ("c_o",),
              BLK_C_Z: ("c_z",), BLK_D_Z: ("d_z",), BLK_MISC: ("b_k", "b_v"),
              N_BLK + BLK16_A_I: ("a_i",), N_BLK + BLK16_D_Q: ("d_q",)}
    placed = []
    for blk, group in sorted(blocks.items()):
        dst = blk * D_GROUP
        for name in group:
            placed.append((name, dst))
            dst += src[name][1]
    placed += [("c_v", W_ROW_C_V), ("c_ig", W_ROW_GATE), ("c_fg", W_ROW_GATE + src["c_ig"][1])]
    segs, gaps, end = [], [], 0
    for name, dst in placed:
        off, width = src[name]
        if dst > end:
            gaps.append((end, dst))
        if segs and segs[-1][0] + segs[-1][1] == off and segs[-1][2] + segs[-1][1] == dst:
            segs[-1] = (segs[-1][0], segs[-1][1] + width, segs[-1][2])
        else:
            segs.append((off, width, dst))
        end = dst + width
    assert end == N_W_ROWS
    return tuple(segs), tuple(gaps)


def _w_prep_kernel(wt_ref, wc_ref):
    segs, gaps = _w_in_segments()
    for off, width, dst in segs:
        wc_ref[dst:dst + width, :] = wt_ref[off:off + width, :].astype(BF16)
    for lo, hi in gaps:
        wc_ref[lo:hi, :] = jnp.zeros((hi - lo, wc_ref.shape[1]), BF16)


def _prep_w_in(w):
    depth, d, d_in = w.shape
    wt = jnp.swapaxes(w, 1, 2)
    tk = W_PREP_TILE
    return pl.pallas_call(
        _w_prep_kernel,
        out_shape=jax.ShapeDtypeStruct((depth, N_W_ROWS, d), BF16),
        grid=(depth, d // tk),
        in_specs=[pl.BlockSpec((None, d_in, tk), lambda l, i: (l, 0, i))],
        out_specs=pl.BlockSpec((None, N_W_ROWS, tk), lambda l, i: (l, 0, i)),
        compiler_params=pltpu.CompilerParams(
            dimension_semantics=("parallel", "parallel"), vmem_limit_bytes=VMEM_LIMIT),
        name="w_prep",
    )(wt)


def _level_ref(b, h, reverse):
    n_rows, n = b.shape
    blk = 2 * h
    r0 = h if reverse else h - 1
    if blk >= SUBLANE:
        pieces = [jnp.broadcast_to(b[s + r0:s + r0 + 1, :], (blk, n)) for s in range(0, n_rows, blk)]
        return pieces[0] if len(pieces) == 1 else jnp.concatenate(pieces, axis=0)
    res = lax.broadcasted_iota(jnp.int32, b.shape, 0) & (blk - 1)
    out = None
    for m in range(blk):
        shift = (m - r0) % n_rows
        cand = b if shift == 0 else pltpu.roll(b, shift, axis=0)
        out = cand if out is None else jnp.where(res == m, cand, out)
    return out


def _hgrn_lower_bound(lbl_ref, layer, depth):
    rows = [lbl_ref[j:j + 1, :] for j in range(depth)]
    mx = functools.reduce(jnp.maximum, rows)
    es = [jnp.exp(r - mx) for r in rows]
    tot = functools.reduce(lambda a, c: a + c, es)
    lb = jnp.zeros_like(rows[0])
    for j in range(1, layer + 1):
        lb = lb + es[j] / tot
    return lb


def _cumsum_rows_mxu(x, reverse):
    n = x.shape[0]
    r = lax.broadcasted_iota(jnp.int32, (n, n), 0)
    c = lax.broadcasted_iota(jnp.int32, (n, n), 1)
    tri = jnp.where((c >= r) if reverse else (c <= r), 1.0, 0.0).astype(BF16)
    hi = x.astype(BF16)
    r1 = x - hi.astype(F32)
    mid = r1.astype(BF16)
    lo = (r1 - mid.astype(F32)).astype(BF16)
    return _dot(jnp.concatenate([tri, tri, tri], axis=1), jnp.concatenate([hi, mid, lo], axis=0))


def _hgrn_level_operand(q, k, f, b2, h, reverse):
    n_rows = q.shape[0]
    blk = 2 * h
    r0 = h if reverse else h - 1
    if h >= SUBLANE:
        pieces = []
        for s0 in range(0, n_rows, blk):
            ref = b2[s0 + r0:s0 + r0 + 1, :]
            lo, up = slice(s0, s0 + h), slice(s0 + h, s0 + blk)
            if reverse:
                pieces += [q[lo] * jnp.exp2(b2[lo] - ref), k[up] * jnp.exp2(ref - b2[up])]
            else:
                pieces += [k[lo] * jnp.exp2(ref - b2[lo]), q[up] * jnp.exp2(b2[up] - ref)]
        return jnp.concatenate(pieces, axis=0)
    row = lax.broadcasted_iota(jnp.int32, q.shape, 0)
    is_q = ((row & h) == 0) if reverse else ((row & h) != 0)
    qk = jnp.where(is_q, q, k)
    if h == 1:
        return jnp.where(is_q, q * f, k)
    if h == 2:
        res = row & 3
        f_dn = pltpu.roll(f, 1, axis=0)
        f_up = pltpu.roll(f, n_rows - 1, axis=0)
        if reverse:
            e = jnp.where(res == 0, f * f_up, jnp.where(res == 1, f, jnp.where(res == 2, 1.0, f_dn)))
        else:
            e = jnp.where(res == 0, f_up, jnp.where(res == 1, 1.0, jnp.where(res == 2, f, f_dn * f)))
        return qk * e
    return qk * jnp.exp2(-jnp.abs(b2 - _level_ref(b2, h, reverse)))


def _hgrn_kernel(*refs, layer, depth, reverse, final):
    if final:
        q_ref, v_ref, zg_ref, lbl_ref, z_ref, ob_ref, ng_ref, out_ref, st_ref = refs
    else:
        q_ref, v_ref, zg_ref, lbl_ref, out_ref, st_ref = refs

    @pl.when(pl.program_id(1) == 0)
    def _():
        st_ref[...] = jnp.zeros_like(st_ref)

    rows = range(q_ref.shape[0])
    n_rows = q_ref.shape[1]
    ti = lax.broadcasted_iota(jnp.int32, (n_rows, n_rows), 0)
    si = lax.broadcasted_iota(jnp.int32, (n_rows, n_rows), 1)
    ahead = (ti < si) if reverse else (ti > si)
    level = jnp.where(ahead, 31 - lax.clz(ti ^ si), jnp.where(ti == si, -1, -2))
    preps, bounds = zip(*[_hgrn_prep(q_ref.at[g], zg_ref.at[g], lbl_ref, layer, depth, reverse)
                          for g in rows])

    def run(fast):
        outs = [[] for _ in rows]
        for hd in range(A_HEADS):
            sl = slice(hd * A_HD, (hd + 1) * A_HD)
            heads = [tuple(a[:, sl] for a in preps[g]) for g in rows]
            if fast:
                atts = [_hgrn_scores_fast(heads[g], level) for g in rows]
            else:
                atts = [_hgrn_scores(heads[g], level, n_rows, None, reverse) for g in rows]
                h = n_rows // 2
                while h >= 1:
                    atts = [_hgrn_scores(heads[g], level, h, atts[g], reverse) for g in rows]
                    h //= 2
            for g in rows:
                outs[g].append(_hgrn_apply(heads[g], atts[g], v_ref.at[g, :, sl],
                                           st_ref.at[g, hd], reverse))
        for g in rows:
            o = jnp.concatenate(outs[g], axis=1)
            if final:
                _hgrn_finish(o, z_ref.at[g], ob_ref.at[g], ng_ref.at[layer:layer + 1],
                             out_ref.at[g])
            else:
                out_ref[g] = o

    safe = jnp.max(functools.reduce(jnp.maximum, bounds)) <= HGRN_SAFE_LOG2

    @pl.when(safe)
    def _():
        run(True)

    @pl.when(jnp.logical_not(safe))
    def _():
        run(False)


def _hgrn_prep(q_ref, zg_ref, lbl_ref, layer, depth, reverse):
    n_rows = q_ref.shape[0]
    q = q_ref[...]
    zg = zg_ref[...]
    e = jnp.exp(-jnp.abs(zg))
    r = 1.0 / (1.0 + e)
    er = e * r
    pos = zg >= 0.0
    f = jnp.where(pos, r, er)
    k = jnp.where(pos, er, r)
    if layer > 0:
        lb = _hgrn_lower_bound(lbl_ref, layer, depth)
        f = lb + (1.0 - lb) * f
        k = (1.0 - lb) * k
    g2 = jnp.log2(f)
    half = HGRN_SUB // 2
    bound = None
    for s0 in range(0, n_rows, HGRN_SUB):
        first = jnp.sum(g2[s0:s0 + half], axis=0, keepdims=True) + g2[s0 + half:s0 + half + 1]
        second = jnp.sum(g2[s0 + half:s0 + HGRN_SUB], axis=0, keepdims=True)
        worst = -jnp.minimum(first, second)
        bound = worst if bound is None else jnp.maximum(bound, worst)
    b2 = _cumsum_rows_mxu(g2, reverse)
    return (q, k, f, b2), bound


def _hgrn_scores(head, level, h, att, reverse):
    q, k, f, b2 = head
    if att is None:
        return jnp.where(level == -1, _dot_nt(q.astype(BF16), k.astype(BF16)), 0.0)
    c = _hgrn_level_operand(q, k, f, b2, h, reverse).astype(BF16)
    return jnp.where(level == (h.bit_length() - 1), _dot_nt(c, c), att)


def _hgrn_scores_fast(head, level):
    q, k, _, b2 = head
    n_rows = q.shape[0]
    blocks = []
    for s0 in range(0, n_rows, HGRN_SUB):
        rows = slice(s0, s0 + HGRN_SUB)
        ref = b2[s0 + HGRN_SUB // 2:s0 + HGRN_SUB // 2 + 1, :]
        qs = (q[rows] * jnp.exp2(b2[rows] - ref)).astype(BF16)
        ks = (k * jnp.exp2(jnp.minimum(ref - b2, HGRN_SAFE_LOG2))).astype(BF16)
        blocks.append(_dot_nt(qs, ks))
    return jnp.where(level >= -1, jnp.concatenate(blocks, axis=0), 0.0)


def _hgrn_apply(head, att, v_ref, st_ref, reverse):
    q, k, _, b2 = head
    n_rows = q.shape[0]
    tot = b2[0:1, :] if reverse else b2[n_rows - 1:n_rows, :]
    v_bf = v_ref[...].astype(BF16)
    q_in = (q * jnp.exp2(b2)).astype(BF16)
    k_out = (k * jnp.exp2(tot - b2)).astype(BF16)
    st = st_ref[...]
    o = _dot(att.astype(BF16), v_bf) + _dot(q_in, st.astype(BF16))
    dec_col = jnp.broadcast_to(jnp.exp2(tot), (SUBLANE, tot.shape[1])).T[:, 0:1]
    st_ref[...] = dec_col * st + _dot_tn(k_out, v_bf)
    return o


def _hgrn_finish(o, z_ref, ob_ref, ng_ref, out_ref):
    o = o + ob_ref[...]
    ys = []
    for hd in range(A_HEADS):
        sl = slice(hd * A_HD, (hd + 1) * A_HD)
        oh = o[:, sl]
        ms = jnp.mean(oh * oh, axis=-1, keepdims=True)
        ys.append(oh * lax.rsqrt(ms + EPS) * ng_ref[:, sl])
    y = jnp.concatenate(ys, axis=1) * _silu(z_ref[...])
    out_ref[...] = y.astype(out_ref.dtype)


def _chunk_index(c, n_chunks, reverse):
    return n_chunks - 1 - c if reverse else c


def _hgrn(u3, u16, lb_logits, layer, *, reverse, ob=None, norm_g=None):
    batch, seq, _ = u3.shape
    n_chunks = seq // CHUNK
    depth = lb_logits.shape[0]
    final = ob is not None
    grp = REC_GROUP

    def ublk(blk):
        return pl.BlockSpec((grp, CHUNK, D_GROUP),
                            lambda bg, c: (bg, _chunk_index(c, n_chunks, reverse), blk))

    row_spec = ublk(0)
    in_specs = [ublk(BLK_A_Q), ublk(BLK16_A_I), ublk(BLK_A_FB if reverse else BLK_A_FF),
                pl.BlockSpec((depth, D_GROUP), lambda bg, c: (0, 0))]
    args = [u3, u16, u3, lb_logits]
    if final:
        in_specs += [ublk(BLK_A_Z), row_spec, pl.BlockSpec(norm_g.shape, lambda bg, c: (0, 0))]
        args += [u3, ob, norm_g]
    return pl.pallas_call(
        functools.partial(_hgrn_kernel, layer=layer, depth=depth, reverse=reverse, final=final),
        out_shape=jax.ShapeDtypeStruct((batch, seq, D_GROUP), BF16 if final else F32),
        grid=(batch // grp, n_chunks),
        in_specs=in_specs,
        out_specs=row_spec,
        scratch_shapes=[pltpu.VMEM((grp, A_HEADS, A_HD, A_HD), F32)],
        compiler_params=pltpu.CompilerParams(
            dimension_semantics=("parallel", "arbitrary"), vmem_limit_bytes=VMEM_LIMIT),
        name="hgrn_fwd" if final else "hgrn_bwd",
    )(*args)


def _split3(x):
    hi = x.astype(BF16).astype(F32)
    r1 = x - hi
    mid = r1.astype(BF16).astype(F32)
    lo = (r1 - mid).astype(BF16).astype(F32)
    return hi, mid, lo


def _conv_silu_qk(qkp_ref, qkc_ref, qkn_ref, cw_ref, reverse):
    n_rows = qkc_ref.shape[0]
    c = pl.program_id(1)
    n_chunks = pl.num_programs(1)
    cc = _chunk_index(c, n_chunks, reverse)
    halo = SUBLANE
    prev = qkp_ref[n_rows - halo:n_rows, :] * (cc > 0).astype(F32)
    nxt = qkn_ref[0:halo, :] * (cc < n_chunks - 1).astype(F32)
    xcat = jnp.concatenate([prev, qkc_ref[...], nxt], axis=0)
    acc = None
    for j in range(CONV_W):
        off = halo + j - CONV_W // 2
        term = cw_ref[j:j + 1, :] * xcat[off:off + n_rows, :]
        acc = term if acc is None else acc + term
    qk = _silu(acc)
    n_qk = C_HEADS * C_QK
    return qk[:, n_qk:].astype(BF16), (qk[:, :n_qk] * (C_QK ** -0.5)).T.astype(BF16)


def _mlstm_kernel(*refs, layer, reverse, final):
    if final:
        (k_ref, qt_ref, v_ref, gt_ref, og_ref, z_ref, hb_ref, ng_ref,
         out_ref, cst_ref, m_ref) = refs
    else:
        (qkp_ref, qkc_ref, qkn_ref, cw_ref, v_ref, gt_ref,
         out_ref, ko_ref, qto_ref, cst_ref, m_ref) = refs

    @pl.when(pl.program_id(1) == 0)
    def _():
        cst_ref[...] = jnp.zeros_like(cst_ref)
        m_ref[...] = jnp.zeros_like(m_ref)

    rows = range(v_ref.shape[0])
    if final:
        qks = [(k_ref[g], qt_ref[g]) for g in rows]
    else:
        qks = [_conv_silu_qk(qkp_ref.at[g], qkc_ref.at[g], qkn_ref.at[g], cw_ref.at[layer], reverse)
               for g in rows]
        for g in rows:
            ko_ref[g], qto_ref[g] = qks[g]
    gates = [_mlstm_gates(gt_ref.at[g], m_ref.at[g], reverse) for g in rows]
    outs = [[] for _ in rows]
    for hd in range(C_HEADS):
        weights = [_mlstm_weights(hd, qks[g], gates[g], reverse) for g in rows]
        for g in rows:
            h_t = _mlstm_apply(hd, qks[g], v_ref.at[g], gates[g], weights[g], cst_ref.at[g], reverse)
            if final:
                outs[g].append((h_t + hb_ref[g, hd * C_V:(hd + 1) * C_V, :]).T)
            else:
                out_ref[g, hd * C_V:(hd + 1) * C_V, :] = h_t
    if final:
        for g in rows:
            _mlstm_finish(jnp.concatenate(outs[g], axis=1), og_ref.at[g], z_ref.at[g],
                          ng_ref.at[layer:layer + 1], out_ref.at[g])


def _mlstm_gates(gt_ref, m_ref, reverse):
    n_rows = gt_ref.shape[1]
    n_dir = 2 * C_HEADS

    li = gt_ref[0:n_dir, :]
    b = gt_ref[n_dir:, :]
    b_last = b[:, 0:1] if reverse else b[:, n_rows - 1:n_rows]
    m_prev = m_ref[...]
    d_inter = b + m_prev
    a = (b_last - b) + li
    m_new = jnp.maximum(b_last + m_prev, jnp.max(a, axis=1, keepdims=True))
    ws = jnp.exp(a - m_new)
    decay = jnp.exp((b_last + m_prev) - m_new)
    m_ref[...] = m_new

    one8 = jnp.ones((n_dir, n_rows), F32)
    lhs = jnp.concatenate(list(_split3(li - b)) + [one8, one8, one8], axis=0).T.astype(BF16)
    rhs_all = jnp.concatenate([one8, one8, one8] + list(_split3(b)), axis=0)
    return lhs, rhs_all, d_inter, ws, decay


def _mlstm_weights(hd, qk, gates, reverse):
    lhs, rhs_all, d_inter, _, _ = gates
    k, q_t = qk
    n_rows = k.shape[0]
    n_dir = 2 * C_HEADS
    gi = (C_HEADS if reverse else 0) + hd
    gate_row = lax.broadcasted_iota(jnp.int32, rhs_all.shape, 0) & (n_dir - 1)
    si = lax.broadcasted_iota(jnp.int32, (n_rows, n_rows), 0)
    ti = lax.broadcasted_iota(jnp.int32, (n_rows, n_rows), 1)
    causal = (si >= ti) if reverse else (si <= ti)

    sl = slice(hd * C_QK, (hd + 1) * C_QK)
    rhs = jnp.where(gate_row == gi, rhs_all, 0.0).astype(BF16)
    d = jnp.where(causal, _dot(lhs, rhs), NEG)
    d_int = d_inter[gi:gi + 1, :]
    m_t = jnp.maximum(jnp.max(d, axis=0, keepdims=True), d_int)
    w = (_dot(k[:, sl], q_t[sl, :]) * jnp.exp(d - m_t)).astype(BF16)
    return w, m_t, jnp.exp(d_int - m_t)


def _mlstm_apply(hd, qk, vt_ref, gates, weights, cst_ref, reverse):
    _, _, _, ws, decay = gates
    w, m_t, g = weights
    k, q_t = qk
    n_rows = k.shape[0]
    gi = (C_HEADS if reverse else 0) + hd
    sl = slice(hd * C_QK, (hd + 1) * C_QK)
    v_t = vt_ref[hd * C_V:(hd + 1) * C_V, :]
    v_ext = jnp.concatenate([v_t, jnp.ones((C_ONES, n_rows), BF16)], axis=0)
    ws_row = ws[gi:gi + 1, :]
    v_scaled = jnp.concatenate([(v_t.astype(F32) * ws_row).astype(BF16),
                                jnp.broadcast_to(ws_row, (C_ONES, n_rows)).astype(BF16)], axis=0)
    cst = cst_ref[hd]
    num = _dot(v_ext, w) + g * _dot(cst.astype(BF16), q_t[sl, :])
    den = jnp.maximum(jnp.abs(num[C_V:C_V + 1, :]), jnp.exp(-m_t))
    cst_ref[hd] = decay[gi:gi + 1, :C_QK] * cst + _dot(v_scaled, k[:, sl])
    return num[:C_V, :] / den


def _mlstm_finish(hsum, og_ref, z_ref, ng_ref, out_ref):
    hsum = hsum * _sigmoid(og_ref[...])
    ys = []
    for hd in range(C_HEADS):
        sl = slice(hd * C_V, (hd + 1) * C_V)
        hh = hsum[:, sl]
        mu = jnp.mean(hh, axis=-1, keepdims=True)
        var = jnp.mean(jnp.square(hh - mu), axis=-1, keepdims=True)
        ys.append((hh - mu) * lax.rsqrt(var + EPS) * ng_ref[:, sl])
    y = jnp.concatenate(ys, axis=1) * _silu(z_ref[...])
    out_ref[...] = y.astype(out_ref.dtype)


def _mlstm(u3, gt, vt, conv_w, layer, *, reverse, qk=None, hb=None, norm_g=None):
    batch, seq, _ = u3.shape
    n_chunks = seq // CHUNK
    final = hb is not None
    grp = REC_GROUP
    assert CHUNK == LANE

    def chunk(c, delta=0):
        return jnp.clip(_chunk_index(c, n_chunks, reverse) + delta, 0, n_chunks - 1)

    def ublk(blk, delta=0):
        return pl.BlockSpec((grp, CHUNK, D_GROUP), lambda bg, c: (bg, chunk(c, delta), blk))

    row_spec = ublk(0)
    t_spec = lambda ch: pl.BlockSpec((grp, ch, CHUNK), lambda bg, c: (bg, 0, chunk(c)))
    gate_specs = [t_spec(N_GATE)]
    gate_args = [gt]
    n_qk = C_HEADS * C_QK
    k_spec = pl.BlockSpec((grp, CHUNK, n_qk), lambda bg, c: (bg, chunk(c), 0))
    if final:
        in_specs = [k_spec, t_spec(n_qk), t_spec(D_GROUP)] + gate_specs + [
            ublk(BLK_C_O), ublk(BLK_C_Z), t_spec(D_GROUP),
            pl.BlockSpec(norm_g.shape, lambda bg, c: (0, 0))]
        args = [qk[0], qk[1], vt] + gate_args + [u3, u3, hb, norm_g]
        out_shape = jax.ShapeDtypeStruct((batch, seq, D_GROUP), BF16)
        out_specs = row_spec
    else:
        in_specs = [ublk(BLK_C_QK, -1), ublk(BLK_C_QK), ublk(BLK_C_QK, 1),
                    pl.BlockSpec(conv_w.shape, lambda bg, c: (0, 0, 0)),
                    t_spec(D_GROUP)] + gate_specs
        args = [u3, u3, u3, conv_w, vt] + gate_args
        out_shape = (jax.ShapeDtypeStruct((batch, D_GROUP, seq), F32),
                     jax.ShapeDtypeStruct((batch, seq, n_qk), BF16),
                     jax.ShapeDtypeStruct((batch, n_qk, seq), BF16))
        out_specs = (t_spec(D_GROUP), k_spec, t_spec(n_qk))
    return pl.pallas_call(
        functools.partial(_mlstm_kernel, layer=layer, reverse=reverse, final=final),
        out_shape=out_shape,
        grid=(batch // grp, n_chunks),
        in_specs=in_specs,
        out_specs=out_specs,
        scratch_shapes=[pltpu.VMEM((grp, C_HEADS, C_V + C_ONES, C_QK), F32),
                        pltpu.VMEM((grp, 2 * C_HEADS, CHUNK), F32)],
        compiler_params=pltpu.CompilerParams(
            dimension_semantics=("parallel", "arbitrary"), vmem_limit_bytes=VMEM_LIMIT),
        name="mlstm_fwd" if final else "mlstm_bwd",
    )(*args)


def _rope_table_kernel(pos_ref, invf_ref, cos_ref, sin_ref):
    ang = invf_ref[...] * pos_ref[...].astype(F32)
    n_rows = ang.shape[1]
    half = ROPE_DIM // 2
    r = lax.broadcasted_iota(jnp.int32, ang.shape, 0)
    sn = jnp.sin(ang)
    sn = jnp.where(r < half, -sn, sn)
    pad = jnp.zeros((LANE - ROPE_DIM, n_rows), F32)

    def expand(x):
        t = jnp.concatenate([x, pad], axis=0).T
        return t + pltpu.roll(t, B_HD, axis=1)

    lane = lax.broadcasted_iota(jnp.int32, (n_rows, LANE), 1) & (B_HD - 1)
    cos_ref[...] = jnp.where(lane < ROPE_DIM, expand(jnp.cos(ang)), 1.0)
    sin_ref[...] = expand(sn)


def _rope_tables(positions):
    rows = positions.size
    half = ROPE_DIM // 2
    inv_freq = ROPE_THETA ** (-jnp.arange(0, ROPE_DIM, 2, dtype=F32) / ROPE_DIM)
    invf = jnp.concatenate([inv_freq, inv_freq]).reshape(ROPE_DIM, 1)
    tm = ROW_TILE
    return pl.pallas_call(
        _rope_table_kernel,
        out_shape=(jax.ShapeDtypeStruct((rows, LANE), F32),) * 2,
        grid=(rows // tm,),
        in_specs=[pl.BlockSpec((1, tm), lambda i: (0, i)),
                  pl.BlockSpec((ROPE_DIM, 1), lambda i: (0, 0))],
        out_specs=(pl.BlockSpec((tm, LANE), lambda i: (i, 0)),) * 2,
        compiler_params=pltpu.CompilerParams(dimension_semantics=("parallel",)),
        name="rope_tables",
    )(positions.reshape(1, rows), invf)


def _rope(x, cos, sin):
    width = x.shape[1]
    if width > LANE:
        cos = jnp.concatenate([cos] * (width // LANE), axis=1)
        sin = jnp.concatenate([sin] * (width // LANE), axis=1)
    half = ROPE_DIM // 2
    r = lax.broadcasted_iota(jnp.int32, x.shape, 1) & (B_HD - 1)
    partner = jnp.where(r < half, pltpu.roll(x, width - half, axis=1), pltpu.roll(x, half, axis=1))
    return x * cos + partner * sin


def _swa_kernel(q_ref, z_ref, mp_ref, mc_ref, mn_ref, cp_ref, cc_ref, cn_ref,
                sp_ref, sc_ref, sn_ref, sink_ref, out_ref, *, layer):
    tile = q_ref.shape[0]
    blk = mp_ref.shape[0]
    n_sub = tile // blk
    i = pl.program_id(1)
    n_tiles = pl.num_programs(1)
    kv_w = B_KV_HEADS * B_HD
    grp = B_HEADS // B_KV_HEADS

    q = (_rope(q_ref[...], cc_ref[...], sc_ref[...]) * (B_HD ** -0.5)).astype(BF16)
    k = jnp.concatenate([
        _rope(mp_ref[:, MISC_K:MISC_K + kv_w], cp_ref[...], sp_ref[...]),
        _rope(mc_ref[:, MISC_K:MISC_K + kv_w], cc_ref[...], sc_ref[...]),
        _rope(mn_ref[:, MISC_K:MISC_K + kv_w], cn_ref[...], sn_ref[...])], axis=0).astype(BF16)
    v = jnp.concatenate([mp_ref[:, MISC_V:MISC_V + kv_w], mc_ref[:, MISC_V:MISC_V + kv_w],
                         mn_ref[:, MISC_V:MISC_V + kv_w]], axis=0)
    n_keys = v.shape[0]

    kr = lax.broadcasted_iota(jnp.int32, (3 * blk, blk), 0)
    qc = lax.broadcasted_iota(jnp.int32, (3 * blk, blk), 1)
    in_window = jnp.abs(qc - (kr - blk)) <= WINDOW
    biases = []
    for j in range(n_sub):
        valid = in_window
        if j == 0:
            valid = valid & (kr >= jnp.where(i > 0, 0, blk))
        if j == n_sub - 1:
            valid = valid & (kr < jnp.where(i < n_tiles - 1, 3 * blk, 2 * blk))
        biases.append(jnp.where(valid, 0.0, NEG))

    ones = jnp.ones((n_keys, B_HD), F32)
    vts = [jnp.concatenate([v[:, kvh * B_HD:(kvh + 1) * B_HD], ones], axis=1).T.astype(BF16)
           for kvh in range(B_KV_HEADS)]

    tiles = [[] for _ in range(n_sub)]
    pairs = [[] for _ in range(n_sub)]
    for hq in range(B_HEADS):
        kvh = hq // grp
        ksl = slice(kvh * B_HD, (kvh + 1) * B_HD)
        sk = sink_ref[layer:layer + 1, hq:hq + 1]
        for j in range(n_sub):
            keys = slice(j * blk, (j + 3) * blk)
            qh = q[j * blk:(j + 1) * blk, hq * B_HD:(hq + 1) * B_HD]
            s = _dot_nt(k[keys, ksl], qh)
            s = jnp.concatenate([s[:blk] + biases[j][:blk], s[blk:2 * blk],
                                 s[2 * blk:] + biases[j][2 * blk:]], axis=0)
            m = jnp.maximum(jnp.max(s, axis=0, keepdims=True), sk)
            p = jnp.exp(s - m).astype(BF16)
            o = _dot(vts[kvh][:, keys], p)
            denom = o[B_HD:B_HD + 1, :] + jnp.exp(sk - m)
            pairs[j].append(o[:B_HD, :] / denom)
            if len(pairs[j]) == LANE // B_HD:
                tiles[j].append(jnp.concatenate(pairs[j], axis=0).T)
                pairs[j] = []
    for j in range(n_sub):
        rows = slice(j * blk, (j + 1) * blk)
        y = jnp.concatenate(tiles[j], axis=1) * _silu(z_ref[rows, :])
        out_ref[rows, :] = y.astype(out_ref.dtype)


def _swa(u, cos_t, sin_t, sink, layer, batch):
    rows = u.shape[0]
    n_blocks = rows // batch // CHUNK
    n_sub = SWA_TILE // CHUNK
    n_tiles = n_blocks // n_sub

    def tile_spec(width, blk):
        return pl.BlockSpec((SWA_TILE, width), lambda bi, i: (bi * n_tiles + i, blk))

    def halo_spec(width, blk, after):
        def index(bi, i):
            nb = (i + 1) * n_sub if after else i * n_sub - 1
            return (bi * n_blocks + jnp.clip(nb, 0, n_blocks - 1), blk)
        return pl.BlockSpec((CHUNK, width), index)

    def table_specs():
        return [halo_spec(LANE, 0, False), tile_spec(LANE, 0), halo_spec(LANE, 0, True)]

    return pl.pallas_call(
        functools.partial(_swa_kernel, layer=layer),
        out_shape=jax.ShapeDtypeStruct((rows, D_GROUP), BF16),
        grid=(batch, n_tiles),
        in_specs=[tile_spec(D_GROUP, BLK_B_Q), tile_spec(D_GROUP, BLK_B_Z),
                  halo_spec(D_GROUP, BLK_MISC, False), tile_spec(D_GROUP, BLK_MISC),
                  halo_spec(D_GROUP, BLK_MISC, True)] + table_specs() + table_specs() + [
                  pl.BlockSpec(sink.shape, lambda bi, i: (0, 0))],
        out_specs=tile_spec(D_GROUP, 0),
        compiler_params=pltpu.CompilerParams(
            dimension_semantics=("parallel", "parallel"), vmem_limit_bytes=VMEM_LIMIT),
        name="window_attn",
    )(u, u, u, u, u, cos_t, cos_t, cos_t, sin_t, sin_t, sin_t, sink)


def _mem_kv_kernel(mem_ref, g_ref, w_ref, kt_ref, v_ref, *, layer):
    x = mem_ref[...]
    ms = jnp.mean(x * x, axis=-1, keepdims=True)
    h = (x * lax.rsqrt(ms + EPS) * g_ref[layer:layer + 1, :]).astype(BF16)
    kv = _dot(h, w_ref[...].astype(BF16))
    kt_ref[...] = kv[:, :D_GROUP].T.astype(kt_ref.dtype)
    v_ref[...] = kv[:, D_GROUP:].astype(v_ref.dtype)


def _mem_kv(mem, g_all, w_all, layer):
    batch, m_len, d = mem.shape
    n_out = w_all.shape[2]
    return pl.pallas_call(
        functools.partial(_mem_kv_kernel, layer=layer),
        out_shape=(jax.ShapeDtypeStruct((batch, D_GROUP, m_len), BF16),
                   jax.ShapeDtypeStruct((batch, m_len, D_GROUP), BF16)),
        grid=(batch,),
        in_specs=[pl.BlockSpec((None, m_len, d), lambda bi: (bi, 0, 0)),
                  pl.BlockSpec(g_all.shape, lambda bi: (0, 0)),
                  pl.BlockSpec((None, d, n_out), lambda bi: (layer, 0, 0))],
        out_specs=(pl.BlockSpec((None, D_GROUP, m_len), lambda bi: (bi, 0, 0)),
                   pl.BlockSpec((None, m_len, D_GROUP), lambda bi: (bi, 0, 0))),
        compiler_params=pltpu.CompilerParams(
            dimension_semantics=("parallel",), vmem_limit_bytes=VMEM_LIMIT),
        name="mem_kv",
    )(mem, g_all, w_all)


def _mem_attn_out_kernel(q_ref, z_ref, kt_ref, v_ref, ya_ref, yb_ref, yc_ref, w_ref, x_ref, g_ref,
                         o_ref, *, final):
    q = q_ref[...].astype(BF16)
    outs = []
    for hd in range(D_HEADS):
        sl = slice(hd * D_HD, (hd + 1) * D_HD)
        s = _dot(q[:, sl], kt_ref[sl, :])
        m = jnp.max(s, axis=-1, keepdims=True)
        p = jnp.exp2((s - m) * (D_HD ** -0.5 * LOG2_E))
        den = jnp.sum(p, axis=-1, keepdims=True)
        outs.append(_dot(p.astype(BF16), v_ref[:, sl]) / den)
    yd = (jnp.concatenate(outs, axis=1) * _silu(z_ref[...])).astype(BF16)

    acc = x_ref[...]
    for i, y in enumerate((ya_ref[...], yb_ref[...], yc_ref[...], yd)):
        acc = acc + _dot(y, w_ref[i * D_GROUP:(i + 1) * D_GROUP, :].astype(BF16))
    if final:
        ms = jnp.mean(acc * acc, axis=-1, keepdims=True)
        acc = acc * lax.rsqrt(ms + EPS) * g_ref[...]
    o_ref[...] = acc


def _mem_attn_out(u, u16, kt, v, ya, yb, yc, w_all, xf, g, layer, batch, final):
    rows, d = xf.shape
    tm = ROW_TILE
    per_batch = rows // batch // tm
    row = lambda width, blk: pl.BlockSpec((tm, width), lambda bi, i: (bi * per_batch + i, blk))
    return pl.pallas_call(
        functools.partial(_mem_attn_out_kernel, final=final),
        out_shape=jax.ShapeDtypeStruct((rows, d), F32),
        grid=(batch, per_batch),
        in_specs=[row(D_GROUP, BLK16_D_Q), row(D_GROUP, BLK_D_Z),
                  pl.BlockSpec((None,) + kt.shape[1:], lambda bi, i: (bi, 0, 0)),
                  pl.BlockSpec((None,) + v.shape[1:], lambda bi, i: (bi, 0, 0)),
                  row(D_GROUP, 0), row(D_GROUP, 0), row(D_GROUP, 0),
                  pl.BlockSpec((None, 4 * D_GROUP, d), lambda bi, i: (layer, 0, 0)),
                  row(d, 0),
                  pl.BlockSpec((1, d), lambda bi, i: (0, 0))],
        out_specs=row(d, 0),
        compiler_params=pltpu.CompilerParams(
            dimension_semantics=("parallel", "parallel"), vmem_limit_bytes=VMEM_LIMIT),
        name="mem_attn_out",
    )(u16, u, kt, v, ya, yb, yc, w_all, xf, g.reshape(1, d))


def kernel(x, mem, positions, norm_g, w_in, hgrn_lb_logits, hgrn_norm_g, attn_sink, mlstm_conv_w,
           mlstm_gate_b, mlstm_norm_g, mem_norm_g, w_mem_kv, w_out, final_norm_g):
    batch, seq, d = x.shape
    depth = w_in.shape[0]
    assert all(seq % t == 0 for t in (CHUNK, ROW_TILE, IN_ROW_TILE, SWA_TILE))
    assert batch % REC_GROUP == 0
    xf = x.reshape(batch * seq, d)
    cos_t, sin_t = _rope_tables(positions)
    w_all = _prep_w_in(w_in)
    gate_b = mlstm_gate_b.reshape(depth, N_GATE, 1)
    for layer in range(depth):
        u, u16, gt, vt = _in_proj(xf, norm_g, w_all, gate_b, layer, batch)
        u3 = u.reshape(batch, seq, u.shape[1])
        u16_3 = u16.reshape(batch, seq, u16.shape[1])
        ob = _hgrn(u3, u16_3, hgrn_lb_logits, layer, reverse=True)
        ya = _hgrn(u3, u16_3, hgrn_lb_logits, layer, reverse=False, ob=ob, norm_g=hgrn_norm_g)
        yb = _swa(u, cos_t, sin_t, attn_sink, layer, batch)
        hb, k_c, qt_c = _mlstm(u3, gt, vt, mlstm_conv_w, layer, reverse=True)
        yc = _mlstm(u3, gt, vt, None, layer, reverse=False, qk=(k_c, qt_c), hb=hb,
                    norm_g=mlstm_norm_g)
        mem_kt, mem_v = _mem_kv(mem, mem_norm_g, w_mem_kv, layer)
        xf = _mem_attn_out(u, u16, mem_kt, mem_v, ya.reshape(batch * seq, D_GROUP), yb,
                           yc.reshape(batch * seq, D_GROUP), w_out, xf, final_norm_g, layer, batch,
                           final=(layer == depth - 1))
    return xf.reshape(batch, seq, d)
```

```python
import functools

import jax
import jax.numpy as jnp
import numpy as np
from jax import lax
from jax.experimental import pallas as pl
from jax.experimental.pallas import tpu as pltpu

F32 = jnp.float32
BF16 = jnp.bfloat16

D_GROUP = 512
A_HEADS, A_HD = 4, 128
B_HEADS, B_KV_HEADS, B_HD = 8, 2, 64
WINDOW = 128
ROPE_THETA = 500000.0
ROPE_DIM = 16
C_HEADS, C_QK, C_V = 4, 64, 128
CONV_W = 5
D_HEADS, D_HD = 4, 128
EPS = 1e-6
IN_SIZES = (
    D_GROUP, D_GROUP, D_GROUP, D_GROUP, D_GROUP,
    B_HEADS * B_HD, B_KV_HEADS * B_HD, B_KV_HEADS * B_HD, D_GROUP,
    C_HEADS * C_QK, C_HEADS * C_QK, C_HEADS * C_V, D_GROUP, D_GROUP,
    2 * C_HEADS, 2 * C_HEADS,
    D_GROUP, D_GROUP,
)
SPLIT_POINTS = tuple(int(s) for s in np.cumsum(IN_SIZES)[:-1])

LANE = 128
SUBLANE = 8
VMEM_LIMIT = 48 * 1024 * 1024

CHUNK = 128
ROW_TILE = 1024
IN_ROW_TILE = 512
IN_PROJ_VMEM_LIMIT = 56 * 1024 * 1024
W_PREP_TILE = 256
HGRN_SUB = 64
HGRN_SAFE_LOG2 = 100.0
SWA_TILE = 1024
REC_GROUP = 4
NEG = -1e30
LOG2_E = 1.4426950408889634

(BLK_A_Q, BLK_A_FF, BLK_A_FB, BLK_A_Z, BLK_B_Q, BLK_B_Z, BLK_C_QK, BLK_C_O,
 BLK_C_Z, BLK_D_Z, BLK_MISC) = range(11)
N_BLK = 11
BLK16_A_I, BLK16_D_Q = range(2)
N_BLK16 = 2
W_ROW_C_V = (N_BLK + N_BLK16) * D_GROUP
N_GATE = 4 * C_HEADS
W_ROW_GATE = W_ROW_C_V + D_GROUP
N_W_ROWS = W_ROW_GATE + N_GATE
MISC_K, MISC_V = 0, 128
C_ONES = 16


def _dot(a, b):
    return jnp.dot(a, b, preferred_element_type=F32)


def _dot_nt(a, b):
    return lax.dot_general(a, b, (((1,), (1,)), ((), ())), preferred_element_type=F32)


def _dot_tn(a, b):
    return lax.dot_general(a, b, (((0,), (0,)), ((), ())), preferred_element_type=F32)


def _sigmoid(x):
    return 1.0 / (1.0 + jnp.exp(-x))


def _silu(x):
    return x * _sigmoid(x)


def _log_sigmoid(x):
    return jnp.minimum(x, 0.0) - jnp.log(1.0 + jnp.exp(-jnp.abs(x)))


def _chunk_cumsum_lanes(x, reverse):
    n = x.shape[1]
    pos = lax.broadcasted_iota(jnp.int32, x.shape, 1) & (CHUNK - 1)
    s = 1
    while s < CHUNK:
        if reverse:
            x = x + jnp.where(pos < CHUNK - s, pltpu.roll(x, n - s, axis=1), 0.0)
        else:
            x = x + jnp.where(pos >= s, pltpu.roll(x, s, axis=1), 0.0)
        s *= 2
    return x


def _in_proj_kernel(x_ref, g_ref, w_ref, gb_ref, u_ref, u16_ref, gt_ref, vt_ref, *, layer):
    x = x_ref[...]
    ms = jnp.mean(x * x, axis=-1, keepdims=True)
    h = (x * lax.rsqrt(ms + EPS) * g_ref[layer:layer + 1, :]).astype(BF16)
    vg = _dot_nt(w_ref[W_ROW_C_V:W_ROW_GATE + N_GATE, :], h)
    vt_ref[...] = vg[:D_GROUP, :].astype(BF16)
    gates = vg[D_GROUP:, :] + gb_ref[layer]
    n_dir = 2 * C_HEADS
    log_f = _log_sigmoid(gates[n_dir:, :])
    is_fwd = lax.broadcasted_iota(jnp.int32, log_f.shape, 0) < C_HEADS
    cum = jnp.where(is_fwd, _chunk_cumsum_lanes(log_f, False), _chunk_cumsum_lanes(log_f, True))
    gt_ref[...] = jnp.concatenate([gates[:n_dir, :], cum], axis=0)
    for j in range(N_BLK):
        u_ref[:, j * D_GROUP:(j + 1) * D_GROUP] = _dot_nt(h, w_ref[j * D_GROUP:(j + 1) * D_GROUP, :])
    for j in range(N_BLK16):
        rows = slice((N_BLK + j) * D_GROUP, (N_BLK + j + 1) * D_GROUP)
        u16_ref[:, j * D_GROUP:(j + 1) * D_GROUP] = _dot_nt(h, w_ref[rows, :]).astype(BF16)


def _in_proj(xf, g_all, w_all, gate_b, layer, batch):
    rows, d = xf.shape
    tm = IN_ROW_TILE
    per_batch = rows // batch // tm
    return pl.pallas_call(
        functools.partial(_in_proj_kernel, layer=layer),
        out_shape=(jax.ShapeDtypeStruct((rows, N_BLK * D_GROUP), F32),
                   jax.ShapeDtypeStruct((rows, N_BLK16 * D_GROUP), BF16),
                   jax.ShapeDtypeStruct((batch, N_GATE, rows // batch), F32),
                   jax.ShapeDtypeStruct((batch, D_GROUP, rows // batch), BF16)),
        grid=(rows // tm,),
        in_specs=[
            pl.BlockSpec((tm, d), lambda i: (i, 0)),
            pl.BlockSpec(g_all.shape, lambda i: (0, 0)),
            pl.BlockSpec((None, N_W_ROWS, d), lambda i: (layer, 0, 0), pipeline_mode=pl.Buffered(1)),
            pl.BlockSpec(gate_b.shape, lambda i: (0, 0, 0)),
        ],
        out_specs=(
            pl.BlockSpec((tm, N_BLK * D_GROUP), lambda i: (i, 0)),
            pl.BlockSpec((tm, N_BLK16 * D_GROUP), lambda i: (i, 0)),
            pl.BlockSpec((None, N_GATE, tm), lambda i: (i // per_batch, 0, i % per_batch)),
            pl.BlockSpec((None, D_GROUP, tm), lambda i: (i // per_batch, 0, i % per_batch)),
        ),
        compiler_params=pltpu.CompilerParams(
            dimension_semantics=("parallel",), vmem_limit_bytes=IN_PROJ_VMEM_LIMIT),
        name="in_proj",
    )(xf, g_all, w_all, gate_b)


def _w_in_segments():
    names = ("a_q", "a_i", "a_ff", "a_fb", "a_z", "b_q", "b_k", "b_v", "b_z",
             "c_q", "c_k", "c_v", "c_o", "c_z", "c_ig", "c_fg", "d_q", "d_z")
    src = dict(zip(names, zip((0,) + SPLIT_POINTS, IN_SIZES)))
    blocks = {BLK_A_Q: ("a_q",), BLK_A_FF: ("a_ff",), BLK_A_FB: ("a_fb",), BLK_A_Z: ("a_z",),
              BLK_B_Q: ("b_q",), BLK_B_Z: ("b_z",), BLK_C_QK: ("c_q", "c_k"), BLK_C_O: ("c_o",),
              BLK_C_Z: ("c_z",), BLK_D_Z: ("d_z",), BLK_MISC: ("b_k", "b_v"),
              N_BLK + BLK16_A_I: ("a_i",), N_BLK + BLK16_D_Q: ("d_q",)}
    placed = []
    for blk, group in sorted(blocks.items()):
        dst = blk * D_GROUP
        for name in group:
            placed.append((name, dst))
            dst += src[name][1]
    placed += [("c_v", W_ROW_C_V), ("c_ig", W_ROW_GATE), ("c_fg", W_ROW_GATE + src["c_ig"][1])]
    segs, gaps, end = [], [], 0
    for name, dst in placed:
        off, width = src[name]
        if dst > end:
            gaps.append((end, dst))
        if segs and segs[-1][0] + segs[-1][1] == off and segs[-1][2] + segs[-1][1] == dst:
            segs[-1] = (segs[-1][0], segs[-1][1] + width, segs[-1][2])
        else:
            segs.append((off, width, dst))
        end = dst + width
    assert end == N_W_ROWS
    return tuple(segs), tuple(gaps)


def _w_prep_kernel(wt_ref, wc_ref):
    segs, gaps = _w_in_segments()
    for off, width, dst in segs:
        wc_ref[dst:dst + width, :] = wt_ref[off:off + width, :].astype(BF16)
    for lo, hi in gaps:
        wc_ref[lo:hi, :] = jnp.zeros((hi - lo, wc_ref.shape[1]), BF16)


def _prep_w_in(w):
    depth, d, d_in = w.shape
    wt = jnp.swapaxes(w, 1, 2)
    tk = W_PREP_TILE
    return pl.pallas_call(
        _w_prep_kernel,
        out_shape=jax.ShapeDtypeStruct((depth, N_W_ROWS, d), BF16),
        grid=(depth, d // tk),
        in_specs=[pl.BlockSpec((None, d_in, tk), lambda l, i: (l, 0, i))],
        out_specs=pl.BlockSpec((None, N_W_ROWS, tk), lambda l, i: (l, 0, i)),
        compiler_params=pltpu.CompilerParams(
            dimension_semantics=("parallel", "parallel"), vmem_limit_bytes=VMEM_LIMIT),
        name="w_prep",
    )(wt)


def _level_ref(b, h, reverse):
    n_rows, n = b.shape
    blk = 2 * h
    r0 = h if reverse else h - 1
    if blk >= SUBLANE:
        pieces = [jnp.broadcast_to(b[s + r0:s + r0 + 1, :], (blk, n)) for s in range(0, n_rows, blk)]
        return pieces[0] if len(pieces) == 1 else jnp.concatenate(pieces, axis=0)
    res = lax.broadcasted_iota(jnp.int32, b.shape, 0) & (blk - 1)
    out = None
    for m in range(blk):
        shift = (m - r0) % n_rows
        cand = b if shift == 0 else pltpu.roll(b, shift, axis=0)
        out = cand if out is None else jnp.where(res == m, cand, out)
    return out


def _hgrn_lower_bound(lbl_ref, layer, depth):
    rows = [lbl_ref[j:j + 1, :] for j in range(depth)]
    mx = functools.reduce(jnp.maximum, rows)
    es = [jnp.exp(r - mx) for r in rows]
    tot = functools.reduce(lambda a, c: a + c, es)
    lb = jnp.zeros_like(rows[0])
    for j in range(1, layer + 1):
        lb = lb + es[j] / tot
    return lb


def _cumsum_rows_mxu(x, reverse):
    n = x.shape[0]
    r = lax.broadcasted_iota(jnp.int32, (n, n), 0)
    c = lax.broadcasted_iota(jnp.int32, (n, n), 1)
    tri = jnp.where((c >= r) if reverse else (c <= r), 1.0, 0.0).astype(BF16)
    hi = x.astype(BF16)
    r1 = x - hi.astype(F32)
    mid = r1.astype(BF16)
    lo = (r1 - mid.astype(F32)).astype(BF16)
    return _dot(jnp.concatenate([tri, tri, tri], axis=1), jnp.concatenate([hi, mid, lo], axis=0))


def _hgrn_level_operand(q, k, f, b2, h, reverse):
    n_rows = q.shape[0]
    blk = 2 * h
    r0 = h if reverse else h - 1
    if h >= SUBLANE:
        pieces = []
        for s0 in range(0, n_rows, blk):
            ref = b2[s0 + r0:s0 + r0 + 1, :]
            lo, up = slice(s0, s0 + h), slice(s0 + h, s0 + blk)
            if reverse:
                pieces += [q[lo] * jnp.exp2(b2[lo] - ref), k[up] * jnp.exp2(ref - b2[up])]
            else:
                pieces += [k[lo] * jnp.exp2(ref - b2[lo]), q[up] * jnp.exp2(b2[up] - ref)]
        return jnp.concatenate(pieces, axis=0)
    row = lax.broadcasted_iota(jnp.int32, q.shape, 0)
    is_q = ((row & h) == 0) if reverse else ((row & h) != 0)
    qk = jnp.where(is_q, q, k)
    if h == 1:
        return jnp.where(is_q, q * f, k)
    if h == 2:
        res = row & 3
        f_dn = pltpu.roll(f, 1, axis=0)
        f_up = pltpu.roll(f, n_rows - 1, axis=0)
        if reverse:
            e = jnp.where(res == 0, f * f_up, jnp.where(res == 1, f, jnp.where(res == 2, 1.0, f_dn)))
        else:
            e = jnp.where(res == 0, f_up, jnp.where(res == 1, 1.0, jnp.where(res == 2, f, f_dn * f)))
        return qk * e
    return qk * jnp.exp2(-jnp.abs(b2 - _level_ref(b2, h, reverse)))


def _hgrn_kernel(*refs, layer, depth, reverse, final):
    if final:
        q_ref, v_ref, zg_ref, lbl_ref, z_ref, ob_ref, ng_ref, out_ref, st_ref = refs
    else:
        q_ref, v_ref, zg_ref, lbl_ref, out_ref, st_ref = refs

    @pl.when(pl.program_id(1) == 0)
    def _():
        st_ref[...] = jnp.zeros_like(st_ref)

    rows = range(q_ref.shape[0])
    n_rows = q_ref.shape[1]
    ti = lax.broadcasted_iota(jnp.int32, (n_rows, n_rows), 0)
    si = lax.broadcasted_iota(jnp.int32, (n_rows, n_rows), 1)
    ahead = (ti < si) if reverse else (ti > si)
    level = jnp.where(ahead, 31 - lax.clz(ti ^ si), jnp.where(ti == si, -1, -2))
    preps, bounds = zip(*[_hgrn_prep(q_ref.at[g], zg_ref.at[g], lbl_ref, layer, depth, reverse)
                          for g in rows])

    def run(fast):
        outs = [[] for _ in rows]
        for hd in range(A_HEADS):
            sl = slice(hd * A_HD, (hd + 1) * A_HD)
            heads = [tuple(a[:, sl] for a in preps[g]) for g in rows]
            if fast:
                atts = [_hgrn_scores_fast(heads[g], level) for g in rows]
            else:
                atts = [_hgrn_scores(heads[g], level, n_rows, None, reverse) for g in rows]
                h = n_rows // 2
                while h >= 1:
                    atts = [_hgrn_scores(heads[g], level, h, atts[g], reverse) for g in rows]
                    h //= 2
            for g in rows:
                outs[g].append(_hgrn_apply(heads[g], atts[g], v_ref.at[g, :, sl],
                                           st_ref.at[g, hd], reverse))
        for g in rows:
            o = jnp.concatenate(outs[g], axis=1)
            if final:
                _hgrn_finish(o, z_ref.at[g], ob_ref.at[g], ng_ref.at[layer:layer + 1],
                             out_ref.at[g])
            else:
                out_ref[g] = o

    safe = jnp.max(functools.reduce(jnp.maximum, bounds)) <= HGRN_SAFE_LOG2

    @pl.when(safe)
    def _():
        run(True)

    @pl.when(jnp.logical_not(safe))
    def _():
        run(False)


def _hgrn_prep(q_ref, zg_ref, lbl_ref, layer, depth, reverse):
    n_rows = q_ref.shape[0]
    q = q_ref[...]
    zg = zg_ref[...]
    e = jnp.exp(-jnp.abs(zg))
    r = 1.0 / (1.0 + e)
    er = e * r
    pos = zg >= 0.0
    f = jnp.where(pos, r, er)
    k = jnp.where(pos, er, r)
    if layer > 0:
        lb = _hgrn_lower_bound(lbl_ref, layer, depth)
        f = lb + (1.0 - lb) * f
        k = (1.0 - lb) * k
    g2 = jnp.log2(f)
    half = HGRN_SUB // 2
    bound = None
    for s0 in range(0, n_rows, HGRN_SUB):
        first = jnp.sum(g2[s0:s0 + half], axis=0, keepdims=True) + g2[s0 + half:s0 + half + 1]
        second = jnp.sum(g2[s0 + half:s0 + HGRN_SUB], axis=0, keepdims=True)
        worst = -jnp.minimum(first, second)
        bound = worst if bound is None else jnp.maximum(bound, worst)
    b2 = _cumsum_rows_mxu(g2, reverse)
    return (q, k, f, b2), bound


def _hgrn_scores(head, level, h, att, reverse):
    q, k, f, b2 = head
    if att is None:
        return jnp.where(level == -1, _dot_nt(q.astype(BF16), k.astype(BF16)), 0.0)
    c = _hgrn_level_operand(q, k, f, b2, h, reverse).astype(BF16)
    return jnp.where(level == (h.bit_length() - 1), _dot_nt(c, c), att)


def _hgrn_scores_fast(head, level):
    q, k, _, b2 = head
    n_rows = q.shape[0]
    blocks = []
    for s0 in range(0, n_rows, HGRN_SUB):
        rows = slice(s0, s0 + HGRN_SUB)
        ref = b2[s0 + HGRN_SUB // 2:s0 + HGRN_SUB // 2 + 1, :]
        qs = (q[rows] * jnp.exp2(b2[rows] - ref)).astype(BF16)
        ks = (k * jnp.exp2(jnp.minimum(ref - b2, HGRN_SAFE_LOG2))).astype(BF16)
        blocks.append(_dot_nt(qs, ks))
    return jnp.where(level >= -1, jnp.concatenate(blocks, axis=0), 0.0)


def _hgrn_apply(head, att, v_ref, st_ref, reverse):
    q, k, _, b2 = head
    n_rows = q.shape[0]
    tot = b2[0:1, :] if reverse else b2[n_rows - 1:n_rows, :]
    v_bf = v_ref[...].astype(BF16)
    q_in = (q * jnp.exp2(b2)).astype(BF16)
    k_out = (k * jnp.exp2(tot - b2)).astype(BF16)
    st = st_ref[...]
    o = _dot(att.astype(BF16), v_bf) + _dot(q_in, st.astype(BF16))
    dec_col = jnp.broadcast_to(jnp.exp2(tot), (SUBLANE, tot.shape[1])).T[:, 0:1]
    st_ref[...] = dec_col * st + _dot_tn(k_out, v_bf)
    return o


def _hgrn_finish(o, z_ref, ob_ref, ng_ref, out_ref):
    o = o + ob_ref[...]
    ys = []
    for hd in range(A_HEADS):
        sl = slice(hd * A_HD, (hd + 1) * A_HD)
        oh = o[:, sl]
        ms = jnp.mean(oh * oh, axis=-1, keepdims=True)
        ys.append(oh * lax.rsqrt(ms + EPS) * ng_ref[:, sl])
    y = jnp.concatenate(ys, axis=1) * _silu(z_ref[...])
    out_ref[...] = y.astype(out_ref.dtype)


def _chunk_index(c, n_chunks, reverse):
    return n_chunks - 1 - c if reverse else c


def _hgrn(u3, u16, lb_logits, layer, *, reverse, ob=None, norm_g=None):
    batch, seq, _ = u3.shape
    n_chunks = seq // CHUNK
    depth = lb_logits.shape[0]
    final = ob is not None
    grp = REC_GROUP

    def ublk(blk):
        return pl.BlockSpec((grp, CHUNK, D_GROUP),
                            lambda bg, c: (bg, _chunk_index(c, n_chunks, reverse), blk))

    row_spec = ublk(0)
    in_specs = [ublk(BLK_A_Q), ublk(BLK16_A_I), ublk(BLK_A_FB if reverse else BLK_A_FF),
                pl.BlockSpec((depth, D_GROUP), lambda bg, c: (0, 0))]
    args = [u3, u16, u3, lb_logits]
    if final:
        in_specs += [ublk(BLK_A_Z), row_spec, pl.BlockSpec(norm_g.shape, lambda bg, c: (0, 0))]
        args += [u3, ob, norm_g]
    return pl.pallas_call(
        functools.partial(_hgrn_kernel, layer=layer, depth=depth, reverse=reverse, final=final),
        out_shape=jax.ShapeDtypeStruct((batch, seq, D_GROUP), BF16 if final else F32),
        grid=(batch // grp, n_chunks),
        in_specs=in_specs,
        out_specs=row_spec,
        scratch_shapes=[pltpu.VMEM((grp, A_HEADS, A_HD, A_HD), F32)],
        compiler_params=pltpu.CompilerParams(
            dimension_semantics=("parallel", "arbitrary"), vmem_limit_bytes=VMEM_LIMIT),
        name="hgrn_fwd" if final else "hgrn_bwd",
    )(*args)


def _split3(x):
    hi = x.astype(BF16).astype(F32)
    r1 = x - hi
    mid = r1.astype(BF16).astype(F32)
    lo = (r1 - mid).astype(BF16).astype(F32)
    return hi, mid, lo


def _conv_silu_qk(qkp_ref, qkc_ref, qkn_ref, cw_ref, reverse):
    n_rows = qkc_ref.shape[0]
    c = pl.program_id(1)
    n_chunks = pl.num_programs(1)
    cc = _chunk_index(c, n_chunks, reverse)
    halo = SUBLANE
    prev = qkp_ref[n_rows - halo:n_rows, :] * (cc > 0).astype(F32)
    nxt = qkn_ref[0:halo, :] * (cc < n_chunks - 1).astype(F32)
    xcat = jnp.concatenate([prev, qkc_ref[...], nxt], axis=0)
    acc = None
    for j in range(CONV_W):
        off = halo + j - CONV_W // 2
        term = cw_ref[j:j + 1, :] * xcat[off:off + n_rows, :]
        acc = term if acc is None else acc + term
    qk = _silu(acc)
    n_qk = C_HEADS * C_QK
    return qk[:, n_qk:].astype(BF16), (qk[:, :n_qk] * (C_QK ** -0.5)).T.astype(BF16)


def _mlstm_kernel(*refs, layer, reverse, final):
    if final:
        (k_ref, qt_ref, v_ref, gt_ref, og_ref, z_ref, hb_ref, ng_ref,
         out_ref, cst_ref, m_ref) = refs
    else:
        (qkp_ref, qkc_ref, qkn_ref, cw_ref, v_ref, gt_ref,
         out_ref, ko_ref, qto_ref, cst_ref, m_ref) = refs

    @pl.when(pl.program_id(1) == 0)
    def _():
        cst_ref[...] = jnp.zeros_like(cst_ref)
        m_ref[...] = jnp.zeros_like(m_ref)

    rows = range(v_ref.shape[0])
    if final:
        qks = [(k_ref[g], qt_ref[g]) for g in rows]
    else:
        qks = [_conv_silu_qk(qkp_ref.at[g], qkc_ref.at[g], qkn_ref.at[g], cw_ref.at[layer], reverse)
               for g in rows]
        for g in rows:
            ko_ref[g], qto_ref[g] = qks[g]
    gates = [_mlstm_gates(gt_ref.at[g], m_ref.at[g], reverse) for g in rows]
    outs = [[] for _ in rows]
    for hd in range(C_HEADS):
        weights = [_mlstm_weights(hd, qks[g], gates[g], reverse) for g in rows]
        for g in rows:
            h_t = _mlstm_apply(hd, qks[g], v_ref.at[g], gates[g], weights[g], cst_ref.at[g], reverse)
            if final:
                outs[g].append((h_t + hb_ref[g, hd * C_V:(hd + 1) * C_V, :]).T)
            else:
                out_ref[g, hd * C_V:(hd + 1) * C_V, :] = h_t
    if final:
        for g in rows:
            _mlstm_finish(jnp.concatenate(outs[g], axis=1), og_ref.at[g], z_ref.at[g],
                          ng_ref.at[layer:layer + 1], out_ref.at[g])


def _mlstm_gates(gt_ref, m_ref, reverse):
    n_rows = gt_ref.shape[1]
    n_dir = 2 * C_HEADS

    li = gt_ref[0:n_dir, :]
    b = gt_ref[n_dir:, :]
    b_last = b[:, 0:1] if reverse else b[:, n_rows - 1:n_rows]
    m_prev = m_ref[...]
    d_inter = b + m_prev
    a = (b_last - b) + li
    m_new = jnp.maximum(b_last + m_prev, jnp.max(a, axis=1, keepdims=True))
    ws = jnp.exp(a - m_new)
    decay = jnp.exp((b_last + m_prev) - m_new)
    m_ref[...] = m_new

    one8 = jnp.ones((n_dir, n_rows), F32)
    lhs = jnp.concatenate(list(_split3(li - b)) + [one8, one8, one8], axis=0).T.astype(BF16)
    rhs_all = jnp.concatenate([one8, one8, one8] + list(_split3(b)), axis=0)
    return lhs, rhs_all, d_inter, ws, decay


def _mlstm_weights(hd, qk, gates, reverse):
    lhs, rhs_all, d_inter, _, _ = gates
    k, q_t = qk
    n_rows = k.shape[0]
    n_dir = 2 * C_HEADS
    gi = (C_HEADS if reverse else 0) + hd
    gate_row = lax.broadcasted_iota(jnp.int32, rhs_all.shape, 0) & (n_dir - 1)
    si = lax.broadcasted_iota(jnp.int32, (n_rows, n_rows), 0)
    ti = lax.broadcasted_iota(jnp.int32, (n_rows, n_rows), 1)
    causal = (si >= ti) if reverse else (si <= ti)

    sl = slice(hd * C_QK, (hd + 1) * C_QK)
    rhs = jnp.where(gate_row == gi, rhs_all, 0.0).astype(BF16)
    d = jnp.where(causal, _dot(lhs, rhs), NEG)
    d_int = d_inter[gi:gi + 1, :]
    m_t = jnp.maximum(jnp.max(d, axis=0, keepdims=True), d_int)
    w = (_dot(k[:, sl], q_t[sl, :]) * jnp.exp(d - m_t)).astype(BF16)
    return w, m_t, jnp.exp(d_int - m_t)


def _mlstm_apply(hd, qk, vt_ref, gates, weights, cst_ref, reverse):
    _, _, _, ws, decay = gates
    w, m_t, g = weights
    k, q_t = qk
    n_rows = k.shape[0]
    gi = (C_HEADS if reverse else 0) + hd
    sl = slice(hd * C_QK, (hd + 1) * C_QK)
    v_t = vt_ref[hd * C_V:(hd + 1) * C_V, :]
    v_ext = jnp.concatenate([v_t, jnp.ones((C_ONES, n_rows), BF16)], axis=0)
    ws_row = ws[gi:gi + 1, :]
    v_scaled = jnp.concatenate([(v_t.astype(F32) * ws_row).astype(BF16),
                                jnp.broadcast_to(ws_row, (C_ONES, n_rows)).astype(BF16)], axis=0)
    cst = cst_ref[hd]
    num = _dot(v_ext, w) + g * _dot(cst.astype(BF16), q_t[sl, :])
    den = jnp.maximum(jnp.abs(num[C_V:C_V + 1, :]), jnp.exp(-m_t))
    cst_ref[hd] = decay[gi:gi + 1, :C_QK] * cst + _dot(v_scaled, k[:, sl])
    return num[:C_V, :] / den


def _mlstm_finish(hsum, og_ref, z_ref, ng_ref, out_ref):
    hsum = hsum * _sigmoid(og_ref[...])
    ys = []
    for hd in range(C_HEADS):
        sl = slice(hd * C_V, (hd + 1) * C_V)
        hh = hsum[:, sl]
        mu = jnp.mean(hh, axis=-1, keepdims=True)
        var = jnp.mean(jnp.square(hh - mu), axis=-1, keepdims=True)
        ys.append((hh - mu) * lax.rsqrt(var + EPS) * ng_ref[:, sl])
    y = jnp.concatenate(ys, axis=1) * _silu(z_ref[...])
    out_ref[...] = y.astype(out_ref.dtype)


def _mlstm(u3, gt, vt, conv_w, layer, *, reverse, qk=None, hb=None, norm_g=None):
    batch, seq, _ = u3.shape
    n_chunks = seq // CHUNK
    final = hb is not None
    grp = REC_GROUP
    assert CHUNK == LANE

    def chunk(c, delta=0):
        return jnp.clip(_chunk_index(c, n_chunks, reverse) + delta, 0, n_chunks - 1)

    def ublk(blk, delta=0):
        return pl.BlockSpec((grp, CHUNK, D_GROUP), lambda bg, c: (bg, chunk(c, delta), blk))

    row_spec = ublk(0)
    t_spec = lambda ch: pl.BlockSpec((grp, ch, CHUNK), lambda bg, c: (bg, 0, chunk(c)))
    gate_specs = [t_spec(N_GATE)]
    gate_args = [gt]
    n_qk = C_HEADS * C_QK
    k_spec = pl.BlockSpec((grp, CHUNK, n_qk), lambda bg, c: (bg, chunk(c), 0))
    if final:
        in_specs = [k_spec, t_spec(n_qk), t_spec(D_GROUP)] + gate_specs + [
            ublk(BLK_C_O), ublk(BLK_C_Z), t_spec(D_GROUP),
            pl.BlockSpec(norm_g.shape, lambda bg, c: (0, 0))]
        args = [qk[0], qk[1], vt] + gate_args + [u3, u3, hb, norm_g]
        out_shape = jax.ShapeDtypeStruct((batch, seq, D_GROUP), BF16)
        out_specs = row_spec
    else:
        in_specs = [ublk(BLK_C_QK, -1), ublk(BLK_C_QK), ublk(BLK_C_QK, 1),
                    pl.BlockSpec(conv_w.shape, lambda bg, c: (0, 0, 0)),
                    t_spec(D_GROUP)] + gate_specs
        args = [u3, u3, u3, conv_w, vt] + gate_args
        out_shape = (jax.ShapeDtypeStruct((batch, D_GROUP, seq), F32),
                     jax.ShapeDtypeStruct((batch, seq, n_qk), BF16),
                     jax.ShapeDtypeStruct((batch, n_qk, seq), BF16))
        out_specs = (t_spec(D_GROUP), k_spec, t_spec(n_qk))
    return pl.pallas_call(
        functools.partial(_mlstm_kernel, layer=layer, reverse=reverse, final=final),
        out_shape=out_shape,
        grid=(batch // grp, n_chunks),
        in_specs=in_specs,
        out_specs=out_specs,
        scratch_shapes=[pltpu.VMEM((grp, C_HEADS, C_V + C_ONES, C_QK), F32),
                        pltpu.VMEM((grp, 2 * C_HEADS, CHUNK), F32)],
        compiler_params=pltpu.CompilerParams(
            dimension_semantics=("parallel", "arbitrary"), vmem_limit_bytes=VMEM_LIMIT),
        name="mlstm_fwd" if final else "mlstm_bwd",
    )(*args)


def _rope_table_kernel(pos_ref, invf_ref, cos_ref, sin_ref):
    ang = invf_ref[...] * pos_ref[...].astype(F32)
    n_rows = ang.shape[1]
    half = ROPE_DIM // 2
    r = lax.broadcasted_iota(jnp.int32, ang.shape, 0)
    sn = jnp.sin(ang)
    sn = jnp.where(r < half, -sn, sn)
    e_row = lax.broadcasted_iota(jnp.int32, (ROPE_DIM, LANE), 0)
    e_lane = lax.broadcasted_iota(jnp.int32, (ROPE_DIM, LANE), 1) & (B_HD - 1)
    place = jnp.where(e_row == e_lane, 1.0, 0.0).astype(BF16)
    place3 = jnp.concatenate([place, place, place], axis=0)

    def expand(x):
        return _dot_tn(jnp.concatenate(list(_split3(x)), axis=0).astype(BF16), place3)

    lane = lax.broadcasted_iota(jnp.int32, (n_rows, LANE), 1) & (B_HD - 1)
    cos_ref[...] = jnp.where(lane < ROPE_DIM, expand(jnp.cos(ang)), 1.0)
    sin_ref[...] = expand(sn)


def _rope_tables(positions):
    rows = positions.size
    half = ROPE_DIM // 2
    inv_freq = ROPE_THETA ** (-jnp.arange(0, ROPE_DIM, 2, dtype=F32) / ROPE_DIM)
    invf = jnp.concatenate([inv_freq, inv_freq]).reshape(ROPE_DIM, 1)
    tm = ROW_TILE
    return pl.pallas_call(
        _rope_table_kernel,
        out_shape=(jax.ShapeDtypeStruct((rows, LANE), F32),) * 2,
        grid=(rows // tm,),
        in_specs=[pl.BlockSpec((1, tm), lambda i: (0, i)),
                  pl.BlockSpec((ROPE_DIM, 1), lambda i: (0, 0))],
        out_specs=(pl.BlockSpec((tm, LANE), lambda i: (i, 0)),) * 2,
        compiler_params=pltpu.CompilerParams(dimension_semantics=("parallel",)),
        name="rope_tables",
    )(positions.reshape(1, rows), invf)


def _rope(x, cos, sin):
    width = x.shape[1]
    if width > LANE:
        cos = jnp.concatenate([cos] * (width // LANE), axis=1)
        sin = jnp.concatenate([sin] * (width // LANE), axis=1)
    half = ROPE_DIM // 2
    r = lax.broadcasted_iota(jnp.int32, x.shape, 1) & (B_HD - 1)
    partner = jnp.where(r < half, pltpu.roll(x, width - half, axis=1), pltpu.roll(x, half, axis=1))
    return x * cos + partner * sin


def _swa_kernel(q_ref, z_ref, mp_ref, mc_ref, mn_ref, cp_ref, cc_ref, cn_ref,
                sp_ref, sc_ref, sn_ref, sink_ref, out_ref, *, layer):
    tile = q_ref.shape[0]
    blk = mp_ref.shape[0]
    n_sub = tile // blk
    i = pl.program_id(1)
    n_tiles = pl.num_programs(1)
    kv_w = B_KV_HEADS * B_HD
    grp = B_HEADS // B_KV_HEADS

    q = (_rope(q_ref[...], cc_ref[...], sc_ref[...]) * (B_HD ** -0.5)).astype(BF16)
    k = jnp.concatenate([
        _rope(mp_ref[:, MISC_K:MISC_K + kv_w], cp_ref[...], sp_ref[...]),
        _rope(mc_ref[:, MISC_K:MISC_K + kv_w], cc_ref[...], sc_ref[...]),
        _rope(mn_ref[:, MISC_K:MISC_K + kv_w], cn_ref[...], sn_ref[...])], axis=0).astype(BF16)
    v = jnp.concatenate([mp_ref[:, MISC_V:MISC_V + kv_w], mc_ref[:, MISC_V:MISC_V + kv_w],
                         mn_ref[:, MISC_V:MISC_V + kv_w]], axis=0)
    n_keys = v.shape[0]

    kr = lax.broadcasted_iota(jnp.int32, (3 * blk, blk), 0)
    qc = lax.broadcasted_iota(jnp.int32, (3 * blk, blk), 1)
    in_window = jnp.abs(qc - (kr - blk)) <= WINDOW
    biases = []
    for j in range(n_sub):
        valid = in_window
        if j == 0:
            valid = valid & (kr >= jnp.where(i > 0, 0, blk))
        if j == n_sub - 1:
            valid = valid & (kr < jnp.where(i < n_tiles - 1, 3 * blk, 2 * blk))
        biases.append(jnp.where(valid, 0.0, NEG))

    ones = jnp.ones((n_keys, B_HD), F32)
    vts = [jnp.concatenate([v[:, kvh * B_HD:(kvh + 1) * B_HD], ones], axis=1).T.astype(BF16)
           for kvh in range(B_KV_HEADS)]

    tiles = [[] for _ in range(n_sub)]
    pairs = [[] for _ in range(n_sub)]
    for hq in range(B_HEADS):
        kvh = hq // grp
        ksl = slice(kvh * B_HD, (kvh + 1) * B_HD)
        sk = sink_ref[layer:layer + 1, hq:hq + 1]
        for j in range(n_sub):
            keys = slice(j * blk, (j + 3) * blk)
            qh = q[j * blk:(j + 1) * blk, hq * B_HD:(hq + 1) * B_HD]
            s = _dot_nt(k[keys, ksl], qh)
            s = jnp.concatenate([s[:blk] + biases[j][:blk], s[blk:2 * blk],
                                 s[2 * blk:] + biases[j][2 * blk:]], axis=0)
            m = jnp.maximum(jnp.max(s, axis=0, keepdims=True), sk)
            p = jnp.exp(s - m).astype(BF16)
            o = _dot(vts[kvh][:, keys], p)
            denom = o[B_HD:B_HD + 1, :] + jnp.exp(sk - m)
            pairs[j].append(o[:B_HD, :] / denom)
            if len(pairs[j]) == LANE // B_HD:
                tiles[j].append(jnp.concatenate(pairs[j], axis=0).T)
                pairs[j] = []
    for j in range(n_sub):
        rows = slice(j * blk, (j + 1) * blk)
        y = jnp.concatenate(tiles[j], axis=1) * _silu(z_ref[rows, :])
        out_ref[rows, :] = y.astype(out_ref.dtype)


def _swa(u, cos_t, sin_t, sink, layer, batch):
    rows = u.shape[0]
    n_blocks = rows // batch // CHUNK
    n_sub = SWA_TILE // CHUNK
    n_tiles = n_blocks // n_sub

    def tile_spec(width, blk):
        return pl.BlockSpec((SWA_TILE, width), lambda bi, i: (bi * n_tiles + i, blk))

    def halo_spec(width, blk, after):
        def index(bi, i):
            nb = (i + 1) * n_sub if after else i * n_sub - 1
            return (bi * n_blocks + jnp.clip(nb, 0, n_blocks - 1), blk)
        return pl.BlockSpec((CHUNK, width), index)

    def table_specs():
        return [halo_spec(LANE, 0, False), tile_spec(LANE, 0), halo_spec(LANE, 0, True)]

    return pl.pallas_call(
        functools.partial(_swa_kernel, layer=layer),
        out_shape=jax.ShapeDtypeStruct((rows, D_GROUP), BF16),
        grid=(batch, n_tiles),
        in_specs=[tile_spec(D_GROUP, BLK_B_Q), tile_spec(D_GROUP, BLK_B_Z),
                  halo_spec(D_GROUP, BLK_MISC, False), tile_spec(D_GROUP, BLK_MISC),
                  halo_spec(D_GROUP, BLK_MISC, True)] + table_specs() + table_specs() + [
                  pl.BlockSpec(sink.shape, lambda bi, i: (0, 0))],
        out_specs=tile_spec(D_GROUP, 0),
        compiler_params=pltpu.CompilerParams(
            dimension_semantics=("parallel", "parallel"), vmem_limit_bytes=VMEM_LIMIT),
        name="window_attn",
    )(u, u, u, u, u, cos_t, cos_t, cos_t, sin_t, sin_t, sin_t, sink)


def _mem_kv_kernel(mem_ref, g_ref, w_ref, kt_ref, v_ref, *, layer):
    x = mem_ref[...]
    ms = jnp.mean(x * x, axis=-1, keepdims=True)
    h = (x * lax.rsqrt(ms + EPS) * g_ref[layer:layer + 1, :]).astype(BF16)
    kv = _dot(h, w_ref[...].astype(BF16))
    kt_ref[...] = kv[:, :D_GROUP].T.astype(kt_ref.dtype)
    v_ref[...] = kv[:, D_GROUP:].astype(v_ref.dtype)


def _mem_kv(mem, g_all, w_all, layer):
    batch, m_len, d = mem.shape
    n_out = w_all.shape[2]
    return pl.pallas_call(
        functools.partial(_mem_kv_kernel, layer=layer),
        out_shape=(jax.ShapeDtypeStruct((batch, D_GROUP, m_len), BF16),
                   jax.ShapeDtypeStruct((batch, m_len, D_GROUP), BF16)),
        grid=(batch,),
        in_specs=[pl.BlockSpec((None, m_len, d), lambda bi: (bi, 0, 0)),
                  pl.BlockSpec(g_all.shape, lambda bi: (0, 0)),
                  pl.BlockSpec((None, d, n_out), lambda bi: (layer, 0, 0))],
        out_specs=(pl.BlockSpec((None, D_GROUP, m_len), lambda bi: (bi, 0, 0)),
                   pl.BlockSpec((None, m_len, D_GROUP), lambda bi: (bi, 0, 0))),
        compiler_params=pltpu.CompilerParams(
            dimension_semantics=("parallel",), vmem_limit_bytes=VMEM_LIMIT),
        name="mem_kv",
    )(mem, g_all, w_all)


def _mem_attn_out_kernel(q_ref, z_ref, kt_ref, v_ref, ya_ref, yb_ref, yc_ref, w_ref, x_ref, g_ref,
                         o_ref, *, final):
    q = q_ref[...].astype(BF16)
    outs = []
    for hd in range(D_HEADS):
        sl = slice(hd * D_HD, (hd + 1) * D_HD)
        s = _dot(q[:, sl], kt_ref[sl, :])
        m = jnp.max(s, axis=-1, keepdims=True)
        p = jnp.exp2((s - m) * (D_HD ** -0.5 * LOG2_E))
        den = jnp.sum(p, axis=-1, keepdims=True)
        outs.append(_dot(p.astype(BF16), v_ref[:, sl]) / den)
    yd = (jnp.concatenate(outs, axis=1) * _silu(z_ref[...])).astype(BF16)

    acc = x_ref[...]
    for i, y in enumerate((ya_ref[...], yb_ref[...], yc_ref[...], yd)):
        acc = acc + _dot(y, w_ref[i * D_GROUP:(i + 1) * D_GROUP, :].astype(BF16))
    if final:
        ms = jnp.mean(acc * acc, axis=-1, keepdims=True)
        acc = acc * lax.rsqrt(ms + EPS) * g_ref[...]
    o_ref[...] = acc


def _mem_attn_out(u, u16, kt, v, ya, yb, yc, w_all, xf, g, layer, batch, final):
    rows, d = xf.shape
    tm = ROW_TILE
    per_batch = rows // batch // tm
    row = lambda width, blk: pl.BlockSpec((tm, width), lambda bi, i: (bi * per_batch + i, blk))
    return pl.pallas_call(
        functools.partial(_mem_attn_out_kernel, final=final),
        out_shape=jax.ShapeDtypeStruct((rows, d), F32),
        grid=(batch, per_batch),
        in_specs=[row(D_GROUP, BLK16_D_Q), row(D_GROUP, BLK_D_Z),
                  pl.BlockSpec((None,) + kt.shape[1:], lambda bi, i: (bi, 0, 0)),
                  pl.BlockSpec((None,) + v.shape[1:], lambda bi, i: (bi, 0, 0)),
                  row(D_GROUP, 0), row(D_GROUP, 0), row(D_GROUP, 0),
                  pl.BlockSpec((None, 4 * D_GROUP, d), lambda bi, i: (layer, 0, 0)),
                  row(d, 0),
                  pl.BlockSpec((1, d), lambda bi, i: (0, 0))],
        out_specs=row(d, 0),
        compiler_params=pltpu.CompilerParams(
            dimension_semantics=("parallel", "parallel"), vmem_limit_bytes=VMEM_LIMIT),
        name="mem_attn_out",
    )(u16, u, kt, v, ya, yb, yc, w_all, xf, g.reshape(1, d))


def kernel(x, mem, positions, norm_g, w_in, hgrn_lb_logits, hgrn_norm_g, attn_sink, mlstm_conv_w,
           mlstm_gate_b, mlstm_norm_g, mem_norm_g, w_mem_kv, w_out, final_norm_g):
    batch, seq, d = x.shape
    depth = w_in.shape[0]
    assert all(seq % t == 0 for t in (CHUNK, ROW_TILE, IN_ROW_TILE, SWA_TILE))
    assert batch % REC_GROUP == 0
    xf = x.reshape(batch * seq, d)
    cos_t, sin_t = _rope_tables(positions)
    w_all = _prep_w_in(w_in)
    gate_b = mlstm_gate_b.reshape(depth, N_GATE, 1)
    for layer in range(depth):
        u, u16, gt, vt = _in_proj(xf, norm_g, w_all, gate_b, layer, batch)
        u3 = u.reshape(batch, seq, u.shape[1])
        u16_3 = u16.reshape(batch, seq, u16.shape[1])
        ob = _hgrn(u3, u16_3, hgrn_lb_logits, layer, reverse=True)
        ya = _hgrn(u3, u16_3, hgrn_lb_logits, layer, reverse=False, ob=ob, norm_g=hgrn_norm_g)
        yb = _swa(u, cos_t, sin_t, attn_sink, layer, batch)
        hb, k_c, qt_c = _mlstm(u3, gt, vt, mlstm_conv_w, layer, reverse=True)
        yc = _mlstm(u3, gt, vt, None, layer, reverse=False, qk=(k_c, qt_c), hb=hb,
                    norm_g=mlstm_norm_g)
        mem_kt, mem_v = _mem_kv(mem, mem_norm_g, w_mem_kv, layer)
        xf = _mem_attn_out(u, u16, mem_kt, mem_v, ya.reshape(batch * seq, D_GROUP), yb,
                           yc.reshape(batch * seq, D_GROUP), w_out, xf, final_norm_g, layer, batch,
                           final=(layer == depth - 1))
    return xf.reshape(batch, seq, d)
```

```python
import functools

import jax
import jax.numpy as jnp
import numpy as np
from jax import lax
from jax.experimental import pallas as pl
from jax.experimental.pallas import tpu as pltpu

F32 = jnp.float32
BF16 = jnp.bfloat16

D_GROUP = 512
A_HEADS, A_HD = 4, 128
B_HEADS, B_KV_HEADS, B_HD = 8, 2, 64
WINDOW = 128
ROPE_THETA = 500000.0
ROPE_DIM = 16
C_HEADS, C_QK, C_V = 4, 64, 128
CONV_W = 5
D_HEADS, D_HD = 4, 128
EPS = 1e-6
IN_SIZES = (
    D_GROUP, D_GROUP, D_GROUP, D_GROUP, D_GROUP,
    B_HEADS * B_HD, B_KV_HEADS * B_HD, B_KV_HEADS * B_HD, D_GROUP,
    C_HEADS * C_QK, C_HEADS * C_QK, C_HEADS * C_V, D_GROUP, D_GROUP,
    2 * C_HEADS, 2 * C_HEADS,
    D_GROUP, D_GROUP,
)
SPLIT_POINTS = tuple(int(s) for s in np.cumsum(IN_SIZES)[:-1])

LANE = 128
SUBLANE = 8
VMEM_LIMIT = 48 * 1024 * 1024

CHUNK = 128
ROW_TILE = 1024
IN_ROW_TILE = 512
IN_PROJ_VMEM_LIMIT = 56 * 1024 * 1024
W_PREP_TILE = 256
HGRN_SUB = 64
HGRN_SAFE_LOG2 = 100.0
SWA_TILE = 1024
REC_GROUP = 4
NEG = -1e30
LOG2_E = 1.4426950408889634

(BLK_A_Q, BLK_A_FF, BLK_A_FB, BLK_A_Z, BLK_B_Q, BLK_B_Z, BLK_C_QK, BLK_C_O,
 BLK_C_Z, BLK_D_Z, BLK_MISC) = range(11)
N_BLK = 11
BLK_A_GF, BLK_A_GB = N_BLK, N_BLK + 1
N_UBLK = N_BLK + 2
BLK16_A_I, BLK16_D_Q = range(2)
N_BLK16 = 2
W_ROW_C_V = (N_BLK + N_BLK16) * D_GROUP
N_GATE = 4 * C_HEADS
W_ROW_GATE = W_ROW_C_V + D_GROUP
N_W_ROWS = W_ROW_GATE + N_GATE
MISC_K, MISC_V = 0, 128
C_ONES = 16


def _dot(a, b):
    return jnp.dot(a, b, preferred_element_type=F32)


def _dot_nt(a, b):
    return lax.dot_general(a, b, (((1,), (1,)), ((), ())), preferred_element_type=F32)


def _dot_tn(a, b):
    return lax.dot_general(a, b, (((0,), (0,)), ((), ())), preferred_element_type=F32)


def _sigmoid(x):
    return 1.0 / (1.0 + jnp.exp(-x))


def _silu(x):
    return x * _sigmoid(x)


def _log_sigmoid(x):
    return jnp.minimum(x, 0.0) - jnp.log(1.0 + jnp.exp(-jnp.abs(x)))


def _chunk_cumsum_lanes(x, reverse):
    n = x.shape[1]
    pos = lax.broadcasted_iota(jnp.int32, x.shape, 1) & (CHUNK - 1)
    s = 1
    while s < CHUNK:
        if reverse:
            x = x + jnp.where(pos < CHUNK - s, pltpu.roll(x, n - s, axis=1), 0.0)
        else:
            x = x + jnp.where(pos >= s, pltpu.roll(x, s, axis=1), 0.0)
        s *= 2
    return x


def _hgrn_lower_bound(lbl_ref, layer):
    rows = [lbl_ref[j:j + 1, :] for j in range(lbl_ref.shape[0])]
    mx = functools.reduce(jnp.maximum, rows)
    es = [jnp.exp(r - mx) for r in rows]
    tot = functools.reduce(lambda a, c: a + c, es)
    lb = jnp.zeros_like(rows[0])
    for j in range(1, layer + 1):
        lb = lb + es[j] / tot
    return lb


def _hgrn_gates(z, lbl_ref, layer):
    e = jnp.exp(-jnp.abs(z))
    r = 1.0 / (1.0 + e)
    er = e * r
    pos = z >= 0.0
    f = jnp.where(pos, r, er)
    k = jnp.where(pos, er, r)
    if layer > 0:
        lb = _hgrn_lower_bound(lbl_ref, layer)
        f = lb + (1.0 - lb) * f
        k = (1.0 - lb) * k
    return k, jnp.log2(f)


def _in_proj_kernel(x_ref, g_ref, w_ref, gb_ref, lbl_ref, u_ref, u16_ref, gt_ref, vt_ref, *, layer):
    x = x_ref[...]
    ms = jnp.mean(x * x, axis=-1, keepdims=True)
    h = (x * lax.rsqrt(ms + EPS) * g_ref[layer:layer + 1, :]).astype(BF16)
    vg = _dot_nt(w_ref[W_ROW_C_V:W_ROW_GATE + N_GATE, :], h)
    vt_ref[...] = vg[:D_GROUP, :].astype(BF16)
    gates = vg[D_GROUP:, :] + gb_ref[layer]
    n_dir = 2 * C_HEADS
    log_f = _log_sigmoid(gates[n_dir:, :])
    is_fwd = lax.broadcasted_iota(jnp.int32, log_f.shape, 0) < C_HEADS
    cum = jnp.where(is_fwd, _chunk_cumsum_lanes(log_f, False), _chunk_cumsum_lanes(log_f, True))
    gt_ref[...] = jnp.concatenate([gates[:n_dir, :], cum], axis=0)
    log2_f_blk = {BLK_A_FF: BLK_A_GF, BLK_A_FB: BLK_A_GB}
    for j in range(N_BLK):
        res = _dot_nt(h, w_ref[j * D_GROUP:(j + 1) * D_GROUP, :])
        if j in log2_f_blk:
            res, log2_f = _hgrn_gates(res, lbl_ref, layer)
            g = log2_f_blk[j]
            u_ref[:, g * D_GROUP:(g + 1) * D_GROUP] = log2_f
        u_ref[:, j * D_GROUP:(j + 1) * D_GROUP] = res
    for j in range(N_BLK16):
        rows = slice((N_BLK + j) * D_GROUP, (N_BLK + j + 1) * D_GROUP)
        u16_ref[:, j * D_GROUP:(j + 1) * D_GROUP] = _dot_nt(h, w_ref[rows, :]).astype(BF16)


def _in_proj(xf, g_all, w_all, gate_b, lb_logits, layer, batch):
    rows, d = xf.shape
    tm = IN_ROW_TILE
    per_batch = rows // batch // tm
    return pl.pallas_call(
        functools.partial(_in_proj_kernel, layer=layer),
        out_shape=(jax.ShapeDtypeStruct((rows, N_UBLK * D_GROUP), F32),
                   jax.ShapeDtypeStruct((rows, N_BLK16 * D_GROUP), BF16),
                   jax.ShapeDtypeStruct((batch, N_GATE, rows // batch), F32),
                   jax.ShapeDtypeStruct((batch, D_GROUP, rows // batch), BF16)),
        grid=(rows // tm,),
        in_specs=[
            pl.BlockSpec((tm, d), lambda i: (i, 0)),
            pl.BlockSpec(g_all.shape, lambda i: (0, 0)),
            pl.BlockSpec((None, N_W_ROWS, d), lambda i: (layer, 0, 0), pipeline_mode=pl.Buffered(1)),
            pl.BlockSpec(gate_b.shape, lambda i: (0, 0, 0)),
            pl.BlockSpec(lb_logits.shape, lambda i: (0, 0)),
        ],
        out_specs=(
            pl.BlockSpec((tm, N_UBLK * D_GROUP), lambda i: (i, 0)),
            pl.BlockSpec((tm, N_BLK16 * D_GROUP), lambda i: (i, 0)),
            pl.BlockSpec((None, N_GATE, tm), lambda i: (i // per_batch, 0, i % per_batch)),
            pl.BlockSpec((None, D_GROUP, tm), lambda i: (i // per_batch, 0, i % per_batch)),
        ),
        compiler_params=pltpu.CompilerParams(
            dimension_semantics=("parallel",), vmem_limit_bytes=IN_PROJ_VMEM_LIMIT),
        name="in_proj",
    )(xf, g_all, w_all, gate_b, lb_logits)


def _w_in_segments():
    names = ("a_q", "a_i", "a_ff", "a_fb", "a_z", "b_q", "b_k", "b_v", "b_z",
             "c_q", "c_k", "c_v", "c_o", "c_z", "c_ig", "c_fg", "d_q", "d_z")
    src = dict(zip(names, zip((0,) + SPLIT_POINTS, IN_SIZES)))
    blocks = {BLK_A_Q: ("a_q",), BLK_A_FF: ("a_ff",), BLK_A_FB: ("a_fb",), BLK_A_Z: ("a_z",),
              BLK_B_Q: ("b_q",), BLK_B_Z: ("b_z",), BLK_C_QK: ("c_q", "c_k"), BLK_C_O: ("c_o",),
              BLK_C_Z: ("c_z",), BLK_D_Z: ("d_z",), BLK_MISC: ("b_k", "b_v"),
              N_BLK + BLK16_A_I: ("a_i",), N_BLK + BLK16_D_Q: ("d_q",)}
    placed = []
    for blk, group in sorted(blocks.items()):
        dst = blk * D_GROUP
        for name in group:
            placed.append((name, dst))
            dst += src[name][1]
    placed += [("c_v", W_ROW_C_V), ("c_ig", W_ROW_GATE), ("c_fg", W_ROW_GATE + src["c_ig"][1])]
    segs, gaps, end = [], [], 0
    for name, dst in placed:
        off, width = src[name]
        if dst > end:
            gaps.append((end, dst))
        if segs and segs[-1][0] + segs[-1][1] == off and segs[-1][2] + segs[-1][1] == dst:
            segs[-1] = (segs[-1][0], segs[-1][1] + width, segs[-1][2])
        else:
            segs.append((off, width, dst))
        end = dst + width
    assert end == N_W_ROWS
    return tuple(segs), tuple(gaps)


def _w_prep_kernel(wt_ref, wc_ref):
    segs, gaps = _w_in_segments()
    for off, width, dst in segs:
        wc_ref[dst:dst + width, :] = wt_ref[off:off + width, :].astype(BF16)
    for lo, hi in gaps:
        wc_ref[lo:hi, :] = jnp.zeros((hi - lo, wc_ref.shape[1]), BF16)


def _prep_w_in(w):
    depth, d, d_in = w.shape
    wt = jnp.swapaxes(w, 1, 2)
    tk = W_PREP_TILE
    return pl.pallas_call(
        _w_prep_kernel,
        out_shape=jax.ShapeDtypeStruct((depth, N_W_ROWS, d), BF16),
        grid=(depth, d // tk),
        in_specs=[pl.BlockSpec((None, d_in, tk), lambda l, i: (l, 0, i))],
        out_specs=pl.BlockSpec((None, N_W_ROWS, tk), lambda l, i: (l, 0, i)),
        compiler_params=pltpu.CompilerParams(
            dimension_semantics=("parallel", "parallel"), vmem_limit_bytes=VMEM_LIMIT),
        name="w_prep",
    )(wt)


def _level_ref(b, h, reverse):
    n_rows, n = b.shape
    blk = 2 * h
    r0 = h if reverse else h - 1
    if blk >= SUBLANE:
        pieces = [jnp.broadcast_to(b[s + r0:s + r0 + 1, :], (blk, n)) for s in range(0, n_rows, blk)]
        return pieces[0] if len(pieces) == 1 else jnp.concatenate(pieces, axis=0)
    res = lax.broadcasted_iota(jnp.int32, b.shape, 0) & (blk - 1)
    out = None
    for m in range(blk):
        shift = (m - r0) % n_rows
        cand = b if shift == 0 else pltpu.roll(b, shift, axis=0)
        out = cand if out is None else jnp.where(res == m, cand, out)
    return out


def _cumsum_rows_mxu(x, reverse):
    n = x.shape[0]
    r = lax.broadcasted_iota(jnp.int32, (n, n), 0)
    c = lax.broadcasted_iota(jnp.int32, (n, n), 1)
    tri = jnp.where((c >= r) if reverse else (c <= r), 1.0, 0.0).astype(BF16)
    hi = x.astype(BF16)
    r1 = x - hi.astype(F32)
    mid = r1.astype(BF16)
    lo = (r1 - mid.astype(F32)).astype(BF16)
    return _dot(jnp.concatenate([tri, tri, tri], axis=1), jnp.concatenate([hi, mid, lo], axis=0))


def _hgrn_level_operand(q, k, g2, b2, h, reverse):
    n_rows = q.shape[0]
    blk = 2 * h
    r0 = h if reverse else h - 1
    if h >= SUBLANE:
        pieces = []
        for s0 in range(0, n_rows, blk):
            ref = b2[s0 + r0:s0 + r0 + 1, :]
            lo, up = slice(s0, s0 + h), slice(s0 + h, s0 + blk)
            if reverse:
                pieces += [q[lo] * jnp.exp2(b2[lo] - ref), k[up] * jnp.exp2(ref - b2[up])]
            else:
                pieces += [k[lo] * jnp.exp2(ref - b2[lo]), q[up] * jnp.exp2(b2[up] - ref)]
        return jnp.concatenate(pieces, axis=0)
    row = lax.broadcasted_iota(jnp.int32, q.shape, 0)
    is_q = ((row & h) == 0) if reverse else ((row & h) != 0)
    qk = jnp.where(is_q, q, k)
    f = jnp.exp2(g2)
    if h == 1:
        return jnp.where(is_q, q * f, k)
    if h == 2:
        res = row & 3
        f_dn = pltpu.roll(f, 1, axis=0)
        f_up = pltpu.roll(f, n_rows - 1, axis=0)
        if reverse:
            e = jnp.where(res == 0, f * f_up, jnp.where(res == 1, f, jnp.where(res == 2, 1.0, f_dn)))
        else:
            e = jnp.where(res == 0, f_up, jnp.where(res == 1, 1.0, jnp.where(res == 2, f, f_dn * f)))
        return qk * e
    return qk * jnp.exp2(-jnp.abs(b2 - _level_ref(b2, h, reverse)))


def _hgrn_kernel(*refs, layer, reverse, final):
    if final:
        q_ref, v_ref, k_ref, g2_ref, z_ref, ob_ref, ng_ref, out_ref, st_ref = refs
    else:
        q_ref, v_ref, k_ref, g2_ref, out_ref, st_ref = refs

    @pl.when(pl.program_id(1) == 0)
    def _():
        st_ref[...] = jnp.zeros_like(st_ref)

    rows = range(q_ref.shape[0])
    n_rows = q_ref.shape[1]
    ti = lax.broadcasted_iota(jnp.int32, (n_rows, n_rows), 0)
    si = lax.broadcasted_iota(jnp.int32, (n_rows, n_rows), 1)
    ahead = (ti < si) if reverse else (ti > si)
    level = jnp.where(ahead, 31 - lax.clz(ti ^ si), jnp.where(ti == si, -1, -2))
    preps, bounds = zip(*[_hgrn_prep(q_ref.at[g], k_ref.at[g], g2_ref.at[g], reverse)
                          for g in rows])

    def run(fast):
        outs = [[] for _ in rows]
        for hd in range(A_HEADS):
            sl = slice(hd * A_HD, (hd + 1) * A_HD)
            heads = [tuple(a[:, sl] for a in preps[g]) for g in rows]
            if fast:
                atts = [_hgrn_scores_fast(heads[g], level) for g in rows]
            else:
                atts = [_hgrn_scores(heads[g], level, n_rows, None, reverse) for g in rows]
                h = n_rows // 2
                while h >= 1:
                    atts = [_hgrn_scores(heads[g], level, h, atts[g], reverse) for g in rows]
                    h //= 2
            for g in rows:
                outs[g].append(_hgrn_apply(heads[g], atts[g], v_ref.at[g, :, sl],
                                           st_ref.at[g, hd], reverse))
        for g in rows:
            o = jnp.concatenate(outs[g], axis=1)
            if final:
                _hgrn_finish(o, z_ref.at[g], ob_ref.at[g], ng_ref.at[layer:layer + 1],
                             out_ref.at[g])
            else:
                out_ref[g] = o

    safe = jnp.max(functools.reduce(jnp.maximum, bounds)) <= HGRN_SAFE_LOG2

    @pl.when(safe)
    def _():
        run(True)

    @pl.when(jnp.logical_not(safe))
    def _():
        run(False)


def _hgrn_prep(q_ref, k_ref, g2_ref, reverse):
    n_rows = q_ref.shape[0]
    q = q_ref[...]
    k = k_ref[...]
    g2 = g2_ref[...]
    half = HGRN_SUB // 2
    bound = None
    for s0 in range(0, n_rows, HGRN_SUB):
        first = jnp.sum(g2[s0:s0 + half], axis=0, keepdims=True) + g2[s0 + half:s0 + half + 1]
        second = jnp.sum(g2[s0 + half:s0 + HGRN_SUB], axis=0, keepdims=True)
        worst = -jnp.minimum(first, second)
        bound = worst if bound is None else jnp.maximum(bound, worst)
    b2 = _cumsum_rows_mxu(g2, reverse)
    return (q, k, g2, b2), bound


def _hgrn_scores(head, level, h, att, reverse):
    q, k, g2, b2 = head
    if att is None:
        return jnp.where(level == -1, _dot_nt(q.astype(BF16), k.astype(BF16)), 0.0)
    c = _hgrn_level_operand(q, k, g2, b2, h, reverse).astype(BF16)
    return jnp.where(level == (h.bit_length() - 1), _dot_nt(c, c), att)


def _hgrn_scores_fast(head, level):
    q, k, _, b2 = head
    n_rows = q.shape[0]
    blocks = []
    for s0 in range(0, n_rows, HGRN_SUB):
        rows = slice(s0, s0 + HGRN_SUB)
        ref = b2[s0 + HGRN_SUB // 2:s0 + HGRN_SUB // 2 + 1, :]
        qs = (q[rows] * jnp.exp2(b2[rows] - ref)).astype(BF16)
        ks = (k * jnp.exp2(jnp.minimum(ref - b2, HGRN_SAFE_LOG2))).astype(BF16)
        blocks.append(_dot_nt(qs, ks))
    return jnp.where(level >= -1, jnp.concatenate(blocks, axis=0), 0.0)


def _hgrn_apply(head, att, v_ref, st_ref, reverse):
    q, k, _, b2 = head
    n_rows = q.shape[0]
    tot = b2[0:1, :] if reverse else b2[n_rows - 1:n_rows, :]
    v_bf = v_ref[...].astype(BF16)
    q_in = (q * jnp.exp2(b2)).astype(BF16)
    k_out = (k * jnp.exp2(tot - b2)).astype(BF16)
    st = st_ref[...]
    o = _dot(att.astype(BF16), v_bf) + _dot(q_in, st.astype(BF16))
    dec_col = jnp.broadcast_to(jnp.exp2(tot), (SUBLANE, tot.shape[1])).T[:, 0:1]
    st_ref[...] = dec_col * st + _dot_tn(k_out, v_bf)
    return o


def _hgrn_finish(o, z_ref, ob_ref, ng_ref, out_ref):
    o = o + ob_ref[...]
    ys = []
    for hd in range(A_HEADS):
        sl = slice(hd * A_HD, (hd + 1) * A_HD)
        oh = o[:, sl]
        ms = jnp.mean(oh * oh, axis=-1, keepdims=True)
        ys.append(oh * lax.rsqrt(ms + EPS) * ng_ref[:, sl])
    y = jnp.concatenate(ys, axis=1) * _silu(z_ref[...])
    out_ref[...] = y.astype(out_ref.dtype)


def _chunk_index(c, n_chunks, reverse):
    return n_chunks - 1 - c if reverse else c


def _hgrn(u3, u16, layer, *, reverse, ob=None, norm_g=None):
    batch, seq, _ = u3.shape
    n_chunks = seq // CHUNK
    final = ob is not None
    grp = REC_GROUP

    def ublk(blk):
        return pl.BlockSpec((grp, CHUNK, D_GROUP),
                            lambda bg, c: (bg, _chunk_index(c, n_chunks, reverse), blk))

    row_spec = ublk(0)
    in_specs = [ublk(BLK_A_Q), ublk(BLK16_A_I), ublk(BLK_A_FB if reverse else BLK_A_FF),
                ublk(BLK_A_GB if reverse else BLK_A_GF)]
    args = [u3, u16, u3, u3]
    if final:
        in_specs += [ublk(BLK_A_Z), row_spec, pl.BlockSpec(norm_g.shape, lambda bg, c: (0, 0))]
        args += [u3, ob, norm_g]
    return pl.pallas_call(
        functools.partial(_hgrn_kernel, layer=layer, reverse=reverse, final=final),
        out_shape=jax.ShapeDtypeStruct((batch, seq, D_GROUP), BF16 if final else F32),
        grid=(batch // grp, n_chunks),
        in_specs=in_specs,
        out_specs=row_spec,
        scratch_shapes=[pltpu.VMEM((grp, A_HEADS, A_HD, A_HD), F32)],
        compiler_params=pltpu.CompilerParams(
            dimension_semantics=("parallel", "arbitrary"), vmem_limit_bytes=VMEM_LIMIT),
        name="hgrn_fwd" if final else "hgrn_bwd",
    )(*args)


def _split3(x):
    hi = x.astype(BF16).astype(F32)
    r1 = x - hi
    mid = r1.astype(BF16).astype(F32)
    lo = (r1 - mid).astype(BF16).astype(F32)
    return hi, mid, lo


def _conv_silu_qk(qkp_ref, qkc_ref, qkn_ref, cw_ref, reverse):
    n_rows = qkc_ref.shape[0]
    c = pl.program_id(1)
    n_chunks = pl.num_programs(1)
    cc = _chunk_index(c, n_chunks, reverse)
    halo = SUBLANE
    prev = qkp_ref[n_rows - halo:n_rows, :] * (cc > 0).astype(F32)
    nxt = qkn_ref[0:halo, :] * (cc < n_chunks - 1).astype(F32)
    xcat = jnp.concatenate([prev, qkc_ref[...], nxt], axis=0)
    acc = None
    for j in range(CONV_W):
        off = halo + j - CONV_W // 2
        term = cw_ref[j:j + 1, :] * xcat[off:off + n_rows, :]
        acc = term if acc is None else acc + term
    qk = _silu(acc)
    n_qk = C_HEADS * C_QK
    return qk[:, n_qk:].astype(BF16), (qk[:, :n_qk] * (C_QK ** -0.5)).T.astype(BF16)


def _mlstm_kernel(*refs, layer, reverse, final):
    if final:
        (k_ref, qt_ref, v_ref, gt_ref, og_ref, z_ref, hb_ref, ng_ref,
         out_ref, cst_ref, m_ref) = refs
    else:
        (qkp_ref, qkc_ref, qkn_ref, cw_ref, v_ref, gt_ref,
         out_ref, ko_ref, qto_ref, cst_ref, m_ref) = refs

    @pl.when(pl.program_id(1) == 0)
    def _():
        cst_ref[...] = jnp.zeros_like(cst_ref)
        m_ref[...] = jnp.zeros_like(m_ref)

    rows = range(v_ref.shape[0])
    if final:
        qks = [(k_ref[g], qt_ref[g]) for g in rows]
    else:
        qks = [_conv_silu_qk(qkp_ref.at[g], qkc_ref.at[g], qkn_ref.at[g], cw_ref.at[layer], reverse)
               for g in rows]
        for g in rows:
            ko_ref[g], qto_ref[g] = qks[g]
    gates = [_mlstm_gates(gt_ref.at[g], m_ref.at[g], reverse) for g in rows]
    outs = [[] for _ in rows]
    for hd in range(C_HEADS):
        weights = [_mlstm_weights(hd, qks[g], gates[g], reverse) for g in rows]
        for g in rows:
            h_t = _mlstm_apply(hd, qks[g], v_ref.at[g], gates[g], weights[g], cst_ref.at[g], reverse)
            if final:
                outs[g].append((h_t + hb_ref[g, hd * C_V:(hd + 1) * C_V, :]).T)
            else:
                out_ref[g, hd * C_V:(hd + 1) * C_V, :] = h_t
    if final:
        for g in rows:
            _mlstm_finish(jnp.concatenate(outs[g], axis=1), og_ref.at[g], z_ref.at[g],
                          ng_ref.at[layer:layer + 1], out_ref.at[g])


def _mlstm_gates(gt_ref, m_ref, reverse):
    n_rows = gt_ref.shape[1]
    n_dir = 2 * C_HEADS

    li = gt_ref[0:n_dir, :]
    b = gt_ref[n_dir:, :]
    b_last = b[:, 0:1] if reverse else b[:, n_rows - 1:n_rows]
    m_prev = m_ref[...]
    d_inter = b + m_prev
    a = (b_last - b) + li
    m_new = jnp.maximum(b_last + m_prev, jnp.max(a, axis=1, keepdims=True))
    ws = jnp.exp(a - m_new)
    decay = jnp.exp((b_last + m_prev) - m_new)
    m_ref[...] = m_new

    one8 = jnp.ones((n_dir, n_rows), F32)
    lhs = jnp.concatenate(list(_split3(li - b)) + [one8, one8, one8], axis=0).T.astype(BF16)
    rhs_all = jnp.concatenate([one8, one8, one8] + list(_split3(b)), axis=0)
    return lhs, rhs_all, d_inter, ws, decay


def _mlstm_weights(hd, qk, gates, reverse):
    lhs, rhs_all, d_inter, _, _ = gates
    k, q_t = qk
    n_rows = k.shape[0]
    n_dir = 2 * C_HEADS
    gi = (C_HEADS if reverse else 0) + hd
    gate_row = lax.broadcasted_iota(jnp.int32, rhs_all.shape, 0) & (n_dir - 1)
    si = lax.broadcasted_iota(jnp.int32, (n_rows, n_rows), 0)
    ti = lax.broadcasted_iota(jnp.int32, (n_rows, n_rows), 1)
    causal = (si >= ti) if reverse else (si <= ti)

    sl = slice(hd * C_QK, (hd + 1) * C_QK)
    rhs = jnp.where(gate_row == gi, rhs_all, 0.0).astype(BF16)
    d = jnp.where(causal, _dot(lhs, rhs), NEG)
    d_int = d_inter[gi:gi + 1, :]
    m_t = jnp.maximum(jnp.max(d, axis=0, keepdims=True), d_int)
    w = (_dot(k[:, sl], q_t[sl, :]) * jnp.exp(d - m_t)).astype(BF16)
    return w, m_t, jnp.exp(d_int - m_t)


def _mlstm_apply(hd, qk, vt_ref, gates, weights, cst_ref, reverse):
    _, _, _, ws, decay = gates
    w, m_t, g = weights
    k, q_t = qk
    n_rows = k.shape[0]
    gi = (C_HEADS if reverse else 0) + hd
    sl = slice(hd * C_QK, (hd + 1) * C_QK)
    v_t = vt_ref[hd * C_V:(hd + 1) * C_V, :]
    v_ext = jnp.concatenate([v_t, jnp.ones((C_ONES, n_rows), BF16)], axis=0)
    ws_row = ws[gi:gi + 1, :]
    v_scaled = jnp.concatenate([(v_t.astype(F32) * ws_row).astype(BF16),
                                jnp.broadcast_to(ws_row, (C_ONES, n_rows)).astype(BF16)], axis=0)
    cst = cst_ref[hd]
    num = _dot(v_ext, w) + g * _dot(cst.astype(BF16), q_t[sl, :])
    den = jnp.maximum(jnp.abs(num[C_V:C_V + 1, :]), jnp.exp(-m_t))
    cst_ref[hd] = decay[gi:gi + 1, :C_QK] * cst + _dot(v_scaled, k[:, sl])
    return num[:C_V, :] / den


def _mlstm_finish(hsum, og_ref, z_ref, ng_ref, out_ref):
    hsum = hsum * _sigmoid(og_ref[...])
    ys = []
    for hd in range(C_HEADS):
        sl = slice(hd * C_V, (hd + 1) * C_V)
        hh = hsum[:, sl]
        mu = jnp.mean(hh, axis=-1, keepdims=True)
        var = jnp.mean(jnp.square(hh - mu), axis=-1, keepdims=True)
        ys.append((hh - mu) * lax.rsqrt(var + EPS) * ng_ref[:, sl])
    y = jnp.concatenate(ys, axis=1) * _silu(z_ref[...])
    out_ref[...] = y.astype(out_ref.dtype)


def _mlstm(u3, gt, vt, conv_w, layer, *, reverse, qk=None, hb=None, norm_g=None):
    batch, seq, _ = u3.shape
    n_chunks = seq // CHUNK
    final = hb is not None
    grp = REC_GROUP
    assert CHUNK == LANE

    def chunk(c, delta=0):
        return jnp.clip(_chunk_index(c, n_chunks, reverse) + delta, 0, n_chunks - 1)

    def ublk(blk, delta=0):
        return pl.BlockSpec((grp, CHUNK, D_GROUP), lambda bg, c: (bg, chunk(c, delta), blk))

    row_spec = ublk(0)
    t_spec = lambda ch: pl.BlockSpec((grp, ch, CHUNK), lambda bg, c: (bg, 0, chunk(c)))
    gate_specs = [t_spec(N_GATE)]
    gate_args = [gt]
    n_qk = C_HEADS * C_QK
    k_spec = pl.BlockSpec((grp, CHUNK, n_qk), lambda bg, c: (bg, chunk(c), 0))
    if final:
        in_specs = [k_spec, t_spec(n_qk), t_spec(D_GROUP)] + gate_specs + [
            ublk(BLK_C_O), ublk(BLK_C_Z), t_spec(D_GROUP),
            pl.BlockSpec(norm_g.shape, lambda bg, c: (0, 0))]
        args = [qk[0], qk[1], vt] + gate_args + [u3, u3, hb, norm_g]
        out_shape = jax.ShapeDtypeStruct((batch, seq, D_GROUP), BF16)
        out_specs = row_spec
    else:
        in_specs = [ublk(BLK_C_QK, -1), ublk(BLK_C_QK), ublk(BLK_C_QK, 1),
                    pl.BlockSpec(conv_w.shape, lambda bg, c: (0, 0, 0)),
                    t_spec(D_GROUP)] + gate_specs
        args = [u3, u3, u3, conv_w, vt] + gate_args
        out_shape = (jax.ShapeDtypeStruct((batch, D_GROUP, seq), F32),
                     jax.ShapeDtypeStruct((batch, seq, n_qk), BF16),
                     jax.ShapeDtypeStruct((batch, n_qk, seq), BF16))
        out_specs = (t_spec(D_GROUP), k_spec, t_spec(n_qk))
    return pl.pallas_call(
        functools.partial(_mlstm_kernel, layer=layer, reverse=reverse, final=final),
        out_shape=out_shape,
        grid=(batch // grp, n_chunks),
        in_specs=in_specs,
        out_specs=out_specs,
        scratch_shapes=[pltpu.VMEM((grp, C_HEADS, C_V + C_ONES, C_QK), F32),
                        pltpu.VMEM((grp, 2 * C_HEADS, CHUNK), F32)],
        compiler_params=pltpu.CompilerParams(
            dimension_semantics=("parallel", "arbitrary"), vmem_limit_bytes=VMEM_LIMIT),
        name="mlstm_fwd" if final else "mlstm_bwd",
    )(*args)


def _rope_table_kernel(pos_ref, invf_ref, cos_ref, sin_ref):
    ang = invf_ref[...] * pos_ref[...].astype(F32)
    n_rows = ang.shape[1]
    half = ROPE_DIM // 2
    r = lax.broadcasted_iota(jnp.int32, ang.shape, 0)
    sn = jnp.sin(ang)
    sn = jnp.where(r < half, -sn, sn)
    pad = jnp.zeros((LANE - ROPE_DIM, n_rows), F32)

    def expand(x):
        t = jnp.concatenate([x, pad], axis=0).T
        return t + pltpu.roll(t, B_HD, axis=1)

    lane = lax.broadcasted_iota(jnp.int32, (n_rows, LANE), 1) & (B_HD - 1)
    cos_ref[...] = jnp.where(lane < ROPE_DIM, expand(jnp.cos(ang)), 1.0)
    sin_ref[...] = expand(sn)


def _rope_tables(positions):
    rows = positions.size
    half = ROPE_DIM // 2
    inv_freq = ROPE_THETA ** (-jnp.arange(0, ROPE_DIM, 2, dtype=F32) / ROPE_DIM)
    invf = jnp.concatenate([inv_freq, inv_freq]).reshape(ROPE_DIM, 1)
    tm = ROW_TILE
    return pl.pallas_call(
        _rope_table_kernel,
        out_shape=(jax.ShapeDtypeStruct((rows, LANE), F32),) * 2,
        grid=(rows // tm,),
        in_specs=[pl.BlockSpec((1, tm), lambda i: (0, i)),
                  pl.BlockSpec((ROPE_DIM, 1), lambda i: (0, 0))],
        out_specs=(pl.BlockSpec((tm, LANE), lambda i: (i, 0)),) * 2,
        compiler_params=pltpu.CompilerParams(dimension_semantics=("parallel",)),
        name="rope_tables",
    )(positions.reshape(1, rows), invf)


def _rope(x, cos, sin):
    width = x.shape[1]
    if width > LANE:
        cos = jnp.concatenate([cos] * (width // LANE), axis=1)
        sin = jnp.concatenate([sin] * (width // LANE), axis=1)
    half = ROPE_DIM // 2
    r = lax.broadcasted_iota(jnp.int32, x.shape, 1) & (B_HD - 1)
    partner = jnp.where(r < half, pltpu.roll(x, width - half, axis=1), pltpu.roll(x, half, axis=1))
    return x * cos + partner * sin


def _swa_kernel(q_ref, z_ref, mp_ref, mc_ref, mn_ref, cp_ref, cc_ref, cn_ref,
                sp_ref, sc_ref, sn_ref, sink_ref, out_ref, *, layer):
    tile = q_ref.shape[0]
    blk = mp_ref.shape[0]
    n_sub = tile // blk
    i = pl.program_id(1)
    n_tiles = pl.num_programs(1)
    kv_w = B_KV_HEADS * B_HD
    grp = B_HEADS // B_KV_HEADS

    q = (_rope(q_ref[...], cc_ref[...], sc_ref[...]) * (B_HD ** -0.5)).astype(BF16)
    k = jnp.concatenate([
        _rope(mp_ref[:, MISC_K:MISC_K + kv_w], cp_ref[...], sp_ref[...]),
        _rope(mc_ref[:, MISC_K:MISC_K + kv_w], cc_ref[...], sc_ref[...]),
        _rope(mn_ref[:, MISC_K:MISC_K + kv_w], cn_ref[...], sn_ref[...])], axis=0).astype(BF16)
    v = jnp.concatenate([mp_ref[:, MISC_V:MISC_V + kv_w], mc_ref[:, MISC_V:MISC_V + kv_w],
                         mn_ref[:, MISC_V:MISC_V + kv_w]], axis=0)
    n_keys = v.shape[0]

    kr = lax.broadcasted_iota(jnp.int32, (3 * blk, blk), 0)
    qc = lax.broadcasted_iota(jnp.int32, (3 * blk, blk), 1)
    in_window = jnp.abs(qc - (kr - blk)) <= WINDOW
    biases = []
    for j in range(n_sub):
        valid = in_window
        if j == 0:
            valid = valid & (kr >= jnp.where(i > 0, 0, blk))
        if j == n_sub - 1:
            valid = valid & (kr < jnp.where(i < n_tiles - 1, 3 * blk, 2 * blk))
        biases.append(jnp.where(valid, 0.0, NEG))

    ones = jnp.ones((n_keys, B_HD), F32)
    vts = [jnp.concatenate([v[:, kvh * B_HD:(kvh + 1) * B_HD], ones], axis=1).T.astype(BF16)
           for kvh in range(B_KV_HEADS)]

    tiles = [[] for _ in range(n_sub)]
    pairs = [[] for _ in range(n_sub)]
    for hq in range(B_HEADS):
        kvh = hq // grp
        ksl = slice(kvh * B_HD, (kvh + 1) * B_HD)
        sk = sink_ref[layer:layer + 1, hq:hq + 1]
        for j in range(n_sub):
            keys = slice(j * blk, (j + 3) * blk)
            qh = q[j * blk:(j + 1) * blk, hq * B_HD:(hq + 1) * B_HD]
            s = _dot_nt(k[keys, ksl], qh)
            s = jnp.concatenate([s[:blk] + biases[j][:blk], s[blk:2 * blk],
                                 s[2 * blk:] + biases[j][2 * blk:]], axis=0)
            m = jnp.maximum(jnp.max(s, axis=0, keepdims=True), sk)
            p = jnp.exp(s - m).astype(BF16)
            o = _dot(vts[kvh][:, keys], p)
            denom = o[B_HD:B_HD + 1, :] + jnp.exp(sk - m)
            pairs[j].append(o[:B_HD, :] / denom)
            if len(pairs[j]) == LANE // B_HD:
                tiles[j].append(jnp.concatenate(pairs[j], axis=0).T)
                pairs[j] = []
    for j in range(n_sub):
        rows = slice(j * blk, (j + 1) * blk)
        y = jnp.concatenate(tiles[j], axis=1) * _silu(z_ref[rows, :])
        out_ref[rows, :] = y.astype(out_ref.dtype)


def _swa(u, cos_t, sin_t, sink, layer, batch):
    rows = u.shape[0]
    n_blocks = rows // batch // CHUNK
    n_sub = SWA_TILE // CHUNK
    n_tiles = n_blocks // n_sub

    def tile_spec(width, blk):
        return pl.BlockSpec((SWA_TILE, width), lambda bi, i: (bi * n_tiles + i, blk))

    def halo_spec(width, blk, after):
        def index(bi, i):
            nb = (i + 1) * n_sub if after else i * n_sub - 1
            return (bi * n_blocks + jnp.clip(nb, 0, n_blocks - 1), blk)
        return pl.BlockSpec((CHUNK, width), index)

    def table_specs():
        return [halo_spec(LANE, 0, False), tile_spec(LANE, 0), halo_spec(LANE, 0, True)]

    return pl.pallas_call(
        functools.partial(_swa_kernel, layer=layer),
        out_shape=jax.ShapeDtypeStruct((rows, D_GROUP), BF16),
        grid=(batch, n_tiles),
        in_specs=[tile_spec(D_GROUP, BLK_B_Q), tile_spec(D_GROUP, BLK_B_Z),
                  halo_spec(D_GROUP, BLK_MISC, False), tile_spec(D_GROUP, BLK_MISC),
                  halo_spec(D_GROUP, BLK_MISC, True)] + table_specs() + table_specs() + [
                  pl.BlockSpec(sink.shape, lambda bi, i: (0, 0))],
        out_specs=tile_spec(D_GROUP, 0),
        compiler_params=pltpu.CompilerParams(
            dimension_semantics=("parallel", "parallel"), vmem_limit_bytes=VMEM_LIMIT),
        name="window_attn",
    )(u, u, u, u, u, cos_t, cos_t, cos_t, sin_t, sin_t, sin_t, sink)


def _mem_kv_kernel(mem_ref, g_ref, w_ref, kt_ref, v_ref, *, layer):
    x = mem_ref[...]
    ms = jnp.mean(x * x, axis=-1, keepdims=True)
    h = (x * lax.rsqrt(ms + EPS) * g_ref[layer:layer + 1, :]).astype(BF16)
    kv = _dot(h, w_ref[...].astype(BF16))
    kt_ref[...] = kv[:, :D_GROUP].T.astype(kt_ref.dtype)
    v_ref[...] = kv[:, D_GROUP:].astype(v_ref.dtype)


def _mem_kv(mem, g_all, w_all, layer):
    batch, m_len, d = mem.shape
    n_out = w_all.shape[2]
    return pl.pallas_call(
        functools.partial(_mem_kv_kernel, layer=layer),
        out_shape=(jax.ShapeDtypeStruct((batch, D_GROUP, m_len), BF16),
                   jax.ShapeDtypeStruct((batch, m_len, D_GROUP), BF16)),
        grid=(batch,),
        in_specs=[pl.BlockSpec((None, m_len, d), lambda bi: (bi, 0, 0)),
                  pl.BlockSpec(g_all.shape, lambda bi: (0, 0)),
                  pl.BlockSpec((None, d, n_out), lambda bi: (layer, 0, 0))],
        out_specs=(pl.BlockSpec((None, D_GROUP, m_len), lambda bi: (bi, 0, 0)),
                   pl.BlockSpec((None, m_len, D_GROUP), lambda bi: (bi, 0, 0))),
        compiler_params=pltpu.CompilerParams(
            dimension_semantics=("parallel",), vmem_limit_bytes=VMEM_LIMIT),
        name="mem_kv",
    )(mem, g_all, w_all)


def _mem_attn_out_kernel(q_ref, z_ref, kt_ref, v_ref, ya_ref, yb_ref, yc_ref, w_ref, x_ref, g_ref,
                         o_ref, *, final):
    q = q_ref[...].astype(BF16)
    outs = []
    for hd in range(D_HEADS):
        sl = slice(hd * D_HD, (hd + 1) * D_HD)
        s = _dot(q[:, sl], kt_ref[sl, :])
        m = jnp.max(s, axis=-1, keepdims=True)
        p = jnp.exp2((s - m) * (D_HD ** -0.5 * LOG2_E))
        den = jnp.sum(p, axis=-1, keepdims=True)
        outs.append(_dot(p.astype(BF16), v_ref[:, sl]) / den)
    yd = (jnp.concatenate(outs, axis=1) * _silu(z_ref[...])).astype(BF16)

    acc = x_ref[...]
    for i, y in enumerate((ya_ref[...], yb_ref[...], yc_ref[...], yd)):
        acc = acc + _dot(y, w_ref[i * D_GROUP:(i + 1) * D_GROUP, :].astype(BF16))
    if final:
        ms = jnp.mean(acc * acc, axis=-1, keepdims=True)
        acc = acc * lax.rsqrt(ms + EPS) * g_ref[...]
    o_ref[...] = acc


def _mem_attn_out(u, u16, kt, v, ya, yb, yc, w_all, xf, g, layer, batch, final):
    rows, d = xf.shape
    tm = ROW_TILE
    per_batch = rows // batch // tm
    row = lambda width, blk: pl.BlockSpec((tm, width), lambda bi, i: (bi * per_batch + i, blk))
    return pl.pallas_call(
        functools.partial(_mem_attn_out_kernel, final=final),
        out_shape=jax.ShapeDtypeStruct((rows, d), F32),
        grid=(batch, per_batch),
        in_specs=[row(D_GROUP, BLK16_D_Q), row(D_GROUP, BLK_D_Z),
                  pl.BlockSpec((None,) + kt.shape[1:], lambda bi, i: (bi, 0, 0)),
                  pl.BlockSpec((None,) + v.shape[1:], lambda bi, i: (bi, 0, 0)),
                  row(D_GROUP, 0), row(D_GROUP, 0), row(D_GROUP, 0),
                  pl.BlockSpec((None, 4 * D_GROUP, d), lambda bi, i: (layer, 0, 0)),
                  row(d, 0),
                  pl.BlockSpec((1, d), lambda bi, i: (0, 0))],
        out_specs=row(d, 0),
        compiler_params=pltpu.CompilerParams(
            dimension_semantics=("parallel", "parallel"), vmem_limit_bytes=VMEM_LIMIT),
        name="mem_attn_out",
    )(u16, u, kt, v, ya, yb, yc, w_all, xf, g.reshape(1, d))


def kernel(x, mem, positions, norm_g, w_in, hgrn_lb_logits, hgrn_norm_g, attn_sink, mlstm_conv_w,
           mlstm_gate_b, mlstm_norm_g, mem_norm_g, w_mem_kv, w_out, final_norm_g):
    batch, seq, d = x.shape
    depth = w_in.shape[0]
    assert all(seq % t == 0 for t in (CHUNK, ROW_TILE, IN_ROW_TILE, SWA_TILE))
    assert batch % REC_GROUP == 0
    xf = x.reshape(batch * seq, d)
    cos_t, sin_t = _rope_tables(positions)
    w_all = _prep_w_in(w_in)
    gate_b = mlstm_gate_b.reshape(depth, N_GATE, 1)
    for layer in range(depth):
        u, u16, gt, vt = _in_proj(xf, norm_g, w_all, gate_b, hgrn_lb_logits, layer, batch)
        u3 = u.reshape(batch, seq, u.shape[1])
        u16_3 = u16.reshape(batch, seq, u16.shape[1])
        ob = _hgrn(u3, u16_3, layer, reverse=True)
        ya = _hgrn(u3, u16_3, layer, reverse=False, ob=ob, norm_g=hgrn_norm_g)
        yb = _swa(u, cos_t, sin_t, attn_sink, layer, batch)
        hb, k_c, qt_c = _mlstm(u3, gt, vt, mlstm_conv_w, layer, reverse=True)
        yc = _mlstm(u3, gt, vt, None, layer, reverse=False, qk=(k_c, qt_c), hb=hb,
                    norm_g=mlstm_norm_g)
        mem_kt, mem_v = _mem_kv(mem, mem_norm_g, w_mem_kv, layer)
        xf = _mem_attn_out(u, u16, mem_kt, mem_v, ya.reshape(batch * seq, D_GROUP), yb,
                           yc.reshape(batch * seq, D_GROUP), w_out, xf, final_norm_g, layer, batch,
                           final=(layer == depth - 1))
    return xf.reshape(batch, seq, d)
```

```python
import functools

import jax
import jax.numpy as jnp
import numpy as np
from jax import lax
from jax.experimental import pallas as pl
from jax.experimental.pallas import tpu as pltpu

F32 = jnp.float32
BF16 = jnp.bfloat16

D_GROUP = 512
A_HEADS, A_HD = 4, 128
B_HEADS, B_KV_HEADS, B_HD = 8, 2, 64
WINDOW = 128
ROPE_THETA = 500000.0
ROPE_DIM = 16
C_HEADS, C_QK, C_V = 4, 64, 128
CONV_W = 5
D_HEADS, D_HD = 4, 128
EPS = 1e-6
IN_SIZES = (
    D_GROUP, D_GROUP, D_GROUP, D_GROUP, D_GROUP,
    B_HEADS * B_HD, B_KV_HEADS * B_HD, B_KV_HEADS * B_HD, D_GROUP,
    C_HEADS * C_QK, C_HEADS * C_QK, C_HEADS * C_V, D_GROUP, D_GROUP,
    2 * C_HEADS, 2 * C_HEADS,
    D_GROUP, D_GROUP,
)
SPLIT_POINTS = tuple(int(s) for s in np.cumsum(IN_SIZES)[:-1])

LANE = 128
SUBLANE = 8
VMEM_LIMIT = 48 * 1024 * 1024

CHUNK = 128
ROW_TILE = 1024
IN_ROW_TILE = 512
IN_PROJ_VMEM_LIMIT = 56 * 1024 * 1024
W_PREP_TILE = 256
HGRN_SUB = 64
HGRN_SAFE_LOG2 = 100.0
SWA_TILE = 1024
REC_GROUP = 4
NEG = -1e30
LOG2_E = 1.4426950408889634

(BLK_A_Q, BLK_A_FF, BLK_A_FB, BLK_A_Z, BLK_B_Q, BLK_B_Z, BLK_C_QK, BLK_C_O,
 BLK_C_Z, BLK_D_Z, BLK_MISC) = range(11)
N_BLK = 11
BLK16_A_I, BLK16_D_Q = range(2)
N_BLK16 = 2
W_ROW_C_V = (N_BLK + N_BLK16) * D_GROUP
N_GATE = 4 * C_HEADS
W_ROW_GATE = W_ROW_C_V + D_GROUP
N_W_ROWS = W_ROW_GATE + N_GATE
MISC_K, MISC_V = 0, 128
C_ONES = 16


def _dot(a, b):
    return jnp.dot(a, b, preferred_element_type=F32)


def _dot_nt(a, b):
    return lax.dot_general(a, b, (((1,), (1,)), ((), ())), preferred_element_type=F32)


def _dot_tn(a, b):
    return lax.dot_general(a, b, (((0,), (0,)), ((), ())), preferred_element_type=F32)


def _sigmoid(x):
    return 1.0 / (1.0 + jnp.exp(-x))


def _silu(x):
    return x * _sigmoid(x)


def _log_sigmoid(x):
    return jnp.minimum(x, 0.0) - jnp.log(1.0 + jnp.exp(-jnp.abs(x)))


def _chunk_cumsum_lanes(x, reverse):
    n = x.shape[1]
    pos = lax.broadcasted_iota(jnp.int32, x.shape, 1) & (CHUNK - 1)
    s = 1
    while s < CHUNK:
        if reverse:
            x = x + jnp.where(pos < CHUNK - s, pltpu.roll(x, n - s, axis=1), 0.0)
        else:
            x = x + jnp.where(pos >= s, pltpu.roll(x, s, axis=1), 0.0)
        s *= 2
    return x


def _in_proj_kernel(x_ref, g_ref, w_ref, gb_ref, u_ref, u16_ref, gt_ref, vt_ref, *, layer):
    x = x_ref[...]
    ms = jnp.mean(x * x, axis=-1, keepdims=True)
    h = (x * lax.rsqrt(ms + EPS) * g_ref[layer:layer + 1, :]).astype(BF16)
    vg = _dot_nt(w_ref[W_ROW_C_V:W_ROW_GATE + N_GATE, :], h)
    vt_ref[...] = vg[:D_GROUP, :].astype(BF16)
    gates = vg[D_GROUP:, :] + gb_ref[layer]
    n_dir = 2 * C_HEADS
    log_f = _log_sigmoid(gates[n_dir:, :])
    is_fwd = lax.broadcasted_iota(jnp.int32, log_f.shape, 0) < C_HEADS
    cum = jnp.where(is_fwd, _chunk_cumsum_lanes(log_f, False), _chunk_cumsum_lanes(log_f, True))
    gt_ref[...] = jnp.concatenate([gates[:n_dir, :], cum], axis=0)
    for j in range(N_BLK):
        u_ref[:, j * D_GROUP:(j + 1) * D_GROUP] = _dot_nt(h, w_ref[j * D_GROUP:(j + 1) * D_GROUP, :])
    for j in range(N_BLK16):
        rows = slice((N_BLK + j) * D_GROUP, (N_BLK + j + 1) * D_GROUP)
        u16_ref[:, j * D_GROUP:(j + 1) * D_GROUP] = _dot_nt(h, w_ref[rows, :]).astype(BF16)


def _in_proj(xf, g_all, w_all, gate_b, layer, batch):
    rows, d = xf.shape
    tm = IN_ROW_TILE
    per_batch = rows // batch // tm
    return pl.pallas_call(
        functools.partial(_in_proj_kernel, layer=layer),
        out_shape=(jax.ShapeDtypeStruct((rows, N_BLK * D_GROUP), F32),
                   jax.ShapeDtypeStruct((rows, N_BLK16 * D_GROUP), BF16),
                   jax.ShapeDtypeStruct((batch, N_GATE, rows // batch), F32),
                   jax.ShapeDtypeStruct((batch, D_GROUP, rows // batch), BF16)),
        grid=(rows // tm,),
        in_specs=[
            pl.BlockSpec((tm, d), lambda i: (i, 0)),
            pl.BlockSpec(g_all.shape, lambda i: (0, 0)),
            pl.BlockSpec((None, N_W_ROWS, d), lambda i: (layer, 0, 0), pipeline_mode=pl.Buffered(1)),
            pl.BlockSpec(gate_b.shape, lambda i: (0, 0, 0)),
        ],
        out_specs=(
            pl.BlockSpec((tm, N_BLK * D_GROUP), lambda i: (i, 0)),
            pl.BlockSpec((tm, N_BLK16 * D_GROUP), lambda i: (i, 0)),
            pl.BlockSpec((None, N_GATE, tm), lambda i: (i // per_batch, 0, i % per_batch)),
            pl.BlockSpec((None, D_GROUP, tm), lambda i: (i // per_batch, 0, i % per_batch)),
        ),
        compiler_params=pltpu.CompilerParams(
            dimension_semantics=("parallel",), vmem_limit_bytes=IN_PROJ_VMEM_LIMIT),
        name="in_proj",
    )(xf, g_all, w_all, gate_b)


def _w_in_segments():
    names = ("a_q", "a_i", "a_ff", "a_fb", "a_z", "b_q", "b_k", "b_v", "b_z",
             "c_q", "c_k", "c_v", "c_o", "c_z", "c_ig", "c_fg", "d_q", "d_z")
    src = dict(zip(names, zip((0,) + SPLIT_POINTS, IN_SIZES)))
    blocks = {BLK_A_Q: ("a_q",), BLK_A_FF: ("a_ff",), BLK_A_FB: ("a_fb",), BLK_A_Z: ("a_z",),
              BLK_B_Q: ("b_q",), BLK_B_Z: ("b_z",), BLK_C_QK: ("c_q", "c_k"), BLK_C_O: ("c_o",),
              BLK_C_Z: ("c_z",), BLK_D_Z: ("d_z",), BLK_MISC: ("b_k", "b_v"),
              N_BLK + BLK16_A_I: ("a_i",), N_BLK + BLK16_D_Q: ("d_q",)}
    placed = []
    for blk, group in sorted(blocks.items()):
        dst = blk * D_GROUP
        for name in group:
            placed.append((name, dst))
            dst += src[name][1]
    placed += [("c_v", W_ROW_C_V), ("c_ig", W_ROW_GATE), ("c_fg", W_ROW_GATE + src["c_ig"][1])]
    segs, gaps, end = [], [], 0
    for name, dst in placed:
        off, width = src[name]
        if dst > end:
            gaps.append((end, dst))
        if segs and segs[-1][0] + segs[-1][1] == off and segs[-1][2] + segs[-1][1] == dst:
            segs[-1] = (segs[-1][0], segs[-1][1] + width, segs[-1][2])
        else:
            segs.append((off, width, dst))
        end = dst + width
    assert end == N_W_ROWS
    return tuple(segs), tuple(gaps)


def _w_prep_kernel(wt_ref, wc_ref):
    segs, gaps = _w_in_segments()
    for off, width, dst in segs:
        wc_ref[dst:dst + width, :] = wt_ref[off:off + width, :].astype(BF16)
    for lo, hi in gaps:
        wc_ref[lo:hi, :] = jnp.zeros((hi - lo, wc_ref.shape[1]), BF16)


def _prep_w_in(w):
    depth, d, d_in = w.shape
    wt = jnp.swapaxes(w, 1, 2)
    tk = W_PREP_TILE
    return pl.pallas_call(
        _w_prep_kernel,
        out_shape=jax.ShapeDtypeStruct((depth, N_W_ROWS, d), BF16),
        grid=(depth, d // tk),
        in_specs=[pl.BlockSpec((None, d_in, tk), lambda l, i: (l, 0, i))],
        out_specs=pl.BlockSpec((None, N_W_ROWS, tk), lambda l, i: (l, 0, i)),
        compiler_params=pltpu.CompilerParams(
            dimension_semantics=("parallel", "parallel"), vmem_limit_bytes=VMEM_LIMIT),
        name="w_prep",
    )(wt)


def _level_ref(b, h, reverse):
    n_rows, n = b.shape
    blk = 2 * h
    r0 = h if reverse else h - 1
    if blk >= SUBLANE:
        pieces = [jnp.broadcast_to(b[s + r0:s + r0 + 1, :], (blk, n)) for s in range(0, n_rows, blk)]
        return pieces[0] if len(pieces) == 1 else jnp.concatenate(pieces, axis=0)
    res = lax.broadcasted_iota(jnp.int32, b.shape, 0) & (blk - 1)
    out = None
    for m in range(blk):
        shift = (m - r0) % n_rows
        cand = b if shift == 0 else pltpu.roll(b, shift, axis=0)
        out = cand if out is None else jnp.where(res == m, cand, out)
    return out


def _hgrn_lower_bound(lbl_ref, layer, depth):
    rows = [lbl_ref[j:j + 1, :] for j in range(depth)]
    mx = functools.reduce(jnp.maximum, rows)
    es = [jnp.exp(r - mx) for r in rows]
    tot = functools.reduce(lambda a, c: a + c, es)
    lb = jnp.zeros_like(rows[0])
    for j in range(1, layer + 1):
        lb = lb + es[j] / tot
    return lb


def _cumsum_rows_mxu(x, reverse):
    n = x.shape[0]
    r = lax.broadcasted_iota(jnp.int32, (n, n), 0)
    c = lax.broadcasted_iota(jnp.int32, (n, n), 1)
    tri = jnp.where((c >= r) if reverse else (c <= r), 1.0, 0.0).astype(BF16)
    hi = x.astype(BF16)
    r1 = x - hi.astype(F32)
    mid = r1.astype(BF16)
    lo = (r1 - mid.astype(F32)).astype(BF16)
    return _dot(jnp.concatenate([tri, tri, tri], axis=1), jnp.concatenate([hi, mid, lo], axis=0))


def _hgrn_level_operand(q, k, f, b2, h, reverse):
    n_rows = q.shape[0]
    blk = 2 * h
    r0 = h if reverse else h - 1
    if h >= SUBLANE:
        pieces = []
        for s0 in range(0, n_rows, blk):
            ref = b2[s0 + r0:s0 + r0 + 1, :]
            lo, up = slice(s0, s0 + h), slice(s0 + h, s0 + blk)
            if reverse:
                pieces += [q[lo] * jnp.exp2(b2[lo] - ref), k[up] * jnp.exp2(ref - b2[up])]
            else:
                pieces += [k[lo] * jnp.exp2(ref - b2[lo]), q[up] * jnp.exp2(b2[up] - ref)]
        return jnp.concatenate(pieces, axis=0)
    row = lax.broadcasted_iota(jnp.int32, q.shape, 0)
    is_q = ((row & h) == 0) if reverse else ((row & h) != 0)
    qk = jnp.where(is_q, q, k)
    if h == 1:
        return jnp.where(is_q, q * f, k)
    if h == 2:
        res = row & 3
        f_dn = pltpu.roll(f, 1, axis=0)
        f_up = pltpu.roll(f, n_rows - 1, axis=0)
        if reverse:
            e = jnp.where(res == 0, f * f_up, jnp.where(res == 1, f, jnp.where(res == 2, 1.0, f_dn)))
        else:
            e = jnp.where(res == 0, f_up, jnp.where(res == 1, 1.0, jnp.where(res == 2, f, f_dn * f)))
        return qk * e
    return qk * jnp.exp2(-jnp.abs(b2 - _level_ref(b2, h, reverse)))


def _hgrn_kernel(*refs, layer, depth, reverse, final):
    if final:
        q_ref, v_ref, zg_ref, lbl_ref, z_ref, ob_ref, ng_ref, out_ref, st_ref = refs
    else:
        q_ref, v_ref, zg_ref, lbl_ref, out_ref, st_ref = refs

    @pl.when(pl.program_id(1) == 0)
    def _():
        st_ref[...] = jnp.zeros_like(st_ref)

    rows = range(q_ref.shape[0])
    n_rows = q_ref.shape[1]
    ti = lax.broadcasted_iota(jnp.int32, (n_rows, n_rows), 0)
    si = lax.broadcasted_iota(jnp.int32, (n_rows, n_rows), 1)
    ahead = (ti < si) if reverse else (ti > si)
    level = jnp.where(ahead, 31 - lax.clz(ti ^ si), jnp.where(ti == si, -1, -2))
    preps, bounds = zip(*[_hgrn_prep(q_ref.at[g], zg_ref.at[g], lbl_ref, layer, depth, reverse)
                          for g in rows])

    def run(fast):
        outs = [[] for _ in rows]
        for hd in range(A_HEADS):
            sl = slice(hd * A_HD, (hd + 1) * A_HD)
            heads = [tuple(a[:, sl] for a in preps[g]) for g in rows]
            if fast:
                atts = [_hgrn_scores_fast(heads[g], level) for g in rows]
            else:
                atts = [_hgrn_scores(heads[g], level, n_rows, None, reverse) for g in rows]
                h = n_rows // 2
                while h >= 1:
                    atts = [_hgrn_scores(heads[g], level, h, atts[g], reverse) for g in rows]
                    h //= 2
            for g in rows:
                outs[g].append(_hgrn_apply(heads[g], atts[g], v_ref.at[g, :, sl],
                                           st_ref.at[g, hd], reverse))
        for g in rows:
            o = jnp.concatenate(outs[g], axis=1)
            if final:
                _hgrn_finish(o, z_ref.at[g], ob_ref.at[g], ng_ref.at[layer:layer + 1],
                             out_ref.at[g])
            else:
                out_ref[g] = o

    safe = jnp.max(functools.reduce(jnp.maximum, bounds)) <= HGRN_SAFE_LOG2

    @pl.when(safe)
    def _():
        run(True)

    @pl.when(jnp.logical_not(safe))
    def _():
        run(False)


def _hgrn_prep(q_ref, zg_ref, lbl_ref, layer, depth, reverse):
    n_rows = q_ref.shape[0]
    q = q_ref[...]
    zg = zg_ref[...]
    e = jnp.exp(-jnp.abs(zg))
    r = 1.0 / (1.0 + e)
    er = e * r
    pos = zg >= 0.0
    f = jnp.where(pos, r, er)
    k = jnp.where(pos, er, r)
    if layer > 0:
        lb = _hgrn_lower_bound(lbl_ref, layer, depth)
        f = lb + (1.0 - lb) * f
        k = (1.0 - lb) * k
    g2 = jnp.log2(f)
    half = HGRN_SUB // 2
    bound = None
    for s0 in range(0, n_rows, HGRN_SUB):
        first = jnp.sum(g2[s0:s0 + half], axis=0, keepdims=True) + g2[s0 + half:s0 + half + 1]
        second = jnp.sum(g2[s0 + half:s0 + HGRN_SUB], axis=0, keepdims=True)
        worst = -jnp.minimum(first, second)
        bound = worst if bound is None else jnp.maximum(bound, worst)
    b2 = _cumsum_rows_mxu(g2, reverse)
    return (q, k, f, b2), bound


def _hgrn_scores(head, level, h, att, reverse):
    q, k, f, b2 = head
    if att is None:
        return jnp.where(level == -1, _dot_nt(q.astype(BF16), k.astype(BF16)), 0.0)
    c = _hgrn_level_operand(q, k, f, b2, h, reverse).astype(BF16)
    return jnp.where(level == (h.bit_length() - 1), _dot_nt(c, c), att)


def _hgrn_scores_fast(head, level):
    q, k, _, b2 = head
    n_rows = q.shape[0]
    blocks = []
    for s0 in range(0, n_rows, HGRN_SUB):
        rows = slice(s0, s0 + HGRN_SUB)
        ref = b2[s0 + HGRN_SUB // 2:s0 + HGRN_SUB // 2 + 1, :]
        qs = (q[rows] * jnp.exp2(b2[rows] - ref)).astype(BF16)
        ks = (k * jnp.exp2(jnp.minimum(ref - b2, HGRN_SAFE_LOG2))).astype(BF16)
        blocks.append(_dot_nt(qs, ks))
    return jnp.where(level >= -1, jnp.concatenate(blocks, axis=0), 0.0)


def _hgrn_apply(head, att, v_ref, st_ref, reverse):
    q, k, _, b2 = head
    n_rows = q.shape[0]
    tot = b2[0:1, :] if reverse else b2[n_rows - 1:n_rows, :]
    v_bf = v_ref[...].astype(BF16)
    q_in = (q * jnp.exp2(b2)).astype(BF16)
    k_out = (k * jnp.exp2(tot - b2)).astype(BF16)
    st = st_ref[...]
    o = _dot(att.astype(BF16), v_bf) + _dot(q_in, st.astype(BF16))
    dec_col = jnp.broadcast_to(jnp.exp2(tot), (SUBLANE, tot.shape[1])).T[:, 0:1]
    st_ref[...] = dec_col * st + _dot_tn(k_out, v_bf)
    return o


def _hgrn_finish(o, z_ref, ob_ref, ng_ref, out_ref):
    o = o + ob_ref[...]
    ys = []
    for hd in range(A_HEADS):
        sl = slice(hd * A_HD, (hd + 1) * A_HD)
        oh = o[:, sl]
        ms = jnp.mean(oh * oh, axis=-1, keepdims=True)
        ys.append(oh * lax.rsqrt(ms + EPS) * ng_ref[:, sl])
    y = jnp.concatenate(ys, axis=1) * _silu(z_ref[...])
    out_ref[...] = y.astype(out_ref.dtype)


def _chunk_index(c, n_chunks, reverse):
    return n_chunks - 1 - c if reverse else c


def _hgrn(u3, u16, lb_logits, layer, *, reverse, ob=None, norm_g=None):
    batch, seq, _ = u3.shape
    n_chunks = seq // CHUNK
    depth = lb_logits.shape[0]
    final = ob is not None
    grp = REC_GROUP

    def ublk(blk):
        return pl.BlockSpec((grp, CHUNK, D_GROUP),
                            lambda bg, c: (bg, _chunk_index(c, n_chunks, reverse), blk))

    row_spec = ublk(0)
    in_specs = [ublk(BLK_A_Q), ublk(BLK16_A_I), ublk(BLK_A_FB if reverse else BLK_A_FF),
                pl.BlockSpec((depth, D_GROUP), lambda bg, c: (0, 0))]
    args = [u3, u16, u3, lb_logits]
    if final:
        in_specs += [ublk(BLK_A_Z), row_spec, pl.BlockSpec(norm_g.shape, lambda bg, c: (0, 0))]
        args += [u3, ob, norm_g]
    return pl.pallas_call(
        functools.partial(_hgrn_kernel, layer=layer, depth=depth, reverse=reverse, final=final),
        out_shape=jax.ShapeDtypeStruct((batch, seq, D_GROUP), BF16 if final else F32),
        grid=(batch // grp, n_chunks),
        in_specs=in_specs,
        out_specs=row_spec,
        scratch_shapes=[pltpu.VMEM((grp, A_HEADS, A_HD, A_HD), F32)],
        compiler_params=pltpu.CompilerParams(
            dimension_semantics=("parallel", "arbitrary"), vmem_limit_bytes=VMEM_LIMIT),
        name="hgrn_fwd" if final else "hgrn_bwd",
    )(*args)


def _split3(x):
    hi = x.astype(BF16).astype(F32)
    r1 = x - hi
    mid = r1.astype(BF16).astype(F32)
    lo = (r1 - mid).astype(BF16).astype(F32)
    return hi, mid, lo


def _conv_silu_qk(qkp_ref, qkc_ref, qkn_ref, cw_ref, reverse):
    n_rows = qkc_ref.shape[0]
    c = pl.program_id(1)
    n_chunks = pl.num_programs(1)
    cc = _chunk_index(c, n_chunks, reverse)
    halo = SUBLANE
    prev = qkp_ref[n_rows - halo:n_rows, :] * (cc > 0).astype(F32)
    nxt = qkn_ref[0:halo, :] * (cc < n_chunks - 1).astype(F32)
    xcat = jnp.concatenate([prev, qkc_ref[...], nxt], axis=0)
    acc = None
    for j in range(CONV_W):
        off = halo + j - CONV_W // 2
        term = cw_ref[j:j + 1, :] * xcat[off:off + n_rows, :]
        acc = term if acc is None else acc + term
    qk = _silu(acc)
    n_qk = C_HEADS * C_QK
    return qk[:, n_qk:].astype(BF16), (qk[:, :n_qk] * (C_QK ** -0.5)).T.astype(BF16)


def _mlstm_kernel(*refs, layer, reverse, final):
    if final:
        (k_ref, qt_ref, v_ref, gt_ref, og_ref, z_ref, hb_ref, ng_ref,
         out_ref, cst_ref, m_ref) = refs
    else:
        (qkp_ref, qkc_ref, qkn_ref, cw_ref, v_ref, gt_ref,
         out_ref, ko_ref, qto_ref, cst_ref, m_ref) = refs

    @pl.when(pl.program_id(1) == 0)
    def _():
        cst_ref[...] = jnp.zeros_like(cst_ref)
        m_ref[...] = jnp.zeros_like(m_ref)

    rows = range(v_ref.shape[0])
    if final:
        qks = [(k_ref[g], qt_ref[g]) for g in rows]
    else:
        qks = [_conv_silu_qk(qkp_ref.at[g], qkc_ref.at[g], qkn_ref.at[g], cw_ref.at[layer], reverse)
               for g in rows]
        for g in rows:
            ko_ref[g], qto_ref[g] = qks[g]
    gates = [_mlstm_gates(gt_ref.at[g], m_ref.at[g], reverse) for g in rows]
    outs = [[] for _ in rows]
    for hd in range(C_HEADS):
        weights = [_mlstm_weights(hd, qks[g], gates[g], reverse) for g in rows]
        for g in rows:
            h_t = _mlstm_apply(hd, qks[g], v_ref.at[g], gates[g], weights[g], cst_ref.at[g], reverse)
            if final:
                outs[g].append((h_t + hb_ref[g, hd * C_V:(hd + 1) * C_V, :]).T)
            else:
                out_ref[g, hd * C_V:(hd + 1) * C_V, :] = h_t
    if final:
        for g in rows:
            _mlstm_finish(jnp.concatenate(outs[g], axis=1), og_ref.at[g], z_ref.at[g],
                          ng_ref.at[layer:layer + 1], out_ref.at[g])


def _mlstm_gates(gt_ref, m_ref, reverse):
    n_rows = gt_ref.shape[1]
    n_dir = 2 * C_HEADS

    li = gt_ref[0:n_dir, :]
    b = gt_ref[n_dir:, :]
    b_last = b[:, 0:1] if reverse else b[:, n_rows - 1:n_rows]
    m_prev = m_ref[...]
    d_inter = b + m_prev
    a = (b_last - b) + li
    m_new = jnp.maximum(b_last + m_prev, jnp.max(a, axis=1, keepdims=True))
    ws = jnp.exp(a - m_new)
    decay = jnp.exp((b_last + m_prev) - m_new)
    m_ref[...] = m_new

    one8 = jnp.ones((n_dir, n_rows), F32)
    lhs = jnp.concatenate(list(_split3(li - b)) + [one8, one8, one8], axis=0).T.astype(BF16)
    rhs_all = jnp.concatenate([one8, one8, one8] + list(_split3(b)), axis=0)
    return lhs, rhs_all, d_inter, ws, decay


def _mlstm_weights(hd, qk, gates, reverse):
    lhs, rhs_all, d_inter, _, _ = gates
    k, q_t = qk
    n_rows = k.shape[0]
    n_dir = 2 * C_HEADS
    gi = (C_HEADS if reverse else 0) + hd
    gate_row = lax.broadcasted_iota(jnp.int32, rhs_all.shape, 0) & (n_dir - 1)
    si = lax.broadcasted_iota(jnp.int32, (n_rows, n_rows), 0)
    ti = lax.broadcasted_iota(jnp.int32, (n_rows, n_rows), 1)
    causal = (si >= ti) if reverse else (si <= ti)

    sl = slice(hd * C_QK, (hd + 1) * C_QK)
    rhs = jnp.where(gate_row == gi, rhs_all, 0.0).astype(BF16)
    d = jnp.where(causal, _dot(lhs, rhs), NEG)
    d_int = d_inter[gi:gi + 1, :]
    m_t = jnp.maximum(jnp.max(d, axis=0, keepdims=True), d_int)
    w = (_dot(k[:, sl], q_t[sl, :]) * jnp.exp(d - m_t)).astype(BF16)
    return w, m_t, jnp.exp(d_int - m_t)


def _mlstm_apply(hd, qk, vt_ref, gates, weights, cst_ref, reverse):
    _, _, _, ws, decay = gates
    w, m_t, g = weights
    k, q_t = qk
    n_rows = k.shape[0]
    gi = (C_HEADS if reverse else 0) + hd
    sl = slice(hd * C_QK, (hd + 1) * C_QK)
    v_t = vt_ref[hd * C_V:(hd + 1) * C_V, :]
    v_ext = jnp.concatenate([v_t, jnp.ones((C_ONES, n_rows), BF16)], axis=0)
    ws_row = ws[gi:gi + 1, :]
    v_scaled = jnp.concatenate([(v_t.astype(F32) * ws_row).astype(BF16),
                                jnp.broadcast_to(ws_row, (C_ONES, n_rows)).astype(BF16)], axis=0)
    cst = cst_ref[hd]
    num = _dot(v_ext, w) + g * _dot(cst.astype(BF16), q_t[sl, :])
    den = jnp.maximum(jnp.abs(num[C_V:C_V + 1, :]), jnp.exp(-m_t))
    cst_ref[hd] = decay[gi:gi + 1, :C_QK] * cst + _dot(v_scaled, k[:, sl])
    return num[:C_V, :] / den


def _mlstm_finish(hsum, og_ref, z_ref, ng_ref, out_ref):
    hsum = hsum * _sigmoid(og_ref[...])
    ys = []
    for hd in range(C_HEADS):
        sl = slice(hd * C_V, (hd + 1) * C_V)
        hh = hsum[:, sl]
        mu = jnp.mean(hh, axis=-1, keepdims=True)
        var = jnp.mean(jnp.square(hh - mu), axis=-1, keepdims=True)
        ys.append((hh - mu) * lax.rsqrt(var + EPS) * ng_ref[:, sl])
    y = jnp.concatenate(ys, axis=1) * _silu(z_ref[...])
    out_ref[...] = y.astype(out_ref.dtype)


def _mlstm(u3, gt, vt, conv_w, layer, *, reverse, qk=None, hb=None, norm_g=None):
    batch, seq, _ = u3.shape
    n_chunks = seq // CHUNK
    final = hb is not None
    grp = REC_GROUP
    assert CHUNK == LANE

    def chunk(c, delta=0):
        return jnp.clip(_chunk_index(c, n_chunks, reverse) + delta, 0, n_chunks - 1)

    def ublk(blk, delta=0):
        return pl.BlockSpec((grp, CHUNK, D_GROUP), lambda bg, c: (bg, chunk(c, delta), blk))

    row_spec = ublk(0)
    t_spec = lambda ch: pl.BlockSpec((grp, ch, CHUNK), lambda bg, c: (bg, 0, chunk(c)))
    gate_specs = [t_spec(N_GATE)]
    gate_args = [gt]
    n_qk = C_HEADS * C_QK
    k_spec = pl.BlockSpec((grp, CHUNK, n_qk), lambda bg, c: (bg, chunk(c), 0))
    if final:
        in_specs = [k_spec, t_spec(n_qk), t_spec(D_GROUP)] + gate_specs + [
            ublk(BLK_C_O), ublk(BLK_C_Z), t_spec(D_GROUP),
            pl.BlockSpec(norm_g.shape, lambda bg, c: (0, 0))]
        args = [qk[0], qk[1], vt] + gate_args + [u3, u3, hb, norm_g]
        out_shape = jax.ShapeDtypeStruct((batch, seq, D_GROUP), BF16)
        out_specs = row_spec
    else:
        in_specs = [ublk(BLK_C_QK, -1), ublk(BLK_C_QK), ublk(BLK_C_QK, 1),
                    pl.BlockSpec(conv_w.shape, lambda bg, c: (0, 0, 0)),
                    t_spec(D_GROUP)] + gate_specs
        args = [u3, u3, u3, conv_w, vt] + gate_args
        out_shape = (jax.ShapeDtypeStruct((batch, D_GROUP, seq), F32),
                     jax.ShapeDtypeStruct((batch, seq, n_qk), BF16),
                     jax.ShapeDtypeStruct((batch, n_qk, seq), BF16))
        out_specs = (t_spec(D_GROUP), k_spec, t_spec(n_qk))
    return pl.pallas_call(
        functools.partial(_mlstm_kernel, layer=layer, reverse=reverse, final=final),
        out_shape=out_shape,
        grid=(batch // grp, n_chunks),
        in_specs=in_specs,
        out_specs=out_specs,
        scratch_shapes=[pltpu.VMEM((grp, C_HEADS, C_V + C_ONES, C_QK), F32),
                        pltpu.VMEM((grp, 2 * C_HEADS, CHUNK), F32)],
        compiler_params=pltpu.CompilerParams(
            dimension_semantics=("parallel", "arbitrary"), vmem_limit_bytes=VMEM_LIMIT),
        name="mlstm_fwd" if final else "mlstm_bwd",
    )(*args)


def _rope_table_kernel(pos_ref, invf_ref, cos_ref, sin_ref):
    ang = invf_ref[...] * pos_ref[...].astype(F32)
    n_rows = ang.shape[1]
    half = ROPE_DIM // 2
    r = lax.broadcasted_iota(jnp.int32, ang.shape, 0)
    sn = jnp.sin(ang)
    sn = jnp.where(r < half, -sn, sn)
    pad = jnp.zeros((LANE - ROPE_DIM, n_rows), F32)

    def expand(x):
        t = jnp.concatenate([x, pad], axis=0).T
        return t + pltpu.roll(t, B_HD, axis=1)

    lane = lax.broadcasted_iota(jnp.int32, (n_rows, LANE), 1) & (B_HD - 1)
    cos_ref[...] = jnp.where(lane < ROPE_DIM, expand(jnp.cos(ang)), 1.0)
    sin_ref[...] = expand(sn)


def _rope_tables(positions):
    rows = positions.size
    half = ROPE_DIM // 2
    inv_freq = ROPE_THETA ** (-jnp.arange(0, ROPE_DIM, 2, dtype=F32) / ROPE_DIM)
    invf = jnp.concatenate([inv_freq, inv_freq]).reshape(ROPE_DIM, 1)
    tm = ROW_TILE
    return pl.pallas_call(
        _rope_table_kernel,
        out_shape=(jax.ShapeDtypeStruct((rows, LANE), F32),) * 2,
        grid=(rows // tm,),
        in_specs=[pl.BlockSpec((1, tm), lambda i: (0, i)),
                  pl.BlockSpec((ROPE_DIM, 1), lambda i: (0, 0))],
        out_specs=(pl.BlockSpec((tm, LANE), lambda i: (i, 0)),) * 2,
        compiler_params=pltpu.CompilerParams(dimension_semantics=("parallel",)),
        name="rope_tables",
    )(positions.reshape(1, rows), invf)


def _rope(x, cos, sin):
    width = x.shape[1]
    if width > LANE:
        cos = jnp.concatenate([cos] * (width // LANE), axis=1)
        sin = jnp.concatenate([sin] * (width // LANE), axis=1)
    half = ROPE_DIM // 2
    r = lax.broadcasted_iota(jnp.int32, x.shape, 1) & (B_HD - 1)
    partner = jnp.where(r < half, pltpu.roll(x, width - half, axis=1), pltpu.roll(x, half, axis=1))
    return x * cos + partner * sin


def _swa_kernel(q_ref, z_ref, mp_ref, mc_ref, mn_ref, cp_ref, cc_ref, cn_ref,
                sp_ref, sc_ref, sn_ref, sink_ref, out_ref, *, layer):
    tile = q_ref.shape[0]
    blk = mp_ref.shape[0]
    n_sub = tile // blk
    i = pl.program_id(1)
    n_tiles = pl.num_programs(1)
    kv_w = B_KV_HEADS * B_HD
    grp = B_HEADS // B_KV_HEADS

    q = (_rope(q_ref[...], cc_ref[...], sc_ref[...]) * (B_HD ** -0.5)).astype(BF16)
    k = jnp.concatenate([
        _rope(mp_ref[:, MISC_K:MISC_K + kv_w], cp_ref[...], sp_ref[...]),
        _rope(mc_ref[:, MISC_K:MISC_K + kv_w], cc_ref[...], sc_ref[...]),
        _rope(mn_ref[:, MISC_K:MISC_K + kv_w], cn_ref[...], sn_ref[...])], axis=0).astype(BF16)
    v = jnp.concatenate([mp_ref[:, MISC_V:MISC_V + kv_w], mc_ref[:, MISC_V:MISC_V + kv_w],
                         mn_ref[:, MISC_V:MISC_V + kv_w]], axis=0)
    n_keys = v.shape[0]

    kr = lax.broadcasted_iota(jnp.int32, (3 * blk, blk), 0)
    qc = lax.broadcasted_iota(jnp.int32, (3 * blk, blk), 1)
    in_window = jnp.abs(qc - (kr - blk)) <= WINDOW
    biases = []
    for j in range(n_sub):
        valid = in_window
        if j == 0:
            valid = valid & (kr >= jnp.where(i > 0, 0, blk))
        if j == n_sub - 1:
            valid = valid & (kr < jnp.where(i < n_tiles - 1, 3 * blk, 2 * blk))
        biases.append(jnp.where(valid, 0.0, NEG))

    ones = jnp.ones((n_keys, B_HD), F32)
    vts = [jnp.concatenate([v[:, kvh * B_HD:(kvh + 1) * B_HD], ones], axis=1).T.astype(BF16)
           for kvh in range(B_KV_HEADS)]

    tiles = [[] for _ in range(n_sub)]
    pairs = [[] for _ in range(n_sub)]
    for hq in range(B_HEADS):
        kvh = hq // grp
        ksl = slice(kvh * B_HD, (kvh + 1) * B_HD)
        sk = sink_ref[layer:layer + 1, hq:hq + 1]
        for j in range(n_sub):
            keys = slice(j * blk, (j + 3) * blk)
            qh = q[j * blk:(j + 1) * blk, hq * B_HD:(hq + 1) * B_HD]
            s = _dot_nt(k[keys, ksl], qh)
            s = jnp.concatenate([s[:blk] + biases[j][:blk], s[blk:2 * blk],
                                 s[2 * blk:] + biases[j][2 * blk:]], axis=0)
            m = jnp.maximum(jnp.max(s, axis=0, keepdims=True), sk)
            p = jnp.exp(s - m).astype(BF16)
            o = _dot(vts[kvh][:, keys], p)
            denom = o[B_HD:B_HD + 1, :] + jnp.exp(sk - m)
            pairs[j].append(o[:B_HD, :] / denom)
            if len(pairs[j]) == LANE // B_HD:
                tiles[j].append(jnp.concatenate(pairs[j], axis=0).T)
                pairs[j] = []
    for j in range(n_sub):
        rows = slice(j * blk, (j + 1) * blk)
        y = jnp.concatenate(tiles[j], axis=1) * _silu(z_ref[rows, :])
        out_ref[rows, :] = y.astype(out_ref.dtype)


def _swa(u, cos_t, sin_t, sink, layer, batch):
    rows = u.shape[0]
    n_blocks = rows // batch // CHUNK
    n_sub = SWA_TILE // CHUNK
    n_tiles = n_blocks // n_sub

    def tile_spec(width, blk):
        return pl.BlockSpec((SWA_TILE, width), lambda bi, i: (bi * n_tiles + i, blk))

    def halo_spec(width, blk, after):
        def index(bi, i):
            nb = (i + 1) * n_sub if after else i * n_sub - 1
            return (bi * n_blocks + jnp.clip(nb, 0, n_blocks - 1), blk)
        return pl.BlockSpec((CHUNK, width), index)

    def table_specs():
        return [halo_spec(LANE, 0, False), tile_spec(LANE, 0), halo_spec(LANE, 0, True)]

    return pl.pallas_call(
        functools.partial(_swa_kernel, layer=layer),
        out_shape=jax.ShapeDtypeStruct((rows, D_GROUP), BF16),
        grid=(batch, n_tiles),
        in_specs=[tile_spec(D_GROUP, BLK_B_Q), tile_spec(D_GROUP, BLK_B_Z),
                  halo_spec(D_GROUP, BLK_MISC, False), tile_spec(D_GROUP, BLK_MISC),
                  halo_spec(D_GROUP, BLK_MISC, True)] + table_specs() + table_specs() + [
                  pl.BlockSpec(sink.shape, lambda bi, i: (0, 0))],
        out_specs=tile_spec(D_GROUP, 0),
        compiler_params=pltpu.CompilerParams(
            dimension_semantics=("parallel", "parallel"), vmem_limit_bytes=VMEM_LIMIT),
        name="window_attn",
    )(u, u, u, u, u, cos_t, cos_t, cos_t, sin_t, sin_t, sin_t, sink)


def _mem_kv_kernel(mem_ref, g_ref, w_ref, kt_ref, v_ref, *, layer):
    x = mem_ref[...]
    ms = jnp.mean(x * x, axis=-1, keepdims=True)
    h = (x * lax.rsqrt(ms + EPS) * g_ref[layer:layer + 1, :]).astype(BF16)
    kv = _dot(h, w_ref[...].astype(BF16))
    kt_ref[...] = kv[:, :D_GROUP].T.astype(kt_ref.dtype)
    v_ref[...] = kv[:, D_GROUP:].astype(v_ref.dtype)


def _mem_kv(mem, g_all, w_all, layer):
    batch, m_len, d = mem.shape
    n_out = w_all.shape[2]
    return pl.pallas_call(
        functools.partial(_mem_kv_kernel, layer=layer),
        out_shape=(jax.ShapeDtypeStruct((batch, D_GROUP, m_len), BF16),
                   jax.ShapeDtypeStruct((batch, m_len, D_GROUP), BF16)),
        grid=(batch,),
        in_specs=[pl.BlockSpec((None, m_len, d), lambda bi: (bi, 0, 0)),
                  pl.BlockSpec(g_all.shape, lambda bi: (0, 0)),
                  pl.BlockSpec((None, d, n_out), lambda bi: (layer, 0, 0))],
        out_specs=(pl.BlockSpec((None, D_GROUP, m_len), lambda bi: (bi, 0, 0)),
                   pl.BlockSpec((None, m_len, D_GROUP), lambda bi: (bi, 0, 0))),
        compiler_params=pltpu.CompilerParams(
            dimension_semantics=("parallel",), vmem_limit_bytes=VMEM_LIMIT),
        name="mem_kv",
    )(mem, g_all, w_all)


def _mem_attn_out_kernel(q_ref, z_ref, kt_ref, v_ref, ya_ref, yb_ref, yc_ref, w_ref, x_ref, g_ref,
                         o_ref, w16_ref, *, final):
    @pl.when((pl.program_id(0) == 0) & (pl.program_id(1) == 0))
    def _():
        w16_ref[...] = w_ref[...].astype(BF16)

    q = q_ref[...].astype(BF16)
    outs = []
    for hd in range(D_HEADS):
        sl = slice(hd * D_HD, (hd + 1) * D_HD)
        s = _dot(q[:, sl], kt_ref[sl, :])
        m = jnp.max(s, axis=-1, keepdims=True)
        p = jnp.exp2((s - m) * (D_HD ** -0.5 * LOG2_E))
        den = jnp.sum(p, axis=-1, keepdims=True)
        outs.append(_dot(p.astype(BF16), v_ref[:, sl]) / den)
    yd = (jnp.concatenate(outs, axis=1) * _silu(z_ref[...])).astype(BF16)

    y = jnp.concatenate([ya_ref[...], yb_ref[...], yc_ref[...], yd], axis=1)
    acc = x_ref[...] + _dot(y, w16_ref[...])
    if final:
        ms = jnp.mean(acc * acc, axis=-1, keepdims=True)
        acc = acc * lax.rsqrt(ms + EPS) * g_ref[...]
    o_ref[...] = acc


def _mem_attn_out(u, u16, kt, v, ya, yb, yc, w_all, xf, g, layer, batch, final):
    rows, d = xf.shape
    tm = ROW_TILE
    per_batch = rows // batch // tm
    row = lambda width, blk: pl.BlockSpec((tm, width), lambda bi, i: (bi * per_batch + i, blk))
    return pl.pallas_call(
        functools.partial(_mem_attn_out_kernel, final=final),
        out_shape=jax.ShapeDtypeStruct((rows, d), F32),
        grid=(batch, per_batch),
        in_specs=[row(D_GROUP, BLK16_D_Q), row(D_GROUP, BLK_D_Z),
                  pl.BlockSpec((None,) + kt.shape[1:], lambda bi, i: (bi, 0, 0)),
                  pl.BlockSpec((None,) + v.shape[1:], lambda bi, i: (bi, 0, 0)),
                  row(D_GROUP, 0), row(D_GROUP, 0), row(D_GROUP, 0),
                  pl.BlockSpec((None, 4 * D_GROUP, d), lambda bi, i: (layer, 0, 0),
                               pipeline_mode=pl.Buffered(1)),
                  row(d, 0),
                  pl.BlockSpec((1, d), lambda bi, i: (0, 0))],
        out_specs=row(d, 0),
        scratch_shapes=[pltpu.VMEM((4 * D_GROUP, d), BF16)],
        compiler_params=pltpu.CompilerParams(
            dimension_semantics=("arbitrary", "arbitrary"), vmem_limit_bytes=VMEM_LIMIT),
        name="mem_attn_out",
    )(u16, u, kt, v, ya, yb, yc, w_all, xf, g.reshape(1, d))


def kernel(x, mem, positions, norm_g, w_in, hgrn_lb_logits, hgrn_norm_g, attn_sink, mlstm_conv_w,
           mlstm_gate_b, mlstm_norm_g, mem_norm_g, w_mem_kv, w_out, final_norm_g):
    batch, seq, d = x.shape
    depth = w_in.shape[0]
    assert all(seq % t == 0 for t in (CHUNK, ROW_TILE, IN_ROW_TILE, SWA_TILE))
    assert batch % REC_GROUP == 0
    xf = x.reshape(batch * seq, d)
    cos_t, sin_t = _rope_tables(positions)
    w_all = _prep_w_in(w_in)
    gate_b = mlstm_gate_b.reshape(depth, N_GATE, 1)
    for layer in range(depth):
        u, u16, gt, vt = _in_proj(xf, norm_g, w_all, gate_b, layer, batch)
        u3 = u.reshape(batch, seq, u.shape[1])
        u16_3 = u16.reshape(batch, seq, u16.shape[1])
        ob = _hgrn(u3, u16_3, hgrn_lb_logits, layer, reverse=True)
        ya = _hgrn(u3, u16_3, hgrn_lb_logits, layer, reverse=False, ob=ob, norm_g=hgrn_norm_g)
        yb = _swa(u, cos_t, sin_t, attn_sink, layer, batch)
        hb, k_c, qt_c = _mlstm(u3, gt, vt, mlstm_conv_w, layer, reverse=True)
        yc = _mlstm(u3, gt, vt, None, layer, reverse=False, qk=(k_c, qt_c), hb=hb,
                    norm_g=mlstm_norm_g)
        mem_kt, mem_v = _mem_kv(mem, mem_norm_g, w_mem_kv, layer)
        xf = _mem_attn_out(u, u16, mem_kt, mem_v, ya.reshape(batch * seq, D_GROUP), yb,
                           yc.reshape(batch * seq, D_GROUP), w_out, xf, final_norm_g, layer, batch,
                           final=(layer == depth - 1))
    return xf.reshape(batch, seq, d)
```

```python
import functools

import jax
import jax.numpy as jnp
import numpy as np
from jax import lax
from jax.experimental import pallas as pl
from jax.experimental.pallas import tpu as pltpu

F32 = jnp.float32
BF16 = jnp.bfloat16

D_GROUP = 512
A_HEADS, A_HD = 4, 128
B_HEADS, B_KV_HEADS, B_HD = 8, 2, 64
WINDOW = 128
ROPE_THETA = 500000.0
ROPE_DIM = 16
C_HEADS, C_QK, C_V = 4, 64, 128
CONV_W = 5
D_HEADS, D_HD = 4, 128
EPS = 1e-6
IN_SIZES = (
    D_GROUP, D_GROUP, D_GROUP, D_GROUP, D_GROUP,
    B_HEADS * B_HD, B_KV_HEADS * B_HD, B_KV_HEADS * B_HD, D_GROUP,
    C_HEADS * C_QK, C_HEADS * C_QK, C_HEADS * C_V, D_GROUP, D_GROUP,
    2 * C_HEADS, 2 * C_HEADS,
    D_GROUP, D_GROUP,
)
SPLIT_POINTS = tuple(int(s) for s in np.cumsum(IN_SIZES)[:-1])

LANE = 128
SUBLANE = 8
VMEM_LIMIT = 48 * 1024 * 1024

CHUNK = 128
ROW_TILE = 1024
IN_ROW_TILE = 512
IN_PROJ_VMEM_LIMIT = 56 * 1024 * 1024
W_PREP_TILE = 256
HGRN_SUB = 64
HGRN_SAFE_LOG2 = 100.0
SWA_TILE = 1024
REC_GROUP = 4
NEG = -1e30
LOG2_E = 1.4426950408889634

(BLK_A_Q, BLK_A_FF, BLK_A_FB, BLK_A_Z, BLK_B_Q, BLK_B_Z, BLK_C_QK, BLK_C_O,
 BLK_C_Z, BLK_D_Z, BLK_MISC) = range(11)
N_BLK = 11
BLK16_A_I, BLK16_D_Q = range(2)
N_BLK16 = 2
W_ROW_C_V = (N_BLK + N_BLK16) * D_GROUP
N_GATE = 4 * C_HEADS
W_ROW_GATE = W_ROW_C_V + D_GROUP
N_W_ROWS = W_ROW_GATE + N_GATE
MISC_K, MISC_V = 0, 128
C_ONES = 16


def _dot(a, b):
    return jnp.dot(a, b, preferred_element_type=F32)


def _dot_nt(a, b):
    return lax.dot_general(a, b, (((1,), (1,)), ((), ())), preferred_element_type=F32)


def _dot_tn(a, b):
    return lax.dot_general(a, b, (((0,), (0,)), ((), ())), preferred_element_type=F32)


def _sigmoid(x):
    return 1.0 / (1.0 + jnp.exp(-x))


def _silu(x):
    return x * _sigmoid(x)


def _log_sigmoid(x):
    return jnp.minimum(x, 0.0) - jnp.log(1.0 + jnp.exp(-jnp.abs(x)))


def _chunk_cumsum_lanes(x, reverse):
    n = x.shape[1]
    pos = lax.broadcasted_iota(jnp.int32, x.shape, 1) & (CHUNK - 1)
    s = 1
    while s < CHUNK:
        if reverse:
            x = x + jnp.where(pos < CHUNK - s, pltpu.roll(x, n - s, axis=1), 0.0)
        else:
            x = x + jnp.where(pos >= s, pltpu.roll(x, s, axis=1), 0.0)
        s *= 2
    return x


def _in_proj_kernel(x_ref, g_ref, w_ref, gb_ref, u_ref, u16_ref, gt_ref, vt_ref, *, layer):
    x = x_ref[...]
    ms = jnp.mean(x * x, axis=-1, keepdims=True)
    h = (x * lax.rsqrt(ms + EPS) * g_ref[layer:layer + 1, :]).astype(BF16)
    vg = _dot_nt(w_ref[W_ROW_C_V:W_ROW_GATE + N_GATE, :], h)
    vt_ref[...] = vg[:D_GROUP, :].astype(BF16)
    gates = vg[D_GROUP:, :] + gb_ref[layer]
    n_dir = 2 * C_HEADS
    log_f = _log_sigmoid(gates[n_dir:, :])
    is_fwd = lax.broadcasted_iota(jnp.int32, log_f.shape, 0) < C_HEADS
    cum = jnp.where(is_fwd, _chunk_cumsum_lanes(log_f, False), _chunk_cumsum_lanes(log_f, True))
    gt_ref[...] = jnp.concatenate([gates[:n_dir, :], cum], axis=0)
    for j in range(N_BLK):
        u_ref[:, j * D_GROUP:(j + 1) * D_GROUP] = _dot_nt(h, w_ref[j * D_GROUP:(j + 1) * D_GROUP, :])
    for j in range(N_BLK16):
        rows = slice((N_BLK + j) * D_GROUP, (N_BLK + j + 1) * D_GROUP)
        u16_ref[:, j * D_GROUP:(j + 1) * D_GROUP] = _dot_nt(h, w_ref[rows, :]).astype(BF16)


def _in_proj(xf, g_all, w_all, gate_b, layer, batch):
    rows, d = xf.shape
    tm = IN_ROW_TILE
    per_batch = rows // batch // tm
    return pl.pallas_call(
        functools.partial(_in_proj_kernel, layer=layer),
        out_shape=(jax.ShapeDtypeStruct((rows, N_BLK * D_GROUP), F32),
                   jax.ShapeDtypeStruct((rows, N_BLK16 * D_GROUP), BF16),
                   jax.ShapeDtypeStruct((batch, N_GATE, rows // batch), F32),
                   jax.ShapeDtypeStruct((batch, D_GROUP, rows // batch), BF16)),
        grid=(rows // tm,),
        in_specs=[
            pl.BlockSpec((tm, d), lambda i: (i, 0)),
            pl.BlockSpec(g_all.shape, lambda i: (0, 0)),
            pl.BlockSpec((None, N_W_ROWS, d), lambda i: (layer, 0, 0), pipeline_mode=pl.Buffered(1)),
            pl.BlockSpec(gate_b.shape, lambda i: (0, 0, 0)),
        ],
        out_specs=(
            pl.BlockSpec((tm, N_BLK * D_GROUP), lambda i: (i, 0)),
            pl.BlockSpec((tm, N_BLK16 * D_GROUP), lambda i: (i, 0)),
            pl.BlockSpec((None, N_GATE, tm), lambda i: (i // per_batch, 0, i % per_batch)),
            pl.BlockSpec((None, D_GROUP, tm), lambda i: (i // per_batch, 0, i % per_batch)),
        ),
        compiler_params=pltpu.CompilerParams(
            dimension_semantics=("parallel",), vmem_limit_bytes=IN_PROJ_VMEM_LIMIT),
        name="in_proj",
    )(xf, g_all, w_all, gate_b)


def _w_in_segments():
    names = ("a_q", "a_i", "a_ff", "a_fb", "a_z", "b_q", "b_k", "b_v", "b_z",
             "c_q", "c_k", "c_v", "c_o", "c_z", "c_ig", "c_fg", "d_q", "d_z")
    src = dict(zip(names, zip((0,) + SPLIT_POINTS, IN_SIZES)))
    blocks = {BLK_A_Q: ("a_q",), BLK_A_FF: ("a_ff",), BLK_A_FB: ("a_fb",), BLK_A_Z: ("a_z",),
              BLK_B_Q: ("b_q",), BLK_B_Z: ("b_z",), BLK_C_QK: ("c_q", "c_k"), BLK_C_O: ("c_o",),
              BLK_C_Z: ("c_z",), BLK_D_Z: ("d_z",), BLK_MISC: ("b_k", "b_v"),
              N_BLK + BLK16_A_I: ("a_i",), N_BLK + BLK16_D_Q: ("d_q",)}
    placed = []
    for blk, group in sorted(blocks.items()):
        dst = blk * D_GROUP
        for name in group:
            placed.append((name, dst))
            dst += src[name][1]
    placed += [("c_v", W_ROW_C_V), ("c_ig", W_ROW_GATE), ("c_fg", W_ROW_GATE + src["c_ig"][1])]
    segs, gaps, end = [], [], 0
    for name, dst in placed:
        off, width = src[name]
        if dst > end:
            gaps.append((end, dst))
        if segs and segs[-1][0] + segs[-1][1] == off and segs[-1][2] + segs[-1][1] == dst:
            segs[-1] = (segs[-1][0], segs[-1][1] + width, segs[-1][2])
        else:
            segs.append((off, width, dst))
        end = dst + width
    assert end == N_W_ROWS
    return tuple(segs), tuple(gaps)


def _w_prep_kernel(wt_ref, wc_ref):
    segs, gaps = _w_in_segments()
    for off, width, dst in segs:
        wc_ref[dst:dst + width, :] = wt_ref[off:off + width, :].astype(BF16)
    for lo, hi in gaps:
        wc_ref[lo:hi, :] = jnp.zeros((hi - lo, wc_ref.shape[1]), BF16)


def _prep_w_in(w):
    depth, d, d_in = w.shape
    wt = jnp.swapaxes(w, 1, 2)
    tk = W_PREP_TILE
    return pl.pallas_call(
        _w_prep_kernel,
        out_shape=jax.ShapeDtypeStruct((depth, N_W_ROWS, d), BF16),
        grid=(depth, d // tk),
        in_specs=[pl.BlockSpec((None, d_in, tk), lambda l, i: (l, 0, i))],
        out_specs=pl.BlockSpec((None, N_W_ROWS, tk), lambda l, i: (l, 0, i)),
        compiler_params=pltpu.CompilerParams(
            dimension_semantics=("parallel", "parallel"), vmem_limit_bytes=VMEM_LIMIT),
        name="w_prep",
    )(wt)


def _level_ref(b, h, reverse):
    n_rows, n = b.shape
    blk = 2 * h
    r0 = h if reverse else h - 1
    if blk >= SUBLANE:
        pieces = [jnp.broadcast_to(b[s + r0:s + r0 + 1, :], (blk, n)) for s in range(0, n_rows, blk)]
        return pieces[0] if len(pieces) == 1 else jnp.concatenate(pieces, axis=0)
    res = lax.broadcasted_iota(jnp.int32, b.shape, 0) & (blk - 1)
    out = None
    for m in range(blk):
        shift = (m - r0) % n_rows
        cand = b if shift == 0 else pltpu.roll(b, shift, axis=0)
        out = cand if out is None else jnp.where(res == m, cand, out)
    return out


def _hgrn_lower_bound(lbl_ref, layer, depth):
    rows = [lbl_ref[j:j + 1, :] for j in range(depth)]
    mx = functools.reduce(jnp.maximum, rows)
    es = [jnp.exp(r - mx) for r in rows]
    tot = functools.reduce(lambda a, c: a + c, es)
    lb = jnp.zeros_like(rows[0])
    for j in range(1, layer + 1):
        lb = lb + es[j] / tot
    return lb


def _cumsum_rows_mxu(x, reverse):
    n = x.shape[0]
    r = lax.broadcasted_iota(jnp.int32, (n, n), 0)
    c = lax.broadcasted_iota(jnp.int32, (n, n), 1)
    tri = jnp.where((c >= r) if reverse else (c <= r), 1.0, 0.0).astype(BF16)
    hi = x.astype(BF16)
    r1 = x - hi.astype(F32)
    mid = r1.astype(BF16)
    lo = (r1 - mid.astype(F32)).astype(BF16)
    return _dot(jnp.concatenate([tri, tri, tri], axis=1), jnp.concatenate([hi, mid, lo], axis=0))


def _hgrn_level_operand(q, k, f, b2, h, reverse):
    n_rows = q.shape[0]
    blk = 2 * h
    r0 = h if reverse else h - 1
    if h >= SUBLANE:
        pieces = []
        for s0 in range(0, n_rows, blk):
            ref = b2[s0 + r0:s0 + r0 + 1, :]
            lo, up = slice(s0, s0 + h), slice(s0 + h, s0 + blk)
            if reverse:
                pieces += [q[lo] * jnp.exp2(b2[lo] - ref), k[up] * jnp.exp2(ref - b2[up])]
            else:
                pieces += [k[lo] * jnp.exp2(ref - b2[lo]), q[up] * jnp.exp2(b2[up] - ref)]
        return jnp.concatenate(pieces, axis=0)
    row = lax.broadcasted_iota(jnp.int32, q.shape, 0)
    is_q = ((row & h) == 0) if reverse else ((row & h) != 0)
    qk = jnp.where(is_q, q, k)
    if h == 1:
        return jnp.where(is_q, q * f, k)
    if h == 2:
        res = row & 3
        f_dn = pltpu.roll(f, 1, axis=0)
        f_up = pltpu.roll(f, n_rows - 1, axis=0)
        if reverse:
            e = jnp.where(res == 0, f * f_up, jnp.where(res == 1, f, jnp.where(res == 2, 1.0, f_dn)))
        else:
            e = jnp.where(res == 0, f_up, jnp.where(res == 1, 1.0, jnp.where(res == 2, f, f_dn * f)))
        return qk * e
    return qk * jnp.exp2(-jnp.abs(b2 - _level_ref(b2, h, reverse)))


def _hgrn_kernel(*refs, layer, depth, reverse, final):
    if final:
        q_ref, v_ref, zg_ref, lbl_ref, z_ref, ob_ref, ng_ref, out_ref, st_ref = refs
    else:
        q_ref, v_ref, zg_ref, lbl_ref, out_ref, st_ref = refs

    @pl.when(pl.program_id(1) == 0)
    def _():
        st_ref[...] = jnp.zeros_like(st_ref)

    rows = range(q_ref.shape[0])
    n_rows = q_ref.shape[1]
    ti = lax.broadcasted_iota(jnp.int32, (n_rows, n_rows), 0)
    si = lax.broadcasted_iota(jnp.int32, (n_rows, n_rows), 1)
    ahead = (ti < si) if reverse else (ti > si)
    level = jnp.where(ahead, 31 - lax.clz(ti ^ si), jnp.where(ti == si, -1, -2))
    preps, bounds = zip(*[_hgrn_prep(q_ref.at[g], zg_ref.at[g], lbl_ref, layer, depth, reverse)
                          for g in rows])

    def run(fast):
        outs = [[] for _ in rows]
        for hd in range(A_HEADS):
            sl = slice(hd * A_HD, (hd + 1) * A_HD)
            heads = [tuple(a[:, sl] for a in preps[g]) for g in rows]
            if fast:
                atts = [_hgrn_scores_fast(heads[g], level) for g in rows]
            else:
                atts = [_hgrn_scores(heads[g], level, n_rows, None, reverse) for g in rows]
                h = n_rows // 2
                while h >= 1:
                    atts = [_hgrn_scores(heads[g], level, h, atts[g], reverse) for g in rows]
                    h //= 2
            for g in rows:
                outs[g].append(_hgrn_apply(heads[g], atts[g], v_ref.at[g, :, sl],
                                           st_ref.at[g, hd], reverse))
        for g in rows:
            o = jnp.concatenate(outs[g], axis=1)
            if final:
                _hgrn_finish(o, z_ref.at[g], ob_ref.at[g], ng_ref.at[layer:layer + 1],
                             out_ref.at[g])
            else:
                out_ref[g] = o

    safe = jnp.max(functools.reduce(jnp.maximum, bounds)) <= HGRN_SAFE_LOG2

    @pl.when(safe)
    def _():
        run(True)

    @pl.when(jnp.logical_not(safe))
    def _():
        run(False)


def _hgrn_prep(q_ref, zg_ref, lbl_ref, layer, depth, reverse):
    n_rows = q_ref.shape[0]
    q = q_ref[...]
    zg = zg_ref[...]
    e = jnp.exp(-jnp.abs(zg))
    r = 1.0 / (1.0 + e)
    er = e * r
    pos = zg >= 0.0
    f = jnp.where(pos, r, er)
    k = jnp.where(pos, er, r)
    if layer > 0:
        lb = _hgrn_lower_bound(lbl_ref, layer, depth)
        f = lb + (1.0 - lb) * f
        k = (1.0 - lb) * k
    g2 = jnp.log2(f)
    half = HGRN_SUB // 2
    bound = None
    for s0 in range(0, n_rows, HGRN_SUB):
        first = jnp.sum(g2[s0:s0 + half], axis=0, keepdims=True) + g2[s0 + half:s0 + half + 1]
        second = jnp.sum(g2[s0 + half:s0 + HGRN_SUB], axis=0, keepdims=True)
        worst = -jnp.minimum(first, second)
        bound = worst if bound is None else jnp.maximum(bound, worst)
    b2 = _cumsum_rows_mxu(g2, reverse)
    return (q, k, f, b2), bound


def _hgrn_scores(head, level, h, att, reverse):
    q, k, f, b2 = head
    if att is None:
        return jnp.where(level == -1, _dot_nt(q.astype(BF16), k.astype(BF16)), 0.0)
    c = _hgrn_level_operand(q, k, f, b2, h, reverse).astype(BF16)
    return jnp.where(level == (h.bit_length() - 1), _dot_nt(c, c), att)


def _hgrn_scores_fast(head, level):
    q, k, _, b2 = head
    n_rows = q.shape[0]
    blocks = []
    for s0 in range(0, n_rows, HGRN_SUB):
        rows = slice(s0, s0 + HGRN_SUB)
        ref = b2[s0 + HGRN_SUB // 2:s0 + HGRN_SUB // 2 + 1, :]
        qs = (q[rows] * jnp.exp2(b2[rows] - ref)).astype(BF16)
        ks = (k * jnp.exp2(jnp.minimum(ref - b2, HGRN_SAFE_LOG2))).astype(BF16)
        blocks.append(_dot_nt(qs, ks))
    return jnp.where(level >= -1, jnp.concatenate(blocks, axis=0), 0.0)


def _hgrn_apply(head, att, v_ref, st_ref, reverse):
    q, k, _, b2 = head
    n_rows = q.shape[0]
    tot = b2[0:1, :] if reverse else b2[n_rows - 1:n_rows, :]
    v_bf = v_ref[...].astype(BF16)
    q_in = (q * jnp.exp2(b2)).astype(BF16)
    k_out = (k * jnp.exp2(tot - b2)).astype(BF16)
    st = st_ref[...]
    o = _dot(att.astype(BF16), v_bf) + _dot(q_in, st.astype(BF16))
    dec_col = jnp.broadcast_to(jnp.exp2(tot), (SUBLANE, tot.shape[1])).T[:, 0:1]
    st_ref[...] = dec_col * st + _dot_tn(k_out, v_bf)
    return o


def _hgrn_finish(o, z_ref, ob_ref, ng_ref, out_ref):
    o = o + ob_ref[...]
    ys = []
    for hd in range(A_HEADS):
        sl = slice(hd * A_HD, (hd + 1) * A_HD)
        oh = o[:, sl]
        ms = jnp.mean(oh * oh, axis=-1, keepdims=True)
        ys.append(oh * lax.rsqrt(ms + EPS) * ng_ref[:, sl])
    y = jnp.concatenate(ys, axis=1) * _silu(z_ref[...])
    out_ref[...] = y.astype(out_ref.dtype)


def _chunk_index(c, n_chunks, reverse):
    return n_chunks - 1 - c if reverse else c


def _hgrn(u3, u16, lb_logits, layer, *, reverse, ob=None, norm_g=None):
    batch, seq, _ = u3.shape
    n_chunks = seq // CHUNK
    depth = lb_logits.shape[0]
    final = ob is not None
    grp = REC_GROUP

    def ublk(blk):
        return pl.BlockSpec((grp, CHUNK, D_GROUP),
                            lambda bg, c: (bg, _chunk_index(c, n_chunks, reverse), blk))

    row_spec = ublk(0)
    in_specs = [ublk(BLK_A_Q), ublk(BLK16_A_I), ublk(BLK_A_FB if reverse else BLK_A_FF),
                pl.BlockSpec((depth, D_GROUP), lambda bg, c: (0, 0))]
    args = [u3, u16, u3, lb_logits]
    if final:
        in_specs += [ublk(BLK_A_Z), row_spec, pl.BlockSpec(norm_g.shape, lambda bg, c: (0, 0))]
        args += [u3, ob, norm_g]
    return pl.pallas_call(
        functools.partial(_hgrn_kernel, layer=layer, depth=depth, reverse=reverse, final=final),
        out_shape=jax.ShapeDtypeStruct((batch, seq, D_GROUP), BF16 if final else F32),
        grid=(batch // grp, n_chunks),
        in_specs=in_specs,
        out_specs=row_spec,
        scratch_shapes=[pltpu.VMEM((grp, A_HEADS, A_HD, A_HD), F32)],
        compiler_params=pltpu.CompilerParams(
            dimension_semantics=("parallel", "arbitrary"), vmem_limit_bytes=VMEM_LIMIT),
        name="hgrn_fwd" if final else "hgrn_bwd",
    )(*args)


def _split3(x):
    hi = x.astype(BF16).astype(F32)
    r1 = x - hi
    mid = r1.astype(BF16).astype(F32)
    lo = (r1 - mid).astype(BF16).astype(F32)
    return hi, mid, lo


def _conv_silu_qk(qkp_ref, qkc_ref, qkn_ref, cw_ref, reverse):
    n_rows = qkc_ref.shape[0]
    c = pl.program_id(1)
    n_chunks = pl.num_programs(1)
    cc = _chunk_index(c, n_chunks, reverse)
    halo = SUBLANE
    prev = qkp_ref[...] * (cc > 0).astype(F32)
    nxt = qkn_ref[...] * (cc < n_chunks - 1).astype(F32)
    xcat = jnp.concatenate([prev, qkc_ref[...], nxt], axis=0)
    acc = None
    for j in range(CONV_W):
        off = halo + j - CONV_W // 2
        term = cw_ref[j:j + 1, :] * xcat[off:off + n_rows, :]
        acc = term if acc is None else acc + term
    qk = _silu(acc)
    n_qk = C_HEADS * C_QK
    return qk[:, n_qk:].astype(BF16), (qk[:, :n_qk] * (C_QK ** -0.5)).T.astype(BF16)


def _mlstm_kernel(*refs, layer, reverse, final):
    if final:
        (k_ref, qt_ref, v_ref, gt_ref, og_ref, z_ref, hb_ref, ng_ref,
         out_ref, cst_ref, m_ref) = refs
    else:
        (qkp_ref, qkc_ref, qkn_ref, cw_ref, v_ref, gt_ref,
         out_ref, ko_ref, qto_ref, cst_ref, m_ref) = refs

    @pl.when(pl.program_id(1) == 0)
    def _():
        cst_ref[...] = jnp.zeros_like(cst_ref)
        m_ref[...] = jnp.zeros_like(m_ref)

    rows = range(v_ref.shape[0])
    if final:
        qks = [(k_ref[g], qt_ref[g]) for g in rows]
    else:
        qks = [_conv_silu_qk(qkp_ref.at[g], qkc_ref.at[g], qkn_ref.at[g], cw_ref.at[layer], reverse)
               for g in rows]
        for g in rows:
            ko_ref[g], qto_ref[g] = qks[g]
    gates = [_mlstm_gates(gt_ref.at[g], m_ref.at[g], reverse) for g in rows]
    outs = [[] for _ in rows]
    for hd in range(C_HEADS):
        weights = [_mlstm_weights(hd, qks[g], gates[g], reverse) for g in rows]
        for g in rows:
            h_t = _mlstm_apply(hd, qks[g], v_ref.at[g], gates[g], weights[g], cst_ref.at[g], reverse)
            if final:
                outs[g].append((h_t + hb_ref[g, hd * C_V:(hd + 1) * C_V, :]).T)
            else:
                out_ref[g, hd * C_V:(hd + 1) * C_V, :] = h_t
    if final:
        for g in rows:
            _mlstm_finish(jnp.concatenate(outs[g], axis=1), og_ref.at[g], z_ref.at[g],
                          ng_ref.at[layer:layer + 1], out_ref.at[g])


def _mlstm_gates(gt_ref, m_ref, reverse):
    n_rows = gt_ref.shape[1]
    n_dir = 2 * C_HEADS

    li = gt_ref[0:n_dir, :]
    b = gt_ref[n_dir:, :]
    b_last = b[:, 0:1] if reverse else b[:, n_rows - 1:n_rows]
    m_prev = m_ref[...]
    d_inter = b + m_prev
    a = (b_last - b) + li
    m_new = jnp.maximum(b_last + m_prev, jnp.max(a, axis=1, keepdims=True))
    ws = jnp.exp(a - m_new)
    decay = jnp.exp((b_last + m_prev) - m_new)
    m_ref[...] = m_new

    one8 = jnp.ones((n_dir, n_rows), F32)
    lhs = jnp.concatenate(list(_split3(li - b)) + [one8, one8, one8], axis=0).T.astype(BF16)
    rhs_all = jnp.concatenate([one8, one8, one8] + list(_split3(b)), axis=0)
    return lhs, rhs_all, d_inter, ws, decay


def _mlstm_weights(hd, qk, gates, reverse):
    lhs, rhs_all, d_inter, _, _ = gates
    k, q_t = qk
    n_rows = k.shape[0]
    n_dir = 2 * C_HEADS
    gi = (C_HEADS if reverse else 0) + hd
    gate_row = lax.broadcasted_iota(jnp.int32, rhs_all.shape, 0) & (n_dir - 1)
    si = lax.broadcasted_iota(jnp.int32, (n_rows, n_rows), 0)
    ti = lax.broadcasted_iota(jnp.int32, (n_rows, n_rows), 1)
    causal = (si >= ti) if reverse else (si <= ti)

    sl = slice(hd * C_QK, (hd + 1) * C_QK)
    rhs = jnp.where(gate_row == gi, rhs_all, 0.0).astype(BF16)
    d = jnp.where(causal, _dot(lhs, rhs), NEG)
    d_int = d_inter[gi:gi + 1, :]
    m_t = jnp.maximum(jnp.max(d, axis=0, keepdims=True), d_int)
    w = (_dot(k[:, sl], q_t[sl, :]) * jnp.exp(d - m_t)).astype(BF16)
    return w, m_t, jnp.exp(d_int - m_t)


def _mlstm_apply(hd, qk, vt_ref, gates, weights, cst_ref, reverse):
    _, _, _, ws, decay = gates
    w, m_t, g = weights
    k, q_t = qk
    n_rows = k.shape[0]
    gi = (C_HEADS if reverse else 0) + hd
    sl = slice(hd * C_QK, (hd + 1) * C_QK)
    v_t = vt_ref[hd * C_V:(hd + 1) * C_V, :]
    v_ext = jnp.concatenate([v_t, jnp.ones((C_ONES, n_rows), BF16)], axis=0)
    ws_row = ws[gi:gi + 1, :]
    v_scaled = jnp.concatenate([(v_t.astype(F32) * ws_row).astype(BF16),
                                jnp.broadcast_to(ws_row, (C_ONES, n_rows)).astype(BF16)], axis=0)
    cst = cst_ref[hd]
    num = _dot(v_ext, w) + g * _dot(cst.astype(BF16), q_t[sl, :])
    den = jnp.maximum(jnp.abs(num[C_V:C_V + 1, :]), jnp.exp(-m_t))
    cst_ref[hd] = decay[gi:gi + 1, :C_QK] * cst + _dot(v_scaled, k[:, sl])
    return num[:C_V, :] / den


def _mlstm_finish(hsum, og_ref, z_ref, ng_ref, out_ref):
    hsum = hsum * _sigmoid(og_ref[...])
    ys = []
    for hd in range(C_HEADS):
        sl = slice(hd * C_V, (hd + 1) * C_V)
        hh = hsum[:, sl]
        mu = jnp.mean(hh, axis=-1, keepdims=True)
        var = jnp.mean(jnp.square(hh - mu), axis=-1, keepdims=True)
        ys.append((hh - mu) * lax.rsqrt(var + EPS) * ng_ref[:, sl])
    y = jnp.concatenate(ys, axis=1) * _silu(z_ref[...])
    out_ref[...] = y.astype(out_ref.dtype)


def _mlstm(u3, gt, vt, conv_w, layer, *, reverse, qk=None, hb=None, norm_g=None):
    batch, seq, _ = u3.shape
    n_chunks = seq // CHUNK
    final = hb is not None
    grp = REC_GROUP
    assert CHUNK == LANE

    def chunk(c):
        return _chunk_index(c, n_chunks, reverse)

    def ublk(blk):
        return pl.BlockSpec((grp, CHUNK, D_GROUP), lambda bg, c: (bg, chunk(c), blk))

    row_spec = ublk(0)
    t_spec = lambda ch: pl.BlockSpec((grp, ch, CHUNK), lambda bg, c: (bg, 0, chunk(c)))
    gate_specs = [t_spec(N_GATE)]
    gate_args = [gt]
    n_qk = C_HEADS * C_QK
    k_spec = pl.BlockSpec((grp, CHUNK, n_qk), lambda bg, c: (bg, chunk(c), 0))
    if final:
        in_specs = [k_spec, t_spec(n_qk), t_spec(D_GROUP)] + gate_specs + [
            ublk(BLK_C_O), ublk(BLK_C_Z), t_spec(D_GROUP),
            pl.BlockSpec(norm_g.shape, lambda bg, c: (0, 0))]
        args = [qk[0], qk[1], vt] + gate_args + [u3, u3, hb, norm_g]
        out_shape = jax.ShapeDtypeStruct((batch, seq, D_GROUP), BF16)
        out_specs = row_spec
    else:
        per = CHUNK // SUBLANE

        def halo(after):
            def index(bg, c):
                cc = _chunk_index(c, n_chunks, reverse)
                row = (cc + 1) * per if after else cc * per - 1
                return (bg, jnp.clip(row, 0, n_chunks * per - 1), BLK_C_QK)
            return pl.BlockSpec((grp, SUBLANE, D_GROUP), index)

        in_specs = [halo(False), ublk(BLK_C_QK), halo(True),
                    pl.BlockSpec(conv_w.shape, lambda bg, c: (0, 0, 0)),
                    t_spec(D_GROUP)] + gate_specs
        args = [u3, u3, u3, conv_w, vt] + gate_args
        out_shape = (jax.ShapeDtypeStruct((batch, D_GROUP, seq), F32),
                     jax.ShapeDtypeStruct((batch, seq, n_qk), BF16),
                     jax.ShapeDtypeStruct((batch, n_qk, seq), BF16))
        out_specs = (t_spec(D_GROUP), k_spec, t_spec(n_qk))
    return pl.pallas_call(
        functools.partial(_mlstm_kernel, layer=layer, reverse=reverse, final=final),
        out_shape=out_shape,
        grid=(batch // grp, n_chunks),
        in_specs=in_specs,
        out_specs=out_specs,
        scratch_shapes=[pltpu.VMEM((grp, C_HEADS, C_V + C_ONES, C_QK), F32),
                        pltpu.VMEM((grp, 2 * C_HEADS, CHUNK), F32)],
        compiler_params=pltpu.CompilerParams(
            dimension_semantics=("parallel", "arbitrary"), vmem_limit_bytes=VMEM_LIMIT),
        name="mlstm_fwd" if final else "mlstm_bwd",
    )(*args)


def _rope_table_kernel(pos_ref, invf_ref, cos_ref, sin_ref):
    ang = invf_ref[...] * pos_ref[...].astype(F32)
    n_rows = ang.shape[1]
    half = ROPE_DIM // 2
    r = lax.broadcasted_iota(jnp.int32, ang.shape, 0)
    sn = jnp.sin(ang)
    sn = jnp.where(r < half, -sn, sn)
    pad = jnp.zeros((LANE - ROPE_DIM, n_rows), F32)

    def expand(x):
        t = jnp.concatenate([x, pad], axis=0).T
        return t + pltpu.roll(t, B_HD, axis=1)

    lane = lax.broadcasted_iota(jnp.int32, (n_rows, LANE), 1) & (B_HD - 1)
    cos_ref[...] = jnp.where(lane < ROPE_DIM, expand(jnp.cos(ang)), 1.0)
    sin_ref[...] = expand(sn)


def _rope_tables(positions):
    rows = positions.size
    half = ROPE_DIM // 2
    inv_freq = ROPE_THETA ** (-jnp.arange(0, ROPE_DIM, 2, dtype=F32) / ROPE_DIM)
    invf = jnp.concatenate([inv_freq, inv_freq]).reshape(ROPE_DIM, 1)
    tm = ROW_TILE
    return pl.pallas_call(
        _rope_table_kernel,
        out_shape=(jax.ShapeDtypeStruct((rows, LANE), F32),) * 2,
        grid=(rows // tm,),
        in_specs=[pl.BlockSpec((1, tm), lambda i: (0, i)),
                  pl.BlockSpec((ROPE_DIM, 1), lambda i: (0, 0))],
        out_specs=(pl.BlockSpec((tm, LANE), lambda i: (i, 0)),) * 2,
        compiler_params=pltpu.CompilerParams(dimension_semantics=("parallel",)),
        name="rope_tables",
    )(positions.reshape(1, rows), invf)


def _rope(x, cos, sin):
    width = x.shape[1]
    if width > LANE:
        cos = jnp.concatenate([cos] * (width // LANE), axis=1)
        sin = jnp.concatenate([sin] * (width // LANE), axis=1)
    half = ROPE_DIM // 2
    r = lax.broadcasted_iota(jnp.int32, x.shape, 1) & (B_HD - 1)
    partner = jnp.where(r < half, pltpu.roll(x, width - half, axis=1), pltpu.roll(x, half, axis=1))
    return x * cos + partner * sin


def _swa_kernel(q_ref, z_ref, mp_ref, mc_ref, mn_ref, cp_ref, cc_ref, cn_ref,
                sp_ref, sc_ref, sn_ref, sink_ref, out_ref, *, layer):
    tile = q_ref.shape[0]
    blk = mp_ref.shape[0]
    n_sub = tile // blk
    i = pl.program_id(1)
    n_tiles = pl.num_programs(1)
    kv_w = B_KV_HEADS * B_HD
    grp = B_HEADS // B_KV_HEADS

    q = (_rope(q_ref[...], cc_ref[...], sc_ref[...]) * (B_HD ** -0.5)).astype(BF16)
    k = jnp.concatenate([
        _rope(mp_ref[:, MISC_K:MISC_K + kv_w], cp_ref[...], sp_ref[...]),
        _rope(mc_ref[:, MISC_K:MISC_K + kv_w], cc_ref[...], sc_ref[...]),
        _rope(mn_ref[:, MISC_K:MISC_K + kv_w], cn_ref[...], sn_ref[...])], axis=0).astype(BF16)
    v = jnp.concatenate([mp_ref[:, MISC_V:MISC_V + kv_w], mc_ref[:, MISC_V:MISC_V + kv_w],
                         mn_ref[:, MISC_V:MISC_V + kv_w]], axis=0)
    n_keys = v.shape[0]

    kr = lax.broadcasted_iota(jnp.int32, (3 * blk, blk), 0)
    qc = lax.broadcasted_iota(jnp.int32, (3 * blk, blk), 1)
    in_window = jnp.abs(qc - (kr - blk)) <= WINDOW
    biases = []
    for j in range(n_sub):
        valid = in_window
        if j == 0:
            valid = valid & (kr >= jnp.where(i > 0, 0, blk))
        if j == n_sub - 1:
            valid = valid & (kr < jnp.where(i < n_tiles - 1, 3 * blk, 2 * blk))
        biases.append(jnp.where(valid, 0.0, NEG))

    ones = jnp.ones((n_keys, B_HD), F32)
    vts = [jnp.concatenate([v[:, kvh * B_HD:(kvh + 1) * B_HD], ones], axis=1).T.astype(BF16)
           for kvh in range(B_KV_HEADS)]

    tiles = [[] for _ in range(n_sub)]
    pairs = [[] for _ in range(n_sub)]
    for hq in range(B_HEADS):
        kvh = hq // grp
        ksl = slice(kvh * B_HD, (kvh + 1) * B_HD)
        sk = sink_ref[layer:layer + 1, hq:hq + 1]
        for j in range(n_sub):
            keys = slice(j * blk, (j + 3) * blk)
            qh = q[j * blk:(j + 1) * blk, hq * B_HD:(hq + 1) * B_HD]
            s = _dot_nt(k[keys, ksl], qh)
            s = jnp.concatenate([s[:blk] + biases[j][:blk], s[blk:2 * blk],
                                 s[2 * blk:] + biases[j][2 * blk:]], axis=0)
            m = jnp.maximum(jnp.max(s, axis=0, keepdims=True), sk)
            p = jnp.exp(s - m).astype(BF16)
            o = _dot(vts[kvh][:, keys], p)
            denom = o[B_HD:B_HD + 1, :] + jnp.exp(sk - m)
            pairs[j].append(o[:B_HD, :] / denom)
            if len(pairs[j]) == LANE // B_HD:
                tiles[j].append(jnp.concatenate(pairs[j], axis=0).T)
                pairs[j] = []
    for j in range(n_sub):
        rows = slice(j * blk, (j + 1) * blk)
        y = jnp.concatenate(tiles[j], axis=1) * _silu(z_ref[rows, :])
        out_ref[rows, :] = y.astype(out_ref.dtype)


def _swa(u, cos_t, sin_t, sink, layer, batch):
    rows = u.shape[0]
    n_blocks = rows // batch // CHUNK
    n_sub = SWA_TILE // CHUNK
    n_tiles = n_blocks // n_sub

    def tile_spec(width, blk):
        return pl.BlockSpec((SWA_TILE, width), lambda bi, i: (bi * n_tiles + i, blk))

    def halo_spec(width, blk, after):
        def index(bi, i):
            nb = (i + 1) * n_sub if after else i * n_sub - 1
            return (bi * n_blocks + jnp.clip(nb, 0, n_blocks - 1), blk)
        return pl.BlockSpec((CHUNK, width), index)

    def table_specs():
        return [halo_spec(LANE, 0, False), tile_spec(LANE, 0), halo_spec(LANE, 0, True)]

    return pl.pallas_call(
        functools.partial(_swa_kernel, layer=layer),
        out_shape=jax.ShapeDtypeStruct((rows, D_GROUP), BF16),
        grid=(batch, n_tiles),
        in_specs=[tile_spec(D_GROUP, BLK_B_Q), tile_spec(D_GROUP, BLK_B_Z),
                  halo_spec(D_GROUP, BLK_MISC, False), tile_spec(D_GROUP, BLK_MISC),
                  halo_spec(D_GROUP, BLK_MISC, True)] + table_specs() + table_specs() + [
                  pl.BlockSpec(sink.shape, lambda bi, i: (0, 0))],
        out_specs=tile_spec(D_GROUP, 0),
        compiler_params=pltpu.CompilerParams(
            dimension_semantics=("parallel", "parallel"), vmem_limit_bytes=VMEM_LIMIT),
        name="window_attn",
    )(u, u, u, u, u, cos_t, cos_t, cos_t, sin_t, sin_t, sin_t, sink)


def _mem_kv_kernel(mem_ref, g_ref, w_ref, kt_ref, v_ref, *, layer):
    x = mem_ref[...]
    ms = jnp.mean(x * x, axis=-1, keepdims=True)
    h = (x * lax.rsqrt(ms + EPS) * g_ref[layer:layer + 1, :]).astype(BF16)
    kv = _dot(h, w_ref[...].astype(BF16))
    kt_ref[...] = kv[:, :D_GROUP].T.astype(kt_ref.dtype)
    v_ref[...] = kv[:, D_GROUP:].astype(v_ref.dtype)


def _mem_kv(mem, g_all, w_all, layer):
    batch, m_len, d = mem.shape
    n_out = w_all.shape[2]
    return pl.pallas_call(
        functools.partial(_mem_kv_kernel, layer=layer),
        out_shape=(jax.ShapeDtypeStruct((batch, D_GROUP, m_len), BF16),
                   jax.ShapeDtypeStruct((batch, m_len, D_GROUP), BF16)),
        grid=(batch,),
        in_specs=[pl.BlockSpec((None, m_len, d), lambda bi: (bi, 0, 0)),
                  pl.BlockSpec(g_all.shape, lambda bi: (0, 0)),
                  pl.BlockSpec((None, d, n_out), lambda bi: (layer, 0, 0))],
        out_specs=(pl.BlockSpec((None, D_GROUP, m_len), lambda bi: (bi, 0, 0)),
                   pl.BlockSpec((None, m_len, D_GROUP), lambda bi: (bi, 0, 0))),
        compiler_params=pltpu.CompilerParams(
            dimension_semantics=("parallel",), vmem_limit_bytes=VMEM_LIMIT),
        name="mem_kv",
    )(mem, g_all, w_all)


def _mem_attn_out_kernel(q_ref, z_ref, kt_ref, v_ref, ya_ref, yb_ref, yc_ref, w_ref, x_ref, g_ref,
                         o_ref, w16_ref, *, final):
    @pl.when((pl.program_id(0) == 0) & (pl.program_id(1) == 0))
    def _():
        w16_ref[...] = w_ref[...].astype(BF16)

    q = q_ref[...].astype(BF16)
    outs = []
    for hd in range(D_HEADS):
        sl = slice(hd * D_HD, (hd + 1) * D_HD)
        s = _dot(q[:, sl], kt_ref[sl, :])
        m = jnp.max(s, axis=-1, keepdims=True)
        p = jnp.exp2((s - m) * (D_HD ** -0.5 * LOG2_E))
        den = jnp.sum(p, axis=-1, keepdims=True)
        outs.append(_dot(p.astype(BF16), v_ref[:, sl]) / den)
    yd = (jnp.concatenate(outs, axis=1) * _silu(z_ref[...])).astype(BF16)

    y = jnp.concatenate([ya_ref[...], yb_ref[...], yc_ref[...], yd], axis=1)
    acc = x_ref[...] + _dot(y, w16_ref[...])
    if final:
        ms = jnp.mean(acc * acc, axis=-1, keepdims=True)
        acc = acc * lax.rsqrt(ms + EPS) * g_ref[...]
    o_ref[...] = acc


def _mem_attn_out(u, u16, kt, v, ya, yb, yc, w_all, xf, g, layer, batch, final):
    rows, d = xf.shape
    tm = ROW_TILE
    per_batch = rows // batch // tm
    row = lambda width, blk: pl.BlockSpec((tm, width), lambda bi, i: (bi * per_batch + i, blk))
    return pl.pallas_call(
        functools.partial(_mem_attn_out_kernel, final=final),
        out_shape=jax.ShapeDtypeStruct((rows, d), F32),
        grid=(batch, per_batch),
        in_specs=[row(D_GROUP, BLK16_D_Q), row(D_GROUP, BLK_D_Z),
                  pl.BlockSpec((None,) + kt.shape[1:], lambda bi, i: (bi, 0, 0)),
                  pl.BlockSpec((None,) + v.shape[1:], lambda bi, i: (bi, 0, 0)),
                  row(D_GROUP, 0), row(D_GROUP, 0), row(D_GROUP, 0),
                  pl.BlockSpec((None, 4 * D_GROUP, d), lambda bi, i: (layer, 0, 0),
                               pipeline_mode=pl.Buffered(1)),
                  row(d, 0),
                  pl.BlockSpec((1, d), lambda bi, i: (0, 0))],
        out_specs=row(d, 0),
        scratch_shapes=[pltpu.VMEM((4 * D_GROUP, d), BF16)],
        compiler_params=pltpu.CompilerParams(
            dimension_semantics=("arbitrary", "arbitrary"), vmem_limit_bytes=VMEM_LIMIT),
        name="mem_attn_out",
    )(u16, u, kt, v, ya, yb, yc, w_all, xf, g.reshape(1, d))


def kernel(x, mem, positions, norm_g, w_in, hgrn_lb_logits, hgrn_norm_g, attn_sink, mlstm_conv_w,
           mlstm_gate_b, mlstm_norm_g, mem_norm_g, w_mem_kv, w_out, final_norm_g):
    batch, seq, d = x.shape
    depth = w_in.shape[0]
    assert all(seq % t == 0 for t in (CHUNK, ROW_TILE, IN_ROW_TILE, SWA_TILE))
    assert batch % REC_GROUP == 0
    xf = x.reshape(batch * seq, d)
    cos_t, sin_t = _rope_tables(positions)
    w_all = _prep_w_in(w_in)
    gate_b = mlstm_gate_b.reshape(depth, N_GATE, 1)
    for layer in range(depth):
        u, u16, gt, vt = _in_proj(xf, norm_g, w_all, gate_b, layer, batch)
        u3 = u.reshape(batch, seq, u.shape[1])
        u16_3 = u16.reshape(batch, seq, u16.shape[1])
        ob = _hgrn(u3, u16_3, hgrn_lb_logits, layer, reverse=True)
        ya = _hgrn(u3, u16_3, hgrn_lb_logits, layer, reverse=False, ob=ob, norm_g=hgrn_norm_g)
        yb = _swa(u, cos_t, sin_t, attn_sink, layer, batch)
        hb, k_c, qt_c = _mlstm(u3, gt, vt, mlstm_conv_w, layer, reverse=True)
        yc = _mlstm(u3, gt, vt, None, layer, reverse=False, qk=(k_c, qt_c), hb=hb,
                    norm_g=mlstm_norm_g)
        mem_kt, mem_v = _mem_kv(mem, mem_norm_g, w_mem_kv, layer)
        xf = _mem_attn_out(u, u16, mem_kt, mem_v, ya.reshape(batch * seq, D_GROUP), yb,
                           yc.reshape(batch * seq, D_GROUP), w_out, xf, final_norm_g, layer, batch,
                           final=(layer == depth - 1))
    return xf.reshape(batch, seq, d)
```

```python
import functools

import jax
import jax.numpy as jnp
import numpy as np
from jax import lax
from jax.experimental import pallas as pl
from jax.experimental.pallas import tpu as pltpu

F32 = jnp.float32
BF16 = jnp.bfloat16

D_GROUP = 512
A_HEADS, A_HD = 4, 128
B_HEADS, B_KV_HEADS, B_HD = 8, 2, 64
WINDOW = 128
ROPE_THETA = 500000.0
ROPE_DIM = 16
C_HEADS, C_QK, C_V = 4, 64, 128
CONV_W = 5
D_HEADS, D_HD = 4, 128
EPS = 1e-6
IN_SIZES = (
    D_GROUP, D_GROUP, D_GROUP, D_GROUP, D_GROUP,
    B_HEADS * B_HD, B_KV_HEADS * B_HD, B_KV_HEADS * B_HD, D_GROUP,
    C_HEADS * C_QK, C_HEADS * C_QK, C_HEADS * C_V, D_GROUP, D_GROUP,
    2 * C_HEADS, 2 * C_HEADS,
    D_GROUP, D_GROUP,
)
SPLIT_POINTS = tuple(int(s) for s in np.cumsum(IN_SIZES)[:-1])

LANE = 128
SUBLANE = 8
VMEM_LIMIT = 48 * 1024 * 1024

CHUNK = 128
ROW_TILE = 1024
IN_ROW_TILE = 512
IN_PROJ_VMEM_LIMIT = 56 * 1024 * 1024
W_PREP_TILE = 256
HGRN_STEP_CHUNKS = 2
HGRN_SUB = 64
HGRN_SAFE_LOG2 = 100.0
SWA_TILE = 1024
REC_GROUP = 4
NEG = -1e30
LOG2_E = 1.4426950408889634

(BLK_A_Q, BLK_A_FF, BLK_A_FB, BLK_A_Z, BLK_B_Q, BLK_B_Z, BLK_C_QK, BLK_C_O,
 BLK_C_Z, BLK_D_Z, BLK_MISC) = range(11)
N_BLK = 11
BLK16_A_I, BLK16_D_Q = range(2)
N_BLK16 = 2
W_ROW_C_V = (N_BLK + N_BLK16) * D_GROUP
N_GATE = 4 * C_HEADS
W_ROW_GATE = W_ROW_C_V + D_GROUP
N_W_ROWS = W_ROW_GATE + N_GATE
MISC_K, MISC_V = 0, 128
C_ONES = 16


def _dot(a, b):
    return jnp.dot(a, b, preferred_element_type=F32)


def _dot_nt(a, b):
    return lax.dot_general(a, b, (((1,), (1,)), ((), ())), preferred_element_type=F32)


def _dot_tn(a, b):
    return lax.dot_general(a, b, (((0,), (0,)), ((), ())), preferred_element_type=F32)


def _sigmoid(x):
    return 1.0 / (1.0 + jnp.exp(-x))


def _silu(x):
    return x * _sigmoid(x)


def _log_sigmoid(x):
    return jnp.minimum(x, 0.0) - jnp.log(1.0 + jnp.exp(-jnp.abs(x)))


def _chunk_cumsum_lanes(x, reverse):
    n = x.shape[1]
    pos = lax.broadcasted_iota(jnp.int32, x.shape, 1) & (CHUNK - 1)
    s = 1
    while s < CHUNK:
        if reverse:
            x = x + jnp.where(pos < CHUNK - s, pltpu.roll(x, n - s, axis=1), 0.0)
        else:
            x = x + jnp.where(pos >= s, pltpu.roll(x, s, axis=1), 0.0)
        s *= 2
    return x


def _in_proj_kernel(x_ref, g_ref, w_ref, gb_ref, u_ref, u16_ref, gt_ref, vt_ref, *, layer):
    x = x_ref[...]
    ms = jnp.mean(x * x, axis=-1, keepdims=True)
    h = (x * lax.rsqrt(ms + EPS) * g_ref[layer:layer + 1, :]).astype(BF16)
    vg = _dot_nt(w_ref[W_ROW_C_V:W_ROW_GATE + N_GATE, :], h)
    vt_ref[...] = vg[:D_GROUP, :].astype(BF16)
    gates = vg[D_GROUP:, :] + gb_ref[layer]
    n_dir = 2 * C_HEADS
    log_f = _log_sigmoid(gates[n_dir:, :])
    is_fwd = lax.broadcasted_iota(jnp.int32, log_f.shape, 0) < C_HEADS
    cum = jnp.where(is_fwd, _chunk_cumsum_lanes(log_f, False), _chunk_cumsum_lanes(log_f, True))
    gt_ref[...] = jnp.concatenate([gates[:n_dir, :], cum], axis=0)
    for j in range(N_BLK):
        u_ref[:, j * D_GROUP:(j + 1) * D_GROUP] = _dot_nt(h, w_ref[j * D_GROUP:(j + 1) * D_GROUP, :])
    for j in range(N_BLK16):
        rows = slice((N_BLK + j) * D_GROUP, (N_BLK + j + 1) * D_GROUP)
        u16_ref[:, j * D_GROUP:(j + 1) * D_GROUP] = _dot_nt(h, w_ref[rows, :]).astype(BF16)


def _in_proj(xf, g_all, w_all, gate_b, layer, batch):
    rows, d = xf.shape
    tm = IN_ROW_TILE
    per_batch = rows // batch // tm
    return pl.pallas_call(
        functools.partial(_in_proj_kernel, layer=layer),
        out_shape=(jax.ShapeDtypeStruct((rows, N_BLK * D_GROUP), F32),
                   jax.ShapeDtypeStruct((rows, N_BLK16 * D_GROUP), BF16),
                   jax.ShapeDtypeStruct((batch, N_GATE, rows // batch), F32),
                   jax.ShapeDtypeStruct((batch, D_GROUP, rows // batch), BF16)),
        grid=(rows // tm,),
        in_specs=[
            pl.BlockSpec((tm, d), lambda i: (i, 0)),
            pl.BlockSpec(g_all.shape, lambda i: (0, 0)),
            pl.BlockSpec((None, N_W_ROWS, d), lambda i: (layer, 0, 0), pipeline_mode=pl.Buffered(1)),
            pl.BlockSpec(gate_b.shape, lambda i: (0, 0, 0)),
        ],
        out_specs=(
            pl.BlockSpec((tm, N_BLK * D_GROUP), lambda i: (i, 0)),
            pl.BlockSpec((tm, N_BLK16 * D_GROUP), lambda i: (i, 0)),
            pl.BlockSpec((None, N_GATE, tm), lambda i: (i // per_batch, 0, i % per_batch)),
            pl.BlockSpec((None, D_GROUP, tm), lambda i: (i // per_batch, 0, i % per_batch)),
        ),
        compiler_params=pltpu.CompilerParams(
            dimension_semantics=("parallel",), vmem_limit_bytes=IN_PROJ_VMEM_LIMIT),
        name="in_proj",
    )(xf, g_all, w_all, gate_b)


def _w_in_segments():
    names = ("a_q", "a_i", "a_ff", "a_fb", "a_z", "b_q", "b_k", "b_v", "b_z",
             "c_q", "c_k", "c_v", "c_o", "c_z", "c_ig", "c_fg", "d_q", "d_z")
    src = dict(zip(names, zip((0,) + SPLIT_POINTS, IN_SIZES)))
    blocks = {BLK_A_Q: ("a_q",), BLK_A_FF: ("a_ff",), BLK_A_FB: ("a_fb",), BLK_A_Z: ("a_z",),
              BLK_B_Q: ("b_q",), BLK_B_Z: ("b_z",), BLK_C_QK: ("c_q", "c_k"), BLK_C_O: ("c_o",),
              BLK_C_Z: ("c_z",), BLK_D_Z: ("d_z",), BLK_MISC: ("b_k", "b_v"),
              N_BLK + BLK16_A_I: ("a_i",), N_BLK + BLK16_D_Q: ("d_q",)}
    placed = []
    for blk, group in sorted(blocks.items()):
        dst = blk * D_GROUP
        for name in group:
            placed.append((name, dst))
            dst += src[name][1]
    placed += [("c_v", W_ROW_C_V), ("c_ig", W_ROW_GATE), ("c_fg", W_ROW_GATE + src["c_ig"][1])]
    segs, gaps, end = [], [], 0
    for name, dst in placed:
        off, width = src[name]
        if dst > end:
            gaps.append((end, dst))
        if segs and segs[-1][0] + segs[-1][1] == off and segs[-1][2] + segs[-1][1] == dst:
            segs[-1] = (segs[-1][0], segs[-1][1] + width, segs[-1][2])
        else:
            segs.append((off, width, dst))
        end = dst + width
    assert end == N_W_ROWS
    return tuple(segs), tuple(gaps)


def _w_prep_kernel(wt_ref, wc_ref):
    segs, gaps = _w_in_segments()
    for off, width, dst in segs:
        wc_ref[dst:dst + width, :] = wt_ref[off:off + width, :].astype(BF16)
    for lo, hi in gaps:
        wc_ref[lo:hi, :] = jnp.zeros((hi - lo, wc_ref.shape[1]), BF16)


def _prep_w_in(w):
    depth, d, d_in = w.shape
    wt = jnp.swapaxes(w, 1, 2)
    tk = W_PREP_TILE
    return pl.pallas_call(
        _w_prep_kernel,
        out_shape=jax.ShapeDtypeStruct((depth, N_W_ROWS, d), BF16),
        grid=(depth, d // tk),
        in_specs=[pl.BlockSpec((None, d_in, tk), lambda l, i: (l, 0, i))],
        out_specs=pl.BlockSpec((None, N_W_ROWS, tk), lambda l, i: (l, 0, i)),
        compiler_params=pltpu.CompilerParams(
            dimension_semantics=("parallel", "parallel"), vmem_limit_bytes=VMEM_LIMIT),
        name="w_prep",
    )(wt)


def _level_ref(b, h, reverse):
    n_rows, n = b.shape
    blk = 2 * h
    r0 = h if reverse else h - 1
    if blk >= SUBLANE:
        pieces = [jnp.broadcast_to(b[s + r0:s + r0 + 1, :], (blk, n)) for s in range(0, n_rows, blk)]
        return pieces[0] if len(pieces) == 1 else jnp.concatenate(pieces, axis=0)
    res = lax.broadcasted_iota(jnp.int32, b.shape, 0) & (blk - 1)
    out = None
    for m in range(blk):
        shift = (m - r0) % n_rows
        cand = b if shift == 0 else pltpu.roll(b, shift, axis=0)
        out = cand if out is None else jnp.where(res == m, cand, out)
    return out


def _hgrn_lower_bound(lbl_ref, layer, depth):
    rows = [lbl_ref[j:j + 1, :] for j in range(depth)]
    mx = functools.reduce(jnp.maximum, rows)
    es = [jnp.exp(r - mx) for r in rows]
    tot = functools.reduce(lambda a, c: a + c, es)
    lb = jnp.zeros_like(rows[0])
    for j in range(1, layer + 1):
        lb = lb + es[j] / tot
    return lb


def _cumsum_rows_mxu(x, reverse):
    n = x.shape[0]
    r = lax.broadcasted_iota(jnp.int32, (n, n), 0)
    c = lax.broadcasted_iota(jnp.int32, (n, n), 1)
    tri = jnp.where((c >= r) if reverse else (c <= r), 1.0, 0.0).astype(BF16)
    hi = x.astype(BF16)
    r1 = x - hi.astype(F32)
    mid = r1.astype(BF16)
    lo = (r1 - mid.astype(F32)).astype(BF16)
    return _dot(jnp.concatenate([tri, tri, tri], axis=1), jnp.concatenate([hi, mid, lo], axis=0))


def _hgrn_level_operand(q, k, f, b2, h, reverse):
    n_rows = q.shape[0]
    blk = 2 * h
    r0 = h if reverse else h - 1
    if h >= SUBLANE:
        pieces = []
        for s0 in range(0, n_rows, blk):
            ref = b2[s0 + r0:s0 + r0 + 1, :]
            lo, up = slice(s0, s0 + h), slice(s0 + h, s0 + blk)
            if reverse:
                pieces += [q[lo] * jnp.exp2(b2[lo] - ref), k[up] * jnp.exp2(ref - b2[up])]
            else:
                pieces += [k[lo] * jnp.exp2(ref - b2[lo]), q[up] * jnp.exp2(b2[up] - ref)]
        return jnp.concatenate(pieces, axis=0)
    row = lax.broadcasted_iota(jnp.int32, q.shape, 0)
    is_q = ((row & h) == 0) if reverse else ((row & h) != 0)
    qk = jnp.where(is_q, q, k)
    if h == 1:
        return jnp.where(is_q, q * f, k)
    if h == 2:
        res = row & 3
        f_dn = pltpu.roll(f, 1, axis=0)
        f_up = pltpu.roll(f, n_rows - 1, axis=0)
        if reverse:
            e = jnp.where(res == 0, f * f_up, jnp.where(res == 1, f, jnp.where(res == 2, 1.0, f_dn)))
        else:
            e = jnp.where(res == 0, f_up, jnp.where(res == 1, 1.0, jnp.where(res == 2, f, f_dn * f)))
        return qk * e
    return qk * jnp.exp2(-jnp.abs(b2 - _level_ref(b2, h, reverse)))


def _hgrn_kernel(*refs, layer, depth, reverse, final):
    *io_refs, st_ref = refs

    @pl.when(pl.program_id(1) == 0)
    def _():
        st_ref[...] = jnp.zeros_like(st_ref)

    n_sub = io_refs[0].shape[1] // CHUNK
    for s in (reversed(range(n_sub)) if reverse else range(n_sub)):
        views = [r.at[:, s * CHUNK:(s + 1) * CHUNK] if len(r.shape) == 3 else r for r in io_refs]
        _hgrn_chunk(*views, st_ref, layer=layer, depth=depth, reverse=reverse, final=final)


def _hgrn_chunk(*refs, layer, depth, reverse, final):
    if final:
        q_ref, v_ref, zg_ref, lbl_ref, z_ref, ob_ref, ng_ref, out_ref, st_ref = refs
    else:
        q_ref, v_ref, zg_ref, lbl_ref, out_ref, st_ref = refs

    rows = range(q_ref.shape[0])
    n_rows = q_ref.shape[1]
    ti = lax.broadcasted_iota(jnp.int32, (n_rows, n_rows), 0)
    si = lax.broadcasted_iota(jnp.int32, (n_rows, n_rows), 1)
    ahead = (ti < si) if reverse else (ti > si)
    level = jnp.where(ahead, 31 - lax.clz(ti ^ si), jnp.where(ti == si, -1, -2))
    preps, bounds = zip(*[_hgrn_prep(q_ref.at[g], zg_ref.at[g], lbl_ref, layer, depth, reverse)
                          for g in rows])

    def run(fast):
        outs = [[] for _ in rows]
        for hd in range(A_HEADS):
            sl = slice(hd * A_HD, (hd + 1) * A_HD)
            heads = [tuple(a[:, sl] for a in preps[g]) for g in rows]
            if fast:
                atts = [_hgrn_scores_fast(heads[g], level) for g in rows]
            else:
                atts = [_hgrn_scores(heads[g], level, n_rows, None, reverse) for g in rows]
                h = n_rows // 2
                while h >= 1:
                    atts = [_hgrn_scores(heads[g], level, h, atts[g], reverse) for g in rows]
                    h //= 2
            for g in rows:
                outs[g].append(_hgrn_apply(heads[g], atts[g], v_ref.at[g, :, sl],
                                           st_ref.at[g, hd], reverse))
        for g in rows:
            o = jnp.concatenate(outs[g], axis=1)
            if final:
                _hgrn_finish(o, z_ref.at[g], ob_ref.at[g], ng_ref.at[layer:layer + 1],
                             out_ref.at[g])
            else:
                out_ref[g] = o

    safe = jnp.max(functools.reduce(jnp.maximum, bounds)) <= HGRN_SAFE_LOG2

    @pl.when(safe)
    def _():
        run(True)

    @pl.when(jnp.logical_not(safe))
    def _():
        run(False)


def _hgrn_prep(q_ref, zg_ref, lbl_ref, layer, depth, reverse):
    n_rows = q_ref.shape[0]
    q = q_ref[...]
    zg = zg_ref[...]
    e = jnp.exp(-jnp.abs(zg))
    r = 1.0 / (1.0 + e)
    er = e * r
    pos = zg >= 0.0
    f = jnp.where(pos, r, er)
    k = jnp.where(pos, er, r)
    if layer > 0:
        lb = _hgrn_lower_bound(lbl_ref, layer, depth)
        f = lb + (1.0 - lb) * f
        k = (1.0 - lb) * k
    g2 = jnp.log2(f)
    half = HGRN_SUB // 2
    bound = None
    for s0 in range(0, n_rows, HGRN_SUB):
        first = jnp.sum(g2[s0:s0 + half], axis=0, keepdims=True) + g2[s0 + half:s0 + half + 1]
        second = jnp.sum(g2[s0 + half:s0 + HGRN_SUB], axis=0, keepdims=True)
        worst = -jnp.minimum(first, second)
        bound = worst if bound is None else jnp.maximum(bound, worst)
    b2 = _cumsum_rows_mxu(g2, reverse)
    return (q, k, f, b2), bound


def _hgrn_scores(head, level, h, att, reverse):
    q, k, f, b2 = head
    if att is None:
        return jnp.where(level == -1, _dot_nt(q.astype(BF16), k.astype(BF16)), 0.0)
    c = _hgrn_level_operand(q, k, f, b2, h, reverse).astype(BF16)
    return jnp.where(level == (h.bit_length() - 1), _dot_nt(c, c), att)


def _hgrn_scores_fast(head, level):
    q, k, _, b2 = head
    n_rows = q.shape[0]
    blocks = []
    for s0 in range(0, n_rows, HGRN_SUB):
        rows = slice(s0, s0 + HGRN_SUB)
        ref = b2[s0 + HGRN_SUB // 2:s0 + HGRN_SUB // 2 + 1, :]
        qs = (q[rows] * jnp.exp2(b2[rows] - ref)).astype(BF16)
        ks = (k * jnp.exp2(jnp.minimum(ref - b2, HGRN_SAFE_LOG2))).astype(BF16)
        blocks.append(_dot_nt(qs, ks))
    return jnp.where(level >= -1, jnp.concatenate(blocks, axis=0), 0.0)


def _hgrn_apply(head, att, v_ref, st_ref, reverse):
    q, k, _, b2 = head
    n_rows = q.shape[0]
    tot = b2[0:1, :] if reverse else b2[n_rows - 1:n_rows, :]
    v_bf = v_ref[...].astype(BF16)
    q_in = (q * jnp.exp2(b2)).astype(BF16)
    k_out = (k * jnp.exp2(tot - b2)).astype(BF16)
    st = st_ref[...]
    o = _dot(att.astype(BF16), v_bf) + _dot(q_in, st.astype(BF16))
    dec_col = jnp.broadcast_to(jnp.exp2(tot), (SUBLANE, tot.shape[1])).T[:, 0:1]
    st_ref[...] = dec_col * st + _dot_tn(k_out, v_bf)
    return o


def _hgrn_finish(o, z_ref, ob_ref, ng_ref, out_ref):
    o = o + ob_ref[...]
    ys = []
    for hd in range(A_HEADS):
        sl = slice(hd * A_HD, (hd + 1) * A_HD)
        oh = o[:, sl]
        ms = jnp.mean(oh * oh, axis=-1, keepdims=True)
        ys.append(oh * lax.rsqrt(ms + EPS) * ng_ref[:, sl])
    y = jnp.concatenate(ys, axis=1) * _silu(z_ref[...])
    out_ref[...] = y.astype(out_ref.dtype)


def _chunk_index(c, n_chunks, reverse):
    return n_chunks - 1 - c if reverse else c


def _hgrn(u3, u16, lb_logits, layer, *, reverse, ob=None, norm_g=None):
    batch, seq, _ = u3.shape
    assert seq % (HGRN_STEP_CHUNKS * CHUNK) == 0
    n_chunks = seq // (HGRN_STEP_CHUNKS * CHUNK)
    depth = lb_logits.shape[0]
    final = ob is not None
    grp = REC_GROUP

    def ublk(blk):
        return pl.BlockSpec((grp, HGRN_STEP_CHUNKS * CHUNK, D_GROUP),
                            lambda bg, c: (bg, _chunk_index(c, n_chunks, reverse), blk))

    row_spec = ublk(0)
    in_specs = [ublk(BLK_A_Q), ublk(BLK16_A_I), ublk(BLK_A_FB if reverse else BLK_A_FF),
                pl.BlockSpec((depth, D_GROUP), lambda bg, c: (0, 0))]
    args = [u3, u16, u3, lb_logits]
    if final:
        in_specs += [ublk(BLK_A_Z), row_spec, pl.BlockSpec(norm_g.shape, lambda bg, c: (0, 0))]
        args += [u3, ob, norm_g]
    return pl.pallas_call(
        functools.partial(_hgrn_kernel, layer=layer, depth=depth, reverse=reverse, final=final),
        out_shape=jax.ShapeDtypeStruct((batch, seq, D_GROUP), BF16 if final else F32),
        grid=(batch // grp, n_chunks),
        in_specs=in_specs,
        out_specs=row_spec,
        scratch_shapes=[pltpu.VMEM((grp, A_HEADS, A_HD, A_HD), F32)],
        compiler_params=pltpu.CompilerParams(
            dimension_semantics=("parallel", "arbitrary"), vmem_limit_bytes=VMEM_LIMIT),
        name="hgrn_fwd" if final else "hgrn_bwd",
    )(*args)


def _split3(x):
    hi = x.astype(BF16).astype(F32)
    r1 = x - hi
    mid = r1.astype(BF16).astype(F32)
    lo = (r1 - mid).astype(BF16).astype(F32)
    return hi, mid, lo


def _conv_silu_qk(qkp_ref, qkc_ref, qkn_ref, cw_ref, reverse):
    n_rows = qkc_ref.shape[0]
    c = pl.program_id(1)
    n_chunks = pl.num_programs(1)
    cc = _chunk_index(c, n_chunks, reverse)
    halo = SUBLANE
    prev = qkp_ref[...] * (cc > 0).astype(F32)
    nxt = qkn_ref[...] * (cc < n_chunks - 1).astype(F32)
    xcat = jnp.concatenate([prev, qkc_ref[...], nxt], axis=0)
    acc = None
    for j in range(CONV_W):
        off = halo + j - CONV_W // 2
        term = cw_ref[j:j + 1, :] * xcat[off:off + n_rows, :]
        acc = term if acc is None else acc + term
    qk = _silu(acc)
    n_qk = C_HEADS * C_QK
    return qk[:, n_qk:].astype(BF16), (qk[:, :n_qk] * (C_QK ** -0.5)).T.astype(BF16)


def _mlstm_kernel(*refs, layer, reverse, final):
    if final:
        (k_ref, qt_ref, v_ref, gt_ref, og_ref, z_ref, hb_ref, ng_ref,
         out_ref, cst_ref, m_ref) = refs
    else:
        (qkp_ref, qkc_ref, qkn_ref, cw_ref, v_ref, gt_ref,
         out_ref, ko_ref, qto_ref, cst_ref, m_ref) = refs

    @pl.when(pl.program_id(1) == 0)
    def _():
        cst_ref[...] = jnp.zeros_like(cst_ref)
        m_ref[...] = jnp.zeros_like(m_ref)

    rows = range(v_ref.shape[0])
    if final:
        qks = [(k_ref[g], qt_ref[g]) for g in rows]
    else:
        qks = [_conv_silu_qk(qkp_ref.at[g], qkc_ref.at[g], qkn_ref.at[g], cw_ref.at[layer], reverse)
               for g in rows]
        for g in rows:
            ko_ref[g], qto_ref[g] = qks[g]
    gates = [_mlstm_gates(gt_ref.at[g], m_ref.at[g], reverse) for g in rows]
    outs = [[] for _ in rows]
    for hd in range(C_HEADS):
        weights = [_mlstm_weights(hd, qks[g], gates[g], reverse) for g in rows]
        for g in rows:
            h_t = _mlstm_apply(hd, qks[g], v_ref.at[g], gates[g], weights[g], cst_ref.at[g], reverse)
            if final:
                outs[g].append((h_t + hb_ref[g, hd * C_V:(hd + 1) * C_V, :]).T)
            else:
                out_ref[g, hd * C_V:(hd + 1) * C_V, :] = h_t
    if final:
        for g in rows:
            _mlstm_finish(jnp.concatenate(outs[g], axis=1), og_ref.at[g], z_ref.at[g],
                          ng_ref.at[layer:layer + 1], out_ref.at[g])


def _mlstm_gates(gt_ref, m_ref, reverse):
    n_rows = gt_ref.shape[1]
    n_dir = 2 * C_HEADS

    li = gt_ref[0:n_dir, :]
    b = gt_ref[n_dir:, :]
    b_last = b[:, 0:1] if reverse else b[:, n_rows - 1:n_rows]
    m_prev = m_ref[...]
    d_inter = b + m_prev
    a = (b_last - b) + li
    m_new = jnp.maximum(b_last + m_prev, jnp.max(a, axis=1, keepdims=True))
    ws = jnp.exp(a - m_new)
    decay = jnp.exp((b_last + m_prev) - m_new)
    m_ref[...] = m_new

    one8 = jnp.ones((n_dir, n_rows), F32)
    lhs = jnp.concatenate(list(_split3(li - b)) + [one8, one8, one8], axis=0).T.astype(BF16)
    rhs_all = jnp.concatenate([one8, one8, one8] + list(_split3(b)), axis=0)
    return lhs, rhs_all, d_inter, ws, decay


def _mlstm_weights(hd, qk, gates, reverse):
    lhs, rhs_all, d_inter, _, _ = gates
    k, q_t = qk
    n_rows = k.shape[0]
    n_dir = 2 * C_HEADS
    gi = (C_HEADS if reverse else 0) + hd
    gate_row = lax.broadcasted_iota(jnp.int32, rhs_all.shape, 0) & (n_dir - 1)
    si = lax.broadcasted_iota(jnp.int32, (n_rows, n_rows), 0)
    ti = lax.broadcasted_iota(jnp.int32, (n_rows, n_rows), 1)
    causal = (si >= ti) if reverse else (si <= ti)

    sl = slice(hd * C_QK, (hd + 1) * C_QK)
    rhs = jnp.where(gate_row == gi, rhs_all, 0.0).astype(BF16)
    d = jnp.where(causal, _dot(lhs, rhs), NEG)
    d_int = d_inter[gi:gi + 1, :]
    m_t = jnp.maximum(jnp.max(d, axis=0, keepdims=True), d_int)
    w = (_dot(k[:, sl], q_t[sl, :]) * jnp.exp(d - m_t)).astype(BF16)
    return w, m_t, jnp.exp(d_int - m_t)


def _mlstm_apply(hd, qk, vt_ref, gates, weights, cst_ref, reverse):
    _, _, _, ws, decay = gates
    w, m_t, g = weights
    k, q_t = qk
    n_rows = k.shape[0]
    gi = (C_HEADS if reverse else 0) + hd
    sl = slice(hd * C_QK, (hd + 1) * C_QK)
    v_t = vt_ref[hd * C_V:(hd + 1) * C_V, :]
    v_ext = jnp.concatenate([v_t, jnp.ones((C_ONES, n_rows), BF16)], axis=0)
    ws_row = ws[gi:gi + 1, :]
    v_scaled = jnp.concatenate([(v_t.astype(F32) * ws_row).astype(BF16),
                                jnp.broadcast_to(ws_row, (C_ONES, n_rows)).astype(BF16)], axis=0)
    cst = cst_ref[hd]
    num = _dot(v_ext, w) + g * _dot(cst.astype(BF16), q_t[sl, :])
    den = jnp.maximum(jnp.abs(num[C_V:C_V + 1, :]), jnp.exp(-m_t))
    cst_ref[hd] = decay[gi:gi + 1, :C_QK] * cst + _dot(v_scaled, k[:, sl])
    return num[:C_V, :] / den


def _mlstm_finish(hsum, og_ref, z_ref, ng_ref, out_ref):
    hsum = hsum * _sigmoid(og_ref[...])
    ys = []
    for hd in range(C_HEADS):
        sl = slice(hd * C_V, (hd + 1) * C_V)
        hh = hsum[:, sl]
        mu = jnp.mean(hh, axis=-1, keepdims=True)
        var = jnp.mean(jnp.square(hh - mu), axis=-1, keepdims=True)
        ys.append((hh - mu) * lax.rsqrt(var + EPS) * ng_ref[:, sl])
    y = jnp.concatenate(ys, axis=1) * _silu(z_ref[...])
    out_ref[...] = y.astype(out_ref.dtype)


def _mlstm(u3, gt, vt, conv_w, layer, *, reverse, qk=None, hb=None, norm_g=None):
    batch, seq, _ = u3.shape
    n_chunks = seq // CHUNK
    final = hb is not None
    grp = REC_GROUP
    assert CHUNK == LANE

    def chunk(c):
        return _chunk_index(c, n_chunks, reverse)

    def ublk(blk):
        return pl.BlockSpec((grp, CHUNK, D_GROUP), lambda bg, c: (bg, chunk(c), blk))

    row_spec = ublk(0)
    t_spec = lambda ch: pl.BlockSpec((grp, ch, CHUNK), lambda bg, c: (bg, 0, chunk(c)))
    gate_specs = [t_spec(N_GATE)]
    gate_args = [gt]
    n_qk = C_HEADS * C_QK
    k_spec = pl.BlockSpec((grp, CHUNK, n_qk), lambda bg, c: (bg, chunk(c), 0))
    if final:
        in_specs = [k_spec, t_spec(n_qk), t_spec(D_GROUP)] + gate_specs + [
            ublk(BLK_C_O), ublk(BLK_C_Z), t_spec(D_GROUP),
            pl.BlockSpec(norm_g.shape, lambda bg, c: (0, 0))]
        args = [qk[0], qk[1], vt] + gate_args + [u3, u3, hb, norm_g]
        out_shape = jax.ShapeDtypeStruct((batch, seq, D_GROUP), BF16)
        out_specs = row_spec
    else:
        per = CHUNK // SUBLANE

        def halo(after):
            def index(bg, c):
                cc = _chunk_index(c, n_chunks, reverse)
                row = (cc + 1) * per if after else cc * per - 1
                return (bg, jnp.clip(row, 0, n_chunks * per - 1), BLK_C_QK)
            return pl.BlockSpec((grp, SUBLANE, D_GROUP), index)

        in_specs = [halo(False), ublk(BLK_C_QK), halo(True),
                    pl.BlockSpec(conv_w.shape, lambda bg, c: (0, 0, 0)),
                    t_spec(D_GROUP)] + gate_specs
        args = [u3, u3, u3, conv_w, vt] + gate_args
        out_shape = (jax.ShapeDtypeStruct((batch, D_GROUP, seq), F32),
                     jax.ShapeDtypeStruct((batch, seq, n_qk), BF16),
                     jax.ShapeDtypeStruct((batch, n_qk, seq), BF16))
        out_specs = (t_spec(D_GROUP), k_spec, t_spec(n_qk))
    return pl.pallas_call(
        functools.partial(_mlstm_kernel, layer=layer, reverse=reverse, final=final),
        out_shape=out_shape,
        grid=(batch // grp, n_chunks),
        in_specs=in_specs,
        out_specs=out_specs,
        scratch_shapes=[pltpu.VMEM((grp, C_HEADS, C_V + C_ONES, C_QK), F32),
                        pltpu.VMEM((grp, 2 * C_HEADS, CHUNK), F32)],
        compiler_params=pltpu.CompilerParams(
            dimension_semantics=("parallel", "arbitrary"), vmem_limit_bytes=VMEM_LIMIT),
        name="mlstm_fwd" if final else "mlstm_bwd",
    )(*args)


def _rope_table_kernel(pos_ref, invf_ref, cos_ref, sin_ref):
    ang = invf_ref[...] * pos_ref[...].astype(F32)
    n_rows = ang.shape[1]
    half = ROPE_DIM // 2
    r = lax.broadcasted_iota(jnp.int32, ang.shape, 0)
    sn = jnp.sin(ang)
    sn = jnp.where(r < half, -sn, sn)
    pad = jnp.zeros((LANE - ROPE_DIM, n_rows), F32)

    def expand(x):
        t = jnp.concatenate([x, pad], axis=0).T
        return t + pltpu.roll(t, B_HD, axis=1)

    lane = lax.broadcasted_iota(jnp.int32, (n_rows, LANE), 1) & (B_HD - 1)
    cos_ref[...] = jnp.where(lane < ROPE_DIM, expand(jnp.cos(ang)), 1.0)
    sin_ref[...] = expand(sn)


def _rope_tables(positions):
    rows = positions.size
    half = ROPE_DIM // 2
    inv_freq = ROPE_THETA ** (-jnp.arange(0, ROPE_DIM, 2, dtype=F32) / ROPE_DIM)
    invf = jnp.concatenate([inv_freq, inv_freq]).reshape(ROPE_DIM, 1)
    tm = ROW_TILE
    return pl.pallas_call(
        _rope_table_kernel,
        out_shape=(jax.ShapeDtypeStruct((rows, LANE), F32),) * 2,
        grid=(rows // tm,),
        in_specs=[pl.BlockSpec((1, tm), lambda i: (0, i)),
                  pl.BlockSpec((ROPE_DIM, 1), lambda i: (0, 0))],
        out_specs=(pl.BlockSpec((tm, LANE), lambda i: (i, 0)),) * 2,
        compiler_params=pltpu.CompilerParams(dimension_semantics=("parallel",)),
        name="rope_tables",
    )(positions.reshape(1, rows), invf)


def _rope(x, cos, sin):
    width = x.shape[1]
    if width > LANE:
        cos = jnp.concatenate([cos] * (width // LANE), axis=1)
        sin = jnp.concatenate([sin] * (width // LANE), axis=1)
    half = ROPE_DIM // 2
    r = lax.broadcasted_iota(jnp.int32, x.shape, 1) & (B_HD - 1)
    partner = jnp.where(r < half, pltpu.roll(x, width - half, axis=1), pltpu.roll(x, half, axis=1))
    return x * cos + partner * sin


def _swa_kernel(q_ref, z_ref, mp_ref, mc_ref, mn_ref, cp_ref, cc_ref, cn_ref,
                sp_ref, sc_ref, sn_ref, sink_ref, out_ref, *, layer):
    tile = q_ref.shape[0]
    blk = mp_ref.shape[0]
    n_sub = tile // blk
    i = pl.program_id(1)
    n_tiles = pl.num_programs(1)
    kv_w = B_KV_HEADS * B_HD
    grp = B_HEADS // B_KV_HEADS

    q = (_rope(q_ref[...], cc_ref[...], sc_ref[...]) * (B_HD ** -0.5)).astype(BF16)
    k = jnp.concatenate([
        _rope(mp_ref[:, MISC_K:MISC_K + kv_w], cp_ref[...], sp_ref[...]),
        _rope(mc_ref[:, MISC_K:MISC_K + kv_w], cc_ref[...], sc_ref[...]),
        _rope(mn_ref[:, MISC_K:MISC_K + kv_w], cn_ref[...], sn_ref[...])], axis=0).astype(BF16)
    v = jnp.concatenate([mp_ref[:, MISC_V:MISC_V + kv_w], mc_ref[:, MISC_V:MISC_V + kv_w],
                         mn_ref[:, MISC_V:MISC_V + kv_w]], axis=0)
    n_keys = v.shape[0]

    kr = lax.broadcasted_iota(jnp.int32, (3 * blk, blk), 0)
    qc = lax.broadcasted_iota(jnp.int32, (3 * blk, blk), 1)
    in_window = jnp.abs(qc - (kr - blk)) <= WINDOW
    biases = []
    for j in range(n_sub):
        valid = in_window
        if j == 0:
            valid = valid & (kr >= jnp.where(i > 0, 0, blk))
        if j == n_sub - 1:
            valid = valid & (kr < jnp.where(i < n_tiles - 1, 3 * blk, 2 * blk))
        biases.append(jnp.where(valid, 0.0, NEG))

    ones = jnp.ones((n_keys, B_HD), F32)
    vts = [jnp.concatenate([v[:, kvh * B_HD:(kvh + 1) * B_HD], ones], axis=1).T.astype(BF16)
           for kvh in range(B_KV_HEADS)]

    tiles = [[] for _ in range(n_sub)]
    pairs = [[] for _ in range(n_sub)]
    for hq in range(B_HEADS):
        kvh = hq // grp
        ksl = slice(kvh * B_HD, (kvh + 1) * B_HD)
        sk = sink_ref[layer:layer + 1, hq:hq + 1]
        for j in range(n_sub):
            keys = slice(j * blk, (j + 3) * blk)
            qh = q[j * blk:(j + 1) * blk, hq * B_HD:(hq + 1) * B_HD]
            s = _dot_nt(k[keys, ksl], qh)
            s = jnp.concatenate([s[:blk] + biases[j][:blk], s[blk:2 * blk],
                                 s[2 * blk:] + biases[j][2 * blk:]], axis=0)
            m = jnp.maximum(jnp.max(s, axis=0, keepdims=True), sk)
            p = jnp.exp(s - m).astype(BF16)
            o = _dot(vts[kvh][:, keys], p)
            denom = o[B_HD:B_HD + 1, :] + jnp.exp(sk - m)
            pairs[j].append(o[:B_HD, :] / denom)
            if len(pairs[j]) == LANE // B_HD:
                tiles[j].append(jnp.concatenate(pairs[j], axis=0).T)
                pairs[j] = []
    for j in range(n_sub):
        rows = slice(j * blk, (j + 1) * blk)
        y = jnp.concatenate(tiles[j], axis=1) * _silu(z_ref[rows, :])
        out_ref[rows, :] = y.astype(out_ref.dtype)


def _swa(u, cos_t, sin_t, sink, layer, batch):
    rows = u.shape[0]
    n_blocks = rows // batch // CHUNK
    n_sub = SWA_TILE // CHUNK
    n_tiles = n_blocks // n_sub

    def tile_spec(width, blk):
        return pl.BlockSpec((SWA_TILE, width), lambda bi, i: (bi * n_tiles + i, blk))

    def halo_spec(width, blk, after):
        def index(bi, i):
            nb = (i + 1) * n_sub if after else i * n_sub - 1
            return (bi * n_blocks + jnp.clip(nb, 0, n_blocks - 1), blk)
        return pl.BlockSpec((CHUNK, width), index)

    def table_specs():
        return [halo_spec(LANE, 0, False), tile_spec(LANE, 0), halo_spec(LANE, 0, True)]

    return pl.pallas_call(
        functools.partial(_swa_kernel, layer=layer),
        out_shape=jax.ShapeDtypeStruct((rows, D_GROUP), BF16),
        grid=(batch, n_tiles),
        in_specs=[tile_spec(D_GROUP, BLK_B_Q), tile_spec(D_GROUP, BLK_B_Z),
                  halo_spec(D_GROUP, BLK_MISC, False), tile_spec(D_GROUP, BLK_MISC),
                  halo_spec(D_GROUP, BLK_MISC, True)] + table_specs() + table_specs() + [
                  pl.BlockSpec(sink.shape, lambda bi, i: (0, 0))],
        out_specs=tile_spec(D_GROUP, 0),
        compiler_params=pltpu.CompilerParams(
            dimension_semantics=("parallel", "parallel"), vmem_limit_bytes=VMEM_LIMIT),
        name="window_attn",
    )(u, u, u, u, u, cos_t, cos_t, cos_t, sin_t, sin_t, sin_t, sink)


def _mem_kv_kernel(mem_ref, g_ref, w_ref, kt_ref, v_ref, *, layer):
    x = mem_ref[...]
    ms = jnp.mean(x * x, axis=-1, keepdims=True)
    h = (x * lax.rsqrt(ms + EPS) * g_ref[layer:layer + 1, :]).astype(BF16)
    kv = _dot(h, w_ref[...].astype(BF16))
    kt_ref[...] = kv[:, :D_GROUP].T.astype(kt_ref.dtype)
    v_ref[...] = kv[:, D_GROUP:].astype(v_ref.dtype)


def _mem_kv(mem, g_all, w_all, layer):
    batch, m_len, d = mem.shape
    n_out = w_all.shape[2]
    return pl.pallas_call(
        functools.partial(_mem_kv_kernel, layer=layer),
        out_shape=(jax.ShapeDtypeStruct((batch, D_GROUP, m_len), BF16),
                   jax.ShapeDtypeStruct((batch, m_len, D_GROUP), BF16)),
        grid=(batch,),
        in_specs=[pl.BlockSpec((None, m_len, d), lambda bi: (bi, 0, 0)),
                  pl.BlockSpec(g_all.shape, lambda bi: (0, 0)),
                  pl.BlockSpec((None, d, n_out), lambda bi: (layer, 0, 0))],
        out_specs=(pl.BlockSpec((None, D_GROUP, m_len), lambda bi: (bi, 0, 0)),
                   pl.BlockSpec((None, m_len, D_GROUP), lambda bi: (bi, 0, 0))),
        compiler_params=pltpu.CompilerParams(
            dimension_semantics=("parallel",), vmem_limit_bytes=VMEM_LIMIT),
        name="mem_kv",
    )(mem, g_all, w_all)


def _mem_attn_out_kernel(q_ref, z_ref, kt_ref, v_ref, ya_ref, yb_ref, yc_ref, w_ref, x_ref, g_ref,
                         o_ref, w16_ref, *, final):
    @pl.when((pl.program_id(0) == 0) & (pl.program_id(1) == 0))
    def _():
        w16_ref[...] = w_ref[...].astype(BF16)

    q = q_ref[...].astype(BF16)
    outs = []
    for hd in range(D_HEADS):
        sl = slice(hd * D_HD, (hd + 1) * D_HD)
        s = _dot(q[:, sl], kt_ref[sl, :])
        m = jnp.max(s, axis=-1, keepdims=True)
        p = jnp.exp2((s - m) * (D_HD ** -0.5 * LOG2_E))
        den = jnp.sum(p, axis=-1, keepdims=True)
        outs.append(_dot(p.astype(BF16), v_ref[:, sl]) / den)
    yd = (jnp.concatenate(outs, axis=1) * _silu(z_ref[...])).astype(BF16)

    y = jnp.concatenate([ya_ref[...], yb_ref[...], yc_ref[...], yd], axis=1)
    acc = x_ref[...] + _dot(y, w16_ref[...])
    if final:
        ms = jnp.mean(acc * acc, axis=-1, keepdims=True)
        acc = acc * lax.rsqrt(ms + EPS) * g_ref[...]
    o_ref[...] = acc


def _mem_attn_out(u, u16, kt, v, ya, yb, yc, w_all, xf, g, layer, batch, final):
    rows, d = xf.shape
    tm = ROW_TILE
    per_batch = rows // batch // tm
    row = lambda width, blk: pl.BlockSpec((tm, width), lambda bi, i: (bi * per_batch + i, blk))
    return pl.pallas_call(
        functools.partial(_mem_attn_out_kernel, final=final),
        out_shape=jax.ShapeDtypeStruct((rows, d), F32),
        grid=(batch, per_batch),
        in_specs=[row(D_GROUP, BLK16_D_Q), row(D_GROUP, BLK_D_Z),
                  pl.BlockSpec((None,) + kt.shape[1:], lambda bi, i: (bi, 0, 0)),
                  pl.BlockSpec((None,) + v.shape[1:], lambda bi, i: (bi, 0, 0)),
                  row(D_GROUP, 0), row(D_GROUP, 0), row(D_GROUP, 0),
                  pl.BlockSpec((None, 4 * D_GROUP, d), lambda bi, i: (layer, 0, 0),
                               pipeline_mode=pl.Buffered(1)),
                  row(d, 0),
                  pl.BlockSpec((1, d), lambda bi, i: (0, 0))],
        out_specs=row(d, 0),
        scratch_shapes=[pltpu.VMEM((4 * D_GROUP, d), BF16)],
        compiler_params=pltpu.CompilerParams(
            dimension_semantics=("arbitrary", "arbitrary"), vmem_limit_bytes=VMEM_LIMIT),
        name="mem_attn_out",
    )(u16, u, kt, v, ya, yb, yc, w_all, xf, g.reshape(1, d))


def kernel(x, mem, positions, norm_g, w_in, hgrn_lb_logits, hgrn_norm_g, attn_sink, mlstm_conv_w,
           mlstm_gate_b, mlstm_norm_g, mem_norm_g, w_mem_kv, w_out, final_norm_g):
    batch, seq, d = x.shape
    depth = w_in.shape[0]
    assert all(seq % t == 0 for t in (CHUNK, ROW_TILE, IN_ROW_TILE, SWA_TILE))
    assert batch % REC_GROUP == 0
    xf = x.reshape(batch * seq, d)
    cos_t, sin_t = _rope_tables(positions)
    w_all = _prep_w_in(w_in)
    gate_b = mlstm_gate_b.reshape(depth, N_GATE, 1)
    for layer in range(depth):
        u, u16, gt, vt = _in_proj(xf, norm_g, w_all, gate_b, layer, batch)
        u3 = u.reshape(batch, seq, u.shape[1])
        u16_3 = u16.reshape(batch, seq, u16.shape[1])
        ob = _hgrn(u3, u16_3, hgrn_lb_logits, layer, reverse=True)
        ya = _hgrn(u3, u16_3, hgrn_lb_logits, layer, reverse=False, ob=ob, norm_g=hgrn_norm_g)
        yb = _swa(u, cos_t, sin_t, attn_sink, layer, batch)
        hb, k_c, qt_c = _mlstm(u3, gt, vt, mlstm_conv_w, layer, reverse=True)
        yc = _mlstm(u3, gt, vt, None, layer, reverse=False, qk=(k_c, qt_c), hb=hb,
                    norm_g=mlstm_norm_g)
        mem_kt, mem_v = _mem_kv(mem, mem_norm_g, w_mem_kv, layer)
        xf = _mem_attn_out(u, u16, mem_kt, mem_v, ya.reshape(batch * seq, D_GROUP), yb,
                           yc.reshape(batch * seq, D_GROUP), w_out, xf, final_norm_g, layer, batch,
                           final=(layer == depth - 1))
    return xf.reshape(batch, seq, d)
```

```python
import functools

import jax
import jax.numpy as jnp
import numpy as np
from jax import lax
from jax.experimental import pallas as pl
from jax.experimental.pallas import tpu as pltpu

F32 = jnp.float32
BF16 = jnp.bfloat16

D_GROUP = 512
A_HEADS, A_HD = 4, 128
B_HEADS, B_KV_HEADS, B_HD = 8, 2, 64
WINDOW = 128
ROPE_THETA = 500000.0
ROPE_DIM = 16
C_HEADS, C_QK, C_V = 4, 64, 128
CONV_W = 5
D_HEADS, D_HD = 4, 128
EPS = 1e-6
IN_SIZES = (
    D_GROUP, D_GROUP, D_GROUP, D_GROUP, D_GROUP,
    B_HEADS * B_HD, B_KV_HEADS * B_HD, B_KV_HEADS * B_HD, D_GROUP,
    C_HEADS * C_QK, C_HEADS * C_QK, C_HEADS * C_V, D_GROUP, D_GROUP,
    2 * C_HEADS, 2 * C_HEADS,
    D_GROUP, D_GROUP,
)
SPLIT_POINTS = tuple(int(s) for s in np.cumsum(IN_SIZES)[:-1])

LANE = 128
SUBLANE = 8
VMEM_LIMIT = 48 * 1024 * 1024

CHUNK = 128
ROW_TILE = 1024
IN_ROW_TILE = 512
IN_PROJ_VMEM_LIMIT = 56 * 1024 * 1024
W_PREP_TILE = 256
HGRN_SUB = 64
HGRN_SAFE_LOG2 = 100.0
SWA_INTERLEAVE = 4
SWA_TILE = 1024
REC_GROUP = 4
NEG = -1e30
LOG2_E = 1.4426950408889634

(BLK_A_Q, BLK_A_FF, BLK_A_FB, BLK_A_Z, BLK_B_Q, BLK_B_Z, BLK_C_QK, BLK_C_O,
 BLK_C_Z, BLK_D_Z, BLK_MISC) = range(11)
N_BLK = 11
BLK16_A_I, BLK16_D_Q = range(2)
N_BLK16 = 2
W_ROW_C_V = (N_BLK + N_BLK16) * D_GROUP
N_GATE = 4 * C_HEADS
W_ROW_GATE = W_ROW_C_V + D_GROUP
N_W_ROWS = W_ROW_GATE + N_GATE
MISC_K, MISC_V = 0, 128
C_ONES = 16


def _dot(a, b):
    return jnp.dot(a, b, preferred_element_type=F32)


def _dot_nt(a, b):
    return lax.dot_general(a, b, (((1,), (1,)), ((), ())), preferred_element_type=F32)


def _dot_tn(a, b):
    return lax.dot_general(a, b, (((0,), (0,)), ((), ())), preferred_element_type=F32)


def _sigmoid(x):
    return 1.0 / (1.0 + jnp.exp(-x))


def _silu(x):
    return x * _sigmoid(x)


def _log_sigmoid(x):
    return jnp.minimum(x, 0.0) - jnp.log(1.0 + jnp.exp(-jnp.abs(x)))


def _chunk_cumsum_lanes(x, reverse):
    n = x.shape[1]
    pos = lax.broadcasted_iota(jnp.int32, x.shape, 1) & (CHUNK - 1)
    s = 1
    while s < CHUNK:
        if reverse:
            x = x + jnp.where(pos < CHUNK - s, pltpu.roll(x, n - s, axis=1), 0.0)
        else:
            x = x + jnp.where(pos >= s, pltpu.roll(x, s, axis=1), 0.0)
        s *= 2
    return x


def _in_proj_kernel(x_ref, g_ref, w_ref, gb_ref, u_ref, u16_ref, gt_ref, vt_ref, *, layer):
    x = x_ref[...]
    ms = jnp.mean(x * x, axis=-1, keepdims=True)
    h = (x * lax.rsqrt(ms + EPS) * g_ref[layer:layer + 1, :]).astype(BF16)
    vg = _dot_nt(w_ref[W_ROW_C_V:W_ROW_GATE + N_GATE, :], h)
    vt_ref[...] = vg[:D_GROUP, :].astype(BF16)
    gates = vg[D_GROUP:, :] + gb_ref[layer]
    n_dir = 2 * C_HEADS
    log_f = _log_sigmoid(gates[n_dir:, :])
    is_fwd = lax.broadcasted_iota(jnp.int32, log_f.shape, 0) < C_HEADS
    cum = jnp.where(is_fwd, _chunk_cumsum_lanes(log_f, False), _chunk_cumsum_lanes(log_f, True))
    gt_ref[...] = jnp.concatenate([gates[:n_dir, :], cum], axis=0)
    for j in range(N_BLK):
        u_ref[:, j * D_GROUP:(j + 1) * D_GROUP] = _dot_nt(h, w_ref[j * D_GROUP:(j + 1) * D_GROUP, :])
    for j in range(N_BLK16):
        rows = slice((N_BLK + j) * D_GROUP, (N_BLK + j + 1) * D_GROUP)
        u16_ref[:, j * D_GROUP:(j + 1) * D_GROUP] = _dot_nt(h, w_ref[rows, :]).astype(BF16)


def _in_proj(xf, g_all, w_all, gate_b, layer, batch):
    rows, d = xf.shape
    tm = IN_ROW_TILE
    per_batch = rows // batch // tm
    return pl.pallas_call(
        functools.partial(_in_proj_kernel, layer=layer),
        out_shape=(jax.ShapeDtypeStruct((rows, N_BLK * D_GROUP), F32),
                   jax.ShapeDtypeStruct((rows, N_BLK16 * D_GROUP), BF16),
                   jax.ShapeDtypeStruct((batch, N_GATE, rows // batch), F32),
                   jax.ShapeDtypeStruct((batch, D_GROUP, rows // batch), BF16)),
        grid=(rows // tm,),
        in_specs=[
            pl.BlockSpec((tm, d), lambda i: (i, 0)),
            pl.BlockSpec(g_all.shape, lambda i: (0, 0)),
            pl.BlockSpec((None, N_W_ROWS, d), lambda i: (layer, 0, 0), pipeline_mode=pl.Buffered(1)),
            pl.BlockSpec(gate_b.shape, lambda i: (0, 0, 0)),
        ],
        out_specs=(
            pl.BlockSpec((tm, N_BLK * D_GROUP), lambda i: (i, 0)),
            pl.BlockSpec((tm, N_BLK16 * D_GROUP), lambda i: (i, 0)),
            pl.BlockSpec((None, N_GATE, tm), lambda i: (i // per_batch, 0, i % per_batch)),
            pl.BlockSpec((None, D_GROUP, tm), lambda i: (i // per_batch, 0, i % per_batch)),
        ),
        compiler_params=pltpu.CompilerParams(
            dimension_semantics=("parallel",), vmem_limit_bytes=IN_PROJ_VMEM_LIMIT),
        name="in_proj",
    )(xf, g_all, w_all, gate_b)


def _w_in_segments():
    names = ("a_q", "a_i", "a_ff", "a_fb", "a_z", "b_q", "b_k", "b_v", "b_z",
             "c_q", "c_k", "c_v", "c_o", "c_z", "c_ig", "c_fg", "d_q", "d_z")
    src = dict(zip(names, zip((0,) + SPLIT_POINTS, IN_SIZES)))
    blocks = {BLK_A_Q: ("a_q",), BLK_A_FF: ("a_ff",), BLK_A_FB: ("a_fb",), BLK_A_Z: ("a_z",),
              BLK_B_Q: ("b_q",), BLK_B_Z: ("b_z",), BLK_C_QK: ("c_q", "c_k"), BLK_C_O: ("c_o",),
              BLK_C_Z: ("c_z",), BLK_D_Z: ("d_z",), BLK_MISC: ("b_k", "b_v"),
              N_BLK + BLK16_A_I: ("a_i",), N_BLK + BLK16_D_Q: ("d_q",)}
    placed = []
    for blk, group in sorted(blocks.items()):
        dst = blk * D_GROUP
        for name in group:
            placed.append((name, dst))
            dst += src[name][1]
    placed += [("c_v", W_ROW_C_V), ("c_ig", W_ROW_GATE), ("c_fg", W_ROW_GATE + src["c_ig"][1])]
    segs, gaps, end = [], [], 0
    for name, dst in placed:
        off, width = src[name]
        if dst > end:
            gaps.append((end, dst))
        if segs and segs[-1][0] + segs[-1][1] == off and segs[-1][2] + segs[-1][1] == dst:
            segs[-1] = (segs[-1][0], segs[-1][1] + width, segs[-1][2])
        else:
            segs.append((off, width, dst))
        end = dst + width
    assert end == N_W_ROWS
    return tuple(segs), tuple(gaps)


def _w_prep_kernel(wt_ref, wc_ref):
    segs, gaps = _w_in_segments()
    for off, width, dst in segs:
        wc_ref[dst:dst + width, :] = wt_ref[off:off + width, :].astype(BF16)
    for lo, hi in gaps:
        wc_ref[lo:hi, :] = jnp.zeros((hi - lo, wc_ref.shape[1]), BF16)


def _prep_w_in(w):
    depth, d, d_in = w.shape
    wt = jnp.swapaxes(w, 1, 2)
    tk = W_PREP_TILE
    return pl.pallas_call(
        _w_prep_kernel,
        out_shape=jax.ShapeDtypeStruct((depth, N_W_ROWS, d), BF16),
        grid=(depth, d // tk),
        in_specs=[pl.BlockSpec((None, d_in, tk), lambda l, i: (l, 0, i))],
        out_specs=pl.BlockSpec((None, N_W_ROWS, tk), lambda l, i: (l, 0, i)),
        compiler_params=pltpu.CompilerParams(
            dimension_semantics=("parallel", "parallel"), vmem_limit_bytes=VMEM_LIMIT),
        name="w_prep",
    )(wt)


def _level_ref(b, h, reverse):
    n_rows, n = b.shape
    blk = 2 * h
    r0 = h if reverse else h - 1
    if blk >= SUBLANE:
        pieces = [jnp.broadcast_to(b[s + r0:s + r0 + 1, :], (blk, n)) for s in range(0, n_rows, blk)]
        return pieces[0] if len(pieces) == 1 else jnp.concatenate(pieces, axis=0)
    res = lax.broadcasted_iota(jnp.int32, b.shape, 0) & (blk - 1)
    out = None
    for m in range(blk):
        shift = (m - r0) % n_rows
        cand = b if shift == 0 else pltpu.roll(b, shift, axis=0)
        out = cand if out is None else jnp.where(res == m, cand, out)
    return out


def _hgrn_lower_bound(lbl_ref, layer, depth):
    rows = [lbl_ref[j:j + 1, :] for j in range(depth)]
    mx = functools.reduce(jnp.maximum, rows)
    es = [jnp.exp(r - mx) for r in rows]
    tot = functools.reduce(lambda a, c: a + c, es)
    lb = jnp.zeros_like(rows[0])
    for j in range(1, layer + 1):
        lb = lb + es[j] / tot
    return lb


def _cumsum_rows_mxu(x, reverse):
    n = x.shape[0]
    r = lax.broadcasted_iota(jnp.int32, (n, n), 0)
    c = lax.broadcasted_iota(jnp.int32, (n, n), 1)
    tri = jnp.where((c >= r) if reverse else (c <= r), 1.0, 0.0).astype(BF16)
    hi = x.astype(BF16)
    r1 = x - hi.astype(F32)
    mid = r1.astype(BF16)
    lo = (r1 - mid.astype(F32)).astype(BF16)
    return _dot(jnp.concatenate([tri, tri, tri], axis=1), jnp.concatenate([hi, mid, lo], axis=0))


def _hgrn_level_operand(q, k, f, b2, h, reverse):
    n_rows = q.shape[0]
    blk = 2 * h
    r0 = h if reverse else h - 1
    if h >= SUBLANE:
        pieces = []
        for s0 in range(0, n_rows, blk):
            ref = b2[s0 + r0:s0 + r0 + 1, :]
            lo, up = slice(s0, s0 + h), slice(s0 + h, s0 + blk)
            if reverse:
                pieces += [q[lo] * jnp.exp2(b2[lo] - ref), k[up] * jnp.exp2(ref - b2[up])]
            else:
                pieces += [k[lo] * jnp.exp2(ref - b2[lo]), q[up] * jnp.exp2(b2[up] - ref)]
        return jnp.concatenate(pieces, axis=0)
    row = lax.broadcasted_iota(jnp.int32, q.shape, 0)
    is_q = ((row & h) == 0) if reverse else ((row & h) != 0)
    qk = jnp.where(is_q, q, k)
    if h == 1:
        return jnp.where(is_q, q * f, k)
    if h == 2:
        res = row & 3
        f_dn = pltpu.roll(f, 1, axis=0)
        f_up = pltpu.roll(f, n_rows - 1, axis=0)
        if reverse:
            e = jnp.where(res == 0, f * f_up, jnp.where(res == 1, f, jnp.where(res == 2, 1.0, f_dn)))
        else:
            e = jnp.where(res == 0, f_up, jnp.where(res == 1, 1.0, jnp.where(res == 2, f, f_dn * f)))
        return qk * e
    return qk * jnp.exp2(-jnp.abs(b2 - _level_ref(b2, h, reverse)))


def _hgrn_kernel(*refs, layer, depth, reverse, final):
    if final:
        q_ref, v_ref, zg_ref, lbl_ref, z_ref, ob_ref, ng_ref, out_ref, st_ref = refs
    else:
        q_ref, v_ref, zg_ref, lbl_ref, out_ref, st_ref = refs

    @pl.when(pl.program_id(1) == 0)
    def _():
        st_ref[...] = jnp.zeros_like(st_ref)

    rows = range(q_ref.shape[0])
    n_rows = q_ref.shape[1]
    ti = lax.broadcasted_iota(jnp.int32, (n_rows, n_rows), 0)
    si = lax.broadcasted_iota(jnp.int32, (n_rows, n_rows), 1)
    ahead = (ti < si) if reverse else (ti > si)
    level = jnp.where(ahead, 31 - lax.clz(ti ^ si), jnp.where(ti == si, -1, -2))
    preps, bounds = zip(*[_hgrn_prep(q_ref.at[g], zg_ref.at[g], lbl_ref, layer, depth, reverse)
                          for g in rows])

    def run(fast):
        outs = [[] for _ in rows]
        for hd in range(A_HEADS):
            sl = slice(hd * A_HD, (hd + 1) * A_HD)
            heads = [tuple(a[:, sl] for a in preps[g]) for g in rows]
            if fast:
                atts = [_hgrn_scores_fast(heads[g], level) for g in rows]
            else:
                atts = [_hgrn_scores(heads[g], level, n_rows, None, reverse) for g in rows]
                h = n_rows // 2
                while h >= 1:
                    atts = [_hgrn_scores(heads[g], level, h, atts[g], reverse) for g in rows]
                    h //= 2
            for g in rows:
                outs[g].append(_hgrn_apply(heads[g], atts[g], v_ref.at[g, :, sl],
                                           st_ref.at[g, hd], reverse))
        for g in rows:
            o = jnp.concatenate(outs[g], axis=1)
            if final:
                _hgrn_finish(o, z_ref.at[g], ob_ref.at[g], ng_ref.at[layer:layer + 1],
                             out_ref.at[g])
            else:
                out_ref[g] = o

    safe = jnp.max(functools.reduce(jnp.maximum, bounds)) <= HGRN_SAFE_LOG2

    @pl.when(safe)
    def _():
        run(True)

    @pl.when(jnp.logical_not(safe))
    def _():
        run(False)


def _hgrn_prep(q_ref, zg_ref, lbl_ref, layer, depth, reverse):
    n_rows = q_ref.shape[0]
    q = q_ref[...]
    zg = zg_ref[...]
    e = jnp.exp(-jnp.abs(zg))
    r = 1.0 / (1.0 + e)
    er = e * r
    pos = zg >= 0.0
    f = jnp.where(pos, r, er)
    k = jnp.where(pos, er, r)
    if layer > 0:
        lb = _hgrn_lower_bound(lbl_ref, layer, depth)
        f = lb + (1.0 - lb) * f
        k = (1.0 - lb) * k
    g2 = jnp.log2(f)
    half = HGRN_SUB // 2
    bound = None
    for s0 in range(0, n_rows, HGRN_SUB):
        first = jnp.sum(g2[s0:s0 + half], axis=0, keepdims=True) + g2[s0 + half:s0 + half + 1]
        second = jnp.sum(g2[s0 + half:s0 + HGRN_SUB], axis=0, keepdims=True)
        worst = -jnp.minimum(first, second)
        bound = worst if bound is None else jnp.maximum(bound, worst)
    b2 = _cumsum_rows_mxu(g2, reverse)
    return (q, k, f, b2), bound


def _hgrn_scores(head, level, h, att, reverse):
    q, k, f, b2 = head
    if att is None:
        return jnp.where(level == -1, _dot_nt(q.astype(BF16), k.astype(BF16)), 0.0)
    c = _hgrn_level_operand(q, k, f, b2, h, reverse).astype(BF16)
    return jnp.where(level == (h.bit_length() - 1), _dot_nt(c, c), att)


def _hgrn_scores_fast(head, level):
    q, k, _, b2 = head
    n_rows = q.shape[0]
    blocks = []
    for s0 in range(0, n_rows, HGRN_SUB):
        rows = slice(s0, s0 + HGRN_SUB)
        ref = b2[s0 + HGRN_SUB // 2:s0 + HGRN_SUB // 2 + 1, :]
        qs = (q[rows] * jnp.exp2(b2[rows] - ref)).astype(BF16)
        ks = (k * jnp.exp2(jnp.minimum(ref - b2, HGRN_SAFE_LOG2))).astype(BF16)
        blocks.append(_dot_nt(qs, ks))
    return jnp.where(level >= -1, jnp.concatenate(blocks, axis=0), 0.0)


def _hgrn_apply(head, att, v_ref, st_ref, reverse):
    q, k, _, b2 = head
    n_rows = q.shape[0]
    tot = b2[0:1, :] if reverse else b2[n_rows - 1:n_rows, :]
    v_bf = v_ref[...].astype(BF16)
    q_in = (q * jnp.exp2(b2)).astype(BF16)
    k_out = (k * jnp.exp2(tot - b2)).astype(BF16)
    st = st_ref[...]
    o = _dot(att.astype(BF16), v_bf) + _dot(q_in, st.astype(BF16))
    dec_col = jnp.broadcast_to(jnp.exp2(tot), (SUBLANE, tot.shape[1])).T[:, 0:1]
    st_ref[...] = dec_col * st + _dot_tn(k_out, v_bf)
    return o


def _hgrn_finish(o, z_ref, ob_ref, ng_ref, out_ref):
    o = o + ob_ref[...]
    ys = []
    for hd in range(A_HEADS):
        sl = slice(hd * A_HD, (hd + 1) * A_HD)
        oh = o[:, sl]
        ms = jnp.mean(oh * oh, axis=-1, keepdims=True)
        ys.append(oh * lax.rsqrt(ms + EPS) * ng_ref[:, sl])
    y = jnp.concatenate(ys, axis=1) * _silu(z_ref[...])
    out_ref[...] = y.astype(out_ref.dtype)


def _chunk_index(c, n_chunks, reverse):
    return n_chunks - 1 - c if reverse else c


def _hgrn(u3, u16, lb_logits, layer, *, reverse, ob=None, norm_g=None):
    batch, seq, _ = u3.shape
    n_chunks = seq // CHUNK
    depth = lb_logits.shape[0]
    final = ob is not None
    grp = REC_GROUP

    def ublk(blk):
        return pl.BlockSpec((grp, CHUNK, D_GROUP),
                            lambda bg, c: (bg, _chunk_index(c, n_chunks, reverse), blk))

    row_spec = ublk(0)
    in_specs = [ublk(BLK_A_Q), ublk(BLK16_A_I), ublk(BLK_A_FB if reverse else BLK_A_FF),
                pl.BlockSpec((depth, D_GROUP), lambda bg, c: (0, 0))]
    args = [u3, u16, u3, lb_logits]
    if final:
        in_specs += [ublk(BLK_A_Z), row_spec, pl.BlockSpec(norm_g.shape, lambda bg, c: (0, 0))]
        args += [u3, ob, norm_g]
    return pl.pallas_call(
        functools.partial(_hgrn_kernel, layer=layer, depth=depth, reverse=reverse, final=final),
        out_shape=jax.ShapeDtypeStruct((batch, seq, D_GROUP), BF16 if final else F32),
        grid=(batch // grp, n_chunks),
        in_specs=in_specs,
        out_specs=row_spec,
        scratch_shapes=[pltpu.VMEM((grp, A_HEADS, A_HD, A_HD), F32)],
        compiler_params=pltpu.CompilerParams(
            dimension_semantics=("parallel", "arbitrary"), vmem_limit_bytes=VMEM_LIMIT),
        name="hgrn_fwd" if final else "hgrn_bwd",
    )(*args)


def _split3(x):
    hi = x.astype(BF16).astype(F32)
    r1 = x - hi
    mid = r1.astype(BF16).astype(F32)
    lo = (r1 - mid).astype(BF16).astype(F32)
    return hi, mid, lo


def _conv_silu_qk(qkp_ref, qkc_ref, qkn_ref, cw_ref, reverse):
    n_rows = qkc_ref.shape[0]
    c = pl.program_id(1)
    n_chunks = pl.num_programs(1)
    cc = _chunk_index(c, n_chunks, reverse)
    halo = SUBLANE
    prev = qkp_ref[...] * (cc > 0).astype(F32)
    nxt = qkn_ref[...] * (cc < n_chunks - 1).astype(F32)
    xcat = jnp.concatenate([prev, qkc_ref[...], nxt], axis=0)
    acc = None
    for j in range(CONV_W):
        off = halo + j - CONV_W // 2
        term = cw_ref[j:j + 1, :] * xcat[off:off + n_rows, :]
        acc = term if acc is None else acc + term
    qk = _silu(acc)
    n_qk = C_HEADS * C_QK
    return qk[:, n_qk:].astype(BF16), (qk[:, :n_qk] * (C_QK ** -0.5)).T.astype(BF16)


def _mlstm_kernel(*refs, layer, reverse, final):
    if final:
        (k_ref, qt_ref, v_ref, gt_ref, og_ref, z_ref, hb_ref, ng_ref,
         out_ref, cst_ref, m_ref) = refs
    else:
        (qkp_ref, qkc_ref, qkn_ref, cw_ref, v_ref, gt_ref,
         out_ref, ko_ref, qto_ref, cst_ref, m_ref) = refs

    @pl.when(pl.program_id(1) == 0)
    def _():
        cst_ref[...] = jnp.zeros_like(cst_ref)
        m_ref[...] = jnp.zeros_like(m_ref)

    rows = range(v_ref.shape[0])
    if final:
        qks = [(k_ref[g], qt_ref[g]) for g in rows]
    else:
        qks = [_conv_silu_qk(qkp_ref.at[g], qkc_ref.at[g], qkn_ref.at[g], cw_ref.at[layer], reverse)
               for g in rows]
        for g in rows:
            ko_ref[g], qto_ref[g] = qks[g]
    gates = [_mlstm_gates(gt_ref.at[g], m_ref.at[g], reverse) for g in rows]
    outs = [[] for _ in rows]
    for hd in range(C_HEADS):
        weights = [_mlstm_weights(hd, qks[g], gates[g], reverse) for g in rows]
        for g in rows:
            h_t = _mlstm_apply(hd, qks[g], v_ref.at[g], gates[g], weights[g], cst_ref.at[g], reverse)
            if final:
                outs[g].append((h_t + hb_ref[g, hd * C_V:(hd + 1) * C_V, :]).T)
            else:
                out_ref[g, hd * C_V:(hd + 1) * C_V, :] = h_t
    if final:
        for g in rows:
            _mlstm_finish(jnp.concatenate(outs[g], axis=1), og_ref.at[g], z_ref.at[g],
                          ng_ref.at[layer:layer + 1], out_ref.at[g])


def _mlstm_gates(gt_ref, m_ref, reverse):
    n_rows = gt_ref.shape[1]
    n_dir = 2 * C_HEADS

    li = gt_ref[0:n_dir, :]
    b = gt_ref[n_dir:, :]
    b_last = b[:, 0:1] if reverse else b[:, n_rows - 1:n_rows]
    m_prev = m_ref[...]
    d_inter = b + m_prev
    a = (b_last - b) + li
    m_new = jnp.maximum(b_last + m_prev, jnp.max(a, axis=1, keepdims=True))
    ws = jnp.exp(a - m_new)
    decay = jnp.exp((b_last + m_prev) - m_new)
    m_ref[...] = m_new

    one8 = jnp.ones((n_dir, n_rows), F32)
    lhs = jnp.concatenate(list(_split3(li - b)) + [one8, one8, one8], axis=0).T.astype(BF16)
    rhs_all = jnp.concatenate([one8, one8, one8] + list(_split3(b)), axis=0)
    return lhs, rhs_all, d_inter, ws, decay


def _mlstm_weights(hd, qk, gates, reverse):
    lhs, rhs_all, d_inter, _, _ = gates
    k, q_t = qk
    n_rows = k.shape[0]
    n_dir = 2 * C_HEADS
    gi = (C_HEADS if reverse else 0) + hd
    gate_row = lax.broadcasted_iota(jnp.int32, rhs_all.shape, 0) & (n_dir - 1)
    si = lax.broadcasted_iota(jnp.int32, (n_rows, n_rows), 0)
    ti = lax.broadcasted_iota(jnp.int32, (n_rows, n_rows), 1)
    causal = (si >= ti) if reverse else (si <= ti)

    sl = slice(hd * C_QK, (hd + 1) * C_QK)
    rhs = jnp.where(gate_row == gi, rhs_all, 0.0).astype(BF16)
    d = jnp.where(causal, _dot(lhs, rhs), NEG)
    d_int = d_inter[gi:gi + 1, :]
    m_t = jnp.maximum(jnp.max(d, axis=0, keepdims=True), d_int)
    w = (_dot(k[:, sl], q_t[sl, :]) * jnp.exp(d - m_t)).astype(BF16)
    return w, m_t, jnp.exp(d_int - m_t)


def _mlstm_apply(hd, qk, vt_ref, gates, weights, cst_ref, reverse):
    _, _, _, ws, decay = gates
    w, m_t, g = weights
    k, q_t = qk
    n_rows = k.shape[0]
    gi = (C_HEADS if reverse else 0) + hd
    sl = slice(hd * C_QK, (hd + 1) * C_QK)
    v_t = vt_ref[hd * C_V:(hd + 1) * C_V, :]
    v_ext = jnp.concatenate([v_t, jnp.ones((C_ONES, n_rows), BF16)], axis=0)
    ws_row = ws[gi:gi + 1, :]
    v_scaled = jnp.concatenate([(v_t.astype(F32) * ws_row).astype(BF16),
                                jnp.broadcast_to(ws_row, (C_ONES, n_rows)).astype(BF16)], axis=0)
    cst = cst_ref[hd]
    num = _dot(v_ext, w) + g * _dot(cst.astype(BF16), q_t[sl, :])
    den = jnp.maximum(jnp.abs(num[C_V:C_V + 1, :]), jnp.exp(-m_t))
    cst_ref[hd] = decay[gi:gi + 1, :C_QK] * cst + _dot(v_scaled, k[:, sl])
    return num[:C_V, :] / den


def _mlstm_finish(hsum, og_ref, z_ref, ng_ref, out_ref):
    hsum = hsum * _sigmoid(og_ref[...])
    ys = []
    for hd in range(C_HEADS):
        sl = slice(hd * C_V, (hd + 1) * C_V)
        hh = hsum[:, sl]
        mu = jnp.mean(hh, axis=-1, keepdims=True)
        var = jnp.mean(jnp.square(hh - mu), axis=-1, keepdims=True)
        ys.append((hh - mu) * lax.rsqrt(var + EPS) * ng_ref[:, sl])
    y = jnp.concatenate(ys, axis=1) * _silu(z_ref[...])
    out_ref[...] = y.astype(out_ref.dtype)


def _mlstm(u3, gt, vt, conv_w, layer, *, reverse, qk=None, hb=None, norm_g=None):
    batch, seq, _ = u3.shape
    n_chunks = seq // CHUNK
    final = hb is not None
    grp = REC_GROUP
    assert CHUNK == LANE

    def chunk(c):
        return _chunk_index(c, n_chunks, reverse)

    def ublk(blk):
        return pl.BlockSpec((grp, CHUNK, D_GROUP), lambda bg, c: (bg, chunk(c), blk))

    row_spec = ublk(0)
    t_spec = lambda ch: pl.BlockSpec((grp, ch, CHUNK), lambda bg, c: (bg, 0, chunk(c)))
    gate_specs = [t_spec(N_GATE)]
    gate_args = [gt]
    n_qk = C_HEADS * C_QK
    k_spec = pl.BlockSpec((grp, CHUNK, n_qk), lambda bg, c: (bg, chunk(c), 0))
    if final:
        in_specs = [k_spec, t_spec(n_qk), t_spec(D_GROUP)] + gate_specs + [
            ublk(BLK_C_O), ublk(BLK_C_Z), t_spec(D_GROUP),
            pl.BlockSpec(norm_g.shape, lambda bg, c: (0, 0))]
        args = [qk[0], qk[1], vt] + gate_args + [u3, u3, hb, norm_g]
        out_shape = jax.ShapeDtypeStruct((batch, seq, D_GROUP), BF16)
        out_specs = row_spec
    else:
        per = CHUNK // SUBLANE

        def halo(after):
            def index(bg, c):
                cc = _chunk_index(c, n_chunks, reverse)
                row = (cc + 1) * per if after else cc * per - 1
                return (bg, jnp.clip(row, 0, n_chunks * per - 1), BLK_C_QK)
            return pl.BlockSpec((grp, SUBLANE, D_GROUP), index)

        in_specs = [halo(False), ublk(BLK_C_QK), halo(True),
                    pl.BlockSpec(conv_w.shape, lambda bg, c: (0, 0, 0)),
                    t_spec(D_GROUP)] + gate_specs
        args = [u3, u3, u3, conv_w, vt] + gate_args
        out_shape = (jax.ShapeDtypeStruct((batch, D_GROUP, seq), F32),
                     jax.ShapeDtypeStruct((batch, seq, n_qk), BF16),
                     jax.ShapeDtypeStruct((batch, n_qk, seq), BF16))
        out_specs = (t_spec(D_GROUP), k_spec, t_spec(n_qk))
    return pl.pallas_call(
        functools.partial(_mlstm_kernel, layer=layer, reverse=reverse, final=final),
        out_shape=out_shape,
        grid=(batch // grp, n_chunks),
        in_specs=in_specs,
        out_specs=out_specs,
        scratch_shapes=[pltpu.VMEM((grp, C_HEADS, C_V + C_ONES, C_QK), F32),
                        pltpu.VMEM((grp, 2 * C_HEADS, CHUNK), F32)],
        compiler_params=pltpu.CompilerParams(
            dimension_semantics=("parallel", "arbitrary"), vmem_limit_bytes=VMEM_LIMIT),
        name="mlstm_fwd" if final else "mlstm_bwd",
    )(*args)


def _rope_table_kernel(pos_ref, invf_ref, cos_ref, sin_ref):
    ang = invf_ref[...] * pos_ref[...].astype(F32)
    n_rows = ang.shape[1]
    half = ROPE_DIM // 2
    r = lax.broadcasted_iota(jnp.int32, ang.shape, 0)
    sn = jnp.sin(ang)
    sn = jnp.where(r < half, -sn, sn)
    pad = jnp.zeros((LANE - ROPE_DIM, n_rows), F32)

    def expand(x):
        t = jnp.concatenate([x, pad], axis=0).T
        return t + pltpu.roll(t, B_HD, axis=1)

    lane = lax.broadcasted_iota(jnp.int32, (n_rows, LANE), 1) & (B_HD - 1)
    cos_ref[...] = jnp.where(lane < ROPE_DIM, expand(jnp.cos(ang)), 1.0)
    sin_ref[...] = expand(sn)


def _rope_tables(positions):
    rows = positions.size
    half = ROPE_DIM // 2
    inv_freq = ROPE_THETA ** (-jnp.arange(0, ROPE_DIM, 2, dtype=F32) / ROPE_DIM)
    invf = jnp.concatenate([inv_freq, inv_freq]).reshape(ROPE_DIM, 1)
    tm = ROW_TILE
    return pl.pallas_call(
        _rope_table_kernel,
        out_shape=(jax.ShapeDtypeStruct((rows, LANE), F32),) * 2,
        grid=(rows // tm,),
        in_specs=[pl.BlockSpec((1, tm), lambda i: (0, i)),
                  pl.BlockSpec((ROPE_DIM, 1), lambda i: (0, 0))],
        out_specs=(pl.BlockSpec((tm, LANE), lambda i: (i, 0)),) * 2,
        compiler_params=pltpu.CompilerParams(dimension_semantics=("parallel",)),
        name="rope_tables",
    )(positions.reshape(1, rows), invf)


def _rope(x, cos, sin):
    width = x.shape[1]
    if width > LANE:
        cos = jnp.concatenate([cos] * (width // LANE), axis=1)
        sin = jnp.concatenate([sin] * (width // LANE), axis=1)
    half = ROPE_DIM // 2
    r = lax.broadcasted_iota(jnp.int32, x.shape, 1) & (B_HD - 1)
    partner = jnp.where(r < half, pltpu.roll(x, width - half, axis=1), pltpu.roll(x, half, axis=1))
    return x * cos + partner * sin


def _swa_kernel(q_ref, z_ref, mp_ref, mc_ref, mn_ref, cp_ref, cc_ref, cn_ref,
                sp_ref, sc_ref, sn_ref, sink_ref, out_ref, *, layer):
    tile = q_ref.shape[0]
    blk = mp_ref.shape[0]
    n_sub = tile // blk
    i = pl.program_id(1)
    n_tiles = pl.num_programs(1)
    kv_w = B_KV_HEADS * B_HD
    grp = B_HEADS // B_KV_HEADS

    q = (_rope(q_ref[...], cc_ref[...], sc_ref[...]) * (B_HD ** -0.5)).astype(BF16)
    k = jnp.concatenate([
        _rope(mp_ref[:, MISC_K:MISC_K + kv_w], cp_ref[...], sp_ref[...]),
        _rope(mc_ref[:, MISC_K:MISC_K + kv_w], cc_ref[...], sc_ref[...]),
        _rope(mn_ref[:, MISC_K:MISC_K + kv_w], cn_ref[...], sn_ref[...])], axis=0).astype(BF16)
    v = jnp.concatenate([mp_ref[:, MISC_V:MISC_V + kv_w], mc_ref[:, MISC_V:MISC_V + kv_w],
                         mn_ref[:, MISC_V:MISC_V + kv_w]], axis=0)
    n_keys = v.shape[0]

    kr = lax.broadcasted_iota(jnp.int32, (3 * blk, blk), 0)
    qc = lax.broadcasted_iota(jnp.int32, (3 * blk, blk), 1)
    in_window = jnp.abs(qc - (kr - blk)) <= WINDOW
    biases = []
    for j in range(n_sub):
        valid = in_window
        if j == 0:
            valid = valid & (kr >= jnp.where(i > 0, 0, blk))
        if j == n_sub - 1:
            valid = valid & (kr < jnp.where(i < n_tiles - 1, 3 * blk, 2 * blk))
        biases.append(jnp.where(valid, 0.0, NEG))

    ones = jnp.ones((n_keys, B_HD), F32)
    vts = [jnp.concatenate([v[:, kvh * B_HD:(kvh + 1) * B_HD], ones], axis=1).T.astype(BF16)
           for kvh in range(B_KV_HEADS)]

    tiles = [[] for _ in range(n_sub)]
    pairs = [[] for _ in range(n_sub)]
    for hq, j0 in [(h, g) for g in range(0, n_sub, SWA_INTERLEAVE) for h in range(B_HEADS)]:
        kvh = hq // grp
        ksl = slice(kvh * B_HD, (kvh + 1) * B_HD)
        sk = sink_ref[layer:layer + 1, hq:hq + 1]
        for j in range(j0, j0 + SWA_INTERLEAVE):
            keys = slice(j * blk, (j + 3) * blk)
            qh = q[j * blk:(j + 1) * blk, hq * B_HD:(hq + 1) * B_HD]
            s = _dot_nt(k[keys, ksl], qh)
            s = jnp.concatenate([s[:blk] + biases[j][:blk], s[blk:2 * blk],
                                 s[2 * blk:] + biases[j][2 * blk:]], axis=0)
            m = jnp.maximum(jnp.max(s, axis=0, keepdims=True), sk)
            p = jnp.exp(s - m).astype(BF16)
            o = _dot(vts[kvh][:, keys], p)
            denom = o[B_HD:B_HD + 1, :] + jnp.exp(sk - m)
            pairs[j].append(o[:B_HD, :] / denom)
            if len(pairs[j]) == LANE // B_HD:
                tiles[j].append(jnp.concatenate(pairs[j], axis=0).T)
                pairs[j] = []
    for j in range(n_sub):
        rows = slice(j * blk, (j + 1) * blk)
        y = jnp.concatenate(tiles[j], axis=1) * _silu(z_ref[rows, :])
        out_ref[rows, :] = y.astype(out_ref.dtype)


def _swa(u, cos_t, sin_t, sink, layer, batch):
    rows = u.shape[0]
    n_blocks = rows // batch // CHUNK
    n_sub = SWA_TILE // CHUNK
    n_tiles = n_blocks // n_sub

    def tile_spec(width, blk):
        return pl.BlockSpec((SWA_TILE, width), lambda bi, i: (bi * n_tiles + i, blk))

    def halo_spec(width, blk, after):
        def index(bi, i):
            nb = (i + 1) * n_sub if after else i * n_sub - 1
            return (bi * n_blocks + jnp.clip(nb, 0, n_blocks - 1), blk)
        return pl.BlockSpec((CHUNK, width), index)

    def table_specs():
        return [halo_spec(LANE, 0, False), tile_spec(LANE, 0), halo_spec(LANE, 0, True)]

    return pl.pallas_call(
        functools.partial(_swa_kernel, layer=layer),
        out_shape=jax.ShapeDtypeStruct((rows, D_GROUP), BF16),
        grid=(batch, n_tiles),
        in_specs=[tile_spec(D_GROUP, BLK_B_Q), tile_spec(D_GROUP, BLK_B_Z),
                  halo_spec(D_GROUP, BLK_MISC, False), tile_spec(D_GROUP, BLK_MISC),
                  halo_spec(D_GROUP, BLK_MISC, True)] + table_specs() + table_specs() + [
                  pl.BlockSpec(sink.shape, lambda bi, i: (0, 0))],
        out_specs=tile_spec(D_GROUP, 0),
        compiler_params=pltpu.CompilerParams(
            dimension_semantics=("parallel", "parallel"), vmem_limit_bytes=VMEM_LIMIT),
        name="window_attn",
    )(u, u, u, u, u, cos_t, cos_t, cos_t, sin_t, sin_t, sin_t, sink)


def _mem_kv_kernel(mem_ref, g_ref, w_ref, kt_ref, v_ref, *, layer):
    x = mem_ref[...]
    ms = jnp.mean(x * x, axis=-1, keepdims=True)
    h = (x * lax.rsqrt(ms + EPS) * g_ref[layer:layer + 1, :]).astype(BF16)
    kv = _dot(h, w_ref[...].astype(BF16))
    kt_ref[...] = kv[:, :D_GROUP].T.astype(kt_ref.dtype)
    v_ref[...] = kv[:, D_GROUP:].astype(v_ref.dtype)


def _mem_kv(mem, g_all, w_all, layer):
    batch, m_len, d = mem.shape
    n_out = w_all.shape[2]
    return pl.pallas_call(
        functools.partial(_mem_kv_kernel, layer=layer),
        out_shape=(jax.ShapeDtypeStruct((batch, D_GROUP, m_len), BF16),
                   jax.ShapeDtypeStruct((batch, m_len, D_GROUP), BF16)),
        grid=(batch,),
        in_specs=[pl.BlockSpec((None, m_len, d), lambda bi: (bi, 0, 0)),
                  pl.BlockSpec(g_all.shape, lambda bi: (0, 0)),
                  pl.BlockSpec((None, d, n_out), lambda bi: (layer, 0, 0))],
        out_specs=(pl.BlockSpec((None, D_GROUP, m_len), lambda bi: (bi, 0, 0)),
                   pl.BlockSpec((None, m_len, D_GROUP), lambda bi: (bi, 0, 0))),
        compiler_params=pltpu.CompilerParams(
            dimension_semantics=("parallel",), vmem_limit_bytes=VMEM_LIMIT),
        name="mem_kv",
    )(mem, g_all, w_all)


def _mem_attn_out_kernel(q_ref, z_ref, kt_ref, v_ref, ya_ref, yb_ref, yc_ref, w_ref, x_ref, g_ref,
                         o_ref, w16_ref, *, final):
    @pl.when((pl.program_id(0) == 0) & (pl.program_id(1) == 0))
    def _():
        w16_ref[...] = w_ref[...].astype(BF16)

    q = q_ref[...].astype(BF16)
    outs = []
    for hd in range(D_HEADS):
        sl = slice(hd * D_HD, (hd + 1) * D_HD)
        s = _dot(q[:, sl], kt_ref[sl, :])
        m = jnp.max(s, axis=-1, keepdims=True)
        p = jnp.exp2((s - m) * (D_HD ** -0.5 * LOG2_E))
        den = jnp.sum(p, axis=-1, keepdims=True)
        outs.append(_dot(p.astype(BF16), v_ref[:, sl]) / den)
    yd = (jnp.concatenate(outs, axis=1) * _silu(z_ref[...])).astype(BF16)

    y = jnp.concatenate([ya_ref[...], yb_ref[...], yc_ref[...], yd], axis=1)
    acc = x_ref[...] + _dot(y, w16_ref[...])
    if final:
        ms = jnp.mean(acc * acc, axis=-1, keepdims=True)
        acc = acc * lax.rsqrt(ms + EPS) * g_ref[...]
    o_ref[...] = acc


def _mem_attn_out(u, u16, kt, v, ya, yb, yc, w_all, xf, g, layer, batch, final):
    rows, d = xf.shape
    tm = ROW_TILE
    per_batch = rows // batch // tm
    row = lambda width, blk: pl.BlockSpec((tm, width), lambda bi, i: (bi * per_batch + i, blk))
    return pl.pallas_call(
        functools.partial(_mem_attn_out_kernel, final=final),
        out_shape=jax.ShapeDtypeStruct((rows, d), F32),
        grid=(batch, per_batch),
        in_specs=[row(D_GROUP, BLK16_D_Q), row(D_GROUP, BLK_D_Z),
                  pl.BlockSpec((None,) + kt.shape[1:], lambda bi, i: (bi, 0, 0)),
                  pl.BlockSpec((None,) + v.shape[1:], lambda bi, i: (bi, 0, 0)),
                  row(D_GROUP, 0), row(D_GROUP, 0), row(D_GROUP, 0),
                  pl.BlockSpec((None, 4 * D_GROUP, d), lambda bi, i: (layer, 0, 0),
                               pipeline_mode=pl.Buffered(1)),
                  row(d, 0),
                  pl.BlockSpec((1, d), lambda bi, i: (0, 0))],
        out_specs=row(d, 0),
        scratch_shapes=[pltpu.VMEM((4 * D_GROUP, d), BF16)],
        compiler_params=pltpu.CompilerParams(
            dimension_semantics=("arbitrary", "arbitrary"), vmem_limit_bytes=VMEM_LIMIT),
        name="mem_attn_out",
    )(u16, u, kt, v, ya, yb, yc, w_all, xf, g.reshape(1, d))


def kernel(x, mem, positions, norm_g, w_in, hgrn_lb_logits, hgrn_norm_g, attn_sink, mlstm_conv_w,
           mlstm_gate_b, mlstm_norm_g, mem_norm_g, w_mem_kv, w_out, final_norm_g):
    batch, seq, d = x.shape
    depth = w_in.shape[0]
    assert all(seq % t == 0 for t in (CHUNK, ROW_TILE, IN_ROW_TILE, SWA_TILE))
    assert batch % REC_GROUP == 0
    xf = x.reshape(batch * seq, d)
    cos_t, sin_t = _rope_tables(positions)
    w_all = _prep_w_in(w_in)
    gate_b = mlstm_gate_b.reshape(depth, N_GATE, 1)
    for layer in range(depth):
        u, u16, gt, vt = _in_proj(xf, norm_g, w_all, gate_b, layer, batch)
        u3 = u.reshape(batch, seq, u.shape[1])
        u16_3 = u16.reshape(batch, seq, u16.shape[1])
        ob = _hgrn(u3, u16_3, hgrn_lb_logits, layer, reverse=True)
        ya = _hgrn(u3, u16_3, hgrn_lb_logits, layer, reverse=False, ob=ob, norm_g=hgrn_norm_g)
        yb = _swa(u, cos_t, sin_t, attn_sink, layer, batch)
        hb, k_c, qt_c = _mlstm(u3, gt, vt, mlstm_conv_w, layer, reverse=True)
        yc = _mlstm(u3, gt, vt, None, layer, reverse=False, qk=(k_c, qt_c), hb=hb,
                    norm_g=mlstm_norm_g)
        mem_kt, mem_v = _mem_kv(mem, mem_norm_g, w_mem_kv, layer)
        xf = _mem_attn_out(u, u16, mem_kt, mem_v, ya.reshape(batch * seq, D_GROUP), yb,
                           yc.reshape(batch * seq, D_GROUP), w_out, xf, final_norm_g, layer, batch,
                           final=(layer == depth - 1))
    return xf.reshape(batch, seq, d)
```
